```python
import math
import jax
import jax.numpy as jnp
from jax import lax
import numpy as np

D_MODEL = 1024
BATCH = 16
SEQ = 2048
DEPTH = 1
DEC_BATCH = 32
DEC_SEQ = 16
PAST_LEN = 4096

CHUNK = 64
DK_A = 128
DV_A = 128
H_A = (D_MODEL // 2) // DV_A
QK_A = H_A * DK_A
V_A = H_A * DV_A
QKV_A = 2 * QK_A + V_A
CONV_W = 4
N_B = 64
H_B = (D_MODEL // 2) // N_B
D_B = H_B * N_B
W_LORA = 64
A_LORA = 64
G_LORA = 128
SHIFT_W = 3 * D_B + W_LORA + A_LORA + G_LORA
OFF_ALPHA = QKV_A
OFF_BETA = OFF_ALPHA + H_A
OFF_Z = OFF_BETA + H_A
OFF_RWKV = OFF_Z + V_A
OFF_MERGE = OFF_RWKV + SHIFT_W
IN_W = OFF_MERGE + 2 * D_MODEL
N_EXPERTS = 64
TOP_K = 8
N_GROUPS = 8
TOPK_GROUPS = 4
D_EXPERT = 256
D_SHARED = 256
ROUTED_SCALE = 2.5
DN_ALPHA = (2 * DEPTH) ** 0.25
DN_BETA = (8 * DEPTH) ** -0.25
LN_EPS = 1e-5
GN_EPS = 64e-5
RMS_EPS = 1e-6

kernel_name = 'hybrid_gdn_rwkv7_moe_stream_step'


def _layernorm(x, g, b):
    xf = x.astype(jnp.float32)
    mu = jnp.mean(xf, -1, keepdims=True)
    var = jnp.mean(jnp.square(xf - mu), -1, keepdims=True)
    return ((xf - mu) * lax.rsqrt(var + LN_EPS) * g + b).astype(x.dtype)


def _l2norm(x):
    return x * lax.rsqrt(jnp.sum(x * x, -1, keepdims=True) + 1e-6)


def _causal_conv(u, buf, w):
    up = jnp.concatenate([buf, u], axis=1)
    out = lax.conv_general_dilated(up, w[:, None, :], window_strides=(1,), padding='VALID',
                                   dimension_numbers=('NWC', 'WIO', 'NWC'),
                                   feature_group_count=u.shape[-1])
    return out, up[:, -(CONV_W - 1):]


def _gated_delta_chunked(q, k, v, g, beta, S0):
    B, L, H, DK = q.shape
    DV = v.shape[-1]
    C = CHUNK if L % CHUNK == 0 else L
    N = L // C

    def blocks(t):
        return t.reshape(B, N, C, H, -1).transpose(1, 0, 3, 2, 4)

    qc = blocks(q * DK ** -0.5)
    kc = blocks(k)
    vc = blocks(v)
    gc = blocks(g[..., None])[..., 0]
    bc = blocks(beta[..., None])[..., 0]
    G = jnp.cumsum(gc, axis=-1)
    incl = jnp.tril(jnp.ones((C, C), bool))
    strict = jnp.tril(jnp.ones((C, C), bool), -1)
    diff = G[..., :, None] - G[..., None, :]
    decay = jnp.where(incl, jnp.exp(jnp.where(incl, diff, 0.0)), 0.0)
    kk = jnp.einsum('nbhid,nbhjd->nbhij', kc, kc)
    lower = jnp.where(strict, bc[..., :, None] * kk * decay, 0.0) + jnp.eye(C, dtype=jnp.float32)
    rhs = jnp.concatenate([vc * bc[..., None], kc * (bc * jnp.exp(G))[..., None]], axis=-1)
    sol = lax.linalg.triangular_solve(lower, rhs, left_side=True, lower=True, unit_diagonal=True)
    u, w = sol[..., :DV], sol[..., DV:]
    qk = jnp.einsum('nbhid,nbhjd->nbhij', qc, kc) * decay
    qg = qc * jnp.exp(G)[..., None]
    kd = kc * jnp.exp(G[..., -1:] - G)[..., None]
    gl = jnp.exp(G[..., -1])

    def step(S, xs):
        u_n, w_n, qk_n, qg_n, kd_n, gl_n = xs
        v_new = u_n - jnp.einsum('bhck,bhkv->bhcv', w_n, S)
        o_n = jnp.einsum('bhck,bhkv->bhcv', qg_n, S) + jnp.einsum('bhij,bhjv->bhiv', qk_n, v_new)
        S = S * gl_n[..., None, None] + jnp.einsum('bhck,bhcv->bhkv', kd_n, v_new)
        return S, o_n

    S, o = lax.scan(step, S0, (u, w, qk, qg, kd, gl))
    return o.transpose(1, 0, 3, 2, 4).reshape(B, L, H, DV), S


def _rwkv7_scan(r, decay, k, v, kk, a, S0):
    def step(S, xs):
        r_t, d_t, k_t, v_t, kk_t, a_t = xs
        sa = jnp.einsum('bhvk,bhk->bhv', S, -kk_t)
        S = (S * d_t[:, :, None, :] + sa[..., None] * (kk_t * a_t)[:, :, None, :]
             + v_t[..., None] * k_t[:, :, None, :])
        return S, jnp.einsum('bhvk,bhk->bhv', S, r_t)

    xs = tuple(jnp.swapaxes(t, 0, 1) for t in (r, decay, k, v, kk, a))
    S, y = lax.scan(step, S0, xs)
    return jnp.swapaxes(y, 0, 1), S


def _gdn_branch(proj, conv_buf, S0, p):
    B, L, _ = proj.shape
    qkv, conv_new = _causal_conv(proj[..., :OFF_ALPHA], conv_buf.astype(jnp.float32),
                                 p['conv_w'].astype(jnp.float32))
    qkv = jax.nn.silu(qkv)
    q = _l2norm(qkv[..., :QK_A].reshape(B, L, H_A, DK_A))
    k = _l2norm(qkv[..., QK_A:2 * QK_A].reshape(B, L, H_A, DK_A))
    v = qkv[..., 2 * QK_A:].reshape(B, L, H_A, DV_A)
    g = -jnp.exp(p['a_log'].astype(jnp.float32)) * jax.nn.softplus(proj[..., OFF_ALPHA:OFF_BETA] + p['dt_bias'])
    beta = jax.nn.sigmoid(proj[..., OFF_BETA:OFF_Z])
    o, S = _gated_delta_chunked(q, k, v, g, beta, S0.astype(jnp.float32))
    z = proj[..., OFF_Z:OFF_RWKV].reshape(B, L, H_A, DV_A)
    o = o * lax.rsqrt(jnp.mean(o * o, -1, keepdims=True) + RMS_EPS) * p['gdn_norm_w'] * jax.nn.silu(z)
    return o.reshape(B, L, V_A), conv_new, S


def _rwkv7_branch(proj, shift_buf, S0, p):
    B, L, _ = proj.shape
    cur = proj[..., OFF_RWKV:OFF_MERGE]
    seq = jnp.concatenate([shift_buf.astype(jnp.float32), cur], axis=1)
    mixed = cur + (seq[:, :-1] - cur) * p['mu_shift']
    shift_new = seq[:, -1:]
    r, k, v, wl, al, gl = jnp.split(
        mixed, [D_B, 2 * D_B, 3 * D_B, 3 * D_B + W_LORA, 3 * D_B + W_LORA + A_LORA], axis=-1)
    w = -jax.nn.softplus(-(p['w0'] + jnp.tanh(wl) @ p['w2'])) - 0.5
    decay = jnp.exp(-jnp.exp(w))
    a = jax.nn.sigmoid(p['a0'] + al @ p['a2'])
    gate = jax.nn.sigmoid(gl) @ p['g2']

    def heads(t):
        return t.reshape(B, L, H_B, N_B)

    kk = _l2norm(heads(k * p['k_k']))
    k = k * (1.0 + (a - 1.0) * p['k_a'])
    rh, kh, vh = heads(r), heads(k), heads(v)
    y, S = _rwkv7_scan(rh, heads(decay), kh, vh, kk, heads(a), S0.astype(jnp.float32))
    mu = jnp.mean(y, -1, keepdims=True)
    var = jnp.mean(jnp.square(y - mu), -1, keepdims=True)
    y = ((y - mu) * lax.rsqrt(var + GN_EPS)).reshape(B, L, D_B) * p['lnx_g'] + p['lnx_b']
    y = y + (jnp.sum(rh * kh * p['r_k'], -1, keepdims=True) * vh).reshape(B, L, D_B)
    return y * gate, shift_new, S


def _moe(h, p):
    B, L, D = h.shape
    scores = jax.nn.sigmoid((h @ p['router_w']).astype(jnp.float32))
    choice = scores + p['router_bias'].astype(jnp.float32)
    grp = choice.reshape(B, L, N_GROUPS, N_EXPERTS // N_GROUPS)
    grp_score = jnp.sum(lax.top_k(grp, 2)[0], -1)
    _, gidx = lax.top_k(grp_score, TOPK_GROUPS)
    gmask = jnp.sum(jax.nn.one_hot(gidx, N_GROUPS, dtype=jnp.float32), -2) > 0
    emask = jnp.repeat(gmask, N_EXPERTS // N_GROUPS, axis=-1)
    _, eidx = lax.top_k(jnp.where(emask, choice, -jnp.inf), TOP_K)
    wsel = jnp.take_along_axis(scores, eidx, axis=-1)
    wsel = ROUTED_SCALE * wsel / (jnp.sum(wsel, -1, keepdims=True) + 1e-20)
    gates = jnp.einsum('blk,blke->ble', wsel,
                       jax.nn.one_hot(eidx, N_EXPERTS, dtype=jnp.float32)).astype(h.dtype)

    def routed(args):
        hs, gs = args
        act = (jax.nn.silu(jnp.einsum('ld,edf->lef', hs, p['we_gate']))
               * jnp.einsum('ld,edf->lef', hs, p['we_up']))
        return jnp.einsum('lef,efd->ld', act * gs[..., None], p['we_down'])

    y = lax.map(routed, (h, gates))
    shared = (jax.nn.silu(h @ p['ws_gate']) * (h @ p['ws_up'])) @ p['ws_down']
    return y + shared


def _layer(x, c, conv_buf, gdn_S, shift_buf, rwkv_S, p):
    D = x.shape[-1]
    mod = jax.nn.silu(c) @ p['w_ada'] + p['b_ada']
    sh1, sc1, ga1, sh2, sc2, ga2 = jnp.split(mod[:, None, :], 6, axis=-1)
    h = x * (1.0 + sc1) + sh1
    proj = (h @ p['w_in']).astype(jnp.float32)
    ya, conv_new, gdn_new = _gdn_branch(proj, conv_buf, gdn_S, p)
    yb, shift_new, rwkv_new = _rwkv7_branch(proj, shift_buf, rwkv_S, p)
    mg = jax.nn.sigmoid(proj[..., OFF_MERGE:])
    merged = (mg[..., :D] * (ya.astype(x.dtype) @ p['p_a'])
              + mg[..., D:] * (yb.astype(x.dtype) @ p['p_b']))
    mix = merged.astype(x.dtype) @ p['w_o']
    x = _layernorm(DN_ALPHA * x + (1.0 + ga1) * mix, p['ln1_g'], p['ln1_b'])
    h2 = x * (1.0 + sc2) + sh2
    x = _layernorm(DN_ALPHA * x + (1.0 + ga2) * _moe(h2, p), p['ln2_g'], p['ln2_b'])
    return (x, conv_new.astype(conv_buf.dtype), gdn_new.astype(gdn_S.dtype),
            shift_new.astype(shift_buf.dtype), rwkv_new.astype(rwkv_S.dtype))


def setup_inputs(seed: int = 0) -> dict:
    key = jax.random.key(seed)
    keys = iter(jax.random.split(key, 64))
    f32 = jnp.float32

    def nrm(shape, scale):
        return scale * jax.random.normal(next(keys), shape, f32)

    def unif(shape, lo, hi):
        return jax.random.uniform(next(keys), shape, f32, lo, hi)

    L, D = DEPTH, D_MODEL
    dt = jnp.exp(unif((L, H_A), math.log(1e-3), math.log(1e-1)))
    w0_base = jnp.broadcast_to(jnp.linspace(-6.0, -1.0, N_B, dtype=f32), (L, H_B, N_B)).reshape(L, D_B)
    return {
        'x_prompt': nrm((BATCH, SEQ, D), 1.0),
        'x_sample': nrm((DEC_BATCH, DEC_SEQ, D), 1.0),
        'c_prompt': nrm((BATCH, D), 1.0),
        'c_sample': nrm((DEC_BATCH, D), 1.0),
        'state_gdn_conv': nrm((L, DEC_BATCH, CONV_W - 1, QKV_A), 1.0),
        'state_gdn': nrm((L, DEC_BATCH, H_A, DK_A, DV_A), 0.1),
        'state_rwkv_shift': nrm((L, DEC_BATCH, 1, SHIFT_W), 1.0),
        'state_rwkv': nrm((L, DEC_BATCH, H_B, N_B, N_B), 0.1),
        'w_ada': nrm((L, D, 6 * D), 0.1 * D ** -0.5),
        'b_ada': nrm((L, 6 * D), 0.02),
        'w_in': nrm((L, D, IN_W), D ** -0.5),
        'conv_w': nrm((L, CONV_W, QKV_A), CONV_W ** -0.5),
        'a_log': jnp.log(unif((L, H_A), 1.0, 16.0)),
        'dt_bias': dt + jnp.log(-jnp.expm1(-dt)),
        'gdn_norm_w': 1.0 + nrm((L, DV_A), 0.02),
        'mu_shift': unif((L, SHIFT_W), 0.0, 1.0),
        'w0': w0_base + nrm((L, D_B), 0.1),
        'w2': nrm((L, W_LORA, D_B), 0.1 * W_LORA ** -0.5),
        'a0': nrm((L, D_B), 0.1),
        'a2': nrm((L, A_LORA, D_B), 0.1 * A_LORA ** -0.5),
        'g2': nrm((L, G_LORA, D_B), G_LORA ** -0.5),
        'k_k': 0.85 + nrm((L, D_B), 0.02),
        'k_a': 1.0 + nrm((L, D_B), 0.02),
        'r_k': nrm((L, H_B, N_B), 0.1),
        'lnx_g': 1.0 + nrm((L, D_B), 0.02),
        'lnx_b': nrm((L, D_B), 0.02),
        'p_a': nrm((L, V_A, D), DN_BETA * V_A ** -0.5),
        'p_b': nrm((L, D_B, D), DN_BETA * D_B ** -0.5),
        'w_o': nrm((L, D, D), DN_BETA * D ** -0.5),
        'ln1_g': 1.0 + nrm((L, D), 0.02),
        'ln1_b': nrm((L, D), 0.02),
        'router_w': nrm((L, D, N_EXPERTS), D ** -0.5),
        'router_bias': nrm((L, N_EXPERTS), 0.01),
        'we_gate': nrm((L, N_EXPERTS, D, D_EXPERT), D ** -0.5),
        'we_up': nrm((L, N_EXPERTS, D, D_EXPERT), D ** -0.5),
        'we_down': nrm((L, N_EXPERTS, D_EXPERT, D), DN_BETA * D_EXPERT ** -0.5),
        'ws_gate': nrm((L, D, D_SHARED), D ** -0.5),
        'ws_up': nrm((L, D, D_SHARED), D ** -0.5),
        'ws_down': nrm((L, D_SHARED, D), DN_BETA * D_SHARED ** -0.5),
        'ln2_g': 1.0 + nrm((L, D), 0.02),
        'ln2_b': nrm((L, D), 0.02),
    }


def reference(x_prompt, x_sample, c_prompt, c_sample, state_gdn_conv, state_gdn, state_rwkv_shift,
              state_rwkv, w_ada, b_ada, w_in, conv_w, a_log, dt_bias, gdn_norm_w, mu_shift, w0, w2,
              a0, a2, g2, k_k, k_a, r_k, lnx_g, lnx_b, p_a, p_b, w_o, ln1_g, ln1_b, router_w,
              router_bias, we_gate, we_up, we_down, ws_gate, ws_up, ws_down, ln2_g, ln2_b):
    bp = x_prompt.shape[0]
    dtp = x_prompt.dtype
    yp, ys = x_prompt, x_sample
    new_p = ([], [], [], [])
    new_s = ([], [], [], [])
    for l in range(DEPTH):
        p = {'w_ada': w_ada[l], 'b_ada': b_ada[l], 'w_in': w_in[l], 'conv_w': conv_w[l],
             'a_log': a_log[l], 'dt_bias': dt_bias[l], 'gdn_norm_w': gdn_norm_w[l],
             'mu_shift': mu_shift[l], 'w0': w0[l], 'w2': w2[l], 'a0': a0[l], 'a2': a2[l],
             'g2': g2[l], 'k_k': k_k[l], 'k_a': k_a[l], 'r_k': r_k[l], 'lnx_g': lnx_g[l],
             'lnx_b': lnx_b[l], 'p_a': p_a[l], 'p_b': p_b[l], 'w_o': w_o[l], 'ln1_g': ln1_g[l],
             'ln1_b': ln1_b[l], 'router_w': router_w[l], 'router_bias': router_bias[l],
             'we_gate': we_gate[l], 'we_up': we_up[l], 'we_down': we_down[l],
             'ws_gate': ws_gate[l], 'ws_up': ws_up[l], 'ws_down': ws_down[l],
             'ln2_g': ln2_g[l], 'ln2_b': ln2_b[l]}
        yp, *sp = _layer(yp, c_prompt,
                         jnp.zeros((bp, CONV_W - 1, QKV_A), dtp),
                         jnp.zeros((bp, H_A, DK_A, DV_A), dtp),
                         jnp.zeros((bp, 1, SHIFT_W), dtp),
                         jnp.zeros((bp, H_B, N_B, N_B), dtp), p)
        ys, *ss = _layer(ys, c_sample, state_gdn_conv[l], state_gdn[l], state_rwkv_shift[l],
                         state_rwkv[l], p)
        for lst, val in zip(new_p, sp):
            lst.append(val)
        for lst, val in zip(new_s, ss):
            lst.append(val)
    conv_p, gdn_p, shift_p, rwkv_p = [jnp.stack(t, 0) for t in new_p]
    conv_s, gdn_s, shift_s, rwkv_s = [jnp.stack(t, 0) for t in new_s]
    return (yp, ys, conv_p, gdn_p, shift_p, rwkv_p, conv_s, gdn_s, shift_s, rwkv_s)
```

```python
import functools

import jax
import jax.numpy as jnp
from jax import lax
from jax.experimental import pallas as pl
from jax.experimental.pallas import tpu as pltpu

F32 = jnp.float32
BF16 = jnp.bfloat16
HIGHEST = lax.Precision.HIGHEST

D_MODEL = 1024
DK_A = 128
H_A = 4
QK_A = H_A * DK_A
V_A = H_A * DK_A
QKV_A = 2 * QK_A + V_A
CONV_W = 4
N_B = 64
H_B = 8
D_B = H_B * N_B
W_LORA = 64
A_LORA = 64
G_LORA = 128
SHIFT_W = 3 * D_B + W_LORA + A_LORA + G_LORA
OFF_ALPHA = QKV_A
OFF_BETA = OFF_ALPHA + H_A
OFF_Z = OFF_BETA + H_A
OFF_RWKV = OFF_Z + V_A
OFF_MERGE = OFF_RWKV + SHIFT_W
N_EXPERTS = 64
TOP_K = 8
N_GROUPS = 8
GROUP_SIZE = N_EXPERTS // N_GROUPS
TOPK_GROUPS = 4
D_EXPERT = 256
D_SHARED = 256
ROUTED_SCALE = 2.5
LN_EPS = 1e-5
GN_EPS = 64e-5
RMS_EPS = 1e-6

SUBLANES = 8
LANES = 128

P_QKV = 0
P_Z = QKV_A
P_RW = 2048
RW_BLOCK = 2048
P_MG = P_RW + RW_BLOCK
P_AB = P_MG + 2 * D_MODEL
AB_BLOCK = LANES
N_PROJ = P_AB + AB_BLOCK
PROJ_TN = 896

CHUNK = 128
NEUMANN_STEPS = 6

REC_PRECISION = HIGHEST


def _cparams(sem, vmem_mb):
    return pltpu.CompilerParams(dimension_semantics=sem, vmem_limit_bytes=vmem_mb * 1024 * 1024)


def _dot(a, b):
    return jnp.dot(a.astype(BF16), b.astype(BF16), preferred_element_type=F32)


def _dot_hi(a, b):
    return jnp.dot(a, b, precision=HIGHEST, preferred_element_type=F32)


def _rdot(a, b, dims=((1,), (0,))):
    if REC_PRECISION == HIGHEST:
        return lax.dot_general(a, b, (dims, ((), ())), precision=HIGHEST, preferred_element_type=F32)
    return lax.dot_general(a.astype(BF16), b.astype(BF16), (dims, ((), ())), preferred_element_type=F32)


_NT = ((1,), (1,))
_TN = ((0,), (0,))


def _silu(x):
    return x * jax.nn.sigmoid(x)


def _softplus(x):
    return jnp.maximum(x, 0.0) + jnp.log1p(jnp.exp(-jnp.abs(x)))


def _neumann_inverse(a, eye):
    x = eye + a
    p = a
    for _ in range(NEUMANN_STEPS):
        p = _rdot(p, p)
        x = x + _rdot(x, p)
    return x


def _tri_consts():
    row = lax.broadcasted_iota(jnp.int32, (CHUNK, CHUNK), 0)
    col = lax.broadcasted_iota(jnp.int32, (CHUNK, CHUNK), 1)
    incl = row >= col
    strict = row > col
    eye = jnp.where(row == col, 1.0, 0.0).astype(F32)
    tril = jnp.where(incl, 1.0, 0.0).astype(F32)
    return row, col, incl, strict, eye, tril


def _ada_kernel(c_ref, w_ref, b_ref, o_ref):
    o_ref[...] = _dot(_silu(c_ref[...]), w_ref[...]) + b_ref[...]


def _ada(c, w_ada, b_ada):
    n, d = c.shape
    nout = w_ada.shape[1]
    tn = 768
    return pl.pallas_call(
        _ada_kernel,
        grid=(nout // tn,),
        in_specs=[pl.BlockSpec((n, d), lambda j: (0, 0)),
                  pl.BlockSpec((d, tn), lambda j: (0, j)),
                  pl.BlockSpec((1, tn), lambda j: (0, j))],
        out_specs=pl.BlockSpec((n, tn), lambda j: (0, j)),
        out_shape=jax.ShapeDtypeStruct((n, nout), F32),
        compiler_params=_cparams(("parallel",), 32),
        name="ada",
    )(c, w_ada, b_ada.reshape(1, nout))


def _inproj_kernel(x_ref, sc_ref, sh_ref, w_ref, o_ref, hs_ref):
    @pl.when(pl.program_id(2) == 0)
    def _():
        h = x_ref[...] * (1.0 + sc_ref[...]) + sh_ref[...]
        hs_ref[...] = h.reshape(hs_ref.shape).astype(BF16)

    o = jnp.dot(hs_ref[...], w_ref[...], preferred_element_type=F32)
    o_ref[...] = o.reshape(o_ref.shape)


def _inproj(x, sc, sh, w_p, bb, tl):
    b, l, d = x.shape
    return pl.pallas_call(
        _inproj_kernel,
        grid=(b // bb, l // tl, N_PROJ // PROJ_TN),
        in_specs=[pl.BlockSpec((bb, tl, d), lambda i, j, n: (i, j, 0)),
                  pl.BlockSpec((bb, 1, d), lambda i, j, n: (i, 0, 0)),
                  pl.BlockSpec((bb, 1, d), lambda i, j, n: (i, 0, 0)),
                  pl.BlockSpec((d, PROJ_TN), lambda i, j, n: (0, n))],
        out_specs=pl.BlockSpec((bb, tl, PROJ_TN), lambda i, j, n: (i, j, n)),
        out_shape=jax.ShapeDtypeStruct((b, l, N_PROJ), F32),
        scratch_shapes=[pltpu.VMEM((bb * tl, d), BF16)],
        compiler_params=_cparams(("parallel", "parallel", "arbitrary"), 48),
        name="inproj",
    )(x, sc, sh, w_p)


def _gdn_kernel(qkv_ref, ab_ref, z_ref, cbuf_ref, s0_ref, cw_ref, alog_ref, dt_ref, nw_ref,
                ya_ref, sout_ref, ext_ref, qc_ref, s_ref, *, tl, l_valid, l_total):
    t = pl.program_id(1)

    @pl.when(t == 0)
    def _():
        ext_ref[0:SUBLANES, :] = cbuf_ref[0]
        s_ref[...] = s0_ref[0]

    ext_ref[SUBLANES:SUBLANES + tl, :] = qkv_ref[0]
    first = SUBLANES - (CONV_W - 1)
    acc = cw_ref[0:1, :] * ext_ref[first:first + tl, :]
    for j in range(1, CONV_W):
        acc = acc + cw_ref[j:j + 1, :] * ext_ref[first + j:first + j + tl, :]
    ext_ref[0:SUBLANES, :] = ext_ref[tl:tl + SUBLANES, :]
    qc_ref[...] = _silu(acc)

    row, col, incl, strict, eye, tril = _tri_consts()
    ones = jnp.ones((CHUNK, CHUNK), F32)
    neg_a = -jnp.exp(alog_ref[...])
    dt = dt_ref[...]
    nw = nw_ref[...]

    def chunk(c, carry):
        r0 = pl.multiple_of(c * CHUNK, CHUNK)
        rows = pl.ds(r0, CHUNK)
        ab = ab_ref[0, rows, :]
        g_all = neg_a * _softplus(ab + dt)
        b_all = jax.nn.sigmoid(ab)
        if l_valid < l_total:
            valid = (t * tl + r0 + row) < l_valid
            g_all = jnp.where(valid, g_all, 0.0)
            b_all = jnp.where(valid, b_all, 0.0)
        for h in range(H_A):
            sel_g = jnp.where(row == h, 1.0, 0.0).astype(F32)
            sel_b = jnp.where(row == H_A + h, 1.0, 0.0).astype(F32)
            g_col = _dot_hi(tril, _dot_hi(g_all, sel_g))
            beta = _dot_hi(b_all, sel_b)
            g_row = _dot_hi(ones, jnp.where(row == col, g_col, 0.0))
            decay = jnp.where(incl, jnp.exp(jnp.where(incl, g_col - g_row, 0.0)), 0.0)
            q = qc_ref[rows, h * DK_A:(h + 1) * DK_A]
            k = qc_ref[rows, QK_A + h * DK_A:QK_A + (h + 1) * DK_A]
            v = qc_ref[rows, 2 * QK_A + h * DK_A:2 * QK_A + (h + 1) * DK_A]
            q = q * lax.rsqrt(jnp.sum(q * q, -1, keepdims=True) + 1e-6) * (DK_A ** -0.5)
            k = k * lax.rsqrt(jnp.sum(k * k, -1, keepdims=True) + 1e-6)
            a = jnp.where(strict, beta * _rdot(k, k, _NT) * decay, 0.0)
            x = eye - a
            p = a
            for _ in range(NEUMANN_STEPS):
                p = _rdot(p, p)
                x = x + _rdot(x, p)
            e_g = jnp.exp(g_col)
            u = _rdot(x, v * beta)
            w = _rdot(x, k * (beta * e_g))
            qk = jnp.where(incl, _rdot(q, k, _NT) * decay, 0.0)
            g_last = g_col[CHUNK - 1:CHUNK, :]
            kd = k * jnp.exp(g_last - g_col)
            s = s_ref[h]
            v_new = u - _rdot(w, s)
            o = _rdot(q * e_g, s) + _rdot(qk, v_new)
            s_ref[h] = s * jnp.exp(g_last) + _rdot(kd, v_new, _TN)
            z = z_ref[0, rows, h * DK_A:(h + 1) * DK_A]
            o = o * lax.rsqrt(jnp.mean(o * o, -1, keepdims=True) + RMS_EPS) * nw * _silu(z)
            ya_ref[0, rows, h * DK_A:(h + 1) * DK_A] = o.astype(BF16)
        return carry

    lax.fori_loop(0, tl // CHUNK, chunk, 0)

    @pl.when(t == pl.num_programs(1) - 1)
    def _():
        sout_ref[0] = s_ref[...]


def _gdn(proj, conv_buf8, s0, conv_w, alog_row, dt_row, norm_w, tl, l_valid):
    b, l, _ = proj.shape
    kern = functools.partial(_gdn_kernel, tl=tl, l_valid=l_valid, l_total=l)
    full = lambda *shape: pl.BlockSpec(shape, lambda i, t: (0,) * len(shape))
    return pl.pallas_call(
        kern,
        grid=(b, l // tl),
        in_specs=[pl.BlockSpec((1, tl, QKV_A), lambda i, t: (i, t, P_QKV // QKV_A)),
                  pl.BlockSpec((1, tl, AB_BLOCK), lambda i, t: (i, t, P_AB // AB_BLOCK)),
                  pl.BlockSpec((1, tl, V_A), lambda i, t: (i, t, P_Z // V_A)),
                  pl.BlockSpec((1, SUBLANES, QKV_A), lambda i, t: (i, 0, 0)),
                  pl.BlockSpec((1, H_A, DK_A, DK_A), lambda i, t: (i, 0, 0, 0)),
                  full(CONV_W, QKV_A), full(1, LANES), full(1, LANES), full(1, DK_A)],
        out_specs=[pl.BlockSpec((1, tl, V_A), lambda i, t: (i, t, 0)),
                   pl.BlockSpec((1, H_A, DK_A, DK_A), lambda i, t: (i, 0, 0, 0))],
        out_shape=[jax.ShapeDtypeStruct((b, l, V_A), BF16),
                   jax.ShapeDtypeStruct((b, H_A, DK_A, DK_A), F32)],
        scratch_shapes=[pltpu.VMEM((tl + SUBLANES, QKV_A), F32),
                        pltpu.VMEM((tl, QKV_A), F32),
                        pltpu.VMEM((H_A, DK_A, DK_A), F32)],
        compiler_params=_cparams(("parallel", "arbitrary"), 48),
        name="gdn",
    )(proj, proj, proj, conv_buf8, s0, conv_w, alog_row, dt_row, norm_w)


PAIRS = H_B // 2


def _headsum(x, bd):
    return jnp.concatenate([_dot_hi(x[:, p * LANES:(p + 1) * LANES], bd) for p in range(PAIRS)], axis=1)


def _rwkv_kernel(rw_ref, sbuf_ref, s0_ref, mu_ref, w0_ref, a0_ref, kk_ref, ka_ref, rk_ref, lg_ref, lb_ref,
                 w2_ref, a2_ref, g2_ref, yb_ref, sout_ref,
                 ext_ref, r_s, k_s, v_s, z_s, p_s, lw_s, y_s, s_ref, *, tl, l_valid, l_total):
    t = pl.program_id(1)

    @pl.when(t == 0)
    def _():
        ext_ref[0:SUBLANES, :] = sbuf_ref[0]
        s_ref[...] = s0_ref[0]

    cur = rw_ref[0]
    ext_ref[SUBLANES:SUBLANES + tl, :] = cur
    prev = ext_ref[SUBLANES - 1:SUBLANES - 1 + tl, :]
    mixed = cur + (prev - cur) * mu_ref[...]
    ext_ref[0:SUBLANES, :] = ext_ref[tl:tl + SUBLANES, :]

    row, col, incl, strict, eye, tril = _tri_consts()
    bd = jnp.where((row // N_B) == (col // N_B), 1.0, 0.0).astype(F32)
    same_head = (row // N_B) == (col // N_B)
    lane_a = col < N_B

    r = mixed[:, 0:D_B]
    k = mixed[:, D_B:2 * D_B]
    v = mixed[:, 2 * D_B:3 * D_B]
    lora = mixed[:, 3 * D_B:3 * D_B + W_LORA + A_LORA]
    g_in = mixed[:, 3 * D_B + W_LORA + A_LORA:SHIFT_W]
    w = -_softplus(-(w0_ref[...] + _dot(jnp.tanh(lora), w2_ref[...]))) - 0.5
    lw = -jnp.exp(w)
    a = jax.nn.sigmoid(a0_ref[...] + _dot(lora, a2_ref[...]))
    gate = _dot(jax.nn.sigmoid(g_in), g2_ref[...])
    kkr = k * kk_ref[...]
    kk = kkr * lax.rsqrt(_headsum(kkr * kkr, bd) + 1e-6)
    k = k * (1.0 + (a - 1.0) * ka_ref[...])
    bonus = _headsum(r * k * rk_ref[...], bd) * v
    if l_valid < l_total:
        rvalid = (t * tl + lax.broadcasted_iota(jnp.int32, (tl, 1), 0)) < l_valid
        lw = jnp.where(rvalid, lw, 0.0)
        kk = jnp.where(rvalid, kk, 0.0)
        k = jnp.where(rvalid, k, 0.0)
        v = jnp.where(rvalid, v, 0.0)
    r_s[...] = r
    k_s[...] = k
    v_s[...] = v
    z_s[...] = -kk
    p_s[...] = kk * a
    lw_s[...] = lw

    def chunk(c, carry):
        r0 = pl.multiple_of(c * CHUNK, CHUNK)
        rows = pl.ds(r0, CHUNK)
        for p in range(PAIRS):
            cols = slice(p * LANES, (p + 1) * LANES)
            lw_c = lw_s[rows, cols]
            g_inc = _dot_hi(tril, lw_c)
            g_exc = g_inc - lw_c
            g_mid = g_inc[CHUNK // 2 - 1:CHUNK // 2, :]
            g_end = g_inc[CHUNK - 1:CHUNK, :]
            z = z_s[rows, cols]
            rr = r_s[rows, cols]
            pp = p_s[rows, cols]
            kk_ = k_s[rows, cols]
            vv = v_s[rows, cols]
            zs = z * jnp.exp(g_exc)
            rs = rr * jnp.exp(g_inc)
            zt = z * jnp.exp(g_exc - g_mid)
            rt = rr * jnp.exp(g_inc - g_mid)
            en = jnp.exp(g_mid - g_inc)
            pb = pp * en
            kb = kk_ * en
            s = s_ref[p]
            zt0 = _rdot(zs, s, _NT)
            rt0 = _rdot(rs, s, _NT)
            u_h = []
            arp_h = []
            ark_h = []
            for hh in range(2):
                mine = lane_a if hh == 0 else jnp.logical_not(lane_a)
                zt_h = jnp.where(mine, zt, 0.0)
                rt_h = jnp.where(mine, rt, 0.0)
                azp = jnp.where(strict, _rdot(zt_h, pb, _NT), 0.0)
                azk = jnp.where(strict, _rdot(zt_h, kb, _NT), 0.0)
                arp_h.append(jnp.where(incl, _rdot(rt_h, pb, _NT), 0.0))
                ark_h.append(jnp.where(incl, _rdot(rt_h, kb, _NT), 0.0))
                minv = _neumann_inverse(azp, eye)
                u_h.append(_rdot(minv, zt0 + _rdot(azk, vv)))
            u = jnp.where(lane_a, u_h[0], u_h[1])
            y = rt0 + jnp.where(lane_a,
                                _rdot(arp_h[0], u) + _rdot(ark_h[0], vv),
                                _rdot(arp_h[1], u) + _rdot(ark_h[1], vv))
            tail = jnp.exp(g_end - g_inc)
            s_new = s * jnp.exp(g_end) + _rdot(u, pp * tail, _TN) + _rdot(vv, kk_ * tail, _TN)
            s_ref[p] = jnp.where(same_head, s_new, 0.0)
            y_s[rows, cols] = y
        return carry

    lax.fori_loop(0, tl // CHUNK, chunk, 0)

    y = y_s[...]
    mean = _headsum(y, bd) * (1.0 / N_B)
    dev = y - mean
    var = _headsum(dev * dev, bd) * (1.0 / N_B)
    yn = dev * lax.rsqrt(var + GN_EPS) * lg_ref[...] + lb_ref[...]
    yb_ref[0] = ((yn + bonus) * gate).astype(BF16)

    @pl.when(t == pl.num_programs(1) - 1)
    def _():
        sout_ref[0] = s_ref[...]


def _rwkv(proj, shift_buf8, s0_pairs, vecs, w2p, a2p, g2, tl, l_valid):
    b, l, _ = proj.shape
    kern = functools.partial(_rwkv_kernel, tl=tl, l_valid=l_valid, l_total=l)
    full = lambda *shape: pl.BlockSpec(shape, lambda i, t: (0,) * len(shape))
    mu, w0, a0, k_k, k_a, r_k, lnx_g, lnx_b = vecs
    return pl.pallas_call(
        kern,
        grid=(b, l // tl),
        in_specs=[pl.BlockSpec((1, tl, RW_BLOCK), lambda i, t: (i, t, P_RW // RW_BLOCK)),
                  pl.BlockSpec((1, SUBLANES, RW_BLOCK), lambda i, t: (i, 0, 0)),
                  pl.BlockSpec((1, PAIRS, LANES, LANES), lambda i, t: (i, 0, 0, 0)),
                  full(1, RW_BLOCK)] + [full(1, D_B)] * 7 +
                 [full(W_LORA + A_LORA, D_B), full(W_LORA + A_LORA, D_B), full(G_LORA, D_B)],
        out_specs=[pl.BlockSpec((1, tl, D_B), lambda i, t: (i, t, 0)),
                   pl.BlockSpec((1, PAIRS, LANES, LANES), lambda i, t: (i, 0, 0, 0))],
        out_shape=[jax.ShapeDtypeStruct((b, l, D_B), BF16),
                   jax.ShapeDtypeStruct((b, PAIRS, LANES, LANES), F32)],
        scratch_shapes=[pltpu.VMEM((tl + SUBLANES, RW_BLOCK), F32)] +
                       [pltpu.VMEM((tl, D_B), F32)] * 7 +
                       [pltpu.VMEM((PAIRS, LANES, LANES), F32)],
        compiler_params=_cparams(("parallel", "arbitrary"), 48),
        name="rwkv",
    )(proj, shift_buf8, s0_pairs, mu, w0, a0, k_k, k_a, r_k, lnx_g, lnx_b, w2p, a2p, g2)


def _layernorm(y, g, b):
    mu = jnp.mean(y, -1, keepdims=True)
    dev = y - mu
    var = jnp.mean(dev * dev, -1, keepdims=True)
    return dev * lax.rsqrt(var + LN_EPS) * g + b


def _route(logits_t, bias):
    tm = logits_t.shape[1]
    scores = jax.nn.sigmoid(logits_t)
    choice = scores + bias
    neg_inf = -jnp.inf
    iota_g = lax.broadcasted_iota(jnp.int32, (GROUP_SIZE, tm), 0)
    group_score = []
    for g in range(N_GROUPS):
        xg = choice[g * GROUP_SIZE:(g + 1) * GROUP_SIZE, :]
        m1 = jnp.max(xg, axis=0, keepdims=True)
        first = jnp.min(jnp.where(xg == m1, iota_g, GROUP_SIZE), axis=0, keepdims=True)
        m2 = jnp.max(jnp.where(iota_g == first, neg_inf, xg), axis=0, keepdims=True)
        group_score.append(m1 + m2)
    masked = []
    for g in range(N_GROUPS):
        rank = jnp.zeros((1, tm), jnp.int32)
        for o in range(N_GROUPS):
            if o == g:
                continue
            ahead = group_score[o] > group_score[g]
            if o < g:
                ahead = ahead | (group_score[o] == group_score[g])
            rank = rank + ahead.astype(jnp.int32)
        keep = rank < TOPK_GROUPS
        masked.append(jnp.where(keep, choice[g * GROUP_SIZE:(g + 1) * GROUP_SIZE, :], neg_inf))
    cur = jnp.concatenate(masked, axis=0)
    iota_e = lax.broadcasted_iota(jnp.int32, (N_EXPERTS, tm), 0)
    sel = jnp.zeros((N_EXPERTS, tm), F32)
    for _ in range(TOP_K):
        m = jnp.max(cur, axis=0, keepdims=True)
        first = jnp.min(jnp.where(cur == m, iota_e, N_EXPERTS), axis=0, keepdims=True)
        pick = iota_e == first
        sel = jnp.where(pick, 1.0, sel)
        cur = jnp.where(pick, neg_inf, cur)
    wsel = sel * scores
    denom = jnp.sum(wsel, axis=0, keepdims=True) + 1e-20
    return (ROUTED_SCALE * wsel) / denom


def _merge_kernel(ya_ref, yb_ref, mg_ref, x_ref, ga1_ref, sc2_ref, sh2_ref, pa_ref, pb_ref, wo_ref,
                  g1_ref, b1_ref, rwt_ref, rb_ref, x1_ref, h2_ref, gt_ref, *, dn_alpha):
    bb, tl, d = x_ref.shape
    tm = bb * tl
    ya = ya_ref[...].reshape(tm, V_A)
    yb = yb_ref[...].reshape(tm, D_B)
    mg = mg_ref[...].reshape(tm, 2 * d)
    merged = (jax.nn.sigmoid(mg[:, :d]) * jnp.dot(ya, pa_ref[...], preferred_element_type=F32)
              + jax.nn.sigmoid(mg[:, d:]) * jnp.dot(yb, pb_ref[...], preferred_element_type=F32))
    mix = _dot(merged, wo_ref[...])
    y = dn_alpha * x_ref[...] + (1.0 + ga1_ref[...]) * mix.reshape(bb, tl, d)
    x1 = _layernorm(y, g1_ref[...], b1_ref[...])
    x1_ref[...] = x1
    h2 = (x1 * (1.0 + sc2_ref[...]) + sh2_ref[...]).astype(BF16)
    h2_ref[...] = h2
    logits_t = lax.dot_general(rwt_ref[...], h2.reshape(tm, d), (_NT, ((), ())), preferred_element_type=F32)
    gt_ref[...] = _route(logits_t, rb_ref[...])


def _merge(ya, yb, proj, x, ga1, sc2, sh2, p_a, p_b, w_o, ln_g, ln_b, rw_t, r_bias, bb, tl, dn_alpha):
    b, l, d = x.shape
    nj = l // tl
    tm = bb * tl
    full = lambda *shape: pl.BlockSpec(shape, lambda i, j: (0,) * len(shape))
    mod = pl.BlockSpec((bb, 1, d), lambda i, j: (i, 0, 0))
    tok = lambda width, dtype=None: pl.BlockSpec((bb, tl, width), lambda i, j: (i, j, 0))
    return pl.pallas_call(
        functools.partial(_merge_kernel, dn_alpha=dn_alpha),
        grid=(b // bb, nj),
        in_specs=[tok(V_A), tok(D_B),
                  pl.BlockSpec((bb, tl, 2 * d), lambda i, j: (i, j, P_MG // (2 * d))),
                  tok(d), mod, mod, mod,
                  full(V_A, d), full(D_B, d), full(d, d), full(1, d), full(1, d),
                  full(N_EXPERTS, d), full(N_EXPERTS, 1)],
        out_specs=[tok(d), tok(d), pl.BlockSpec((N_EXPERTS, tm), lambda i, j: (0, i * nj + j))],
        out_shape=[jax.ShapeDtypeStruct((b, l, d), F32),
                   jax.ShapeDtypeStruct((b, l, d), BF16),
                   jax.ShapeDtypeStruct((N_EXPERTS, b * l), F32)],
        compiler_params=_cparams(("parallel", "parallel"), 48),
        name="merge",
    )(ya, yb, proj, x, ga1, sc2, sh2, p_a, p_b, w_o, ln_g, ln_b, rw_t, r_bias)


def _moe_kernel(h_ref, g_ref, x1_ref, ga2_ref, wgu_ref, wd_ref, sgu_ref, sd_ref, g2_ref, b2_ref,
                o_ref, acc_ref, *, dn_alpha):
    e = pl.program_id(2)
    bb, tl, d = h_ref.shape
    tm = bb * tl
    h = h_ref[...].reshape(tm, d)

    @pl.when(e == 0)
    def _():
        su = jnp.dot(h, sgu_ref[...], preferred_element_type=F32)
        act = _silu(su[:, :D_SHARED]) * su[:, D_SHARED:]
        acc_ref[...] = _dot(act, sd_ref[...])

    gu = jnp.dot(h, wgu_ref[0], preferred_element_type=F32)
    onehot = jnp.where(lax.broadcasted_iota(jnp.int32, (N_EXPERTS, D_EXPERT), 0) == e, 1.0, 0.0).astype(F32)
    gate = _dot_hi(g_ref[...], onehot)
    act = _silu(gu[:, :D_EXPERT]) * gu[:, D_EXPERT:] * gate
    acc_ref[...] += _dot(act, wd_ref[0])

    @pl.when(e == pl.num_programs(2) - 1)
    def _():
        y = dn_alpha * x1_ref[...] + (1.0 + ga2_ref[...]) * acc_ref[...].reshape(bb, tl, d)
        o_ref[...] = _layernorm(y, g2_ref[...], b2_ref[...])


def _moe(h2, gates, x1, ga2, we_gu, we_down, ws_gu, ws_down, ln_g, ln_b, bb, tl, dn_alpha):
    b, l, d = x1.shape
    nj = l // tl
    tm = bb * tl
    full = lambda *shape: pl.BlockSpec(shape, lambda i, j, e: (0,) * len(shape))
    tok = pl.BlockSpec((bb, tl, d), lambda i, j, e: (i, j, 0))
    return pl.pallas_call(
        functools.partial(_moe_kernel, dn_alpha=dn_alpha),
        grid=(b // bb, nj, N_EXPERTS),
        in_specs=[tok,
                  pl.BlockSpec((tm, N_EXPERTS), lambda i, j, e: (i * nj + j, 0)),
                  tok,
                  pl.BlockSpec((bb, 1, d), lambda i, j, e: (i, 0, 0)),
                  pl.BlockSpec((1, d, 2 * D_EXPERT), lambda i, j, e: (e, 0, 0)),
                  pl.BlockSpec((1, D_EXPERT, d), lambda i, j, e: (e, 0, 0)),
                  full(d, 2 * D_SHARED), full(D_SHARED, d), full(1, d), full(1, d)],
        out_specs=tok,
        out_shape=jax.ShapeDtypeStruct((b, l, d), F32),
        scratch_shapes=[pltpu.VMEM((tm, d), F32)],
        compiler_params=_cparams(("parallel", "parallel", "arbitrary"), 56),
        name="moe",
    )(h2, gates, x1, ga2, we_gu, we_down, ws_gu, ws_down, ln_g, ln_b)


def _pad_rows(buf, width):
    b, n, w = buf.shape
    return jnp.pad(buf.astype(F32), ((0, 0), (SUBLANES - n, 0), (0, width - w)))


def _to_pairs(s):
    b = s.shape[0]
    s = s.astype(F32).reshape(b, PAIRS, 2, N_B, N_B)
    zero = jnp.zeros_like(s[:, :, 0])
    top = jnp.concatenate([s[:, :, 0], zero], axis=-1)
    bot = jnp.concatenate([zero, s[:, :, 1]], axis=-1)
    return jnp.concatenate([top, bot], axis=-2)


def _from_pairs(sp):
    b = sp.shape[0]
    return jnp.stack([sp[:, :, :N_B, :N_B], sp[:, :, N_B:, N_B:]], axis=2).reshape(b, H_B, N_B, N_B)


def _layer(x, mod, conv_buf, gdn_s, shift_buf, rwkv_s, p, tiles, dn_alpha):
    b, l, d = x.shape
    sh1, sc1, ga1, sh2, sc2, ga2 = mod
    proj = _inproj(x, sc1, sh1, p['w_in'], tiles['bb'], tiles['tl'])

    lp = -(-l // CHUNK) * CHUNK
    proj_r = proj if lp == l else jnp.pad(proj, ((0, 0), (0, lp - l), (0, 0)))
    tl_r = min(tiles['tl_rec'], lp)
    ya, gdn_new = _gdn(proj_r, _pad_rows(conv_buf, QKV_A), gdn_s.astype(F32), p['conv_w'], p['alog_row'],
                       p['dt_row'], p['gdn_norm_w'], tl_r, l)
    yb, rwkv_pairs = _rwkv(proj_r, _pad_rows(shift_buf, RW_BLOCK), _to_pairs(rwkv_s), p['rwkv_vecs'],
                           p['w2p'], p['a2p'], p['g2'], tl_r, l)
    if lp != l:
        ya, yb = ya[:, :l], yb[:, :l]

    x1, h2, gates_t = _merge(ya, yb, proj, x, ga1, sc2, sh2, p['p_a'], p['p_b'], p['w_o'], p['ln1_g'], p['ln1_b'],
                             p['router_wt'], p['router_bias'], tiles['bb'], tiles['tl_merge'], dn_alpha)
    out = _moe(h2, gates_t.T, x1, ga2, p['we_gu'], p['we_down'], p['ws_gu'], p['ws_down'], p['ln2_g'], p['ln2_b'],
               tiles['bb'], tiles['tl'], dn_alpha)

    pre = jnp.concatenate([conv_buf.astype(F32), proj[:, :, P_QKV:P_QKV + QKV_A]], axis=1)
    conv_new = pre[:, -(CONV_W - 1):]
    shift_new = proj[:, l - 1:l, P_RW:P_RW + SHIFT_W]
    return (out, conv_new.astype(conv_buf.dtype), gdn_new.astype(gdn_s.dtype),
            shift_new.astype(shift_buf.dtype), _from_pairs(rwkv_pairs).astype(rwkv_s.dtype))


def _prep_params(l, w_in, conv_w, a_log, dt_bias, gdn_norm_w, mu_shift, w0, w2, a0, a2, g2, k_k, k_a, r_k,
                 lnx_g, lnx_b, p_a, p_b, w_o, ln1_g, ln1_b, router_w, router_bias, we_gate, we_up, we_down,
                 ws_gate, ws_up, ws_down, ln2_g, ln2_b):
    d = D_MODEL
    w = w_in[l]
    w_p = jnp.concatenate(
        [w[:, :QKV_A], w[:, OFF_Z:OFF_RWKV], w[:, OFF_RWKV:OFF_MERGE],
         jnp.zeros((d, RW_BLOCK - SHIFT_W), w.dtype), w[:, OFF_MERGE:], w[:, OFF_ALPHA:OFF_Z],
         jnp.zeros((d, AB_BLOCK - 2 * H_A), w.dtype)], axis=1).astype(BF16)
    row = lambda v, width: jnp.pad(v.astype(F32).reshape(1, -1), ((0, 0), (0, width - v.size)))
    zeros_lora = jnp.zeros((W_LORA, D_B), F32)
    return {
        'w_in': w_p,
        'conv_w': conv_w[l].astype(F32),
        'alog_row': row(a_log[l], LANES),
        'dt_row': row(dt_bias[l], LANES),
        'gdn_norm_w': row(gdn_norm_w[l], DK_A),
        'rwkv_vecs': (row(mu_shift[l], RW_BLOCK), row(w0[l], D_B), row(a0[l], D_B), row(k_k[l], D_B),
                      row(k_a[l], D_B), row(r_k[l], D_B), row(lnx_g[l], D_B), row(lnx_b[l], D_B)),
        'w2p': jnp.concatenate([w2[l].astype(F32), zeros_lora], axis=0),
        'a2p': jnp.concatenate([zeros_lora, a2[l].astype(F32)], axis=0),
        'g2': g2[l].astype(F32),
        'p_a': p_a[l].astype(BF16), 'p_b': p_b[l].astype(BF16), 'w_o': w_o[l].astype(BF16),
        'ln1_g': row(ln1_g[l], d), 'ln1_b': row(ln1_b[l], d),
        'router_wt': router_w[l].T.astype(BF16),
        'router_bias': router_bias[l].astype(F32).reshape(N_EXPERTS, 1),
        'we_gu': jnp.concatenate([we_gate[l], we_up[l]], axis=-1).astype(BF16),
        'we_down': we_down[l].astype(BF16),
        'ws_gu': jnp.concatenate([ws_gate[l], ws_up[l]], axis=-1).astype(BF16),
        'ws_down': ws_down[l].astype(BF16),
        'ln2_g': row(ln2_g[l], d), 'ln2_b': row(ln2_b[l], d),
    }


def kernel(x_prompt, x_sample, c_prompt, c_sample, state_gdn_conv, state_gdn, state_rwkv_shift, state_rwkv, w_ada, b_ada, w_in, conv_w, a_log, dt_bias, gdn_norm_w, mu_shift, w0, w2, a0, a2, g2, k_k, k_a, r_k, lnx_g, lnx_b, p_a, p_b, w_o, ln1_g, ln1_b, router_w, router_bias, we_gate, we_up, we_down, ws_gate, ws_up, ws_down, ln2_g, ln2_b):
    depth = w_ada.shape[0]
    dn_alpha = (2 * depth) ** 0.25
    bp, lp_, d = x_prompt.shape
    bs, ls, _ = x_sample.shape
    dtp = x_prompt.dtype
    tiles_p = {'bb': 1, 'tl': 1024, 'tl_rec': 512, 'tl_merge': 512}
    tiles_s = {'bb': bs, 'tl': ls, 'tl_rec': CHUNK, 'tl_merge': ls}

    yp, ys = x_prompt, x_sample
    new_p = ([], [], [], [])
    new_s = ([], [], [], [])
    for l in range(depth):
        p = _prep_params(l, w_in, conv_w, a_log, dt_bias, gdn_norm_w, mu_shift, w0, w2, a0, a2, g2, k_k, k_a,
                         r_k, lnx_g, lnx_b, p_a, p_b, w_o, ln1_g, ln1_b, router_w, router_bias, we_gate, we_up,
                         we_down, ws_gate, ws_up, ws_down, ln2_g, ln2_b)
        mod = _ada(jnp.concatenate([c_prompt, c_sample], axis=0), w_ada[l], b_ada[l])
        mod_p = tuple(m[:, None, :] for m in jnp.split(mod[:bp], 6, axis=-1))
        mod_s = tuple(m[:, None, :] for m in jnp.split(mod[bp:], 6, axis=-1))
        yp, *sp = _layer(yp, mod_p,
                         jnp.zeros((bp, CONV_W - 1, QKV_A), dtp), jnp.zeros((bp, H_A, DK_A, DK_A), dtp),
                         jnp.zeros((bp, 1, SHIFT_W), dtp), jnp.zeros((bp, H_B, N_B, N_B), dtp),
                         p, tiles_p, dn_alpha)
        ys, *ss = _layer(ys, mod_s, state_gdn_conv[l], state_gdn[l], state_rwkv_shift[l], state_rwkv[l],
                         p, tiles_s, dn_alpha)
        for lst, val in zip(new_p, sp):
            lst.append(val)
        for lst, val in zip(new_s, ss):
            lst.append(val)
    conv_p, gdn_p, shift_p, rwkv_p = [jnp.stack(t, 0) for t in new_p]
    conv_s, gdn_s, shift_s, rwkv_s = [jnp.stack(t, 0) for t in new_s]
    return (yp, ys, conv_p, gdn_p, shift_p, rwkv_p, conv_s, gdn_s, shift_s, rwkv_s)
```

```python
import functools

import jax
import jax.numpy as jnp
from jax import lax
from jax.experimental import pallas as pl
from jax.experimental.pallas import tpu as pltpu

F32 = jnp.float32
BF16 = jnp.bfloat16

D_MODEL = 1024
DK_A = 128
H_A = 4
QK_A = H_A * DK_A
V_A = H_A * DK_A
QKV_A = 2 * QK_A + V_A
CONV_W = 4
N_B = 64
H_B = 8
D_B = H_B * N_B
W_LORA = 64
A_LORA = 64
G_LORA = 128
SHIFT_W = 3 * D_B + W_LORA + A_LORA + G_LORA
OFF_ALPHA = QKV_A
OFF_BETA = OFF_ALPHA + H_A
OFF_Z = OFF_BETA + H_A
OFF_RWKV = OFF_Z + V_A
OFF_MERGE = OFF_RWKV + SHIFT_W
N_EXPERTS = 64
TOP_K = 8
N_GROUPS = 8
GROUP_SIZE = N_EXPERTS // N_GROUPS
TOPK_GROUPS = 4
D_EXPERT = 256
D_SHARED = 256
ROUTED_SCALE = 2.5
LN_EPS = 1e-5
GN_EPS = 64e-5
RMS_EPS = 1e-6

SUBLANES = 8
LANES = 128

P_QKV = 0
P_Z = QKV_A
P_RW = 2048
RW_BLOCK = 2048
P_MG = P_RW + RW_BLOCK
P_AB = P_MG + 2 * D_MODEL
AB_BLOCK = LANES
N_PROJ = P_AB + AB_BLOCK
PROJ_TN = 896

CHUNK = 128
BASE_BLOCK = 8
CUMSUM_PARTS = 3
STAT_PARTS = 2


def _cparams(sem, vmem_mb):
    return pltpu.CompilerParams(dimension_semantics=sem, vmem_limit_bytes=vmem_mb * 1024 * 1024)


def _dot(a, b):
    return jnp.dot(a.astype(BF16), b.astype(BF16), preferred_element_type=F32)


def _rdot(a, b, dims=((1,), (0,))):
    return lax.dot_general(a.astype(BF16), b.astype(BF16), (dims, ((), ())), preferred_element_type=F32)


_NT = ((1,), (1,))
_TN = ((0,), (0,))


def _bf16_parts(x, parts):
    out = []
    rem = x
    for _ in range(parts):
        hi = rem.astype(BF16)
        out.append(hi)
        rem = rem - hi.astype(F32)
    return out


def _mask_dot_left(mask, x, parts):
    return sum(jnp.dot(mask, p, preferred_element_type=F32) for p in _bf16_parts(x, parts))


def _mask_dot_right(x, mask, parts):
    return sum(jnp.dot(p, mask, preferred_element_type=F32) for p in _bf16_parts(x, parts))


def _lane_pick(x, lane_iota, lane):
    return jnp.sum(jnp.where(lane_iota == lane, x, 0.0), axis=-1, keepdims=True)


def _silu(x):
    return x * jax.nn.sigmoid(x)


def _softplus(x):
    return jnp.maximum(x, 0.0) + jnp.log1p(jnp.exp(-jnp.abs(x)))


def _tri_inverse(mats, eye, sign, row, col):
    base = (row // BASE_BLOCK) == (col // BASE_BLOCK)
    ds = [jnp.where(base, a, 0.0) for a in mats]
    xs = [eye + d if sign > 0 else eye - d for d in ds]
    power = 2
    while power < BASE_BLOCK:
        ds = [_rdot(d, d) for d in ds]
        xs = [x + _rdot(x, d) for x, d in zip(xs, ds)]
        power *= 2
    b = BASE_BLOCK
    while b < CHUNK:
        sibling = ((row // b) == (col // b) + 1) & ((row // (2 * b)) == (col // (2 * b)))
        offs = [jnp.where(sibling, a, 0.0) for a in mats]
        ts = [_rdot(o, x) for o, x in zip(offs, xs)]
        if sign > 0:
            xs = [x + _rdot(x, t_) for x, t_ in zip(xs, ts)]
        else:
            xs = [x - _rdot(x, t_) for x, t_ in zip(xs, ts)]
        b *= 2
    return xs


def _tri_consts():
    row = lax.broadcasted_iota(jnp.int32, (CHUNK, CHUNK), 0)
    col = lax.broadcasted_iota(jnp.int32, (CHUNK, CHUNK), 1)
    incl = row >= col
    strict = row > col
    eye = jnp.where(row == col, 1.0, 0.0).astype(F32)
    tril = jnp.where(incl, 1.0, 0.0).astype(BF16)
    return row, col, incl, strict, eye, tril


def _ada_kernel(c_ref, w_ref, b_ref, o_ref):
    o_ref[...] = _dot(_silu(c_ref[...]), w_ref[...]) + b_ref[...]


def _ada(c, w_ada, b_ada):
    n, d = c.shape
    nout = w_ada.shape[1]
    tn = 768
    return pl.pallas_call(
        _ada_kernel,
        grid=(nout // tn,),
        in_specs=[pl.BlockSpec((n, d), lambda j: (0, 0)),
                  pl.BlockSpec((d, tn), lambda j: (0, j)),
                  pl.BlockSpec((1, tn), lambda j: (0, j))],
        out_specs=pl.BlockSpec((n, tn), lambda j: (0, j)),
        out_shape=jax.ShapeDtypeStruct((n, nout), F32),
        compiler_params=_cparams(("parallel",), 32),
        name="ada",
    )(c, w_ada, b_ada.reshape(1, nout))


def _inproj_kernel(x_ref, sc_ref, sh_ref, w_ref, o_ref, hs_ref):
    @pl.when(pl.program_id(2) == 0)
    def _():
        h = x_ref[...] * (1.0 + sc_ref[...]) + sh_ref[...]
        hs_ref[...] = h.reshape(hs_ref.shape).astype(BF16)

    o = jnp.dot(hs_ref[...], w_ref[...], preferred_element_type=F32)
    o_ref[...] = o.reshape(o_ref.shape)


def _inproj(x, sc, sh, w_p, bb, tl):
    b, l, d = x.shape
    return pl.pallas_call(
        _inproj_kernel,
        grid=(b // bb, l // tl, N_PROJ // PROJ_TN),
        in_specs=[pl.BlockSpec((bb, tl, d), lambda i, j, n: (i, j, 0)),
                  pl.BlockSpec((bb, 1, d), lambda i, j, n: (i, 0, 0)),
                  pl.BlockSpec((bb, 1, d), lambda i, j, n: (i, 0, 0)),
                  pl.BlockSpec((d, PROJ_TN), lambda i, j, n: (0, n))],
        out_specs=pl.BlockSpec((bb, tl, PROJ_TN), lambda i, j, n: (i, j, n)),
        out_shape=jax.ShapeDtypeStruct((b, l, N_PROJ), F32),
        scratch_shapes=[pltpu.VMEM((bb * tl, d), BF16)],
        compiler_params=_cparams(("parallel", "parallel", "arbitrary"), 48),
        name="inproj",
    )(x, sc, sh, w_p)


def _gdn_kernel(qkv_ref, ab_ref, z_ref, cbuf_ref, s0_ref, cw_ref, alog_ref, dt_ref, nw_ref,
                ya_ref, sout_ref, ext_ref, qc_ref, s_ref, *, tl, l_valid, l_total):
    t = pl.program_id(1)

    @pl.when(t == 0)
    def _():
        ext_ref[0:SUBLANES, :] = cbuf_ref[0]
        s_ref[...] = s0_ref[0]

    ext_ref[SUBLANES:SUBLANES + tl, :] = qkv_ref[0]
    first = SUBLANES - (CONV_W - 1)
    acc = cw_ref[0:1, :] * ext_ref[first:first + tl, :]
    for j in range(1, CONV_W):
        acc = acc + cw_ref[j:j + 1, :] * ext_ref[first + j:first + j + tl, :]
    ext_ref[0:SUBLANES, :] = ext_ref[tl:tl + SUBLANES, :]
    qc_ref[...] = _silu(acc)

    row, col, incl, strict, eye, tril = _tri_consts()
    neg_a = -jnp.exp(alog_ref[...])
    dt = dt_ref[...]
    nw = nw_ref[...]

    def chunk(c, carry):
        r0 = pl.multiple_of(c * CHUNK, CHUNK)
        rows = pl.ds(r0, CHUNK)
        ab = ab_ref[0, rows, :]
        g_all = neg_a * _softplus(ab + dt)
        b_all = jax.nn.sigmoid(ab)
        if l_valid < l_total:
            valid = (t * tl + r0 + row) < l_valid
            g_all = jnp.where(valid, g_all, 0.0)
            b_all = jnp.where(valid, b_all, 0.0)
        g_cum = _mask_dot_left(tril, g_all, CUMSUM_PARTS)
        heads = range(H_A)
        g_col = [_lane_pick(g_cum, col, h) for h in heads]
        beta = [_lane_pick(b_all, col, H_A + h) for h in heads]
        decay = []
        for h in heads:
            g_b = jnp.broadcast_to(g_col[h], (CHUNK, CHUNK))
            decay.append(jnp.where(incl, jnp.exp(jnp.where(incl, g_b - g_b.T, 0.0)), 0.0))
        q = [qc_ref[rows, h * DK_A:(h + 1) * DK_A] for h in heads]
        k = [qc_ref[rows, QK_A + h * DK_A:QK_A + (h + 1) * DK_A] for h in heads]
        v = [qc_ref[rows, 2 * QK_A + h * DK_A:2 * QK_A + (h + 1) * DK_A] for h in heads]
        q = [x * lax.rsqrt(jnp.sum(x * x, -1, keepdims=True) + 1e-6) * (DK_A ** -0.5) for x in q]
        k = [x * lax.rsqrt(jnp.sum(x * x, -1, keepdims=True) + 1e-6) for x in k]
        kq = [_rdot(jnp.concatenate([k[h], q[h]], axis=0), k[h], _NT) for h in heads]
        a = [jnp.where(strict, beta[h] * kq[h][:CHUNK] * decay[h], 0.0) for h in heads]
        qk = [jnp.where(incl, kq[h][CHUNK:] * decay[h], 0.0) for h in heads]
        x = _tri_inverse(a, eye, -1, row, col)
        e_g = [jnp.exp(g) for g in g_col]
        uw = [_rdot(x[h], jnp.concatenate([v[h] * beta[h], k[h] * (beta[h] * e_g[h])], axis=1)) for h in heads]
        g_last = [g[CHUNK - 1:CHUNK, :] for g in g_col]
        kd = [k[h] * jnp.exp(g_last[h] - g_col[h]) for h in heads]
        s = [s_ref[h] for h in heads]
        ws = [_rdot(jnp.concatenate([uw[h][:, DK_A:], q[h] * e_g[h]], axis=0), s[h]) for h in heads]
        v_new = [uw[h][:, :DK_A] - ws[h][:CHUNK] for h in heads]
        o = [ws[h][CHUNK:] + _rdot(qk[h], v_new[h]) for h in heads]
        s_new = [s[h] * jnp.exp(g_last[h]) + _rdot(kd[h], v_new[h], _TN) for h in heads]
        for h in heads:
            s_ref[h] = s_new[h]
            z = z_ref[0, rows, h * DK_A:(h + 1) * DK_A]
            on = o[h] * lax.rsqrt(jnp.mean(o[h] * o[h], -1, keepdims=True) + RMS_EPS) * nw * _silu(z)
            ya_ref[0, rows, h * DK_A:(h + 1) * DK_A] = on.astype(BF16)
        return carry

    lax.fori_loop(0, tl // CHUNK, chunk, 0)

    @pl.when(t == pl.num_programs(1) - 1)
    def _():
        sout_ref[0] = s_ref[...]


def _gdn(proj, conv_buf8, s0, conv_w, alog_row, dt_row, norm_w, tl, l_valid):
    b, l, _ = proj.shape
    kern = functools.partial(_gdn_kernel, tl=tl, l_valid=l_valid, l_total=l)
    full = lambda *shape: pl.BlockSpec(shape, lambda i, t: (0,) * len(shape))
    return pl.pallas_call(
        kern,
        grid=(b, l // tl),
        in_specs=[pl.BlockSpec((1, tl, QKV_A), lambda i, t: (i, t, P_QKV // QKV_A)),
                  pl.BlockSpec((1, tl, AB_BLOCK), lambda i, t: (i, t, P_AB // AB_BLOCK)),
                  pl.BlockSpec((1, tl, V_A), lambda i, t: (i, t, P_Z // V_A)),
                  pl.BlockSpec((1, SUBLANES, QKV_A), lambda i, t: (i, 0, 0)),
                  pl.BlockSpec((1, H_A, DK_A, DK_A), lambda i, t: (i, 0, 0, 0)),
                  full(CONV_W, QKV_A), full(1, LANES), full(1, LANES), full(1, DK_A)],
        out_specs=[pl.BlockSpec((1, tl, V_A), lambda i, t: (i, t, 0)),
                   pl.BlockSpec((1, H_A, DK_A, DK_A), lambda i, t: (i, 0, 0, 0))],
        out_shape=[jax.ShapeDtypeStruct((b, l, V_A), BF16),
                   jax.ShapeDtypeStruct((b, H_A, DK_A, DK_A), F32)],
        scratch_shapes=[pltpu.VMEM((tl + SUBLANES, QKV_A), F32),
                        pltpu.VMEM((tl, QKV_A), F32),
                        pltpu.VMEM((H_A, DK_A, DK_A), F32)],
        compiler_params=_cparams(("parallel", "arbitrary"), 48),
        name="gdn",
    )(proj, proj, proj, conv_buf8, s0, conv_w, alog_row, dt_row, norm_w)


PAIRS = H_B // 2


def _headsum(x, bd):
    return jnp.concatenate(
        [_mask_dot_right(x[:, p * LANES:(p + 1) * LANES], bd, STAT_PARTS) for p in range(PAIRS)], axis=1)


def _rwkv_kernel(rw_ref, sbuf_ref, s0_ref, mu_ref, w0_ref, a0_ref, kk_ref, ka_ref, rk_ref, lg_ref, lb_ref,
                 w2_ref, a2_ref, g2_ref, yb_ref, sout_ref,
                 ext_ref, r_s, k_s, v_s, z_s, p_s, lw_s, y_s, s_ref, *, tl, l_valid, l_total):
    t = pl.program_id(1)

    @pl.when(t == 0)
    def _():
        ext_ref[0:SUBLANES, :] = sbuf_ref[0]
        s_ref[...] = s0_ref[0]

    cur = rw_ref[0]
    ext_ref[SUBLANES:SUBLANES + tl, :] = cur
    prev = ext_ref[SUBLANES - 1:SUBLANES - 1 + tl, :]
    mixed = cur + (prev - cur) * mu_ref[...]
    ext_ref[0:SUBLANES, :] = ext_ref[tl:tl + SUBLANES, :]

    row, col, incl, strict, eye, tril = _tri_consts()
    same_head = (row // N_B) == (col // N_B)
    bd = jnp.where(same_head, 1.0, 0.0).astype(BF16)
    lane_a = col < N_B

    r = mixed[:, 0:D_B]
    k = mixed[:, D_B:2 * D_B]
    v = mixed[:, 2 * D_B:3 * D_B]
    lora = mixed[:, 3 * D_B:3 * D_B + W_LORA + A_LORA]
    g_in = mixed[:, 3 * D_B + W_LORA + A_LORA:SHIFT_W]
    w = -_softplus(-(w0_ref[...] + _dot(jnp.tanh(lora), w2_ref[...]))) - 0.5
    lw = -jnp.exp(w)
    a = jax.nn.sigmoid(a0_ref[...] + _dot(lora, a2_ref[...]))
    gate = _dot(jax.nn.sigmoid(g_in), g2_ref[...])
    kkr = k * kk_ref[...]
    kk = kkr * lax.rsqrt(_headsum(kkr * kkr, bd) + 1e-6)
    k = k * (1.0 + (a - 1.0) * ka_ref[...])
    bonus = _headsum(r * k * rk_ref[...], bd) * v
    if l_valid < l_total:
        rvalid = (t * tl + lax.broadcasted_iota(jnp.int32, (tl, 1), 0)) < l_valid
        lw = jnp.where(rvalid, lw, 0.0)
        kk = jnp.where(rvalid, kk, 0.0)
        k = jnp.where(rvalid, k, 0.0)
        v = jnp.where(rvalid, v, 0.0)
    r_s[...] = r
    k_s[...] = k
    v_s[...] = v
    z_s[...] = -kk
    p_s[...] = kk * a
    lw_s[...] = lw

    def chunk(c, carry):
        r0 = pl.multiple_of(c * CHUNK, CHUNK)
        rows = pl.ds(r0, CHUNK)
        pairs = range(PAIRS)
        both = range(2 * PAIRS)
        cols = [slice(p * LANES, (p + 1) * LANES) for p in pairs]
        lw_c = [lw_s[rows, c_] for c_ in cols]
        g_inc = [_mask_dot_left(tril, x, CUMSUM_PARTS) for x in lw_c]
        g_exc = [g_inc[p] - lw_c[p] for p in pairs]
        g_mid = [g[CHUNK // 2 - 1:CHUNK // 2, :] for g in g_inc]
        g_end = [g[CHUNK - 1:CHUNK, :] for g in g_inc]
        z = [z_s[rows, c_] for c_ in cols]
        rr = [r_s[rows, c_] for c_ in cols]
        pp = [p_s[rows, c_] for c_ in cols]
        kk_ = [k_s[rows, c_] for c_ in cols]
        vv = [v_s[rows, c_] for c_ in cols]
        zt = [z[p] * jnp.exp(g_exc[p] - g_mid[p]) for p in pairs]
        rt = [rr[p] * jnp.exp(g_inc[p] - g_mid[p]) for p in pairs]
        en = [jnp.exp(g_mid[p] - g_inc[p]) for p in pairs]
        s = [s_ref[p] for p in pairs]
        lhs = [jnp.concatenate([jnp.where(lane_a, zt[p], 0.0), jnp.where(lane_a, 0.0, zt[p]),
                                jnp.where(lane_a, rt[p], 0.0), jnp.where(lane_a, 0.0, rt[p])], axis=0) for p in pairs]
        m = [_rdot(lhs[p], jnp.concatenate([pp[p] * en[p], kk_[p] * en[p]], axis=0), _NT) for p in pairs]
        zr0 = [_rdot(jnp.concatenate([z[p] * jnp.exp(g_exc[p]), rr[p] * jnp.exp(g_inc[p])], axis=0), s[p], _NT)
               for p in pairs]
        mz = [m[i // 2][(i % 2) * CHUNK:(i % 2 + 1) * CHUNK] for i in both]
        azp = [jnp.where(strict, x[:, :CHUNK], 0.0) for x in mz]
        azk = [jnp.where(strict, x[:, CHUNK:], 0.0) for x in mz]
        minv = _tri_inverse(azp, eye, 1, row, col)
        rhs = [zr0[i // 2][:CHUNK] + _rdot(azk[i], vv[i // 2]) for i in both]
        u_h = [_rdot(minv[i], rhs[i]) for i in both]
        u = [jnp.where(lane_a, u_h[2 * p], u_h[2 * p + 1]) for p in pairs]
        uv = [jnp.concatenate([u[p], vv[p]], axis=0) for p in pairs]
        incl2 = jnp.concatenate([incl, incl], axis=1)
        y_h = [_rdot(jnp.where(incl2, m[i // 2][(2 + i % 2) * CHUNK:(3 + i % 2) * CHUNK], 0.0), uv[i // 2])
               for i in both]
        tail = [jnp.exp(g_end[p] - g_inc[p]) for p in pairs]
        s_new = [s[p] * jnp.exp(g_end[p])
                 + _rdot(uv[p], jnp.concatenate([pp[p] * tail[p], kk_[p] * tail[p]], axis=0), _TN) for p in pairs]
        for p in pairs:
            s_ref[p] = jnp.where(same_head, s_new[p], 0.0)
            y_s[rows, cols[p]] = zr0[p][CHUNK:] + jnp.where(lane_a, y_h[2 * p], y_h[2 * p + 1])
        return carry

    lax.fori_loop(0, tl // CHUNK, chunk, 0)

    y = y_s[...]
    mean = _headsum(y, bd) * (1.0 / N_B)
    dev = y - mean
    var = _headsum(dev * dev, bd) * (1.0 / N_B)
    yn = dev * lax.rsqrt(var + GN_EPS) * lg_ref[...] + lb_ref[...]
    yb_ref[0] = ((yn + bonus) * gate).astype(BF16)

    @pl.when(t == pl.num_programs(1) - 1)
    def _():
        sout_ref[0] = s_ref[...]


def _rwkv(proj, shift_buf8, s0_pairs, vecs, w2p, a2p, g2, tl, l_valid):
    b, l, _ = proj.shape
    kern = functools.partial(_rwkv_kernel, tl=tl, l_valid=l_valid, l_total=l)
    full = lambda *shape: pl.BlockSpec(shape, lambda i, t: (0,) * len(shape))
    mu, w0, a0, k_k, k_a, r_k, lnx_g, lnx_b = vecs
    return pl.pallas_call(
        kern,
        grid=(b, l // tl),
        in_specs=[pl.BlockSpec((1, tl, RW_BLOCK), lambda i, t: (i, t, P_RW // RW_BLOCK)),
                  pl.BlockSpec((1, SUBLANES, RW_BLOCK), lambda i, t: (i, 0, 0)),
                  pl.BlockSpec((1, PAIRS, LANES, LANES), lambda i, t: (i, 0, 0, 0)),
                  full(1, RW_BLOCK)] + [full(1, D_B)] * 7 +
                 [full(W_LORA + A_LORA, D_B), full(W_LORA + A_LORA, D_B), full(G_LORA, D_B)],
        out_specs=[pl.BlockSpec((1, tl, D_B), lambda i, t: (i, t, 0)),
                   pl.BlockSpec((1, PAIRS, LANES, LANES), lambda i, t: (i, 0, 0, 0))],
        out_shape=[jax.ShapeDtypeStruct((b, l, D_B), BF16),
                   jax.ShapeDtypeStruct((b, PAIRS, LANES, LANES), F32)],
        scratch_shapes=[pltpu.VMEM((tl + SUBLANES, RW_BLOCK), F32)] +
                       [pltpu.VMEM((tl, D_B), F32)] * 7 +
                       [pltpu.VMEM((PAIRS, LANES, LANES), F32)],
        compiler_params=_cparams(("parallel", "arbitrary"), 48),
        name="rwkv",
    )(proj, shift_buf8, s0_pairs, mu, w0, a0, k_k, k_a, r_k, lnx_g, lnx_b, w2p, a2p, g2)


def _layernorm(y, g, b):
    mu = jnp.mean(y, -1, keepdims=True)
    dev = y - mu
    var = jnp.mean(dev * dev, -1, keepdims=True)
    return dev * lax.rsqrt(var + LN_EPS) * g + b


def _route(logits_t, bias):
    tm = logits_t.shape[1]
    scores = jax.nn.sigmoid(logits_t)
    choice = scores + bias
    neg_inf = -jnp.inf
    iota_g = lax.broadcasted_iota(jnp.int32, (GROUP_SIZE, tm), 0)
    group_score = []
    for g in range(N_GROUPS):
        xg = choice[g * GROUP_SIZE:(g + 1) * GROUP_SIZE, :]
        m1 = jnp.max(xg, axis=0, keepdims=True)
        first = jnp.min(jnp.where(xg == m1, iota_g, GROUP_SIZE), axis=0, keepdims=True)
        m2 = jnp.max(jnp.where(iota_g == first, neg_inf, xg), axis=0, keepdims=True)
        group_score.append(m1 + m2)
    masked = []
    for g in range(N_GROUPS):
        rank = jnp.zeros((1, tm), jnp.int32)
        for o in range(N_GROUPS):
            if o == g:
                continue
            ahead = group_score[o] > group_score[g]
            if o < g:
                ahead = ahead | (group_score[o] == group_score[g])
            rank = rank + ahead.astype(jnp.int32)
        keep = rank < TOPK_GROUPS
        masked.append(jnp.where(keep, choice[g * GROUP_SIZE:(g + 1) * GROUP_SIZE, :], neg_inf))
    cur = jnp.concatenate(masked, axis=0)
    iota_e = lax.broadcasted_iota(jnp.int32, (N_EXPERTS, tm), 0)
    sel = jnp.zeros((N_EXPERTS, tm), F32)
    for _ in range(TOP_K):
        m = jnp.max(cur, axis=0, keepdims=True)
        first = jnp.min(jnp.where(cur == m, iota_e, N_EXPERTS), axis=0, keepdims=True)
        pick = iota_e == first
        sel = jnp.where(pick, 1.0, sel)
        cur = jnp.where(pick, neg_inf, cur)
    wsel = sel * scores
    denom = jnp.sum(wsel, axis=0, keepdims=True) + 1e-20
    return (ROUTED_SCALE * wsel) / denom


def _merge_kernel(ya_ref, yb_ref, mg_ref, x_ref, ga1_ref, sc2_ref, sh2_ref, pa_ref, pb_ref, wo_ref,
                  g1_ref, b1_ref, rwt_ref, rb_ref, x1_ref, h2_ref, gt_ref, *, dn_alpha):
    bb, tl, d = x_ref.shape
    tm = bb * tl
    ya = ya_ref[...].reshape(tm, V_A)
    yb = yb_ref[...].reshape(tm, D_B)
    mg = mg_ref[...].reshape(tm, 2 * d)
    merged = (jax.nn.sigmoid(mg[:, :d]) * jnp.dot(ya, pa_ref[...], preferred_element_type=F32)
              + jax.nn.sigmoid(mg[:, d:]) * jnp.dot(yb, pb_ref[...], preferred_element_type=F32))
    mix = _dot(merged, wo_ref[...])
    y = dn_alpha * x_ref[...] + (1.0 + ga1_ref[...]) * mix.reshape(bb, tl, d)
    x1 = _layernorm(y, g1_ref[...], b1_ref[...])
    x1_ref[...] = x1
    h2 = (x1 * (1.0 + sc2_ref[...]) + sh2_ref[...]).astype(BF16)
    h2_ref[...] = h2
    logits_t = lax.dot_general(rwt_ref[...], h2.reshape(tm, d), (_NT, ((), ())), preferred_element_type=F32)
    gt_ref[...] = _route(logits_t, rb_ref[...])


def _merge(ya, yb, proj, x, ga1, sc2, sh2, p_a, p_b, w_o, ln_g, ln_b, rw_t, r_bias, bb, tl, dn_alpha):
    b, l, d = x.shape
    nj = l // tl
    tm = bb * tl
    full = lambda *shape: pl.BlockSpec(shape, lambda i, j: (0,) * len(shape))
    mod = pl.BlockSpec((bb, 1, d), lambda i, j: (i, 0, 0))
    tok = lambda width, dtype=None: pl.BlockSpec((bb, tl, width), lambda i, j: (i, j, 0))
    return pl.pallas_call(
        functools.partial(_merge_kernel, dn_alpha=dn_alpha),
        grid=(b // bb, nj),
        in_specs=[tok(V_A), tok(D_B),
                  pl.BlockSpec((bb, tl, 2 * d), lambda i, j: (i, j, P_MG // (2 * d))),
                  tok(d), mod, mod, mod,
                  full(V_A, d), full(D_B, d), full(d, d), full(1, d), full(1, d),
                  full(N_EXPERTS, d), full(N_EXPERTS, 1)],
        out_specs=[tok(d), tok(d), pl.BlockSpec((N_EXPERTS, tm), lambda i, j: (0, i * nj + j))],
        out_shape=[jax.ShapeDtypeStruct((b, l, d), F32),
                   jax.ShapeDtypeStruct((b, l, d), BF16),
                   jax.ShapeDtypeStruct((N_EXPERTS, b * l), F32)],
        compiler_params=_cparams(("parallel", "parallel"), 48),
        name="merge",
    )(ya, yb, proj, x, ga1, sc2, sh2, p_a, p_b, w_o, ln_g, ln_b, rw_t, r_bias)


def _moe_kernel(h_ref, g_ref, x1_ref, ga2_ref, wgu_ref, wd_ref, sgu_ref, sd_ref, g2_ref, b2_ref,
                o_ref, acc_ref, *, dn_alpha):
    e = pl.program_id(2)
    bb, tl, d = h_ref.shape
    tm = bb * tl
    h = h_ref[...].reshape(tm, d)

    @pl.when(e == 0)
    def _():
        su = jnp.dot(h, sgu_ref[...], preferred_element_type=F32)
        act = _silu(su[:, :D_SHARED]) * su[:, D_SHARED:]
        acc_ref[...] = _dot(act, sd_ref[...])

    gu = jnp.dot(h, wgu_ref[0], preferred_element_type=F32)
    gate = _lane_pick(g_ref[...], lax.broadcasted_iota(jnp.int32, (tm, N_EXPERTS), 1), e)
    act = _silu(gu[:, :D_EXPERT]) * gu[:, D_EXPERT:] * gate
    acc_ref[...] += _dot(act, wd_ref[0])

    @pl.when(e == pl.num_programs(2) - 1)
    def _():
        y = dn_alpha * x1_ref[...] + (1.0 + ga2_ref[...]) * acc_ref[...].reshape(bb, tl, d)
        o_ref[...] = _layernorm(y, g2_ref[...], b2_ref[...])


def _moe(h2, gates, x1, ga2, we_gu, we_down, ws_gu, ws_down, ln_g, ln_b, bb, tl, dn_alpha):
    b, l, d = x1.shape
    nj = l // tl
    tm = bb * tl
    full = lambda *shape: pl.BlockSpec(shape, lambda i, j, e: (0,) * len(shape))
    tok = pl.BlockSpec((bb, tl, d), lambda i, j, e: (i, j, 0))
    return pl.pallas_call(
        functools.partial(_moe_kernel, dn_alpha=dn_alpha),
        grid=(b // bb, nj, N_EXPERTS),
        in_specs=[tok,
                  pl.BlockSpec((tm, N_EXPERTS), lambda i, j, e: (i * nj + j, 0)),
                  tok,
                  pl.BlockSpec((bb, 1, d), lambda i, j, e: (i, 0, 0)),
                  pl.BlockSpec((1, d, 2 * D_EXPERT), lambda i, j, e: (e, 0, 0)),
                  pl.BlockSpec((1, D_EXPERT, d), lambda i, j, e: (e, 0, 0)),
                  full(d, 2 * D_SHARED), full(D_SHARED, d), full(1, d), full(1, d)],
        out_specs=tok,
        out_shape=jax.ShapeDtypeStruct((b, l, d), F32),
        scratch_shapes=[pltpu.VMEM((tm, d), F32)],
        compiler_params=_cparams(("parallel", "parallel", "arbitrary"), 56),
        name="moe",
    )(h2, gates, x1, ga2, we_gu, we_down, ws_gu, ws_down, ln_g, ln_b)


def _pad_rows(buf, width):
    b, n, w = buf.shape
    return jnp.pad(buf.astype(F32), ((0, 0), (SUBLANES - n, 0), (0, width - w)))


def _to_pairs(s):
    b = s.shape[0]
    s = s.astype(F32).reshape(b, PAIRS, 2, N_B, N_B)
    zero = jnp.zeros_like(s[:, :, 0])
    top = jnp.concatenate([s[:, :, 0], zero], axis=-1)
    bot = jnp.concatenate([zero, s[:, :, 1]], axis=-1)
    return jnp.concatenate([top, bot], axis=-2)


def _from_pairs(sp):
    b = sp.shape[0]
    return jnp.stack([sp[:, :, :N_B, :N_B], sp[:, :, N_B:, N_B:]], axis=2).reshape(b, H_B, N_B, N_B)


def _layer(x, mod, conv_buf, gdn_s, shift_buf, rwkv_s, p, tiles, dn_alpha):
    b, l, d = x.shape
    sh1, sc1, ga1, sh2, sc2, ga2 = mod
    proj = _inproj(x, sc1, sh1, p['w_in'], tiles['bb'], tiles['tl'])

    lp = -(-l // CHUNK) * CHUNK
    proj_r = proj if lp == l else jnp.pad(proj, ((0, 0), (0, lp - l), (0, 0)))
    tl_r = min(tiles['tl_rec'], lp)
    ya, gdn_new = _gdn(proj_r, _pad_rows(conv_buf, QKV_A), gdn_s.astype(F32), p['conv_w'], p['alog_row'],
                       p['dt_row'], p['gdn_norm_w'], tl_r, l)
    yb, rwkv_pairs = _rwkv(proj_r, _pad_rows(shift_buf, RW_BLOCK), _to_pairs(rwkv_s), p['rwkv_vecs'],
                           p['w2p'], p['a2p'], p['g2'], tl_r, l)
    if lp != l:
        ya, yb = ya[:, :l], yb[:, :l]

    x1, h2, gates_t = _merge(ya, yb, proj, x, ga1, sc2, sh2, p['p_a'], p['p_b'], p['w_o'], p['ln1_g'], p['ln1_b'],
                             p['router_wt'], p['router_bias'], tiles['bb'], tiles['tl_merge'], dn_alpha)
    out = _moe(h2, gates_t.T, x1, ga2, p['we_gu'], p['we_down'], p['ws_gu'], p['ws_down'], p['ln2_g'], p['ln2_b'],
               tiles['bb'], tiles['tl'], dn_alpha)

    pre = jnp.concatenate([conv_buf.astype(F32), proj[:, :, P_QKV:P_QKV + QKV_A]], axis=1)
    conv_new = pre[:, -(CONV_W - 1):]
    shift_new = proj[:, l - 1:l, P_RW:P_RW + SHIFT_W]
    return (out, conv_new.astype(conv_buf.dtype), gdn_new.astype(gdn_s.dtype),
            shift_new.astype(shift_buf.dtype), _from_pairs(rwkv_pairs).astype(rwkv_s.dtype))


def _prep_params(l, w_in, conv_w, a_log, dt_bias, gdn_norm_w, mu_shift, w0, w2, a0, a2, g2, k_k, k_a, r_k,
                 lnx_g, lnx_b, p_a, p_b, w_o, ln1_g, ln1_b, router_w, router_bias, we_gate, we_up, we_down,
                 ws_gate, ws_up, ws_down, ln2_g, ln2_b):
    d = D_MODEL
    w = w_in[l]
    w_p = jnp.concatenate(
        [w[:, :QKV_A], w[:, OFF_Z:OFF_RWKV], w[:, OFF_RWKV:OFF_MERGE],
         jnp.zeros((d, RW_BLOCK - SHIFT_W), w.dtype), w[:, OFF_MERGE:], w[:, OFF_ALPHA:OFF_Z],
         jnp.zeros((d, AB_BLOCK - 2 * H_A), w.dtype)], axis=1).astype(BF16)
    row = lambda v, width: jnp.pad(v.astype(F32).reshape(1, -1), ((0, 0), (0, width - v.size)))
    zeros_lora = jnp.zeros((W_LORA, D_B), F32)
    return {
        'w_in': w_p,
        'conv_w': conv_w[l].astype(F32),
        'alog_row': row(a_log[l], LANES),
        'dt_row': row(dt_bias[l], LANES),
        'gdn_norm_w': row(gdn_norm_w[l], DK_A),
        'rwkv_vecs': (row(mu_shift[l], RW_BLOCK), row(w0[l], D_B), row(a0[l], D_B), row(k_k[l], D_B),
                      row(k_a[l], D_B), row(r_k[l], D_B), row(lnx_g[l], D_B), row(lnx_b[l], D_B)),
        'w2p': jnp.concatenate([w2[l].astype(F32), zeros_lora], axis=0),
        'a2p': jnp.concatenate([zeros_lora, a2[l].astype(F32)], axis=0),
        'g2': g2[l].astype(F32),
        'p_a': p_a[l].astype(BF16), 'p_b': p_b[l].astype(BF16), 'w_o': w_o[l].astype(BF16),
        'ln1_g': row(ln1_g[l], d), 'ln1_b': row(ln1_b[l], d),
        'router_wt': router_w[l].T.astype(BF16),
        'router_bias': router_bias[l].astype(F32).reshape(N_EXPERTS, 1),
        'we_gu': jnp.concatenate([we_gate[l], we_up[l]], axis=-1).astype(BF16),
        'we_down': we_down[l].astype(BF16),
        'ws_gu': jnp.concatenate([ws_gate[l], ws_up[l]], axis=-1).astype(BF16),
        'ws_down': ws_down[l].astype(BF16),
        'ln2_g': row(ln2_g[l], d), 'ln2_b': row(ln2_b[l], d),
    }


def kernel(x_prompt, x_sample, c_prompt, c_sample, state_gdn_conv, state_gdn, state_rwkv_shift, state_rwkv, w_ada, b_ada, w_in, conv_w, a_log, dt_bias, gdn_norm_w, mu_shift, w0, w2, a0, a2, g2, k_k, k_a, r_k, lnx_g, lnx_b, p_a, p_b, w_o, ln1_g, ln1_b, router_w, router_bias, we_gate, we_up, we_down, ws_gate, ws_up, ws_down, ln2_g, ln2_b):
    depth = w_ada.shape[0]
    dn_alpha = (2 * depth) ** 0.25
    bp, lp_, d = x_prompt.shape
    bs, ls, _ = x_sample.shape
    dtp = x_prompt.dtype
    tiles_p = {'bb': 1, 'tl': 1024, 'tl_rec': 512, 'tl_merge': 512}
    tiles_s = {'bb': bs, 'tl': ls, 'tl_rec': CHUNK, 'tl_merge': ls}

    yp, ys = x_prompt, x_sample
    new_p = ([], [], [], [])
    new_s = ([], [], [], [])
    for l in range(depth):
        p = _prep_params(l, w_in, conv_w, a_log, dt_bias, gdn_norm_w, mu_shift, w0, w2, a0, a2, g2, k_k, k_a,
                         r_k, lnx_g, lnx_b, p_a, p_b, w_o, ln1_g, ln1_b, router_w, router_bias, we_gate, we_up,
                         we_down, ws_gate, ws_up, ws_down, ln2_g, ln2_b)
        mod = _ada(jnp.concatenate([c_prompt, c_sample], axis=0), w_ada[l], b_ada[l])
        mod_p = tuple(m[:, None, :] for m in jnp.split(mod[:bp], 6, axis=-1))
        mod_s = tuple(m[:, None, :] for m in jnp.split(mod[bp:], 6, axis=-1))
        yp, *sp = _layer(yp, mod_p,
                         jnp.zeros((bp, CONV_W - 1, QKV_A), dtp), jnp.zeros((bp, H_A, DK_A, DK_A), dtp),
                         jnp.zeros((bp, 1, SHIFT_W), dtp), jnp.zeros((bp, H_B, N_B, N_B), dtp),
                         p, tiles_p, dn_alpha)
        ys, *ss = _layer(ys, mod_s, state_gdn_conv[l], state_gdn[l], state_rwkv_shift[l], state_rwkv[l],
                         p, tiles_s, dn_alpha)
        for lst, val in zip(new_p, sp):
            lst.append(val)
        for lst, val in zip(new_s, ss):
            lst.append(val)
    conv_p, gdn_p, shift_p, rwkv_p = [jnp.stack(t, 0) for t in new_p]
    conv_s, gdn_s, shift_s, rwkv_s = [jnp.stack(t, 0) for t in new_s]
    return (yp, ys, conv_p, gdn_p, shift_p, rwkv_p, conv_s, gdn_s, shift_s, rwkv_s)
```

```python
import functools
import math

import jax
import jax.numpy as jnp
from jax import lax
from jax.experimental import pallas as pl
from jax.experimental.pallas import tpu as pltpu

F32 = jnp.float32
BF16 = jnp.bfloat16

D_MODEL = 1024
DK_A = 128
H_A = 4
QK_A = H_A * DK_A
V_A = H_A * DK_A
QKV_A = 2 * QK_A + V_A
CONV_W = 4
N_B = 64
H_B = 8
D_B = H_B * N_B
W_LORA = 64
A_LORA = 64
G_LORA = 128
SHIFT_W = 3 * D_B + W_LORA + A_LORA + G_LORA
OFF_ALPHA = QKV_A
OFF_BETA = OFF_ALPHA + H_A
OFF_Z = OFF_BETA + H_A
OFF_RWKV = OFF_Z + V_A
OFF_MERGE = OFF_RWKV + SHIFT_W
N_EXPERTS = 64
TOP_K = 8
N_GROUPS = 8
GROUP_SIZE = N_EXPERTS // N_GROUPS
TOPK_GROUPS = 4
D_EXPERT = 256
D_SHARED = 256
ROUTED_SCALE = 2.5
LN_EPS = 1e-5
GN_EPS = 64e-5
RMS_EPS = 1e-6
DECAY_SCALE = -math.exp(-0.5)

SUBLANES = 8
LANES = 128

P_QKV = 0
P_Z = QKV_A
P_RW = 2048
RW_BLOCK = 2048
P_MG = P_RW + RW_BLOCK
P_AB = P_MG + 2 * D_MODEL
AB_BLOCK = LANES
N_PROJ = P_AB + AB_BLOCK
PROJ_TN = 896

CHUNK = 128
BASE_BLOCK = 8
CUMSUM_PARTS = 3
STAT_PARTS = 2
EXPERTS_PER_STEP = 2


def _cparams(sem, vmem_mb):
    return pltpu.CompilerParams(dimension_semantics=sem, vmem_limit_bytes=vmem_mb * 1024 * 1024)


def _dot(a, b):
    return jnp.dot(a.astype(BF16), b.astype(BF16), preferred_element_type=F32)


def _rdot(a, b, dims=((1,), (0,))):
    return lax.dot_general(a.astype(BF16), b.astype(BF16), (dims, ((), ())), preferred_element_type=F32)


_NT = ((1,), (1,))
_TN = ((0,), (0,))


def _bf16_parts(x, parts):
    out = []
    rem = x
    for _ in range(parts):
        hi = rem.astype(BF16)
        out.append(hi)
        rem = rem - hi.astype(F32)
    return out


def _mask_dot_left(mask, x, parts):
    return sum(jnp.dot(mask, p, preferred_element_type=F32) for p in _bf16_parts(x, parts))


def _mask_dot_right(x, mask, parts):
    return sum(jnp.dot(p, mask, preferred_element_type=F32) for p in _bf16_parts(x, parts))


def _lane_pick(x, lane_iota, lane):
    return jnp.sum(jnp.where(lane_iota == lane, x, 0.0), axis=-1, keepdims=True)


def _silu(x):
    return x * jax.nn.sigmoid(x)


def _softplus(x):
    return jnp.maximum(x, 0.0) + jnp.log1p(jnp.exp(-jnp.abs(x)))


def _tri_inverse(mats, eye, sign, row, col):
    base = (row // BASE_BLOCK) == (col // BASE_BLOCK)
    ds = [jnp.where(base, a, 0.0) for a in mats]
    xs = [eye + d if sign > 0 else eye - d for d in ds]
    power = 2
    while power < BASE_BLOCK:
        ds = [_rdot(d, d) for d in ds]
        xs = [x + _rdot(x, d) for x, d in zip(xs, ds)]
        power *= 2
    b = BASE_BLOCK
    while b < CHUNK:
        sibling = ((row // b) == (col // b) + 1) & ((row // (2 * b)) == (col // (2 * b)))
        offs = [jnp.where(sibling, a, 0.0) for a in mats]
        ts = [_rdot(o, x) for o, x in zip(offs, xs)]
        if sign > 0:
            xs = [x + _rdot(x, t_) for x, t_ in zip(xs, ts)]
        else:
            xs = [x - _rdot(x, t_) for x, t_ in zip(xs, ts)]
        b *= 2
    return xs


def _tri_consts():
    row = lax.broadcasted_iota(jnp.int32, (CHUNK, CHUNK), 0)
    col = lax.broadcasted_iota(jnp.int32, (CHUNK, CHUNK), 1)
    incl = row >= col
    strict = row > col
    eye = jnp.where(row == col, 1.0, 0.0).astype(F32)
    tril = jnp.where(incl, 1.0, 0.0).astype(BF16)
    return row, col, incl, strict, eye, tril


def _ada_kernel(c_ref, w_ref, b_ref, o_ref):
    o_ref[...] = _dot(_silu(c_ref[...]), w_ref[...]) + b_ref[...]


def _ada(c, w_ada, b_ada):
    n, d = c.shape
    nout = w_ada.shape[1]
    tn = 768
    return pl.pallas_call(
        _ada_kernel,
        grid=(nout // tn,),
        in_specs=[pl.BlockSpec((n, d), lambda j: (0, 0)),
                  pl.BlockSpec((d, tn), lambda j: (0, j)),
                  pl.BlockSpec((1, tn), lambda j: (0, j))],
        out_specs=pl.BlockSpec((n, tn), lambda j: (0, j)),
        out_shape=jax.ShapeDtypeStruct((n, nout), F32),
        compiler_params=_cparams(("parallel",), 32),
        name="ada",
    )(c, w_ada, b_ada.reshape(1, nout))


def _inproj_kernel(x_ref, sc_ref, sh_ref, w_ref, o_ref, hs_ref):
    @pl.when(pl.program_id(2) == 0)
    def _():
        h = x_ref[...] * (1.0 + sc_ref[...]) + sh_ref[...]
        hs_ref[...] = h.reshape(hs_ref.shape).astype(BF16)

    o = jnp.dot(hs_ref[...], w_ref[...], preferred_element_type=F32)
    o_ref[...] = o.reshape(o_ref.shape)


def _inproj(x, sc, sh, w_p, bb, tl):
    b, l, d = x.shape
    return pl.pallas_call(
        _inproj_kernel,
        grid=(b // bb, l // tl, N_PROJ // PROJ_TN),
        in_specs=[pl.BlockSpec((bb, tl, d), lambda i, j, n: (i, j, 0)),
                  pl.BlockSpec((bb, 1, d), lambda i, j, n: (i, 0, 0)),
                  pl.BlockSpec((bb, 1, d), lambda i, j, n: (i, 0, 0)),
                  pl.BlockSpec((d, PROJ_TN), lambda i, j, n: (0, n))],
        out_specs=pl.BlockSpec((bb, tl, PROJ_TN), lambda i, j, n: (i, j, n)),
        out_shape=jax.ShapeDtypeStruct((b, l, N_PROJ), F32),
        scratch_shapes=[pltpu.VMEM((bb * tl, d), BF16)],
        compiler_params=_cparams(("parallel", "parallel", "arbitrary"), 48),
        name="inproj",
    )(x, sc, sh, w_p)


def _gdn_kernel(qkv_ref, ab_ref, z_ref, cbuf_ref, s0_ref, cw_ref, alog_ref, dt_ref, nw_ref,
                ya_ref, sout_ref, ext_ref, qc_ref, s_ref, *, bb, tl, l_valid, l_total):
    t = pl.program_id(1)

    @pl.when(t == 0)
    def _():
        ext_ref[:, 0:SUBLANES, :] = cbuf_ref[...]
        s_ref[...] = s0_ref[...]

    first = SUBLANES - (CONV_W - 1)
    for bi in range(bb):
        ext_ref[bi, SUBLANES:SUBLANES + tl, :] = qkv_ref[bi]
        acc = cw_ref[0:1, :] * ext_ref[bi, first:first + tl, :]
        for j in range(1, CONV_W):
            acc = acc + cw_ref[j:j + 1, :] * ext_ref[bi, first + j:first + j + tl, :]
        ext_ref[bi, 0:SUBLANES, :] = ext_ref[bi, tl:tl + SUBLANES, :]
        qc_ref[bi] = _silu(acc)

    row, col, incl, strict, eye, tril = _tri_consts()
    neg_a = -jnp.exp(alog_ref[...])
    dt = dt_ref[...]
    nw = nw_ref[...]
    chains = [(bi, h) for bi in range(bb) for h in range(H_A)]
    n = range(len(chains))

    def chunk(c, carry):
        r0 = pl.multiple_of(c * CHUNK, CHUNK)
        rows = pl.ds(r0, CHUNK)
        g_cum = []
        b_all = []
        for bi in range(bb):
            ab = ab_ref[bi, rows, :]
            g_bi = neg_a * _softplus(ab + dt)
            b_bi = jax.nn.sigmoid(ab)
            if l_valid < l_total:
                valid = (t * tl + r0 + row) < l_valid
                g_bi = jnp.where(valid, g_bi, 0.0)
                b_bi = jnp.where(valid, b_bi, 0.0)
            g_cum.append(_mask_dot_left(tril, g_bi, CUMSUM_PARTS))
            b_all.append(b_bi)
        g_col = [_lane_pick(g_cum[bi], col, h) for bi, h in chains]
        beta = [_lane_pick(b_all[bi], col, H_A + h) for bi, h in chains]
        decay = []
        for i in n:
            g_b = jnp.broadcast_to(g_col[i], (CHUNK, CHUNK))
            decay.append(jnp.where(incl, jnp.exp(jnp.where(incl, g_b - g_b.T, 0.0)), 0.0))
        q = [qc_ref[bi, rows, h * DK_A:(h + 1) * DK_A] for bi, h in chains]
        k = [qc_ref[bi, rows, QK_A + h * DK_A:QK_A + (h + 1) * DK_A] for bi, h in chains]
        v = [qc_ref[bi, rows, 2 * QK_A + h * DK_A:2 * QK_A + (h + 1) * DK_A] for bi, h in chains]
        q = [x * lax.rsqrt(jnp.sum(x * x, -1, keepdims=True) + 1e-6) * (DK_A ** -0.5) for x in q]
        k = [x * lax.rsqrt(jnp.sum(x * x, -1, keepdims=True) + 1e-6) for x in k]
        kq = [_rdot(jnp.concatenate([k[i], q[i]], axis=0), k[i], _NT) for i in n]
        a = [jnp.where(strict, beta[i] * kq[i][:CHUNK] * decay[i], 0.0) for i in n]
        qk = [jnp.where(incl, kq[i][CHUNK:] * decay[i], 0.0) for i in n]
        x = _tri_inverse(a, eye, -1, row, col)
        e_g = [jnp.exp(g) for g in g_col]
        uw = [_rdot(x[i], jnp.concatenate([v[i] * beta[i], k[i] * (beta[i] * e_g[i])], axis=1)) for i in n]
        g_last = [g[CHUNK - 1:CHUNK, :] for g in g_col]
        kd = [k[i] * jnp.exp(g_last[i] - g_col[i]) for i in n]
        s = [s_ref[bi, h] for bi, h in chains]
        ws = [_rdot(jnp.concatenate([uw[i][:, DK_A:], q[i] * e_g[i]], axis=0), s[i]) for i in n]
        v_new = [uw[i][:, :DK_A] - ws[i][:CHUNK] for i in n]
        o = [ws[i][CHUNK:] + _rdot(qk[i], v_new[i]) for i in n]
        s_new = [s[i] * jnp.exp(g_last[i]) + _rdot(kd[i], v_new[i], _TN) for i in n]
        for i, (bi, h) in enumerate(chains):
            s_ref[bi, h] = s_new[i]
            z = z_ref[bi, rows, h * DK_A:(h + 1) * DK_A]
            on = o[i] * lax.rsqrt(jnp.mean(o[i] * o[i], -1, keepdims=True) + RMS_EPS) * nw * _silu(z)
            ya_ref[bi, rows, h * DK_A:(h + 1) * DK_A] = on.astype(BF16)
        return carry

    lax.fori_loop(0, tl // CHUNK, chunk, 0)

    @pl.when(t == pl.num_programs(1) - 1)
    def _():
        sout_ref[...] = s_ref[...]


def _gdn(proj, conv_buf8, s0, conv_w, alog_row, dt_row, norm_w, bb, tl, l_valid):
    b, l, _ = proj.shape
    kern = functools.partial(_gdn_kernel, bb=bb, tl=tl, l_valid=l_valid, l_total=l)
    full = lambda *shape: pl.BlockSpec(shape, lambda i, t: (0,) * len(shape))
    return pl.pallas_call(
        kern,
        grid=(b // bb, l // tl),
        in_specs=[pl.BlockSpec((bb, tl, QKV_A), lambda i, t: (i, t, P_QKV // QKV_A)),
                  pl.BlockSpec((bb, tl, AB_BLOCK), lambda i, t: (i, t, P_AB // AB_BLOCK)),
                  pl.BlockSpec((bb, tl, V_A), lambda i, t: (i, t, P_Z // V_A)),
                  pl.BlockSpec((bb, SUBLANES, QKV_A), lambda i, t: (i, 0, 0)),
                  pl.BlockSpec((bb, H_A, DK_A, DK_A), lambda i, t: (i, 0, 0, 0)),
                  full(CONV_W, QKV_A), full(1, LANES), full(1, LANES), full(1, DK_A)],
        out_specs=[pl.BlockSpec((bb, tl, V_A), lambda i, t: (i, t, 0)),
                   pl.BlockSpec((bb, H_A, DK_A, DK_A), lambda i, t: (i, 0, 0, 0))],
        out_shape=[jax.ShapeDtypeStruct((b, l, V_A), BF16),
                   jax.ShapeDtypeStruct((b, H_A, DK_A, DK_A), F32)],
        scratch_shapes=[pltpu.VMEM((bb, tl + SUBLANES, QKV_A), F32),
                        pltpu.VMEM((bb, tl, QKV_A), F32),
                        pltpu.VMEM((bb, H_A, DK_A, DK_A), F32)],
        compiler_params=_cparams(("parallel", "arbitrary"), 48),
        name="gdn",
    )(proj, proj, proj, conv_buf8, s0, conv_w, alog_row, dt_row, norm_w)


PAIRS = H_B // 2


def _headsum(x, bd):
    return jnp.concatenate(
        [_mask_dot_right(x[:, p * LANES:(p + 1) * LANES], bd, STAT_PARTS) for p in range(PAIRS)], axis=1)


def _rwkv_kernel(rw_ref, sbuf_ref, s0_ref, mu_ref, w0_ref, a0_ref, kk_ref, ka_ref, rk_ref, lg_ref, lb_ref,
                 w2_ref, a2_ref, g2_ref, yb_ref, sout_ref,
                 ext_ref, r_s, k_s, v_s, z_s, p_s, lw_s, y_s, bonus_s, gate_s, s_ref, *, bb, tl, l_valid, l_total):
    t = pl.program_id(1)

    @pl.when(t == 0)
    def _():
        ext_ref[:, 0:SUBLANES, :] = sbuf_ref[...]
        s_ref[...] = s0_ref[...]

    row, col, incl, strict, eye, tril = _tri_consts()
    same_head = (row // N_B) == (col // N_B)
    bd = jnp.where(same_head, 1.0, 0.0).astype(BF16)
    lane_a = col < N_B

    for bi in range(bb):
        cur = rw_ref[bi]
        ext_ref[bi, SUBLANES:SUBLANES + tl, :] = cur
        prev = ext_ref[bi, SUBLANES - 1:SUBLANES - 1 + tl, :]
        mixed = cur + (prev - cur) * mu_ref[...]
        ext_ref[bi, 0:SUBLANES, :] = ext_ref[bi, tl:tl + SUBLANES, :]
        r = mixed[:, 0:D_B]
        k = mixed[:, D_B:2 * D_B]
        v = mixed[:, 2 * D_B:3 * D_B]
        lora = mixed[:, 3 * D_B:3 * D_B + W_LORA + A_LORA]
        g_in = mixed[:, 3 * D_B + W_LORA + A_LORA:SHIFT_W]
        lw = DECAY_SCALE * jax.nn.sigmoid(w0_ref[...] + _dot(jnp.tanh(lora), w2_ref[...]))
        a = jax.nn.sigmoid(a0_ref[...] + _dot(lora, a2_ref[...]))
        kkr = k * kk_ref[...]
        kk = kkr * lax.rsqrt(_headsum(kkr * kkr, bd) + 1e-6)
        k = k * (1.0 + (a - 1.0) * ka_ref[...])
        tile_rows = pl.ds(bi * tl, tl)
        bonus_s[tile_rows, :] = _headsum(r * k * rk_ref[...], bd) * v
        gate_s[tile_rows, :] = _dot(jax.nn.sigmoid(g_in), g2_ref[...])
        if l_valid < l_total:
            rvalid = (t * tl + lax.broadcasted_iota(jnp.int32, (tl, 1), 0)) < l_valid
            lw = jnp.where(rvalid, lw, 0.0)
            kk = jnp.where(rvalid, kk, 0.0)
            k = jnp.where(rvalid, k, 0.0)
            v = jnp.where(rvalid, v, 0.0)
        r_s[tile_rows, :] = r
        k_s[tile_rows, :] = k
        v_s[tile_rows, :] = v
        z_s[tile_rows, :] = -kk
        p_s[tile_rows, :] = kk * a
        lw_s[tile_rows, :] = lw

    groups = [(bi, p) for bi in range(bb) for p in range(PAIRS)]
    pairs = range(len(groups))
    both = range(2 * len(groups))
    cols = [slice(p * LANES, (p + 1) * LANES) for _, p in groups]

    def chunk(c, carry):
        r0 = pl.multiple_of(c * CHUNK, CHUNK)
        rows = [pl.ds(bi * tl + r0, CHUNK) for bi, _ in groups]
        lw_c = [lw_s[rows[g], cols[g]] for g in pairs]
        g_inc = [_mask_dot_left(tril, x, CUMSUM_PARTS) for x in lw_c]
        g_exc = [g_inc[p] - lw_c[p] for p in pairs]
        g_mid = [g[CHUNK // 2 - 1:CHUNK // 2, :] for g in g_inc]
        g_end = [g[CHUNK - 1:CHUNK, :] for g in g_inc]
        z = [z_s[rows[g], cols[g]] for g in pairs]
        rr = [r_s[rows[g], cols[g]] for g in pairs]
        pp = [p_s[rows[g], cols[g]] for g in pairs]
        kk_ = [k_s[rows[g], cols[g]] for g in pairs]
        vv = [v_s[rows[g], cols[g]] for g in pairs]
        zt = [z[p] * jnp.exp(g_exc[p] - g_mid[p]) for p in pairs]
        rt = [rr[p] * jnp.exp(g_inc[p] - g_mid[p]) for p in pairs]
        en = [jnp.exp(g_mid[p] - g_inc[p]) for p in pairs]
        s = [s_ref[bi, p] for bi, p in groups]
        lhs = [jnp.concatenate([jnp.where(lane_a, zt[p], 0.0), jnp.where(lane_a, 0.0, zt[p]),
                                jnp.where(lane_a, rt[p], 0.0), jnp.where(lane_a, 0.0, rt[p])], axis=0) for p in pairs]
        m = [_rdot(lhs[p], jnp.concatenate([pp[p] * en[p], kk_[p] * en[p]], axis=0), _NT) for p in pairs]
        zr0 = [_rdot(jnp.concatenate([z[p] * jnp.exp(g_exc[p]), rr[p] * jnp.exp(g_inc[p])], axis=0), s[p], _NT)
               for p in pairs]
        mz = [m[i // 2][(i % 2) * CHUNK:(i % 2 + 1) * CHUNK] for i in both]
        azp = [jnp.where(strict, x[:, :CHUNK], 0.0) for x in mz]
        azk = [jnp.where(strict, x[:, CHUNK:], 0.0) for x in mz]
        minv = _tri_inverse(azp, eye, 1, row, col)
        rhs = [zr0[i // 2][:CHUNK] + _rdot(azk[i], vv[i // 2]) for i in both]
        u_h = [_rdot(minv[i], rhs[i]) for i in both]
        u = [jnp.where(lane_a, u_h[2 * p], u_h[2 * p + 1]) for p in pairs]
        uv = [jnp.concatenate([u[p], vv[p]], axis=0) for p in pairs]
        incl2 = jnp.concatenate([incl, incl], axis=1)
        y_h = [_rdot(jnp.where(incl2, m[i // 2][(2 + i % 2) * CHUNK:(3 + i % 2) * CHUNK], 0.0), uv[i // 2])
               for i in both]
        tail = [jnp.exp(g_end[p] - g_inc[p]) for p in pairs]
        s_new = [s[p] * jnp.exp(g_end[p])
                 + _rdot(uv[p], jnp.concatenate([pp[p] * tail[p], kk_[p] * tail[p]], axis=0), _TN) for p in pairs]
        for g, (bi, p) in enumerate(groups):
            s_ref[bi, p] = jnp.where(same_head, s_new[g], 0.0)
            y_s[rows[g], cols[g]] = zr0[g][CHUNK:] + jnp.where(lane_a, y_h[2 * g], y_h[2 * g + 1])
        return carry

    lax.fori_loop(0, tl // CHUNK, chunk, 0)

    for bi in range(bb):
        tile_rows = pl.ds(bi * tl, tl)
        y = y_s[tile_rows, :]
        mean = _headsum(y, bd) * (1.0 / N_B)
        dev = y - mean
        var = _headsum(dev * dev, bd) * (1.0 / N_B)
        yn = dev * lax.rsqrt(var + GN_EPS) * lg_ref[...] + lb_ref[...]
        yb_ref[bi] = ((yn + bonus_s[tile_rows, :]) * gate_s[tile_rows, :]).astype(BF16)

    @pl.when(t == pl.num_programs(1) - 1)
    def _():
        sout_ref[...] = s_ref[...]


def _rwkv(proj, shift_buf8, s0_pairs, vecs, w2p, a2p, g2, bb, tl, l_valid):
    b, l, _ = proj.shape
    kern = functools.partial(_rwkv_kernel, bb=bb, tl=tl, l_valid=l_valid, l_total=l)
    full = lambda *shape: pl.BlockSpec(shape, lambda i, t: (0,) * len(shape))
    mu, w0, a0, k_k, k_a, r_k, lnx_g, lnx_b = vecs
    return pl.pallas_call(
        kern,
        grid=(b // bb, l // tl),
        in_specs=[pl.BlockSpec((bb, tl, RW_BLOCK), lambda i, t: (i, t, P_RW // RW_BLOCK)),
                  pl.BlockSpec((bb, SUBLANES, RW_BLOCK), lambda i, t: (i, 0, 0)),
                  pl.BlockSpec((bb, PAIRS, LANES, LANES), lambda i, t: (i, 0, 0, 0)),
                  full(1, RW_BLOCK)] + [full(1, D_B)] * 7 +
                 [full(W_LORA + A_LORA, D_B), full(W_LORA + A_LORA, D_B), full(G_LORA, D_B)],
        out_specs=[pl.BlockSpec((bb, tl, D_B), lambda i, t: (i, t, 0)),
                   pl.BlockSpec((bb, PAIRS, LANES, LANES), lambda i, t: (i, 0, 0, 0))],
        out_shape=[jax.ShapeDtypeStruct((b, l, D_B), BF16),
                   jax.ShapeDtypeStruct((b, PAIRS, LANES, LANES), F32)],
        scratch_shapes=[pltpu.VMEM((bb, tl + SUBLANES, RW_BLOCK), F32)] +
                       [pltpu.VMEM((bb * tl, D_B), F32)] * 9 +
                       [pltpu.VMEM((bb, PAIRS, LANES, LANES), F32)],
        compiler_params=_cparams(("parallel", "arbitrary"), 48),
        name="rwkv",
    )(proj, shift_buf8, s0_pairs, mu, w0, a0, k_k, k_a, r_k, lnx_g, lnx_b, w2p, a2p, g2)


def _layernorm(y, g, b):
    mu = jnp.mean(y, -1, keepdims=True)
    dev = y - mu
    var = jnp.mean(dev * dev, -1, keepdims=True)
    return dev * lax.rsqrt(var + LN_EPS) * g + b


def _route(logits_t, bias):
    tm = logits_t.shape[1]
    scores = jax.nn.sigmoid(logits_t)
    choice = scores + bias
    neg_inf = -jnp.inf
    iota_g = lax.broadcasted_iota(jnp.int32, (GROUP_SIZE, tm), 0)
    group_score = []
    for g in range(N_GROUPS):
        xg = choice[g * GROUP_SIZE:(g + 1) * GROUP_SIZE, :]
        m1 = jnp.max(xg, axis=0, keepdims=True)
        first = jnp.min(jnp.where(xg == m1, iota_g, GROUP_SIZE), axis=0, keepdims=True)
        m2 = jnp.max(jnp.where(iota_g == first, neg_inf, xg), axis=0, keepdims=True)
        group_score.append(m1 + m2)
    masked = []
    for g in range(N_GROUPS):
        rank = jnp.zeros((1, tm), jnp.int32)
        for o in range(N_GROUPS):
            if o == g:
                continue
            ahead = group_score[o] > group_score[g]
            if o < g:
                ahead = ahead | (group_score[o] == group_score[g])
            rank = rank + ahead.astype(jnp.int32)
        keep = rank < TOPK_GROUPS
        masked.append(jnp.where(keep, choice[g * GROUP_SIZE:(g + 1) * GROUP_SIZE, :], neg_inf))
    cur = jnp.concatenate(masked, axis=0)
    iota_e = lax.broadcasted_iota(jnp.int32, (N_EXPERTS, tm), 0)
    sel = jnp.zeros((N_EXPERTS, tm), F32)
    for _ in range(TOP_K):
        m = jnp.max(cur, axis=0, keepdims=True)
        first = jnp.min(jnp.where(cur == m, iota_e, N_EXPERTS), axis=0, keepdims=True)
        pick = iota_e == first
        sel = jnp.where(pick, 1.0, sel)
        cur = jnp.where(pick, neg_inf, cur)
    wsel = sel * scores
    denom = jnp.sum(wsel, axis=0, keepdims=True) + 1e-20
    return (ROUTED_SCALE * wsel) / denom


def _merge_kernel(ya_ref, yb_ref, mg_ref, x_ref, ga1_ref, sc2_ref, sh2_ref, pa_ref, pb_ref, wo_ref,
                  g1_ref, b1_ref, rwt_ref, rb_ref, x1_ref, h2_ref, gt_ref, *, dn_alpha):
    bb, tl, d = x_ref.shape
    tm = bb * tl
    ya = ya_ref[...].reshape(tm, V_A)
    yb = yb_ref[...].reshape(tm, D_B)
    mg = mg_ref[...].reshape(tm, 2 * d)
    merged = (jax.nn.sigmoid(mg[:, :d]) * jnp.dot(ya, pa_ref[...], preferred_element_type=F32)
              + jax.nn.sigmoid(mg[:, d:]) * jnp.dot(yb, pb_ref[...], preferred_element_type=F32))
    mix = _dot(merged, wo_ref[...])
    y = dn_alpha * x_ref[...] + (1.0 + ga1_ref[...]) * mix.reshape(bb, tl, d)
    x1 = _layernorm(y, g1_ref[...], b1_ref[...])
    x1_ref[...] = x1
    h2 = (x1 * (1.0 + sc2_ref[...]) + sh2_ref[...]).astype(BF16)
    h2_ref[...] = h2
    logits_t = lax.dot_general(rwt_ref[...], h2.reshape(tm, d), (_NT, ((), ())), preferred_element_type=F32)
    gt_ref[...] = _route(logits_t, rb_ref[...])


def _merge(ya, yb, proj, x, ga1, sc2, sh2, p_a, p_b, w_o, ln_g, ln_b, rw_t, r_bias, bb, tl, dn_alpha):
    b, l, d = x.shape
    nj = l // tl
    tm = bb * tl
    full = lambda *shape: pl.BlockSpec(shape, lambda i, j: (0,) * len(shape))
    mod = pl.BlockSpec((bb, 1, d), lambda i, j: (i, 0, 0))
    tok = lambda width, dtype=None: pl.BlockSpec((bb, tl, width), lambda i, j: (i, j, 0))
    return pl.pallas_call(
        functools.partial(_merge_kernel, dn_alpha=dn_alpha),
        grid=(b // bb, nj),
        in_specs=[tok(V_A), tok(D_B),
                  pl.BlockSpec((bb, tl, 2 * d), lambda i, j: (i, j, P_MG // (2 * d))),
                  tok(d), mod, mod, mod,
                  full(V_A, d), full(D_B, d), full(d, d), full(1, d), full(1, d),
                  full(N_EXPERTS, d), full(N_EXPERTS, 1)],
        out_specs=[tok(d), tok(d), pl.BlockSpec((N_EXPERTS, tm), lambda i, j: (0, i * nj + j))],
        out_shape=[jax.ShapeDtypeStruct((b, l, d), F32),
                   jax.ShapeDtypeStruct((b, l, d), BF16),
                   jax.ShapeDtypeStruct((N_EXPERTS, b * l), F32)],
        compiler_params=_cparams(("parallel", "parallel"), 48),
        name="merge",
    )(ya, yb, proj, x, ga1, sc2, sh2, p_a, p_b, w_o, ln_g, ln_b, rw_t, r_bias)


def _moe_kernel(h_ref, g_ref, x1_ref, ga2_ref, wgu_ref, wd_ref, sgu_ref, sd_ref, g2_ref, b2_ref,
                o_ref, acc_ref, *, dn_alpha):
    e = pl.program_id(2)
    bb, tl, d = h_ref.shape
    tm = bb * tl
    h = h_ref[...].reshape(tm, d)

    @pl.when(e == 0)
    def _():
        su = jnp.dot(h, sgu_ref[...], preferred_element_type=F32)
        act = _silu(su[:, :D_SHARED]) * su[:, D_SHARED:]
        acc_ref[...] = _dot(act, sd_ref[...])

    lane = lax.broadcasted_iota(jnp.int32, (tm, N_EXPERTS), 1)
    gates = g_ref[...]
    acts = []
    for i in range(EXPERTS_PER_STEP):
        gu = jnp.dot(h, wgu_ref[i], preferred_element_type=F32)
        gate = _lane_pick(gates, lane, e * EXPERTS_PER_STEP + i)
        acts.append((_silu(gu[:, :D_EXPERT]) * gu[:, D_EXPERT:] * gate).astype(BF16))
    wd = wd_ref[...].reshape(EXPERTS_PER_STEP * D_EXPERT, d)
    acc_ref[...] += jnp.dot(jnp.concatenate(acts, axis=1), wd, preferred_element_type=F32)

    @pl.when(e == pl.num_programs(2) - 1)
    def _():
        y = dn_alpha * x1_ref[...] + (1.0 + ga2_ref[...]) * acc_ref[...].reshape(bb, tl, d)
        o_ref[...] = _layernorm(y, g2_ref[...], b2_ref[...])


def _moe(h2, gates, x1, ga2, we_gu, we_down, ws_gu, ws_down, ln_g, ln_b, bb, tl, dn_alpha):
    b, l, d = x1.shape
    nj = l // tl
    tm = bb * tl
    full = lambda *shape: pl.BlockSpec(shape, lambda i, j, e: (0,) * len(shape))
    tok = pl.BlockSpec((bb, tl, d), lambda i, j, e: (i, j, 0))
    return pl.pallas_call(
        functools.partial(_moe_kernel, dn_alpha=dn_alpha),
        grid=(b // bb, nj, N_EXPERTS // EXPERTS_PER_STEP),
        in_specs=[tok,
                  pl.BlockSpec((tm, N_EXPERTS), lambda i, j, e: (i * nj + j, 0)),
                  tok,
                  pl.BlockSpec((bb, 1, d), lambda i, j, e: (i, 0, 0)),
                  pl.BlockSpec((EXPERTS_PER_STEP, d, 2 * D_EXPERT), lambda i, j, e: (e, 0, 0)),
                  pl.BlockSpec((EXPERTS_PER_STEP, D_EXPERT, d), lambda i, j, e: (e, 0, 0)),
                  full(d, 2 * D_SHARED), full(D_SHARED, d), full(1, d), full(1, d)],
        out_specs=tok,
        out_shape=jax.ShapeDtypeStruct((b, l, d), F32),
        scratch_shapes=[pltpu.VMEM((tm, d), F32)],
        compiler_params=_cparams(("parallel", "parallel", "arbitrary"), 56),
        name="moe",
    )(h2, gates, x1, ga2, we_gu, we_down, ws_gu, ws_down, ln_g, ln_b)


def _pad_rows(buf, width):
    b, n, w = buf.shape
    return jnp.pad(buf.astype(F32), ((0, 0), (SUBLANES - n, 0), (0, width - w)))


def _to_pairs(s):
    b = s.shape[0]
    s = s.astype(F32).reshape(b, PAIRS, 2, N_B, N_B)
    zero = jnp.zeros_like(s[:, :, 0])
    top = jnp.concatenate([s[:, :, 0], zero], axis=-1)
    bot = jnp.concatenate([zero, s[:, :, 1]], axis=-1)
    return jnp.concatenate([top, bot], axis=-2)


def _from_pairs(sp):
    b = sp.shape[0]
    return jnp.stack([sp[:, :, :N_B, :N_B], sp[:, :, N_B:, N_B:]], axis=2).reshape(b, H_B, N_B, N_B)


def _layer(x, mod, conv_buf, gdn_s, shift_buf, rwkv_s, p, tiles, dn_alpha):
    b, l, d = x.shape
    sh1, sc1, ga1, sh2, sc2, ga2 = mod
    proj = _inproj(x, sc1, sh1, p['w_in'], tiles['bb'], tiles['tl'])

    lp = -(-l // CHUNK) * CHUNK
    proj_r = proj if lp == l else jnp.pad(proj, ((0, 0), (0, lp - l), (0, 0)))
    tl_r = min(tiles['tl_rec'], lp)
    ya, gdn_new = _gdn(proj_r, _pad_rows(conv_buf, QKV_A), gdn_s.astype(F32), p['conv_w'], p['alog_row'],
                       p['dt_row'], p['gdn_norm_w'], tiles['bb_rec'], tl_r, l)
    yb, rwkv_pairs = _rwkv(proj_r, _pad_rows(shift_buf, RW_BLOCK), _to_pairs(rwkv_s), p['rwkv_vecs'],
                           p['w2p'], p['a2p'], p['g2'], tiles['bb_rec'], tl_r, l)
    if lp != l:
        ya, yb = ya[:, :l], yb[:, :l]

    x1, h2, gates_t = _merge(ya, yb, proj, x, ga1, sc2, sh2, p['p_a'], p['p_b'], p['w_o'], p['ln1_g'], p['ln1_b'],
                             p['router_wt'], p['router_bias'], tiles['bb'], tiles['tl_merge'], dn_alpha)
    out = _moe(h2, gates_t.T, x1, ga2, p['we_gu'], p['we_down'], p['ws_gu'], p['ws_down'], p['ln2_g'], p['ln2_b'],
               tiles['bb'], tiles['tl'], dn_alpha)

    pre = jnp.concatenate([conv_buf.astype(F32), proj[:, :, P_QKV:P_QKV + QKV_A]], axis=1)
    conv_new = pre[:, -(CONV_W - 1):]
    shift_new = proj[:, l - 1:l, P_RW:P_RW + SHIFT_W]
    return (out, conv_new.astype(conv_buf.dtype), gdn_new.astype(gdn_s.dtype),
            shift_new.astype(shift_buf.dtype), _from_pairs(rwkv_pairs).astype(rwkv_s.dtype))


def _prep_params(l, w_in, conv_w, a_log, dt_bias, gdn_norm_w, mu_shift, w0, w2, a0, a2, g2, k_k, k_a, r_k,
                 lnx_g, lnx_b, p_a, p_b, w_o, ln1_g, ln1_b, router_w, router_bias, we_gate, we_up, we_down,
                 ws_gate, ws_up, ws_down, ln2_g, ln2_b):
    d = D_MODEL
    w = w_in[l]
    w_p = jnp.concatenate(
        [w[:, :QKV_A], w[:, OFF_Z:OFF_RWKV], w[:, OFF_RWKV:OFF_MERGE],
         jnp.zeros((d, RW_BLOCK - SHIFT_W), w.dtype), w[:, OFF_MERGE:], w[:, OFF_ALPHA:OFF_Z],
         jnp.zeros((d, AB_BLOCK - 2 * H_A), w.dtype)], axis=1).astype(BF16)
    row = lambda v, width: jnp.pad(v.astype(F32).reshape(1, -1), ((0, 0), (0, width - v.size)))
    zeros_lora = jnp.zeros((W_LORA, D_B), F32)
    return {
        'w_in': w_p,
        'conv_w': conv_w[l].astype(F32),
        'alog_row': row(a_log[l], LANES),
        'dt_row': row(dt_bias[l], LANES),
        'gdn_norm_w': row(gdn_norm_w[l], DK_A),
        'rwkv_vecs': (row(mu_shift[l], RW_BLOCK), row(w0[l], D_B), row(a0[l], D_B), row(k_k[l], D_B),
                      row(k_a[l], D_B), row(r_k[l], D_B), row(lnx_g[l], D_B), row(lnx_b[l], D_B)),
        'w2p': jnp.concatenate([w2[l].astype(F32), zeros_lora], axis=0),
        'a2p': jnp.concatenate([zeros_lora, a2[l].astype(F32)], axis=0),
        'g2': g2[l].astype(F32),
        'p_a': p_a[l].astype(BF16), 'p_b': p_b[l].astype(BF16), 'w_o': w_o[l].astype(BF16),
        'ln1_g': row(ln1_g[l], d), 'ln1_b': row(ln1_b[l], d),
        'router_wt': router_w[l].T.astype(BF16),
        'router_bias': router_bias[l].astype(F32).reshape(N_EXPERTS, 1),
        'we_gu': jnp.concatenate([we_gate[l], we_up[l]], axis=-1).astype(BF16),
        'we_down': we_down[l].astype(BF16),
        'ws_gu': jnp.concatenate([ws_gate[l], ws_up[l]], axis=-1).astype(BF16),
        'ws_down': ws_down[l].astype(BF16),
        'ln2_g': row(ln2_g[l], d), 'ln2_b': row(ln2_b[l], d),
    }


def kernel(x_prompt, x_sample, c_prompt, c_sample, state_gdn_conv, state_gdn, state_rwkv_shift, state_rwkv, w_ada, b_ada, w_in, conv_w, a_log, dt_bias, gdn_norm_w, mu_shift, w0, w2, a0, a2, g2, k_k, k_a, r_k, lnx_g, lnx_b, p_a, p_b, w_o, ln1_g, ln1_b, router_w, router_bias, we_gate, we_up, we_down, ws_gate, ws_up, ws_down, ln2_g, ln2_b):
    depth = w_ada.shape[0]
    dn_alpha = (2 * depth) ** 0.25
    bp, lp_, d = x_prompt.shape
    bs, ls, _ = x_sample.shape
    dtp = x_prompt.dtype
    tiles_p = {'bb': 1, 'tl': 1024, 'bb_rec': 2, 'tl_rec': 256, 'tl_merge': 512}
    tiles_s = {'bb': bs, 'tl': ls, 'bb_rec': 2, 'tl_rec': CHUNK, 'tl_merge': ls}

    yp, ys = x_prompt, x_sample
    new_p = ([], [], [], [])
    new_s = ([], [], [], [])
    for l in range(depth):
        p = _prep_params(l, w_in, conv_w, a_log, dt_bias, gdn_norm_w, mu_shift, w0, w2, a0, a2, g2, k_k, k_a,
                         r_k, lnx_g, lnx_b, p_a, p_b, w_o, ln1_g, ln1_b, router_w, router_bias, we_gate, we_up,
                         we_down, ws_gate, ws_up, ws_down, ln2_g, ln2_b)
        mod = _ada(jnp.concatenate([c_prompt, c_sample], axis=0), w_ada[l], b_ada[l])
        mod_p = tuple(m[:, None, :] for m in jnp.split(mod[:bp], 6, axis=-1))
        mod_s = tuple(m[:, None, :] for m in jnp.split(mod[bp:], 6, axis=-1))
        yp, *sp = _layer(yp, mod_p,
                         jnp.zeros((bp, CONV_W - 1, QKV_A), dtp), jnp.zeros((bp, H_A, DK_A, DK_A), dtp),
                         jnp.zeros((bp, 1, SHIFT_W), dtp), jnp.zeros((bp, H_B, N_B, N_B), dtp),
                         p, tiles_p, dn_alpha)
        ys, *ss = _layer(ys, mod_s, state_gdn_conv[l], state_gdn[l], state_rwkv_shift[l], state_rwkv[l],
                         p, tiles_s, dn_alpha)
        for lst, val in zip(new_p, sp):
            lst.append(val)
        for lst, val in zip(new_s, ss):
            lst.append(val)
    conv_p, gdn_p, shift_p, rwkv_p = [jnp.stack(t, 0) for t in new_p]
    conv_s, gdn_s, shift_s, rwkv_s = [jnp.stack(t, 0) for t in new_s]
    return (yp, ys, conv_p, gdn_p, shift_p, rwkv_p, conv_s, gdn_s, shift_s, rwkv_s)
```

```python
import functools
import math

import jax
import jax.numpy as jnp
from jax import lax
from jax.experimental import pallas as pl
from jax.experimental.pallas import tpu as pltpu
from jax.experimental.pallas import tpu_sc as plsc

F32 = jnp.float32
BF16 = jnp.bfloat16

D_MODEL = 1024
DK_A = 128
H_A = 4
QK_A = H_A * DK_A
V_A = H_A * DK_A
QKV_A = 2 * QK_A + V_A
CONV_W = 4
N_B = 64
H_B = 8
D_B = H_B * N_B
W_LORA = 64
A_LORA = 64
G_LORA = 128
SHIFT_W = 3 * D_B + W_LORA + A_LORA + G_LORA
OFF_ALPHA = QKV_A
OFF_BETA = OFF_ALPHA + H_A
OFF_Z = OFF_BETA + H_A
OFF_RWKV = OFF_Z + V_A
OFF_MERGE = OFF_RWKV + SHIFT_W
N_EXPERTS = 64
TOP_K = 8
N_GROUPS = 8
GROUP_SIZE = N_EXPERTS // N_GROUPS
TOPK_GROUPS = 4
D_EXPERT = 256
D_SHARED = 256
ROUTED_SCALE = 2.5
LN_EPS = 1e-5
GN_EPS = 64e-5
RMS_EPS = 1e-6
DECAY_SCALE = -math.exp(-0.5)

SUBLANES = 8
LANES = 128

P_QKV = 0
P_Z = QKV_A
P_RW = 2048
RW_BLOCK = 2048
P_MG = P_RW + RW_BLOCK
P_AB = P_MG + 2 * D_MODEL
AB_BLOCK = LANES
N_PROJ = P_AB + AB_BLOCK
PROJ_TN = 896

CHUNK = 128
BASE_BLOCK = 8
CUMSUM_PARTS = 3
STAT_PARTS = 2
EXPERTS_PER_STEP = 2
HIGH_HALF = -65536

EXPERT_TILE = 512
SC_WORKERS = 32
SC_CHUNK = 128
COMBINE_TL = 512


def _cparams(sem, vmem_mb):
    return pltpu.CompilerParams(dimension_semantics=sem, vmem_limit_bytes=vmem_mb * 1024 * 1024)


def _dot(a, b):
    return jnp.dot(a.astype(BF16), b.astype(BF16), preferred_element_type=F32)


def _rdot(a, b, dims=((1,), (0,))):
    return lax.dot_general(a.astype(BF16), b.astype(BF16), (dims, ((), ())), preferred_element_type=F32)


_NT = ((1,), (1,))
_TN = ((0,), (0,))


def _bf16_parts(x, parts):
    out = []
    rem = x
    for _ in range(parts):
        hi = rem.astype(BF16)
        out.append(hi)
        rem = rem - hi.astype(F32)
    return out


def _mask_dot_left(mask, x, parts):
    return sum(jnp.dot(mask, p, preferred_element_type=F32) for p in _bf16_parts(x, parts))


def _mask_dot_right(x, mask, parts):
    return sum(jnp.dot(p, mask, preferred_element_type=F32) for p in _bf16_parts(x, parts))


def _lane_pick(x, lane_iota, lane):
    return jnp.sum(jnp.where(lane_iota == lane, x, 0.0), axis=-1, keepdims=True)


def _pack_bf16_pairs(x):
    n = x.shape[1] // 2
    bits = lax.bitcast_convert_type(x.astype(BF16).astype(F32), jnp.int32)
    return (bits[:, :n] & HIGH_HALF) | lax.shift_right_logical(bits[:, n:], 16)


def _unpack_bf16_pairs(p):
    hi = lax.bitcast_convert_type(p & HIGH_HALF, F32)
    lo = lax.bitcast_convert_type(lax.shift_left(p, 16), F32)
    return jnp.concatenate([hi, lo], axis=1)


def _silu(x):
    return x * jax.nn.sigmoid(x)


def _softplus(x):
    return jnp.maximum(x, 0.0) + jnp.log1p(jnp.exp(-jnp.abs(x)))


def _tri_inverse(mats, eye, sign, row, col):
    base = (row // BASE_BLOCK) == (col // BASE_BLOCK)
    ds = [jnp.where(base, a, 0.0) for a in mats]
    xs = [eye + d if sign > 0 else eye - d for d in ds]
    power = 2
    while power < BASE_BLOCK:
        ds = [_rdot(d, d) for d in ds]
        xs = [x + _rdot(x, d) for x, d in zip(xs, ds)]
        power *= 2
    b = BASE_BLOCK
    while b < CHUNK:
        sibling = ((row // b) == (col // b) + 1) & ((row // (2 * b)) == (col // (2 * b)))
        offs = [jnp.where(sibling, a, 0.0) for a in mats]
        ts = [_rdot(o, x) for o, x in zip(offs, xs)]
        if sign > 0:
            xs = [x + _rdot(x, t_) for x, t_ in zip(xs, ts)]
        else:
            xs = [x - _rdot(x, t_) for x, t_ in zip(xs, ts)]
        b *= 2
    return xs


def _tri_consts():
    row = lax.broadcasted_iota(jnp.int32, (CHUNK, CHUNK), 0)
    col = lax.broadcasted_iota(jnp.int32, (CHUNK, CHUNK), 1)
    incl = row >= col
    strict = row > col
    eye = jnp.where(row == col, 1.0, 0.0).astype(F32)
    tril = jnp.where(incl, 1.0, 0.0).astype(BF16)
    return row, col, incl, strict, eye, tril


def _ada_kernel(c_ref, w_ref, b_ref, o_ref):
    o_ref[...] = _dot(_silu(c_ref[...]), w_ref[...]) + b_ref[...]


def _ada(c, w_ada, b_ada):
    n, d = c.shape
    nout = w_ada.shape[1]
    tn = 768
    return pl.pallas_call(
        _ada_kernel,
        grid=(nout // tn,),
        in_specs=[pl.BlockSpec((n, d), lambda j: (0, 0)),
                  pl.BlockSpec((d, tn), lambda j: (0, j)),
                  pl.BlockSpec((1, tn), lambda j: (0, j))],
        out_specs=pl.BlockSpec((n, tn), lambda j: (0, j)),
        out_shape=jax.ShapeDtypeStruct((n, nout), F32),
        compiler_params=_cparams(("parallel",), 32),
        name="ada",
    )(c, w_ada, b_ada.reshape(1, nout))


def _inproj_kernel(x_ref, sc_ref, sh_ref, w_ref, o_ref, hs_ref):
    @pl.when(pl.program_id(2) == 0)
    def _():
        h = x_ref[...] * (1.0 + sc_ref[...]) + sh_ref[...]
        hs_ref[...] = h.reshape(hs_ref.shape).astype(BF16)

    o = jnp.dot(hs_ref[...], w_ref[...], preferred_element_type=F32)
    o_ref[...] = o.reshape(o_ref.shape)


def _inproj(x, sc, sh, w_p, bb, tl):
    b, l, d = x.shape
    return pl.pallas_call(
        _inproj_kernel,
        grid=(b // bb, l // tl, N_PROJ // PROJ_TN),
        in_specs=[pl.BlockSpec((bb, tl, d), lambda i, j, n: (i, j, 0)),
                  pl.BlockSpec((bb, 1, d), lambda i, j, n: (i, 0, 0)),
                  pl.BlockSpec((bb, 1, d), lambda i, j, n: (i, 0, 0)),
                  pl.BlockSpec((d, PROJ_TN), lambda i, j, n: (0, n))],
        out_specs=pl.BlockSpec((bb, tl, PROJ_TN), lambda i, j, n: (i, j, n)),
        out_shape=jax.ShapeDtypeStruct((b, l, N_PROJ), F32),
        scratch_shapes=[pltpu.VMEM((bb * tl, d), BF16)],
        compiler_params=_cparams(("parallel", "parallel", "arbitrary"), 48),
        name="inproj",
    )(x, sc, sh, w_p)


def _gdn_kernel(qkv_ref, ab_ref, z_ref, cbuf_ref, s0_ref, cw_ref, alog_ref, dt_ref, nw_ref,
                ya_ref, sout_ref, ext_ref, qc_ref, s_ref, *, bb, tl, l_valid, l_total):
    t = pl.program_id(1)

    @pl.when(t == 0)
    def _():
        ext_ref[:, 0:SUBLANES, :] = cbuf_ref[...]
        s_ref[...] = s0_ref[...]

    first = SUBLANES - (CONV_W - 1)
    for bi in range(bb):
        ext_ref[bi, SUBLANES:SUBLANES + tl, :] = qkv_ref[bi]
        acc = cw_ref[0:1, :] * ext_ref[bi, first:first + tl, :]
        for j in range(1, CONV_W):
            acc = acc + cw_ref[j:j + 1, :] * ext_ref[bi, first + j:first + j + tl, :]
        ext_ref[bi, 0:SUBLANES, :] = ext_ref[bi, tl:tl + SUBLANES, :]
        qc_ref[bi] = _silu(acc)

    row, col, incl, strict, eye, tril = _tri_consts()
    neg_a = -jnp.exp(alog_ref[...])
    dt = dt_ref[...]
    nw = nw_ref[...]
    chains = [(bi, h) for bi in range(bb) for h in range(H_A)]
    n = range(len(chains))

    def chunk(c, carry):
        r0 = pl.multiple_of(c * CHUNK, CHUNK)
        rows = pl.ds(r0, CHUNK)
        g_cum = []
        b_all = []
        for bi in range(bb):
            ab = ab_ref[bi, rows, :]
            g_bi = neg_a * _softplus(ab + dt)
            b_bi = jax.nn.sigmoid(ab)
            if l_valid < l_total:
                valid = (t * tl + r0 + row) < l_valid
                g_bi = jnp.where(valid, g_bi, 0.0)
                b_bi = jnp.where(valid, b_bi, 0.0)
            g_cum.append(_mask_dot_left(tril, g_bi, CUMSUM_PARTS))
            b_all.append(b_bi)
        g_col = [_lane_pick(g_cum[bi], col, h) for bi, h in chains]
        beta = [_lane_pick(b_all[bi], col, H_A + h) for bi, h in chains]
        decay = []
        for i in n:
            g_b = jnp.broadcast_to(g_col[i], (CHUNK, CHUNK))
            decay.append(jnp.where(incl, jnp.exp(jnp.where(incl, g_b - g_b.T, 0.0)), 0.0))
        q = [qc_ref[bi, rows, h * DK_A:(h + 1) * DK_A] for bi, h in chains]
        k = [qc_ref[bi, rows, QK_A + h * DK_A:QK_A + (h + 1) * DK_A] for bi, h in chains]
        v = [qc_ref[bi, rows, 2 * QK_A + h * DK_A:2 * QK_A + (h + 1) * DK_A] for bi, h in chains]
        q = [x * lax.rsqrt(jnp.sum(x * x, -1, keepdims=True) + 1e-6) * (DK_A ** -0.5) for x in q]
        k = [x * lax.rsqrt(jnp.sum(x * x, -1, keepdims=True) + 1e-6) for x in k]
        kq = [_rdot(jnp.concatenate([k[i], q[i]], axis=0), k[i], _NT) for i in n]
        a = [jnp.where(strict, beta[i] * kq[i][:CHUNK] * decay[i], 0.0) for i in n]
        qk = [jnp.where(incl, kq[i][CHUNK:] * decay[i], 0.0) for i in n]
        x = _tri_inverse(a, eye, -1, row, col)
        e_g = [jnp.exp(g) for g in g_col]
        uw = [_rdot(x[i], jnp.concatenate([v[i] * beta[i], k[i] * (beta[i] * e_g[i])], axis=1)) for i in n]
        g_last = [g[CHUNK - 1:CHUNK, :] for g in g_col]
        kd = [k[i] * jnp.exp(g_last[i] - g_col[i]) for i in n]
        s = [s_ref[bi, h] for bi, h in chains]
        ws = [_rdot(jnp.concatenate([uw[i][:, DK_A:], q[i] * e_g[i]], axis=0), s[i]) for i in n]
        v_new = [uw[i][:, :DK_A] - ws[i][:CHUNK] for i in n]
        o = [ws[i][CHUNK:] + _rdot(qk[i], v_new[i]) for i in n]
        s_new = [s[i] * jnp.exp(g_last[i]) + _rdot(kd[i], v_new[i], _TN) for i in n]
        for i, (bi, h) in enumerate(chains):
            s_ref[bi, h] = s_new[i]
            z = z_ref[bi, rows, h * DK_A:(h + 1) * DK_A]
            on = o[i] * lax.rsqrt(jnp.mean(o[i] * o[i], -1, keepdims=True) + RMS_EPS) * nw * _silu(z)
            ya_ref[bi, rows, h * DK_A:(h + 1) * DK_A] = on.astype(BF16)
        return carry

    lax.fori_loop(0, tl // CHUNK, chunk, 0)

    @pl.when(t == pl.num_programs(1) - 1)
    def _():
        sout_ref[...] = s_ref[...]


def _gdn(proj, conv_buf8, s0, conv_w, alog_row, dt_row, norm_w, bb, tl, l_valid):
    b, l, _ = proj.shape
    kern = functools.partial(_gdn_kernel, bb=bb, tl=tl, l_valid=l_valid, l_total=l)
    full = lambda *shape: pl.BlockSpec(shape, lambda i, t: (0,) * len(shape))
    return pl.pallas_call(
        kern,
        grid=(b // bb, l // tl),
        in_specs=[pl.BlockSpec((bb, tl, QKV_A), lambda i, t: (i, t, P_QKV // QKV_A)),
                  pl.BlockSpec((bb, tl, AB_BLOCK), lambda i, t: (i, t, P_AB // AB_BLOCK)),
                  pl.BlockSpec((bb, tl, V_A), lambda i, t: (i, t, P_Z // V_A)),
                  pl.BlockSpec((bb, SUBLANES, QKV_A), lambda i, t: (i, 0, 0)),
                  pl.BlockSpec((bb, H_A, DK_A, DK_A), lambda i, t: (i, 0, 0, 0)),
                  full(CONV_W, QKV_A), full(1, LANES), full(1, LANES), full(1, DK_A)],
        out_specs=[pl.BlockSpec((bb, tl, V_A), lambda i, t: (i, t, 0)),
                   pl.BlockSpec((bb, H_A, DK_A, DK_A), lambda i, t: (i, 0, 0, 0))],
        out_shape=[jax.ShapeDtypeStruct((b, l, V_A), BF16),
                   jax.ShapeDtypeStruct((b, H_A, DK_A, DK_A), F32)],
        scratch_shapes=[pltpu.VMEM((bb, tl + SUBLANES, QKV_A), F32),
                        pltpu.VMEM((bb, tl, QKV_A), F32),
                        pltpu.VMEM((bb, H_A, DK_A, DK_A), F32)],
        compiler_params=_cparams(("parallel", "arbitrary"), 48),
        name="gdn",
    )(proj, proj, proj, conv_buf8, s0, conv_w, alog_row, dt_row, norm_w)


PAIRS = H_B // 2


def _headsum(x, bd):
    return jnp.concatenate(
        [_mask_dot_right(x[:, p * LANES:(p + 1) * LANES], bd, STAT_PARTS) for p in range(PAIRS)], axis=1)


def _rwkv_kernel(rw_ref, sbuf_ref, s0_ref, mu_ref, w0_ref, a0_ref, kk_ref, ka_ref, rk_ref, lg_ref, lb_ref,
                 w2_ref, a2_ref, g2_ref, yb_ref, sout_ref,
                 ext_ref, r_s, k_s, v_s, z_s, p_s, lw_s, y_s, bonus_s, gate_s, s_ref, *, bb, tl, l_valid, l_total):
    t = pl.program_id(1)

    @pl.when(t == 0)
    def _():
        ext_ref[:, 0:SUBLANES, :] = sbuf_ref[...]
        s_ref[...] = s0_ref[...]

    row, col, incl, strict, eye, tril = _tri_consts()
    same_head = (row // N_B) == (col // N_B)
    bd = jnp.where(same_head, 1.0, 0.0).astype(BF16)
    lane_a = col < N_B

    for bi in range(bb):
        cur = rw_ref[bi]
        ext_ref[bi, SUBLANES:SUBLANES + tl, :] = cur
        prev = ext_ref[bi, SUBLANES - 1:SUBLANES - 1 + tl, :]
        mixed = cur + (prev - cur) * mu_ref[...]
        ext_ref[bi, 0:SUBLANES, :] = ext_ref[bi, tl:tl + SUBLANES, :]
        r = mixed[:, 0:D_B]
        k = mixed[:, D_B:2 * D_B]
        v = mixed[:, 2 * D_B:3 * D_B]
        lora = mixed[:, 3 * D_B:3 * D_B + W_LORA + A_LORA]
        g_in = mixed[:, 3 * D_B + W_LORA + A_LORA:SHIFT_W]
        lw = DECAY_SCALE * jax.nn.sigmoid(w0_ref[...] + _dot(jnp.tanh(lora), w2_ref[...]))
        a = jax.nn.sigmoid(a0_ref[...] + _dot(lora, a2_ref[...]))
        kkr = k * kk_ref[...]
        kk = kkr * lax.rsqrt(_headsum(kkr * kkr, bd) + 1e-6)
        k = k * (1.0 + (a - 1.0) * ka_ref[...])
        tile_rows = pl.ds(bi * tl, tl)
        bonus_s[tile_rows, :] = _headsum(r * k * rk_ref[...], bd) * v
        gate_s[tile_rows, :] = _dot(jax.nn.sigmoid(g_in), g2_ref[...])
        if l_valid < l_total:
            rvalid = (t * tl + lax.broadcasted_iota(jnp.int32, (tl, 1), 0)) < l_valid
            lw = jnp.where(rvalid, lw, 0.0)
            kk = jnp.where(rvalid, kk, 0.0)
            k = jnp.where(rvalid, k, 0.0)
            v = jnp.where(rvalid, v, 0.0)
        r_s[tile_rows, :] = r
        k_s[tile_rows, :] = k
        v_s[tile_rows, :] = v
        z_s[tile_rows, :] = -kk
        p_s[tile_rows, :] = kk * a
        lw_s[tile_rows, :] = lw

    groups = [(bi, p) for bi in range(bb) for p in range(PAIRS)]
    pairs = range(len(groups))
    both = range(2 * len(groups))
    cols = [slice(p * LANES, (p + 1) * LANES) for _, p in groups]

    def chunk(c, carry):
        r0 = pl.multiple_of(c * CHUNK, CHUNK)
        rows = [pl.ds(bi * tl + r0, CHUNK) for bi, _ in groups]
        lw_c = [lw_s[rows[g], cols[g]] for g in pairs]
        g_inc = [_mask_dot_left(tril, x, CUMSUM_PARTS) for x in lw_c]
        g_exc = [g_inc[p] - lw_c[p] for p in pairs]
        g_mid = [g[CHUNK // 2 - 1:CHUNK // 2, :] for g in g_inc]
        g_end = [g[CHUNK - 1:CHUNK, :] for g in g_inc]
        z = [z_s[rows[g], cols[g]] for g in pairs]
        rr = [r_s[rows[g], cols[g]] for g in pairs]
        pp = [p_s[rows[g], cols[g]] for g in pairs]
        kk_ = [k_s[rows[g], cols[g]] for g in pairs]
        vv = [v_s[rows[g], cols[g]] for g in pairs]
        zt = [z[p] * jnp.exp(g_exc[p] - g_mid[p]) for p in pairs]
        rt = [rr[p] * jnp.exp(g_inc[p] - g_mid[p]) for p in pairs]
        en = [jnp.exp(g_mid[p] - g_inc[p]) for p in pairs]
        s = [s_ref[bi, p] for bi, p in groups]
        lhs = [jnp.concatenate([jnp.where(lane_a, zt[p], 0.0), jnp.where(lane_a, 0.0, zt[p]),
                                jnp.where(lane_a, rt[p], 0.0), jnp.where(lane_a, 0.0, rt[p])], axis=0) for p in pairs]
        m = [_rdot(lhs[p], jnp.concatenate([pp[p] * en[p], kk_[p] * en[p]], axis=0), _NT) for p in pairs]
        zr0 = [_rdot(jnp.concatenate([z[p] * jnp.exp(g_exc[p]), rr[p] * jnp.exp(g_inc[p])], axis=0), s[p], _NT)
               for p in pairs]
        mz = [m[i // 2][(i % 2) * CHUNK:(i % 2 + 1) * CHUNK] for i in both]
        azp = [jnp.where(strict, x[:, :CHUNK], 0.0) for x in mz]
        azk = [jnp.where(strict, x[:, CHUNK:], 0.0) for x in mz]
        minv = _tri_inverse(azp, eye, 1, row, col)
        rhs = [zr0[i // 2][:CHUNK] + _rdot(azk[i], vv[i // 2]) for i in both]
        u_h = [_rdot(minv[i], rhs[i]) for i in both]
        u = [jnp.where(lane_a, u_h[2 * p], u_h[2 * p + 1]) for p in pairs]
        uv = [jnp.concatenate([u[p], vv[p]], axis=0) for p in pairs]
        incl2 = jnp.concatenate([incl, incl], axis=1)
        y_h = [_rdot(jnp.where(incl2, m[i // 2][(2 + i % 2) * CHUNK:(3 + i % 2) * CHUNK], 0.0), uv[i // 2])
               for i in both]
        tail = [jnp.exp(g_end[p] - g_inc[p]) for p in pairs]
        s_new = [s[p] * jnp.exp(g_end[p])
                 + _rdot(uv[p], jnp.concatenate([pp[p] * tail[p], kk_[p] * tail[p]], axis=0), _TN) for p in pairs]
        for g, (bi, p) in enumerate(groups):
            s_ref[bi, p] = jnp.where(same_head, s_new[g], 0.0)
            y_s[rows[g], cols[g]] = zr0[g][CHUNK:] + jnp.where(lane_a, y_h[2 * g], y_h[2 * g + 1])
        return carry

    lax.fori_loop(0, tl // CHUNK, chunk, 0)

    for bi in range(bb):
        tile_rows = pl.ds(bi * tl, tl)
        y = y_s[tile_rows, :]
        mean = _headsum(y, bd) * (1.0 / N_B)
        dev = y - mean
        var = _headsum(dev * dev, bd) * (1.0 / N_B)
        yn = dev * lax.rsqrt(var + GN_EPS) * lg_ref[...] + lb_ref[...]
        yb_ref[bi] = ((yn + bonus_s[tile_rows, :]) * gate_s[tile_rows, :]).astype(BF16)

    @pl.when(t == pl.num_programs(1) - 1)
    def _():
        sout_ref[...] = s_ref[...]


def _rwkv(proj, shift_buf8, s0_pairs, vecs, w2p, a2p, g2, bb, tl, l_valid):
    b, l, _ = proj.shape
    kern = functools.partial(_rwkv_kernel, bb=bb, tl=tl, l_valid=l_valid, l_total=l)
    full = lambda *shape: pl.BlockSpec(shape, lambda i, t: (0,) * len(shape))
    mu, w0, a0, k_k, k_a, r_k, lnx_g, lnx_b = vecs
    return pl.pallas_call(
        kern,
        grid=(b // bb, l // tl),
        in_specs=[pl.BlockSpec((bb, tl, RW_BLOCK), lambda i, t: (i, t, P_RW // RW_BLOCK)),
                  pl.BlockSpec((bb, SUBLANES, RW_BLOCK), lambda i, t: (i, 0, 0)),
                  pl.BlockSpec((bb, PAIRS, LANES, LANES), lambda i, t: (i, 0, 0, 0)),
                  full(1, RW_BLOCK)] + [full(1, D_B)] * 7 +
                 [full(W_LORA + A_LORA, D_B), full(W_LORA + A_LORA, D_B), full(G_LORA, D_B)],
        out_specs=[pl.BlockSpec((bb, tl, D_B), lambda i, t: (i, t, 0)),
                   pl.BlockSpec((bb, PAIRS, LANES, LANES), lambda i, t: (i, 0, 0, 0))],
        out_shape=[jax.ShapeDtypeStruct((b, l, D_B), BF16),
                   jax.ShapeDtypeStruct((b, PAIRS, LANES, LANES), F32)],
        scratch_shapes=[pltpu.VMEM((bb, tl + SUBLANES, RW_BLOCK), F32)] +
                       [pltpu.VMEM((bb * tl, D_B), F32)] * 9 +
                       [pltpu.VMEM((bb, PAIRS, LANES, LANES), F32)],
        compiler_params=_cparams(("parallel", "arbitrary"), 48),
        name="rwkv",
    )(proj, shift_buf8, s0_pairs, mu, w0, a0, k_k, k_a, r_k, lnx_g, lnx_b, w2p, a2p, g2)


def _layernorm(y, g, b):
    mu = jnp.mean(y, -1, keepdims=True)
    dev = y - mu
    var = jnp.mean(dev * dev, -1, keepdims=True)
    return dev * lax.rsqrt(var + LN_EPS) * g + b


def _route(logits_t, bias, base_cnt):
    tm = logits_t.shape[1]
    scores = jax.nn.sigmoid(logits_t)
    choice = scores + bias
    neg_inf = -jnp.inf
    iota_g = lax.broadcasted_iota(jnp.int32, (GROUP_SIZE, tm), 0)
    group_score = []
    for g in range(N_GROUPS):
        xg = choice[g * GROUP_SIZE:(g + 1) * GROUP_SIZE, :]
        m1 = jnp.max(xg, axis=0, keepdims=True)
        first = jnp.min(jnp.where(xg == m1, iota_g, GROUP_SIZE), axis=0, keepdims=True)
        m2 = jnp.max(jnp.where(iota_g == first, neg_inf, xg), axis=0, keepdims=True)
        group_score.append(m1 + m2)
    masked = []
    for g in range(N_GROUPS):
        rank = jnp.zeros((1, tm), jnp.int32)
        for o in range(N_GROUPS):
            if o == g:
                continue
            ahead = group_score[o] > group_score[g]
            if o < g:
                ahead = ahead | (group_score[o] == group_score[g])
            rank = rank + ahead.astype(jnp.int32)
        keep = rank < TOPK_GROUPS
        masked.append(jnp.where(keep, choice[g * GROUP_SIZE:(g + 1) * GROUP_SIZE, :], neg_inf))
    cur = jnp.concatenate(masked, axis=0)
    iota_e = lax.broadcasted_iota(jnp.int32, (N_EXPERTS, tm), 0)
    sel = jnp.zeros((N_EXPERTS, tm), F32)
    picks = []
    firsts = []
    for _ in range(TOP_K):
        m = jnp.max(cur, axis=0, keepdims=True)
        first = jnp.min(jnp.where(cur == m, iota_e, N_EXPERTS), axis=0, keepdims=True)
        pick = iota_e == first
        sel = jnp.where(pick, 1.0, sel)
        cur = jnp.where(pick, neg_inf, cur)
        picks.append(pick)
        firsts.append(first)
    wsel = sel * scores
    denom = jnp.sum(wsel, axis=0, keepdims=True) + 1e-20
    gates = (ROUTED_SCALE * wsel) / denom

    upto = (lax.broadcasted_iota(jnp.int32, (tm, tm), 0) <= lax.broadcasted_iota(jnp.int32, (tm, tm), 1))
    csum = jnp.dot(sel.astype(BF16), jnp.where(upto, 1.0, 0.0).astype(BF16), preferred_element_type=F32)
    before = base_cnt + csum - sel
    iota_k = lax.broadcasted_iota(jnp.int32, (TOP_K, tm), 0)
    eidx = jnp.zeros((TOP_K, tm), jnp.int32)
    rank = jnp.zeros((TOP_K, tm), F32)
    w_k = jnp.zeros((TOP_K, tm), F32)
    for i in range(TOP_K):
        eidx = jnp.where(iota_k == i, firsts[i], eidx)
        rank = jnp.where(iota_k == i, jnp.sum(jnp.where(picks[i], before, 0.0), axis=0, keepdims=True), rank)
        w_k = jnp.where(iota_k == i, jnp.sum(jnp.where(picks[i], gates, 0.0), axis=0, keepdims=True), w_k)
    return gates, eidx, rank.astype(jnp.int32), w_k, jnp.sum(sel, axis=1, keepdims=True)


def _merge_kernel(ya_ref, yb_ref, mg_ref, x_ref, ga1_ref, sc2_ref, sh2_ref, pa_ref, pb_ref, wo_ref,
                  g1_ref, b1_ref, rwt_ref, rb_ref, x1_ref, h2_ref, gt_ref, hp_ref, eidx_ref, rank_ref, wk_ref,
                  cnt_ref, cnt_s, *, dn_alpha):
    bb, tl, d = x_ref.shape
    tm = bb * tl

    @pl.when((pl.program_id(0) == 0) & (pl.program_id(1) == 0))
    def _():
        cnt_s[...] = jnp.zeros_like(cnt_s)

    ya = ya_ref[...].reshape(tm, V_A)
    yb = yb_ref[...].reshape(tm, D_B)
    mg = mg_ref[...].reshape(tm, 2 * d)
    merged = (jax.nn.sigmoid(mg[:, :d]) * jnp.dot(ya, pa_ref[...], preferred_element_type=F32)
              + jax.nn.sigmoid(mg[:, d:]) * jnp.dot(yb, pb_ref[...], preferred_element_type=F32))
    mix = _dot(merged, wo_ref[...])
    y = dn_alpha * x_ref[...] + (1.0 + ga1_ref[...]) * mix.reshape(bb, tl, d)
    x1 = _layernorm(y, g1_ref[...], b1_ref[...])
    x1_ref[...] = x1
    h2 = (x1 * (1.0 + sc2_ref[...]) + sh2_ref[...]).astype(BF16)
    h2_ref[...] = h2
    h2f = h2.reshape(tm, d)
    hp_ref[...] = _pack_bf16_pairs(h2f.astype(F32))
    logits_t = lax.dot_general(rwt_ref[...], h2f, (_NT, ((), ())), preferred_element_type=F32)
    gates, eidx, rank, w_k, tile_cnt = _route(logits_t, rb_ref[...], cnt_s[...])
    gt_ref[...] = gates
    eidx_ref[...] = eidx
    rank_ref[...] = rank
    wk_ref[...] = w_k
    cnt_s[...] = cnt_s[...] + tile_cnt
    cnt_ref[...] = cnt_s[...].astype(jnp.int32)


def _merge(ya, yb, proj, x, ga1, sc2, sh2, p_a, p_b, w_o, ln_g, ln_b, rw_t, r_bias, bb, tl, dn_alpha):
    b, l, d = x.shape
    nj = l // tl
    tm = bb * tl
    t_all = b * l
    full = lambda *shape: pl.BlockSpec(shape, lambda i, j: (0,) * len(shape))
    mod = pl.BlockSpec((bb, 1, d), lambda i, j: (i, 0, 0))
    tok = lambda width, dtype=None: pl.BlockSpec((bb, tl, width), lambda i, j: (i, j, 0))
    per_k = pl.BlockSpec((TOP_K, tm), lambda i, j: (0, i * nj + j))
    return pl.pallas_call(
        functools.partial(_merge_kernel, dn_alpha=dn_alpha),
        grid=(b // bb, nj),
        in_specs=[tok(V_A), tok(D_B),
                  pl.BlockSpec((bb, tl, 2 * d), lambda i, j: (i, j, P_MG // (2 * d))),
                  tok(d), mod, mod, mod,
                  full(V_A, d), full(D_B, d), full(d, d), full(1, d), full(1, d),
                  full(N_EXPERTS, d), full(N_EXPERTS, 1)],
        out_specs=[tok(d), tok(d), pl.BlockSpec((N_EXPERTS, tm), lambda i, j: (0, i * nj + j)),
                   pl.BlockSpec((tm, d // 2), lambda i, j: (i * nj + j, 0)),
                   per_k, per_k, per_k, full(N_EXPERTS, 1)],
        out_shape=[jax.ShapeDtypeStruct((b, l, d), F32),
                   jax.ShapeDtypeStruct((b, l, d), BF16),
                   jax.ShapeDtypeStruct((N_EXPERTS, t_all), F32),
                   jax.ShapeDtypeStruct((t_all, d // 2), jnp.int32),
                   jax.ShapeDtypeStruct((TOP_K, t_all), jnp.int32),
                   jax.ShapeDtypeStruct((TOP_K, t_all), jnp.int32),
                   jax.ShapeDtypeStruct((TOP_K, t_all), F32),
                   jax.ShapeDtypeStruct((N_EXPERTS, 1), jnp.int32)],
        scratch_shapes=[pltpu.VMEM((N_EXPERTS, 1), F32)],
        compiler_params=_cparams(("arbitrary", "arbitrary"), 48),
        name="merge",
    )(ya, yb, proj, x, ga1, sc2, sh2, p_a, p_b, w_o, ln_g, ln_b, rw_t, r_bias)


def _moe_kernel(h_ref, g_ref, x1_ref, ga2_ref, wgu_ref, wd_ref, sgu_ref, sd_ref, g2_ref, b2_ref,
                o_ref, acc_ref, *, dn_alpha):
    e = pl.program_id(2)
    bb, tl, d = h_ref.shape
    tm = bb * tl
    h = h_ref[...].reshape(tm, d)

    @pl.when(e == 0)
    def _():
        su = jnp.dot(h, sgu_ref[...], preferred_element_type=F32)
        act = _silu(su[:, :D_SHARED]) * su[:, D_SHARED:]
        acc_ref[...] = _dot(act, sd_ref[...])

    lane = lax.broadcasted_iota(jnp.int32, (tm, N_EXPERTS), 1)
    gates = g_ref[...]
    acts = []
    for i in range(EXPERTS_PER_STEP):
        gu = jnp.dot(h, wgu_ref[i], preferred_element_type=F32)
        gate = _lane_pick(gates, lane, e * EXPERTS_PER_STEP + i)
        acts.append((_silu(gu[:, :D_EXPERT]) * gu[:, D_EXPERT:] * gate).astype(BF16))
    wd = wd_ref[...].reshape(EXPERTS_PER_STEP * D_EXPERT, d)
    acc_ref[...] += jnp.dot(jnp.concatenate(acts, axis=1), wd, preferred_element_type=F32)

    @pl.when(e == pl.num_programs(2) - 1)
    def _():
        y = dn_alpha * x1_ref[...] + (1.0 + ga2_ref[...]) * acc_ref[...].reshape(bb, tl, d)
        o_ref[...] = _layernorm(y, g2_ref[...], b2_ref[...])


def _moe(h2, gates, x1, ga2, we_gu, we_down, ws_gu, ws_down, ln_g, ln_b, bb, tl, dn_alpha):
    b, l, d = x1.shape
    nj = l // tl
    tm = bb * tl
    full = lambda *shape: pl.BlockSpec(shape, lambda i, j, e: (0,) * len(shape))
    tok = pl.BlockSpec((bb, tl, d), lambda i, j, e: (i, j, 0))
    return pl.pallas_call(
        functools.partial(_moe_kernel, dn_alpha=dn_alpha),
        grid=(b // bb, nj, N_EXPERTS // EXPERTS_PER_STEP),
        in_specs=[tok,
                  pl.BlockSpec((tm, N_EXPERTS), lambda i, j, e: (i * nj + j, 0)),
                  tok,
                  pl.BlockSpec((bb, 1, d), lambda i, j, e: (i, 0, 0)),
                  pl.BlockSpec((EXPERTS_PER_STEP, d, 2 * D_EXPERT), lambda i, j, e: (e, 0, 0)),
                  pl.BlockSpec((EXPERTS_PER_STEP, D_EXPERT, d), lambda i, j, e: (e, 0, 0)),
                  full(d, 2 * D_SHARED), full(D_SHARED, d), full(1, d), full(1, d)],
        out_specs=tok,
        out_shape=jax.ShapeDtypeStruct((b, l, d), F32),
        scratch_shapes=[pltpu.VMEM((tm, d), F32)],
        compiler_params=_cparams(("parallel", "parallel", "arbitrary"), 56),
        name="moe",
    )(h2, gates, x1, ga2, we_gu, we_down, ws_gu, ws_down, ln_g, ln_b)


def _sc_mesh():
    return plsc.VectorSubcoreMesh(core_axis_name="c", subcore_axis_name="s")


def _sc_scatter_rows(x, pos, n_out):
    t, w = x.shape
    k = pos.shape[0]
    t_per_w = t // SC_WORKERS
    n_cores = plsc.get_sparse_core_info().num_cores

    @functools.partial(
        pl.kernel, mesh=_sc_mesh(),
        out_type=jax.ShapeDtypeStruct((n_out, w), jnp.int32),
        scratch_types=[pltpu.VMEM((k, SC_CHUNK), jnp.int32), pltpu.VMEM((SC_CHUNK, w), jnp.int32),
                       pltpu.SemaphoreType.DMA],
    )
    def scatter_kernel(x_hbm, pos_hbm, out_hbm, idx_v, rows_v, sem):
        base = (lax.axis_index("s") * n_cores + lax.axis_index("c")) * t_per_w

        @pl.loop(0, t_per_w // SC_CHUNK)
        def _(i):
            off = pl.multiple_of(base + i * SC_CHUNK, SC_CHUNK)
            pltpu.sync_copy(pos_hbm.at[:, pl.ds(off, SC_CHUNK)], idx_v)
            pltpu.sync_copy(x_hbm.at[pl.ds(off, SC_CHUNK)], rows_v)
            for j in range(k):
                pltpu.async_copy(rows_v, out_hbm.at[idx_v.at[j]], sem).wait()

    return scatter_kernel(x, pos)


def _sc_gather_rows(table, idx):
    n = idx.shape[0]
    w = table.shape[1]
    n_per_w = n // SC_WORKERS
    n_cores = plsc.get_sparse_core_info().num_cores

    @functools.partial(
        pl.kernel, mesh=_sc_mesh(),
        out_type=jax.ShapeDtypeStruct((n, w), jnp.int32),
        scratch_types=[pltpu.VMEM((SC_CHUNK,), jnp.int32), pltpu.VMEM((SC_CHUNK, w), jnp.int32),
                       pltpu.SemaphoreType.DMA],
    )
    def gather_kernel(table_hbm, idx_hbm, out_hbm, idx_v, rows_v, sem):
        base = (lax.axis_index("s") * n_cores + lax.axis_index("c")) * n_per_w

        @pl.loop(0, n_per_w // SC_CHUNK)
        def _(i):
            off = pl.multiple_of(base + i * SC_CHUNK, SC_CHUNK)
            pltpu.sync_copy(idx_hbm.at[pl.ds(off, SC_CHUNK)], idx_v)
            pltpu.async_copy(table_hbm.at[idx_v], rows_v, sem).wait()
            pltpu.sync_copy(rows_v, out_hbm.at[pl.ds(off, SC_CHUNK)])

    return gather_kernel(table, idx)


def _experts_kernel(te_ref, used_ref, xs_ref, wgu_ref, wd_ref, o_ref):
    @pl.when(pl.program_id(0) < used_ref[0])
    def _():
        x = _unpack_bf16_pairs(xs_ref[...]).astype(BF16)
        gu = jnp.dot(x, wgu_ref[0], preferred_element_type=F32)
        act = _silu(gu[:, :D_EXPERT]) * gu[:, D_EXPERT:]
        o_ref[...] = _pack_bf16_pairs(_dot(act, wd_ref[0]))


def _experts(xs, tile_expert, tiles_used, we_gu, we_down):
    r, half = xs.shape
    d = 2 * half
    grid_spec = pltpu.PrefetchScalarGridSpec(
        num_scalar_prefetch=2,
        grid=(r // EXPERT_TILE,),
        in_specs=[pl.BlockSpec((EXPERT_TILE, half), lambda i, te, used: (i, 0)),
                  pl.BlockSpec((1, d, 2 * D_EXPERT), lambda i, te, used: (te[i], 0, 0)),
                  pl.BlockSpec((1, D_EXPERT, d), lambda i, te, used: (te[i], 0, 0))],
        out_specs=pl.BlockSpec((EXPERT_TILE, half), lambda i, te, used: (i, 0)),
    )
    return pl.pallas_call(
        _experts_kernel,
        grid_spec=grid_spec,
        out_shape=jax.ShapeDtypeStruct((r, half), jnp.int32),
        compiler_params=_cparams(("arbitrary",), 32),
        name="experts",
    )(tile_expert, tiles_used, xs, we_gu, we_down)


def _combine_kernel(og_ref, wk_ref, h_ref, x1_ref, ga2_ref, sgu_ref, sd_ref, g2_ref, b2_ref, o_ref, *, dn_alpha):
    bb, tl, d = h_ref.shape
    tm = bb * tl
    h = h_ref[...].reshape(tm, d)
    su = jnp.dot(h, sgu_ref[...], preferred_element_type=F32)
    acc = _dot(_silu(su[:, :D_SHARED]) * su[:, D_SHARED:], sd_ref[...])
    lane = lax.broadcasted_iota(jnp.int32, (tm, TOP_K), 1)
    w_all = wk_ref[...]
    for k in range(TOP_K):
        acc = acc + _lane_pick(w_all, lane, k) * _unpack_bf16_pairs(og_ref[k])
    y = dn_alpha * x1_ref[...] + (1.0 + ga2_ref[...]) * acc.reshape(bb, tl, d)
    o_ref[...] = _layernorm(y, g2_ref[...], b2_ref[...])


def _combine(og, w_tk, h2, x1, ga2, ws_gu, ws_down, ln_g, ln_b, bb, tl, dn_alpha):
    b, l, d = x1.shape
    nj = l // tl
    tm = bb * tl
    full = lambda *shape: pl.BlockSpec(shape, lambda i, j: (0,) * len(shape))
    tok = pl.BlockSpec((bb, tl, d), lambda i, j: (i, j, 0))
    return pl.pallas_call(
        functools.partial(_combine_kernel, dn_alpha=dn_alpha),
        grid=(b // bb, nj),
        in_specs=[pl.BlockSpec((TOP_K, tm, d // 2), lambda i, j: (0, i * nj + j, 0)),
                  pl.BlockSpec((tm, TOP_K), lambda i, j: (i * nj + j, 0)),
                  tok, tok,
                  pl.BlockSpec((bb, 1, d), lambda i, j: (i, 0, 0)),
                  full(d, 2 * D_SHARED), full(D_SHARED, d), full(1, d), full(1, d)],
        out_specs=tok,
        out_shape=jax.ShapeDtypeStruct((b, l, d), F32),
        compiler_params=_cparams(("parallel", "parallel"), 48),
        name="combine",
    )(og, w_tk, h2, x1, ga2, ws_gu, ws_down, ln_g, ln_b)


def _moe_sparse(hp, eidx, rank, w_k, cnt, h2, x1, ga2, p, bb, dn_alpha):
    t_all = hp.shape[0]
    n_tiles = t_all * TOP_K // EXPERT_TILE + N_EXPERTS
    seg_tiles = (cnt[:, 0] + EXPERT_TILE - 1) // EXPERT_TILE
    seg_end = jnp.cumsum(seg_tiles)
    seg_start = (seg_end - seg_tiles) * EXPERT_TILE
    pos = jnp.take(seg_start, eidx, axis=0) + rank
    tile_expert = jnp.minimum(jnp.searchsorted(seg_end, jnp.arange(n_tiles, dtype=jnp.int32), side='right'),
                              N_EXPERTS - 1).astype(jnp.int32)
    xs = _sc_scatter_rows(hp, pos, n_tiles * EXPERT_TILE)
    outs = _experts(xs, tile_expert, seg_end[-1:].astype(jnp.int32), p['we_gu'], p['we_down'])
    og = _sc_gather_rows(outs, pos.reshape(-1)).reshape(TOP_K, t_all, hp.shape[1])
    return _combine(og, w_k.T, h2, x1, ga2, p['ws_gu'], p['ws_down'], p['ln2_g'], p['ln2_b'], bb, COMBINE_TL, dn_alpha)


def _pad_rows(buf, width):
    b, n, w = buf.shape
    return jnp.pad(buf.astype(F32), ((0, 0), (SUBLANES - n, 0), (0, width - w)))


def _to_pairs(s):
    b = s.shape[0]
    s = s.astype(F32).reshape(b, PAIRS, 2, N_B, N_B)
    zero = jnp.zeros_like(s[:, :, 0])
    top = jnp.concatenate([s[:, :, 0], zero], axis=-1)
    bot = jnp.concatenate([zero, s[:, :, 1]], axis=-1)
    return jnp.concatenate([top, bot], axis=-2)


def _from_pairs(sp):
    b = sp.shape[0]
    return jnp.stack([sp[:, :, :N_B, :N_B], sp[:, :, N_B:, N_B:]], axis=2).reshape(b, H_B, N_B, N_B)


def _layer(x, mod, conv_buf, gdn_s, shift_buf, rwkv_s, p, tiles, dn_alpha):
    b, l, d = x.shape
    sh1, sc1, ga1, sh2, sc2, ga2 = mod
    proj = _inproj(x, sc1, sh1, p['w_in'], tiles['bb'], tiles['tl'])

    lp = -(-l // CHUNK) * CHUNK
    proj_r = proj if lp == l else jnp.pad(proj, ((0, 0), (0, lp - l), (0, 0)))
    tl_r = min(tiles['tl_rec'], lp)
    ya, gdn_new = _gdn(proj_r, _pad_rows(conv_buf, QKV_A), gdn_s.astype(F32), p['conv_w'], p['alog_row'],
                       p['dt_row'], p['gdn_norm_w'], tiles['bb_rec'], tl_r, l)
    yb, rwkv_pairs = _rwkv(proj_r, _pad_rows(shift_buf, RW_BLOCK), _to_pairs(rwkv_s), p['rwkv_vecs'],
                           p['w2p'], p['a2p'], p['g2'], tiles['bb_rec'], tl_r, l)
    if lp != l:
        ya, yb = ya[:, :l], yb[:, :l]

    x1, h2, gates_t, hp, eidx, rank, w_k, cnt = _merge(
        ya, yb, proj, x, ga1, sc2, sh2, p['p_a'], p['p_b'], p['w_o'], p['ln1_g'], p['ln1_b'],
        p['router_wt'], p['router_bias'], tiles['bb'], tiles['tl_merge'], dn_alpha)
    if (b * l) % (SC_WORKERS * SC_CHUNK) == 0:
        out = _moe_sparse(hp, eidx, rank, w_k, cnt, h2, x1, ga2, p, tiles['bb'], dn_alpha)
    else:
        out = _moe(h2, gates_t.T, x1, ga2, p['we_gu'], p['we_down'], p['ws_gu'], p['ws_down'], p['ln2_g'],
                   p['ln2_b'], tiles['bb'], tiles['tl'], dn_alpha)

    pre = jnp.concatenate([conv_buf.astype(F32), proj[:, :, P_QKV:P_QKV + QKV_A]], axis=1)
    conv_new = pre[:, -(CONV_W - 1):]
    shift_new = proj[:, l - 1:l, P_RW:P_RW + SHIFT_W]
    return (out, conv_new.astype(conv_buf.dtype), gdn_new.astype(gdn_s.dtype),
            shift_new.astype(shift_buf.dtype), _from_pairs(rwkv_pairs).astype(rwkv_s.dtype))


def _prep_params(l, w_in, conv_w, a_log, dt_bias, gdn_norm_w, mu_shift, w0, w2, a0, a2, g2, k_k, k_a, r_k,
                 lnx_g, lnx_b, p_a, p_b, w_o, ln1_g, ln1_b, router_w, router_bias, we_gate, we_up, we_down,
                 ws_gate, ws_up, ws_down, ln2_g, ln2_b):
    d = D_MODEL
    w = w_in[l]
    w_p = jnp.concatenate(
        [w[:, :QKV_A], w[:, OFF_Z:OFF_RWKV], w[:, OFF_RWKV:OFF_MERGE],
         jnp.zeros((d, RW_BLOCK - SHIFT_W), w.dtype), w[:, OFF_MERGE:], w[:, OFF_ALPHA:OFF_Z],
         jnp.zeros((d, AB_BLOCK - 2 * H_A), w.dtype)], axis=1).astype(BF16)
    row = lambda v, width: jnp.pad(v.astype(F32).reshape(1, -1), ((0, 0), (0, width - v.size)))
    zeros_lora = jnp.zeros((W_LORA, D_B), F32)
    return {
        'w_in': w_p,
        'conv_w': conv_w[l].astype(F32),
        'alog_row': row(a_log[l], LANES),
        'dt_row': row(dt_bias[l], LANES),
        'gdn_norm_w': row(gdn_norm_w[l], DK_A),
        'rwkv_vecs': (row(mu_shift[l], RW_BLOCK), row(w0[l], D_B), row(a0[l], D_B), row(k_k[l], D_B),
                      row(k_a[l], D_B), row(r_k[l], D_B), row(lnx_g[l], D_B), row(lnx_b[l], D_B)),
        'w2p': jnp.concatenate([w2[l].astype(F32), zeros_lora], axis=0),
        'a2p': jnp.concatenate([zeros_lora, a2[l].astype(F32)], axis=0),
        'g2': g2[l].astype(F32),
        'p_a': p_a[l].astype(BF16), 'p_b': p_b[l].astype(BF16), 'w_o': w_o[l].astype(BF16),
        'ln1_g': row(ln1_g[l], d), 'ln1_b': row(ln1_b[l], d),
        'router_wt': router_w[l].T.astype(BF16),
        'router_bias': router_bias[l].astype(F32).reshape(N_EXPERTS, 1),
        'we_gu': jnp.concatenate([we_gate[l], we_up[l]], axis=-1).astype(BF16),
        'we_down': we_down[l].astype(BF16),
        'ws_gu': jnp.concatenate([ws_gate[l], ws_up[l]], axis=-1).astype(BF16),
        'ws_down': ws_down[l].astype(BF16),
        'ln2_g': row(ln2_g[l], d), 'ln2_b': row(ln2_b[l], d),
    }


def kernel(x_prompt, x_sample, c_prompt, c_sample, state_gdn_conv, state_gdn, state_rwkv_shift, state_rwkv, w_ada, b_ada, w_in, conv_w, a_log, dt_bias, gdn_norm_w, mu_shift, w0, w2, a0, a2, g2, k_k, k_a, r_k, lnx_g, lnx_b, p_a, p_b, w_o, ln1_g, ln1_b, router_w, router_bias, we_gate, we_up, we_down, ws_gate, ws_up, ws_down, ln2_g, ln2_b):
    depth = w_ada.shape[0]
    dn_alpha = (2 * depth) ** 0.25
    bp, lp_, d = x_prompt.shape
    bs, ls, _ = x_sample.shape
    dtp = x_prompt.dtype
    tiles_p = {'bb': 1, 'tl': 1024, 'bb_rec': 2, 'tl_rec': 256, 'tl_merge': 512}
    tiles_s = {'bb': bs, 'tl': ls, 'bb_rec': 2, 'tl_rec': CHUNK, 'tl_merge': ls}

    yp, ys = x_prompt, x_sample
    new_p = ([], [], [], [])
    new_s = ([], [], [], [])
    for l in range(depth):
        p = _prep_params(l, w_in, conv_w, a_log, dt_bias, gdn_norm_w, mu_shift, w0, w2, a0, a2, g2, k_k, k_a,
                         r_k, lnx_g, lnx_b, p_a, p_b, w_o, ln1_g, ln1_b, router_w, router_bias, we_gate, we_up,
                         we_down, ws_gate, ws_up, ws_down, ln2_g, ln2_b)
        mod = _ada(jnp.concatenate([c_prompt, c_sample], axis=0), w_ada[l], b_ada[l])
        mod_p = tuple(m[:, None, :] for m in jnp.split(mod[:bp], 6, axis=-1))
        mod_s = tuple(m[:, None, :] for m in jnp.split(mod[bp:], 6, axis=-1))
        yp, *sp = _layer(yp, mod_p,
                         jnp.zeros((bp, CONV_W - 1, QKV_A), dtp), jnp.zeros((bp, H_A, DK_A, DK_A), dtp),
                         jnp.zeros((bp, 1, SHIFT_W), dtp), jnp.zeros((bp, H_B, N_B, N_B), dtp),
                         p, tiles_p, dn_alpha)
        ys, *ss = _layer(ys, mod_s, state_gdn_conv[l], state_gdn[l], state_rwkv_shift[l], state_rwkv[l],
                         p, tiles_s, dn_alpha)
        for lst, val in zip(new_p, sp):
            lst.append(val)
        for lst, val in zip(new_s, ss):
            lst.append(val)
    conv_p, gdn_p, shift_p, rwkv_p = [jnp.stack(t, 0) for t in new_p]
    conv_s, gdn_s, shift_s, rwkv_s = [jnp.stack(t, 0) for t in new_s]
    return (yp, ys, conv_p, gdn_p, shift_p, rwkv_p, conv_s, gdn_s, shift_s, rwkv_s)
```

```python
import functools
import math

import jax
import jax.numpy as jnp
from jax import lax
from jax.experimental import pallas as pl
from jax.experimental.pallas import tpu as pltpu
from jax.experimental.pallas import tpu_sc as plsc

F32 = jnp.float32
BF16 = jnp.bfloat16

D_MODEL = 1024
DK_A = 128
H_A = 4
QK_A = H_A * DK_A
V_A = H_A * DK_A
QKV_A = 2 * QK_A + V_A
CONV_W = 4
N_B = 64
H_B = 8
D_B = H_B * N_B
W_LORA = 64
A_LORA = 64
G_LORA = 128
SHIFT_W = 3 * D_B + W_LORA + A_LORA + G_LORA
OFF_ALPHA = QKV_A
OFF_BETA = OFF_ALPHA + H_A
OFF_Z = OFF_BETA + H_A
OFF_RWKV = OFF_Z + V_A
OFF_MERGE = OFF_RWKV + SHIFT_W
N_EXPERTS = 64
TOP_K = 8
N_GROUPS = 8
GROUP_SIZE = N_EXPERTS // N_GROUPS
TOPK_GROUPS = 4
D_EXPERT = 256
D_SHARED = 256
ROUTED_SCALE = 2.5
LN_EPS = 1e-5
GN_EPS = 64e-5
RMS_EPS = 1e-6
DECAY_SCALE = -math.exp(-0.5)

SUBLANES = 8
LANES = 128

P_QKV = 0
P_Z = QKV_A
P_RW = 2048
RW_BLOCK = 2048
P_MG = P_RW + RW_BLOCK
P_AB = P_MG + 2 * D_MODEL
AB_BLOCK = LANES
N_PROJ = P_AB + AB_BLOCK
PROJ_TN = 896

CHUNK = 128
BASE_BLOCK = 8
CUMSUM_PARTS = 3
STAT_PARTS = 2
EXPERTS_PER_STEP = 2
HIGH_HALF = -65536

EXPERT_TILE = 512
SC_WORKERS = 32
SC_CHUNK = 128
COMBINE_TL = 512


def _cparams(sem, vmem_mb):
    return pltpu.CompilerParams(dimension_semantics=sem, vmem_limit_bytes=vmem_mb * 1024 * 1024)


def _dot(a, b):
    return jnp.dot(a.astype(BF16), b.astype(BF16), preferred_element_type=F32)


def _rdot(a, b, dims=((1,), (0,))):
    return lax.dot_general(a.astype(BF16), b.astype(BF16), (dims, ((), ())), preferred_element_type=F32)


_NT = ((1,), (1,))
_TN = ((0,), (0,))


def _bf16_parts(x, parts):
    out = []
    rem = x
    for _ in range(parts):
        hi = rem.astype(BF16)
        out.append(hi)
        rem = rem - hi.astype(F32)
    return out


def _mask_dot_left(mask, x, parts):
    return sum(jnp.dot(mask, p, preferred_element_type=F32) for p in _bf16_parts(x, parts))


def _mask_dot_right(x, mask, parts):
    return sum(jnp.dot(p, mask, preferred_element_type=F32) for p in _bf16_parts(x, parts))


def _lane_pick(x, lane_iota, lane):
    return jnp.sum(jnp.where(lane_iota == lane, x, 0.0), axis=-1, keepdims=True)


def _pack_bf16_pairs(x):
    n = x.shape[1] // 2
    bits = lax.bitcast_convert_type(x.astype(BF16).astype(F32), jnp.int32)
    return (bits[:, :n] & HIGH_HALF) | lax.shift_right_logical(bits[:, n:], 16)


def _unpack_bf16_pairs(p):
    hi = lax.bitcast_convert_type(p & HIGH_HALF, F32)
    lo = lax.bitcast_convert_type(lax.shift_left(p, 16), F32)
    return jnp.concatenate([hi, lo], axis=1)


def _silu(x):
    return x * jax.nn.sigmoid(x)


def _softplus(x):
    return jnp.maximum(x, 0.0) + jnp.log1p(jnp.exp(-jnp.abs(x)))


def _tri_inverse(mats, eye, sign, row, col):
    base = (row // BASE_BLOCK) == (col // BASE_BLOCK)
    ds = [jnp.where(base, a, 0.0) for a in mats]
    xs = [eye + d if sign > 0 else eye - d for d in ds]
    power = 2
    while power < BASE_BLOCK:
        ds = [_rdot(d, d) for d in ds]
        xs = [x + _rdot(x, d) for x, d in zip(xs, ds)]
        power *= 2
    b = BASE_BLOCK
    while b < CHUNK:
        sibling = ((row // b) == (col // b) + 1) & ((row // (2 * b)) == (col // (2 * b)))
        offs = [jnp.where(sibling, a, 0.0) for a in mats]
        ts = [_rdot(o, x) for o, x in zip(offs, xs)]
        if sign > 0:
            xs = [x + _rdot(x, t_) for x, t_ in zip(xs, ts)]
        else:
            xs = [x - _rdot(x, t_) for x, t_ in zip(xs, ts)]
        b *= 2
    return xs


def _tri_consts():
    row = lax.broadcasted_iota(jnp.int32, (CHUNK, CHUNK), 0)
    col = lax.broadcasted_iota(jnp.int32, (CHUNK, CHUNK), 1)
    incl = row >= col
    strict = row > col
    eye = jnp.where(row == col, 1.0, 0.0).astype(F32)
    tril = jnp.where(incl, 1.0, 0.0).astype(BF16)
    return row, col, incl, strict, eye, tril


def _ada_kernel(c_ref, w_ref, b_ref, o_ref):
    o_ref[...] = _dot(_silu(c_ref[...]), w_ref[...]) + b_ref[...]


def _ada(c, w_ada, b_ada):
    n, d = c.shape
    nout = w_ada.shape[1]
    tn = 768
    return pl.pallas_call(
        _ada_kernel,
        grid=(nout // tn,),
        in_specs=[pl.BlockSpec((n, d), lambda j: (0, 0)),
                  pl.BlockSpec((d, tn), lambda j: (0, j)),
                  pl.BlockSpec((1, tn), lambda j: (0, j))],
        out_specs=pl.BlockSpec((n, tn), lambda j: (0, j)),
        out_shape=jax.ShapeDtypeStruct((n, nout), F32),
        compiler_params=_cparams(("parallel",), 32),
        name="ada",
    )(c, w_ada, b_ada.reshape(1, nout))


def _inproj_kernel(x_ref, sc_ref, sh_ref, w_ref, o_ref, hs_ref):
    @pl.when(pl.program_id(2) == 0)
    def _():
        h = x_ref[...] * (1.0 + sc_ref[...]) + sh_ref[...]
        hs_ref[...] = h.reshape(hs_ref.shape).astype(BF16)

    o = jnp.dot(hs_ref[...], w_ref[...], preferred_element_type=F32)
    o_ref[...] = o.reshape(o_ref.shape)


def _inproj(x, sc, sh, w_p, bb, tl):
    b, l, d = x.shape
    return pl.pallas_call(
        _inproj_kernel,
        grid=(b // bb, l // tl, N_PROJ // PROJ_TN),
        in_specs=[pl.BlockSpec((bb, tl, d), lambda i, j, n: (i, j, 0)),
                  pl.BlockSpec((bb, 1, d), lambda i, j, n: (i, 0, 0)),
                  pl.BlockSpec((bb, 1, d), lambda i, j, n: (i, 0, 0)),
                  pl.BlockSpec((d, PROJ_TN), lambda i, j, n: (0, n))],
        out_specs=pl.BlockSpec((bb, tl, PROJ_TN), lambda i, j, n: (i, j, n)),
        out_shape=jax.ShapeDtypeStruct((b, l, N_PROJ), F32),
        scratch_shapes=[pltpu.VMEM((bb * tl, d), BF16)],
        compiler_params=_cparams(("parallel", "parallel", "arbitrary"), 48),
        name="inproj",
    )(x, sc, sh, w_p)


def _gdn_kernel(qkv_ref, ab_ref, z_ref, cbuf_ref, s0_ref, cw_ref, alog_ref, dt_ref, nw_ref,
                ya_ref, sout_ref, ext_ref, qc_ref, s_ref, *, bb, tl, l_valid, l_total):
    t = pl.program_id(1)

    @pl.when(t == 0)
    def _():
        ext_ref[:, 0:SUBLANES, :] = cbuf_ref[...]
        s_ref[...] = s0_ref[...]

    first = SUBLANES - (CONV_W - 1)
    for bi in range(bb):
        ext_ref[bi, SUBLANES:SUBLANES + tl, :] = qkv_ref[bi]
        acc = cw_ref[0:1, :] * ext_ref[bi, first:first + tl, :]
        for j in range(1, CONV_W):
            acc = acc + cw_ref[j:j + 1, :] * ext_ref[bi, first + j:first + j + tl, :]
        ext_ref[bi, 0:SUBLANES, :] = ext_ref[bi, tl:tl + SUBLANES, :]
        qc_ref[bi] = _silu(acc)

    row, col, incl, strict, eye, tril = _tri_consts()
    neg_a = -jnp.exp(alog_ref[...])
    dt = dt_ref[...]
    nw = nw_ref[...]
    chains = [(bi, h) for bi in range(bb) for h in range(H_A)]
    n = range(len(chains))

    def chunk(c, carry):
        r0 = pl.multiple_of(c * CHUNK, CHUNK)
        rows = pl.ds(r0, CHUNK)
        g_cum = []
        b_all = []
        for bi in range(bb):
            ab = ab_ref[bi, rows, :]
            g_bi = neg_a * _softplus(ab + dt)
            b_bi = jax.nn.sigmoid(ab)
            if l_valid < l_total:
                valid = (t * tl + r0 + row) < l_valid
                g_bi = jnp.where(valid, g_bi, 0.0)
                b_bi = jnp.where(valid, b_bi, 0.0)
            g_cum.append(_mask_dot_left(tril, g_bi, CUMSUM_PARTS))
            b_all.append(b_bi)
        g_col = [_lane_pick(g_cum[bi], col, h) for bi, h in chains]
        beta = [_lane_pick(b_all[bi], col, H_A + h) for bi, h in chains]
        decay = []
        for i in n:
            g_b = jnp.broadcast_to(g_col[i], (CHUNK, CHUNK))
            decay.append(jnp.where(incl, jnp.exp(jnp.where(incl, g_b - g_b.T, 0.0)), 0.0))
        q = [qc_ref[bi, rows, h * DK_A:(h + 1) * DK_A] for bi, h in chains]
        k = [qc_ref[bi, rows, QK_A + h * DK_A:QK_A + (h + 1) * DK_A] for bi, h in chains]
        v = [qc_ref[bi, rows, 2 * QK_A + h * DK_A:2 * QK_A + (h + 1) * DK_A] for bi, h in chains]
        q = [x * lax.rsqrt(jnp.sum(x * x, -1, keepdims=True) + 1e-6) * (DK_A ** -0.5) for x in q]
        k = [x * lax.rsqrt(jnp.sum(x * x, -1, keepdims=True) + 1e-6) for x in k]
        kq = [_rdot(jnp.concatenate([k[i], q[i]], axis=0), k[i], _NT) for i in n]
        a = [jnp.where(strict, beta[i] * kq[i][:CHUNK] * decay[i], 0.0) for i in n]
        qk = [jnp.where(incl, kq[i][CHUNK:] * decay[i], 0.0) for i in n]
        x = _tri_inverse(a, eye, -1, row, col)
        e_g = [jnp.exp(g) for g in g_col]
        uw = [_rdot(x[i], jnp.concatenate([v[i] * beta[i], k[i] * (beta[i] * e_g[i])], axis=1)) for i in n]
        g_last = [g[CHUNK - 1:CHUNK, :] for g in g_col]
        kd = [k[i] * jnp.exp(g_last[i] - g_col[i]) for i in n]
        s = [s_ref[bi, h] for bi, h in chains]
        ws = [_rdot(jnp.concatenate([uw[i][:, DK_A:], q[i] * e_g[i]], axis=0), s[i]) for i in n]
        v_new = [uw[i][:, :DK_A] - ws[i][:CHUNK] for i in n]
        o = [ws[i][CHUNK:] + _rdot(qk[i], v_new[i]) for i in n]
        s_new = [s[i] * jnp.exp(g_last[i]) + _rdot(kd[i], v_new[i], _TN) for i in n]
        for i, (bi, h) in enumerate(chains):
            s_ref[bi, h] = s_new[i]
            z = z_ref[bi, rows, h * DK_A:(h + 1) * DK_A]
            on = o[i] * lax.rsqrt(jnp.mean(o[i] * o[i], -1, keepdims=True) + RMS_EPS) * nw * _silu(z)
            ya_ref[bi, rows, h * DK_A:(h + 1) * DK_A] = on.astype(BF16)
        return carry

    lax.fori_loop(0, tl // CHUNK, chunk, 0)

    @pl.when(t == pl.num_programs(1) - 1)
    def _():
        sout_ref[...] = s_ref[...]


def _gdn(proj, conv_buf8, s0, conv_w, alog_row, dt_row, norm_w, bb, tl, l_valid):
    b, l, _ = proj.shape
    kern = functools.partial(_gdn_kernel, bb=bb, tl=tl, l_valid=l_valid, l_total=l)
    full = lambda *shape: pl.BlockSpec(shape, lambda i, t: (0,) * len(shape))
    return pl.pallas_call(
        kern,
        grid=(b // bb, l // tl),
        in_specs=[pl.BlockSpec((bb, tl, QKV_A), lambda i, t: (i, t, P_QKV // QKV_A)),
                  pl.BlockSpec((bb, tl, AB_BLOCK), lambda i, t: (i, t, P_AB // AB_BLOCK)),
                  pl.BlockSpec((bb, tl, V_A), lambda i, t: (i, t, P_Z // V_A)),
                  pl.BlockSpec((bb, SUBLANES, QKV_A), lambda i, t: (i, 0, 0)),
                  pl.BlockSpec((bb, H_A, DK_A, DK_A), lambda i, t: (i, 0, 0, 0)),
                  full(CONV_W, QKV_A), full(1, LANES), full(1, LANES), full(1, DK_A)],
        out_specs=[pl.BlockSpec((bb, tl, V_A), lambda i, t: (i, t, 0)),
                   pl.BlockSpec((bb, H_A, DK_A, DK_A), lambda i, t: (i, 0, 0, 0))],
        out_shape=[jax.ShapeDtypeStruct((b, l, V_A), BF16),
                   jax.ShapeDtypeStruct((b, H_A, DK_A, DK_A), F32)],
        scratch_shapes=[pltpu.VMEM((bb, tl + SUBLANES, QKV_A), F32),
                        pltpu.VMEM((bb, tl, QKV_A), F32),
                        pltpu.VMEM((bb, H_A, DK_A, DK_A), F32)],
        compiler_params=_cparams(("parallel", "arbitrary"), 48),
        name="gdn",
    )(proj, proj, proj, conv_buf8, s0, conv_w, alog_row, dt_row, norm_w)


PAIRS = H_B // 2


def _headsum(x, bd):
    return jnp.concatenate(
        [_mask_dot_right(x[:, p * LANES:(p + 1) * LANES], bd, STAT_PARTS) for p in range(PAIRS)], axis=1)


def _rwkv_kernel(rw_ref, sbuf_ref, s0_ref, mu_ref, w0_ref, a0_ref, kk_ref, ka_ref, rk_ref, lg_ref, lb_ref,
                 w2_ref, a2_ref, g2_ref, yb_ref, sout_ref,
                 ext_ref, r_s, k_s, v_s, z_s, p_s, lw_s, y_s, bonus_s, gate_s, s_ref, *, bb, tl, l_valid, l_total):
    t = pl.program_id(1)

    @pl.when(t == 0)
    def _():
        ext_ref[:, 0:SUBLANES, :] = sbuf_ref[...]
        s_ref[...] = s0_ref[...]

    row, col, incl, strict, eye, tril = _tri_consts()
    same_head = (row // N_B) == (col // N_B)
    bd = jnp.where(same_head, 1.0, 0.0).astype(BF16)
    lane_a = col < N_B

    for bi in range(bb):
        cur = rw_ref[bi]
        ext_ref[bi, SUBLANES:SUBLANES + tl, :] = cur
        prev = ext_ref[bi, SUBLANES - 1:SUBLANES - 1 + tl, :]
        mixed = cur + (prev - cur) * mu_ref[...]
        ext_ref[bi, 0:SUBLANES, :] = ext_ref[bi, tl:tl + SUBLANES, :]
        r = mixed[:, 0:D_B]
        k = mixed[:, D_B:2 * D_B]
        v = mixed[:, 2 * D_B:3 * D_B]
        lora = mixed[:, 3 * D_B:3 * D_B + W_LORA + A_LORA]
        g_in = mixed[:, 3 * D_B + W_LORA + A_LORA:SHIFT_W]
        lw = DECAY_SCALE * jax.nn.sigmoid(w0_ref[...] + _dot(jnp.tanh(lora), w2_ref[...]))
        a = jax.nn.sigmoid(a0_ref[...] + _dot(lora, a2_ref[...]))
        kkr = k * kk_ref[...]
        kk = kkr * lax.rsqrt(_headsum(kkr * kkr, bd) + 1e-6)
        k = k * (1.0 + (a - 1.0) * ka_ref[...])
        tile_rows = pl.ds(bi * tl, tl)
        bonus_s[tile_rows, :] = _headsum(r * k * rk_ref[...], bd) * v
        gate_s[tile_rows, :] = _dot(jax.nn.sigmoid(g_in), g2_ref[...])
        if l_valid < l_total:
            rvalid = (t * tl + lax.broadcasted_iota(jnp.int32, (tl, 1), 0)) < l_valid
            lw = jnp.where(rvalid, lw, 0.0)
            kk = jnp.where(rvalid, kk, 0.0)
            k = jnp.where(rvalid, k, 0.0)
            v = jnp.where(rvalid, v, 0.0)
        r_s[tile_rows, :] = r
        k_s[tile_rows, :] = k
        v_s[tile_rows, :] = v
        z_s[tile_rows, :] = -kk
        p_s[tile_rows, :] = kk * a
        lw_s[tile_rows, :] = lw

    groups = [(bi, p) for bi in range(bb) for p in range(PAIRS)]
    pairs = range(len(groups))
    both = range(2 * len(groups))
    cols = [slice(p * LANES, (p + 1) * LANES) for _, p in groups]

    def chunk(c, carry):
        r0 = pl.multiple_of(c * CHUNK, CHUNK)
        rows = [pl.ds(bi * tl + r0, CHUNK) for bi, _ in groups]
        lw_c = [lw_s[rows[g], cols[g]] for g in pairs]
        g_inc = [_mask_dot_left(tril, x, CUMSUM_PARTS) for x in lw_c]
        g_exc = [g_inc[p] - lw_c[p] for p in pairs]
        g_mid = [g[CHUNK // 2 - 1:CHUNK // 2, :] for g in g_inc]
        g_end = [g[CHUNK - 1:CHUNK, :] for g in g_inc]
        z = [z_s[rows[g], cols[g]] for g in pairs]
        rr = [r_s[rows[g], cols[g]] for g in pairs]
        pp = [p_s[rows[g], cols[g]] for g in pairs]
        kk_ = [k_s[rows[g], cols[g]] for g in pairs]
        vv = [v_s[rows[g], cols[g]] for g in pairs]
        zt = [z[p] * jnp.exp(g_exc[p] - g_mid[p]) for p in pairs]
        rt = [rr[p] * jnp.exp(g_inc[p] - g_mid[p]) for p in pairs]
        en = [jnp.exp(g_mid[p] - g_inc[p]) for p in pairs]
        s = [s_ref[bi, p] for bi, p in groups]
        lhs = [jnp.concatenate([jnp.where(lane_a, zt[p], 0.0), jnp.where(lane_a, 0.0, zt[p]),
                                jnp.where(lane_a, rt[p], 0.0), jnp.where(lane_a, 0.0, rt[p])], axis=0) for p in pairs]
        m = [_rdot(lhs[p], jnp.concatenate([pp[p] * en[p], kk_[p] * en[p]], axis=0), _NT) for p in pairs]
        zr0 = [_rdot(jnp.concatenate([z[p] * jnp.exp(g_exc[p]), rr[p] * jnp.exp(g_inc[p])], axis=0), s[p], _NT)
               for p in pairs]
        mz = [m[i // 2][(i % 2) * CHUNK:(i % 2 + 1) * CHUNK] for i in both]
        azp = [jnp.where(strict, x[:, :CHUNK], 0.0) for x in mz]
        azk = [jnp.where(strict, x[:, CHUNK:], 0.0) for x in mz]
        minv = _tri_inverse(azp, eye, 1, row, col)
        rhs = [zr0[i // 2][:CHUNK] + _rdot(azk[i], vv[i // 2]) for i in both]
        u_h = [_rdot(minv[i], rhs[i]) for i in both]
        u = [jnp.where(lane_a, u_h[2 * p], u_h[2 * p + 1]) for p in pairs]
        uv = [jnp.concatenate([u[p], vv[p]], axis=0) for p in pairs]
        incl2 = jnp.concatenate([incl, incl], axis=1)
        y_h = [_rdot(jnp.where(incl2, m[i // 2][(2 + i % 2) * CHUNK:(3 + i % 2) * CHUNK], 0.0), uv[i // 2])
               for i in both]
        tail = [jnp.exp(g_end[p] - g_inc[p]) for p in pairs]
        s_new = [s[p] * jnp.exp(g_end[p])
                 + _rdot(uv[p], jnp.concatenate([pp[p] * tail[p], kk_[p] * tail[p]], axis=0), _TN) for p in pairs]
        for g, (bi, p) in enumerate(groups):
            s_ref[bi, p] = jnp.where(same_head, s_new[g], 0.0)
            y_s[rows[g], cols[g]] = zr0[g][CHUNK:] + jnp.where(lane_a, y_h[2 * g], y_h[2 * g + 1])
        return carry

    lax.fori_loop(0, tl // CHUNK, chunk, 0)

    for bi in range(bb):
        tile_rows = pl.ds(bi * tl, tl)
        y = y_s[tile_rows, :]
        mean = _headsum(y, bd) * (1.0 / N_B)
        dev = y - mean
        var = _headsum(dev * dev, bd) * (1.0 / N_B)
        yn = dev * lax.rsqrt(var + GN_EPS) * lg_ref[...] + lb_ref[...]
        yb_ref[bi] = ((yn + bonus_s[tile_rows, :]) * gate_s[tile_rows, :]).astype(BF16)

    @pl.when(t == pl.num_programs(1) - 1)
    def _():
        sout_ref[...] = s_ref[...]


def _rwkv(proj, shift_buf8, s0_pairs, vecs, w2p, a2p, g2, bb, tl, l_valid):
    b, l, _ = proj.shape
    kern = functools.partial(_rwkv_kernel, bb=bb, tl=tl, l_valid=l_valid, l_total=l)
    full = lambda *shape: pl.BlockSpec(shape, lambda i, t: (0,) * len(shape))
    mu, w0, a0, k_k, k_a, r_k, lnx_g, lnx_b = vecs
    return pl.pallas_call(
        kern,
        grid=(b // bb, l // tl),
        in_specs=[pl.BlockSpec((bb, tl, RW_BLOCK), lambda i, t: (i, t, P_RW // RW_BLOCK)),
                  pl.BlockSpec((bb, SUBLANES, RW_BLOCK), lambda i, t: (i, 0, 0)),
                  pl.BlockSpec((bb, PAIRS, LANES, LANES), lambda i, t: (i, 0, 0, 0)),
                  full(1, RW_BLOCK)] + [full(1, D_B)] * 7 +
                 [full(W_LORA + A_LORA, D_B), full(W_LORA + A_LORA, D_B), full(G_LORA, D_B)],
        out_specs=[pl.BlockSpec((bb, tl, D_B), lambda i, t: (i, t, 0)),
                   pl.BlockSpec((bb, PAIRS, LANES, LANES), lambda i, t: (i, 0, 0, 0))],
        out_shape=[jax.ShapeDtypeStruct((b, l, D_B), BF16),
                   jax.ShapeDtypeStruct((b, PAIRS, LANES, LANES), F32)],
        scratch_shapes=[pltpu.VMEM((bb, tl + SUBLANES, RW_BLOCK), F32)] +
                       [pltpu.VMEM((bb * tl, D_B), F32)] * 9 +
                       [pltpu.VMEM((bb, PAIRS, LANES, LANES), F32)],
        compiler_params=_cparams(("parallel", "arbitrary"), 48),
        name="rwkv",
    )(proj, shift_buf8, s0_pairs, mu, w0, a0, k_k, k_a, r_k, lnx_g, lnx_b, w2p, a2p, g2)


def _layernorm(y, g, b):
    mu = jnp.mean(y, -1, keepdims=True)
    dev = y - mu
    var = jnp.mean(dev * dev, -1, keepdims=True)
    return dev * lax.rsqrt(var + LN_EPS) * g + b


def _route(logits_t, bias, base_cnt):
    tm = logits_t.shape[1]
    scores = jax.nn.sigmoid(logits_t)
    choice = scores + bias
    neg_inf = -jnp.inf
    iota_g = lax.broadcasted_iota(jnp.int32, (GROUP_SIZE, tm), 0)
    group_score = []
    for g in range(N_GROUPS):
        xg = choice[g * GROUP_SIZE:(g + 1) * GROUP_SIZE, :]
        m1 = jnp.max(xg, axis=0, keepdims=True)
        first = jnp.min(jnp.where(xg == m1, iota_g, GROUP_SIZE), axis=0, keepdims=True)
        m2 = jnp.max(jnp.where(iota_g == first, neg_inf, xg), axis=0, keepdims=True)
        group_score.append(m1 + m2)
    masked = []
    for g in range(N_GROUPS):
        rank = jnp.zeros((1, tm), jnp.int32)
        for o in range(N_GROUPS):
            if o == g:
                continue
            ahead = group_score[o] > group_score[g]
            if o < g:
                ahead = ahead | (group_score[o] == group_score[g])
            rank = rank + ahead.astype(jnp.int32)
        keep = rank < TOPK_GROUPS
        masked.append(jnp.where(keep, choice[g * GROUP_SIZE:(g + 1) * GROUP_SIZE, :], neg_inf))
    cur = jnp.concatenate(masked, axis=0)
    iota_e = lax.broadcasted_iota(jnp.int32, (N_EXPERTS, tm), 0)
    sel = jnp.zeros((N_EXPERTS, tm), F32)
    picks = []
    firsts = []
    for _ in range(TOP_K):
        m = jnp.max(cur, axis=0, keepdims=True)
        first = jnp.min(jnp.where(cur == m, iota_e, N_EXPERTS), axis=0, keepdims=True)
        pick = iota_e == first
        sel = jnp.where(pick, 1.0, sel)
        cur = jnp.where(pick, neg_inf, cur)
        picks.append(pick)
        firsts.append(first)
    wsel = sel * scores
    denom = jnp.sum(wsel, axis=0, keepdims=True) + 1e-20
    gates = (ROUTED_SCALE * wsel) / denom

    upto = (lax.broadcasted_iota(jnp.int32, (tm, tm), 0) <= lax.broadcasted_iota(jnp.int32, (tm, tm), 1))
    csum = jnp.dot(sel.astype(BF16), jnp.where(upto, 1.0, 0.0).astype(BF16), preferred_element_type=F32)
    before = base_cnt + csum - sel
    iota_k = lax.broadcasted_iota(jnp.int32, (TOP_K, tm), 0)
    eidx = jnp.zeros((TOP_K, tm), jnp.int32)
    rank = jnp.zeros((TOP_K, tm), F32)
    w_k = jnp.zeros((TOP_K, tm), F32)
    for i in range(TOP_K):
        eidx = jnp.where(iota_k == i, firsts[i], eidx)
        rank = jnp.where(iota_k == i, jnp.sum(jnp.where(picks[i], before, 0.0), axis=0, keepdims=True), rank)
        w_k = jnp.where(iota_k == i, jnp.sum(jnp.where(picks[i], gates, 0.0), axis=0, keepdims=True), w_k)
    return gates, eidx, rank.astype(jnp.int32), w_k, jnp.sum(sel, axis=1, keepdims=True)


def _merge_kernel(ya_ref, yb_ref, mg_ref, x_ref, ga1_ref, sc2_ref, sh2_ref, pa_ref, pb_ref, wo_ref,
                  g1_ref, b1_ref, rwt_ref, rb_ref, x1_ref, h2_ref, gt_ref, hp_ref, eidx_ref, rank_ref, wk_ref,
                  cnt_ref, cnt_s, *, dn_alpha):
    bb, tl, d = x_ref.shape
    tm = bb * tl

    @pl.when((pl.program_id(0) == 0) & (pl.program_id(1) == 0))
    def _():
        cnt_s[...] = jnp.zeros_like(cnt_s)

    ya = ya_ref[...].reshape(tm, V_A)
    yb = yb_ref[...].reshape(tm, D_B)
    mg = mg_ref[...].reshape(tm, 2 * d)
    merged = (jax.nn.sigmoid(mg[:, :d]) * jnp.dot(ya, pa_ref[...], preferred_element_type=F32)
              + jax.nn.sigmoid(mg[:, d:]) * jnp.dot(yb, pb_ref[...], preferred_element_type=F32))
    mix = _dot(merged, wo_ref[...])
    y = dn_alpha * x_ref[...] + (1.0 + ga1_ref[...]) * mix.reshape(bb, tl, d)
    x1 = _layernorm(y, g1_ref[...], b1_ref[...])
    x1_ref[...] = x1
    h2 = (x1 * (1.0 + sc2_ref[...]) + sh2_ref[...]).astype(BF16)
    h2_ref[...] = h2
    h2f = h2.reshape(tm, d)
    hp_ref[...] = _pack_bf16_pairs(h2f.astype(F32))
    logits_t = lax.dot_general(rwt_ref[...], h2f, (_NT, ((), ())), preferred_element_type=F32)
    gates, eidx, rank, w_k, tile_cnt = _route(logits_t, rb_ref[...], cnt_s[...])
    gt_ref[...] = gates
    eidx_ref[...] = eidx
    rank_ref[...] = rank
    wk_ref[...] = w_k
    cnt_s[...] = cnt_s[...] + tile_cnt
    cnt_ref[...] = cnt_s[...].astype(jnp.int32)


def _merge(ya, yb, proj, x, ga1, sc2, sh2, p_a, p_b, w_o, ln_g, ln_b, rw_t, r_bias, bb, tl, dn_alpha):
    b, l, d = x.shape
    nj = l // tl
    tm = bb * tl
    t_all = b * l
    full = lambda *shape: pl.BlockSpec(shape, lambda i, j: (0,) * len(shape))
    mod = pl.BlockSpec((bb, 1, d), lambda i, j: (i, 0, 0))
    tok = lambda width, dtype=None: pl.BlockSpec((bb, tl, width), lambda i, j: (i, j, 0))
    per_k = pl.BlockSpec((TOP_K, tm), lambda i, j: (0, i * nj + j))
    return pl.pallas_call(
        functools.partial(_merge_kernel, dn_alpha=dn_alpha),
        grid=(b // bb, nj),
        in_specs=[tok(V_A), tok(D_B),
                  pl.BlockSpec((bb, tl, 2 * d), lambda i, j: (i, j, P_MG // (2 * d))),
                  tok(d), mod, mod, mod,
                  full(V_A, d), full(D_B, d), full(d, d), full(1, d), full(1, d),
                  full(N_EXPERTS, d), full(N_EXPERTS, 1)],
        out_specs=[tok(d), tok(d), pl.BlockSpec((N_EXPERTS, tm), lambda i, j: (0, i * nj + j)),
                   pl.BlockSpec((tm, d // 2), lambda i, j: (i * nj + j, 0)),
                   per_k, per_k, per_k, full(N_EXPERTS, 1)],
        out_shape=[jax.ShapeDtypeStruct((b, l, d), F32),
                   jax.ShapeDtypeStruct((b, l, d), BF16),
                   jax.ShapeDtypeStruct((N_EXPERTS, t_all), F32),
                   jax.ShapeDtypeStruct((t_all, d // 2), jnp.int32),
                   jax.ShapeDtypeStruct((TOP_K, t_all), jnp.int32),
                   jax.ShapeDtypeStruct((TOP_K, t_all), jnp.int32),
                   jax.ShapeDtypeStruct((TOP_K, t_all), F32),
                   jax.ShapeDtypeStruct((N_EXPERTS, 1), jnp.int32)],
        scratch_shapes=[pltpu.VMEM((N_EXPERTS, 1), F32)],
        compiler_params=_cparams(("arbitrary", "arbitrary"), 48),
        name="merge",
    )(ya, yb, proj, x, ga1, sc2, sh2, p_a, p_b, w_o, ln_g, ln_b, rw_t, r_bias)


def _moe_kernel(h_ref, g_ref, x1_ref, ga2_ref, wgu_ref, wd_ref, sgu_ref, sd_ref, g2_ref, b2_ref,
                o_ref, acc_ref, *, dn_alpha):
    e = pl.program_id(2)
    bb, tl, d = h_ref.shape
    tm = bb * tl
    h = h_ref[...].reshape(tm, d)

    @pl.when(e == 0)
    def _():
        su = jnp.dot(h, sgu_ref[...], preferred_element_type=F32)
        act = _silu(su[:, :D_SHARED]) * su[:, D_SHARED:]
        acc_ref[...] = _dot(act, sd_ref[...])

    lane = lax.broadcasted_iota(jnp.int32, (tm, N_EXPERTS), 1)
    gates = g_ref[...]
    acts = []
    for i in range(EXPERTS_PER_STEP):
        gu = jnp.dot(h, wgu_ref[i], preferred_element_type=F32)
        gate = _lane_pick(gates, lane, e * EXPERTS_PER_STEP + i)
        acts.append((_silu(gu[:, :D_EXPERT]) * gu[:, D_EXPERT:] * gate).astype(BF16))
    wd = wd_ref[...].reshape(EXPERTS_PER_STEP * D_EXPERT, d)
    acc_ref[...] += jnp.dot(jnp.concatenate(acts, axis=1), wd, preferred_element_type=F32)

    @pl.when(e == pl.num_programs(2) - 1)
    def _():
        y = dn_alpha * x1_ref[...] + (1.0 + ga2_ref[...]) * acc_ref[...].reshape(bb, tl, d)
        o_ref[...] = _layernorm(y, g2_ref[...], b2_ref[...])


def _moe(h2, gates, x1, ga2, we_gu, we_down, ws_gu, ws_down, ln_g, ln_b, bb, tl, dn_alpha):
    b, l, d = x1.shape
    nj = l // tl
    tm = bb * tl
    full = lambda *shape: pl.BlockSpec(shape, lambda i, j, e: (0,) * len(shape))
    tok = pl.BlockSpec((bb, tl, d), lambda i, j, e: (i, j, 0))
    return pl.pallas_call(
        functools.partial(_moe_kernel, dn_alpha=dn_alpha),
        grid=(b // bb, nj, N_EXPERTS // EXPERTS_PER_STEP),
        in_specs=[tok,
                  pl.BlockSpec((tm, N_EXPERTS), lambda i, j, e: (i * nj + j, 0)),
                  tok,
                  pl.BlockSpec((bb, 1, d), lambda i, j, e: (i, 0, 0)),
                  pl.BlockSpec((EXPERTS_PER_STEP, d, 2 * D_EXPERT), lambda i, j, e: (e, 0, 0)),
                  pl.BlockSpec((EXPERTS_PER_STEP, D_EXPERT, d), lambda i, j, e: (e, 0, 0)),
                  full(d, 2 * D_SHARED), full(D_SHARED, d), full(1, d), full(1, d)],
        out_specs=tok,
        out_shape=jax.ShapeDtypeStruct((b, l, d), F32),
        scratch_shapes=[pltpu.VMEM((tm, d), F32)],
        compiler_params=_cparams(("parallel", "parallel", "arbitrary"), 56),
        name="moe",
    )(h2, gates, x1, ga2, we_gu, we_down, ws_gu, ws_down, ln_g, ln_b)


def _sc_mesh():
    return plsc.VectorSubcoreMesh(core_axis_name="c", subcore_axis_name="s")


def _sc_scatter_rows(x, pos, n_out):
    t, w = x.shape
    k = pos.shape[0]
    t_per_w = t // SC_WORKERS
    n_cores = plsc.get_sparse_core_info().num_cores

    @functools.partial(
        pl.kernel, mesh=_sc_mesh(),
        out_type=jax.ShapeDtypeStruct((n_out, w), jnp.int32),
        scratch_types=[pltpu.VMEM((k, SC_CHUNK), jnp.int32), pltpu.VMEM((SC_CHUNK, w), jnp.int32),
                       pltpu.SemaphoreType.DMA],
    )
    def scatter_kernel(x_hbm, pos_hbm, out_hbm, idx_v, rows_v, sem):
        base = (lax.axis_index("s") * n_cores + lax.axis_index("c")) * t_per_w

        @pl.loop(0, t_per_w // SC_CHUNK)
        def _(i):
            off = pl.multiple_of(base + i * SC_CHUNK, SC_CHUNK)
            pltpu.sync_copy(pos_hbm.at[:, pl.ds(off, SC_CHUNK)], idx_v)
            pltpu.sync_copy(x_hbm.at[pl.ds(off, SC_CHUNK)], rows_v)
            for j in range(k):
                pltpu.async_copy(rows_v, out_hbm.at[idx_v.at[j]], sem).wait()

    return scatter_kernel(x, pos)


def _sc_gather_rows(table, idx):
    n = idx.shape[0]
    w = table.shape[1]
    n_per_w = n // SC_WORKERS
    n_cores = plsc.get_sparse_core_info().num_cores

    @functools.partial(
        pl.kernel, mesh=_sc_mesh(),
        out_type=jax.ShapeDtypeStruct((n, w), jnp.int32),
        scratch_types=[pltpu.VMEM((SC_CHUNK,), jnp.int32), pltpu.VMEM((SC_CHUNK, w), jnp.int32),
                       pltpu.SemaphoreType.DMA],
    )
    def gather_kernel(table_hbm, idx_hbm, out_hbm, idx_v, rows_v, sem):
        base = (lax.axis_index("s") * n_cores + lax.axis_index("c")) * n_per_w

        @pl.loop(0, n_per_w // SC_CHUNK)
        def _(i):
            off = pl.multiple_of(base + i * SC_CHUNK, SC_CHUNK)
            pltpu.sync_copy(idx_hbm.at[pl.ds(off, SC_CHUNK)], idx_v)
            pltpu.async_copy(table_hbm.at[idx_v], rows_v, sem).wait()
            pltpu.sync_copy(rows_v, out_hbm.at[pl.ds(off, SC_CHUNK)])

    return gather_kernel(table, idx)


def _experts_kernel(te_ref, used_ref, xs_ref, wgu_ref, wd_ref, o_ref):
    @pl.when(pl.program_id(0) < used_ref[0])
    def _():
        x = _unpack_bf16_pairs(xs_ref[...]).astype(BF16)
        gu = jnp.dot(x, wgu_ref[0], preferred_element_type=F32)
        act = _silu(gu[:, :D_EXPERT]) * gu[:, D_EXPERT:]
        o_ref[...] = _pack_bf16_pairs(_dot(act, wd_ref[0]))


def _experts(xs, tile_expert, tiles_used, we_gu, we_down):
    r, half = xs.shape
    d = 2 * half
    grid_spec = pltpu.PrefetchScalarGridSpec(
        num_scalar_prefetch=2,
        grid=(r // EXPERT_TILE,),
        in_specs=[pl.BlockSpec((EXPERT_TILE, half), lambda i, te, used: (i, 0)),
                  pl.BlockSpec((1, d, 2 * D_EXPERT), lambda i, te, used: (te[i], 0, 0)),
                  pl.BlockSpec((1, D_EXPERT, d), lambda i, te, used: (te[i], 0, 0))],
        out_specs=pl.BlockSpec((EXPERT_TILE, half), lambda i, te, used: (i, 0)),
    )
    return pl.pallas_call(
        _experts_kernel,
        grid_spec=grid_spec,
        out_shape=jax.ShapeDtypeStruct((r, half), jnp.int32),
        compiler_params=_cparams(("arbitrary",), 32),
        name="experts",
    )(tile_expert, tiles_used, xs, we_gu, we_down)


def _combine_kernel(og_ref, wk_ref, h_ref, x1_ref, ga2_ref, sgu_ref, sd_ref, g2_ref, b2_ref, o_ref, *, dn_alpha):
    bb, tl, d = h_ref.shape
    tm = bb * tl
    h = h_ref[...].reshape(tm, d)
    su = jnp.dot(h, sgu_ref[...], preferred_element_type=F32)
    acc = _dot(_silu(su[:, :D_SHARED]) * su[:, D_SHARED:], sd_ref[...])
    lane = lax.broadcasted_iota(jnp.int32, (tm, TOP_K), 1)
    w_all = wk_ref[...]
    for k in range(TOP_K):
        acc = acc + _lane_pick(w_all, lane, k) * _unpack_bf16_pairs(og_ref[k])
    y = dn_alpha * x1_ref[...] + (1.0 + ga2_ref[...]) * acc.reshape(bb, tl, d)
    o_ref[...] = _layernorm(y, g2_ref[...], b2_ref[...])


def _combine(og, w_tk, h2, x1, ga2, ws_gu, ws_down, ln_g, ln_b, bb, tl, dn_alpha):
    b, l, d = x1.shape
    nj = l // tl
    tm = bb * tl
    full = lambda *shape: pl.BlockSpec(shape, lambda i, j: (0,) * len(shape))
    tok = pl.BlockSpec((bb, tl, d), lambda i, j: (i, j, 0))
    return pl.pallas_call(
        functools.partial(_combine_kernel, dn_alpha=dn_alpha),
        grid=(b // bb, nj),
        in_specs=[pl.BlockSpec((TOP_K, tm, d // 2), lambda i, j: (0, i * nj + j, 0)),
                  pl.BlockSpec((tm, TOP_K), lambda i, j: (i * nj + j, 0)),
                  tok, tok,
                  pl.BlockSpec((bb, 1, d), lambda i, j: (i, 0, 0)),
                  full(d, 2 * D_SHARED), full(D_SHARED, d), full(1, d), full(1, d)],
        out_specs=tok,
        out_shape=jax.ShapeDtypeStruct((b, l, d), F32),
        compiler_params=_cparams(("parallel", "parallel"), 48),
        name="combine",
    )(og, w_tk, h2, x1, ga2, ws_gu, ws_down, ln_g, ln_b)


def _moe_sparse(hp, eidx, rank, w_k, cnt, h2, x1, ga2, p, bb, dn_alpha):
    t_all = hp.shape[0]
    n_tiles = t_all * TOP_K // EXPERT_TILE + N_EXPERTS
    seg_tiles = (cnt[:, 0] + EXPERT_TILE - 1) // EXPERT_TILE
    seg_end = jnp.cumsum(seg_tiles)
    seg_start = (seg_end - seg_tiles) * EXPERT_TILE
    experts = jnp.arange(N_EXPERTS, dtype=jnp.int32)
    pos = rank + jnp.sum(jnp.where(eidx[None] == experts[:, None, None], seg_start[:, None, None], 0), axis=0)
    tiles = jnp.arange(n_tiles, dtype=jnp.int32)
    tile_expert = jnp.minimum(jnp.sum((seg_end[None, :] <= tiles[:, None]).astype(jnp.int32), axis=1),
                              N_EXPERTS - 1)
    xs = _sc_scatter_rows(hp, pos, n_tiles * EXPERT_TILE)
    outs = _experts(xs, tile_expert, seg_end[-1:].astype(jnp.int32), p['we_gu'], p['we_down'])
    og = _sc_gather_rows(outs, pos.reshape(-1)).reshape(TOP_K, t_all, hp.shape[1])
    return _combine(og, w_k.T, h2, x1, ga2, p['ws_gu'], p['ws_down'], p['ln2_g'], p['ln2_b'], bb, COMBINE_TL, dn_alpha)


def _pad_rows(buf, width):
    b, n, w = buf.shape
    return jnp.pad(buf.astype(F32), ((0, 0), (SUBLANES - n, 0), (0, width - w)))


def _to_pairs(s):
    b = s.shape[0]
    s = s.astype(F32).reshape(b, PAIRS, 2, N_B, N_B)
    zero = jnp.zeros_like(s[:, :, 0])
    top = jnp.concatenate([s[:, :, 0], zero], axis=-1)
    bot = jnp.concatenate([zero, s[:, :, 1]], axis=-1)
    return jnp.concatenate([top, bot], axis=-2)


def _from_pairs(sp):
    b = sp.shape[0]
    return jnp.stack([sp[:, :, :N_B, :N_B], sp[:, :, N_B:, N_B:]], axis=2).reshape(b, H_B, N_B, N_B)


def _layer(x, mod, conv_buf, gdn_s, shift_buf, rwkv_s, p, tiles, dn_alpha):
    b, l, d = x.shape
    sh1, sc1, ga1, sh2, sc2, ga2 = mod
    proj = _inproj(x, sc1, sh1, p['w_in'], tiles['bb'], tiles['tl'])

    lp = -(-l // CHUNK) * CHUNK
    proj_r = proj if lp == l else jnp.pad(proj, ((0, 0), (0, lp - l), (0, 0)))
    tl_r = min(tiles['tl_rec'], lp)
    ya, gdn_new = _gdn(proj_r, _pad_rows(conv_buf, QKV_A), gdn_s.astype(F32), p['conv_w'], p['alog_row'],
                       p['dt_row'], p['gdn_norm_w'], tiles['bb_rec'], tl_r, l)
    yb, rwkv_pairs = _rwkv(proj_r, _pad_rows(shift_buf, RW_BLOCK), _to_pairs(rwkv_s), p['rwkv_vecs'],
                           p['w2p'], p['a2p'], p['g2'], tiles['bb_rec'], tl_r, l)
    if lp != l:
        ya, yb = ya[:, :l], yb[:, :l]

    x1, h2, gates_t, hp, eidx, rank, w_k, cnt = _merge(
        ya, yb, proj, x, ga1, sc2, sh2, p['p_a'], p['p_b'], p['w_o'], p['ln1_g'], p['ln1_b'],
        p['router_wt'], p['router_bias'], tiles['bb'], tiles['tl_merge'], dn_alpha)
    if (b * l) % (SC_WORKERS * SC_CHUNK) == 0:
        out = _moe_sparse(hp, eidx, rank, w_k, cnt, h2, x1, ga2, p, tiles['bb'], dn_alpha)
    else:
        out = _moe(h2, gates_t.T, x1, ga2, p['we_gu'], p['we_down'], p['ws_gu'], p['ws_down'], p['ln2_g'],
                   p['ln2_b'], tiles['bb'], tiles['tl'], dn_alpha)

    pre = jnp.concatenate([conv_buf.astype(F32), proj[:, :, P_QKV:P_QKV + QKV_A]], axis=1)
    conv_new = pre[:, -(CONV_W - 1):]
    shift_new = proj[:, l - 1:l, P_RW:P_RW + SHIFT_W]
    return (out, conv_new.astype(conv_buf.dtype), gdn_new.astype(gdn_s.dtype),
            shift_new.astype(shift_buf.dtype), _from_pairs(rwkv_pairs).astype(rwkv_s.dtype))


def _prep_params(l, w_in, conv_w, a_log, dt_bias, gdn_norm_w, mu_shift, w0, w2, a0, a2, g2, k_k, k_a, r_k,
                 lnx_g, lnx_b, p_a, p_b, w_o, ln1_g, ln1_b, router_w, router_bias, we_gate, we_up, we_down,
                 ws_gate, ws_up, ws_down, ln2_g, ln2_b):
    d = D_MODEL
    w = w_in[l]
    w_p = jnp.concatenate(
        [w[:, :QKV_A], w[:, OFF_Z:OFF_RWKV], w[:, OFF_RWKV:OFF_MERGE],
         jnp.zeros((d, RW_BLOCK - SHIFT_W), w.dtype), w[:, OFF_MERGE:], w[:, OFF_ALPHA:OFF_Z],
         jnp.zeros((d, AB_BLOCK - 2 * H_A), w.dtype)], axis=1).astype(BF16)
    row = lambda v, width: jnp.pad(v.astype(F32).reshape(1, -1), ((0, 0), (0, width - v.size)))
    zeros_lora = jnp.zeros((W_LORA, D_B), F32)
    return {
        'w_in': w_p,
        'conv_w': conv_w[l].astype(F32),
        'alog_row': row(a_log[l], LANES),
        'dt_row': row(dt_bias[l], LANES),
        'gdn_norm_w': row(gdn_norm_w[l], DK_A),
        'rwkv_vecs': (row(mu_shift[l], RW_BLOCK), row(w0[l], D_B), row(a0[l], D_B), row(k_k[l], D_B),
                      row(k_a[l], D_B), row(r_k[l], D_B), row(lnx_g[l], D_B), row(lnx_b[l], D_B)),
        'w2p': jnp.concatenate([w2[l].astype(F32), zeros_lora], axis=0),
        'a2p': jnp.concatenate([zeros_lora, a2[l].astype(F32)], axis=0),
        'g2': g2[l].astype(F32),
        'p_a': p_a[l].astype(BF16), 'p_b': p_b[l].astype(BF16), 'w_o': w_o[l].astype(BF16),
        'ln1_g': row(ln1_g[l], d), 'ln1_b': row(ln1_b[l], d),
        'router_wt': router_w[l].T.astype(BF16),
        'router_bias': router_bias[l].astype(F32).reshape(N_EXPERTS, 1),
        'we_gu': jnp.concatenate([we_gate[l], we_up[l]], axis=-1).astype(BF16),
        'we_down': we_down[l].astype(BF16),
        'ws_gu': jnp.concatenate([ws_gate[l], ws_up[l]], axis=-1).astype(BF16),
        'ws_down': ws_down[l].astype(BF16),
        'ln2_g': row(ln2_g[l], d), 'ln2_b': row(ln2_b[l], d),
    }


def kernel(x_prompt, x_sample, c_prompt, c_sample, state_gdn_conv, state_gdn, state_rwkv_shift, state_rwkv, w_ada, b_ada, w_in, conv_w, a_log, dt_bias, gdn_norm_w, mu_shift, w0, w2, a0, a2, g2, k_k, k_a, r_k, lnx_g, lnx_b, p_a, p_b, w_o, ln1_g, ln1_b, router_w, router_bias, we_gate, we_up, we_down, ws_gate, ws_up, ws_down, ln2_g, ln2_b):
    depth = w_ada.shape[0]
    dn_alpha = (2 * depth) ** 0.25
    bp, lp_, d = x_prompt.shape
    bs, ls, _ = x_sample.shape
    dtp = x_prompt.dtype
    tiles_p = {'bb': 1, 'tl': 1024, 'bb_rec': 2, 'tl_rec': 256, 'tl_merge': 512}
    tiles_s = {'bb': bs, 'tl': ls, 'bb_rec': 2, 'tl_rec': CHUNK, 'tl_merge': ls}

    yp, ys = x_prompt, x_sample
    new_p = ([], [], [], [])
    new_s = ([], [], [], [])
    for l in range(depth):
        p = _prep_params(l, w_in, conv_w, a_log, dt_bias, gdn_norm_w, mu_shift, w0, w2, a0, a2, g2, k_k, k_a,
                         r_k, lnx_g, lnx_b, p_a, p_b, w_o, ln1_g, ln1_b, router_w, router_bias, we_gate, we_up,
                         we_down, ws_gate, ws_up, ws_down, ln2_g, ln2_b)
        mod = _ada(jnp.concatenate([c_prompt, c_sample], axis=0), w_ada[l], b_ada[l])
        mod_p = tuple(m[:, None, :] for m in jnp.split(mod[:bp], 6, axis=-1))
        mod_s = tuple(m[:, None, :] for m in jnp.split(mod[bp:], 6, axis=-1))
        yp, *sp = _layer(yp, mod_p,
                         jnp.zeros((bp, CONV_W - 1, QKV_A), dtp), jnp.zeros((bp, H_A, DK_A, DK_A), dtp),
                         jnp.zeros((bp, 1, SHIFT_W), dtp), jnp.zeros((bp, H_B, N_B, N_B), dtp),
                         p, tiles_p, dn_alpha)
        ys, *ss = _layer(ys, mod_s, state_gdn_conv[l], state_gdn[l], state_rwkv_shift[l], state_rwkv[l],
                         p, tiles_s, dn_alpha)
        for lst, val in zip(new_p, sp):
            lst.append(val)
        for lst, val in zip(new_s, ss):
            lst.append(val)
    conv_p, gdn_p, shift_p, rwkv_p = [jnp.stack(t, 0) for t in new_p]
    conv_s, gdn_s, shift_s, rwkv_s = [jnp.stack(t, 0) for t in new_s]
    return (yp, ys, conv_p, gdn_p, shift_p, rwkv_p, conv_s, gdn_s, shift_s, rwkv_s)
```

```python
import functools
import math

import jax
import jax.numpy as jnp
from jax import lax
from jax.experimental import pallas as pl
from jax.experimental.pallas import tpu as pltpu
from jax.experimental.pallas import tpu_sc as plsc

F32 = jnp.float32
BF16 = jnp.bfloat16

D_MODEL = 1024
DK_A = 128
H_A = 4
QK_A = H_A * DK_A
V_A = H_A * DK_A
QKV_A = 2 * QK_A + V_A
CONV_W = 4
N_B = 64
H_B = 8
D_B = H_B * N_B
W_LORA = 64
A_LORA = 64
G_LORA = 128
SHIFT_W = 3 * D_B + W_LORA + A_LORA + G_LORA
OFF_ALPHA = QKV_A
OFF_BETA = OFF_ALPHA + H_A
OFF_Z = OFF_BETA + H_A
OFF_RWKV = OFF_Z + V_A
OFF_MERGE = OFF_RWKV + SHIFT_W
N_EXPERTS = 64
TOP_K = 8
N_GROUPS = 8
GROUP_SIZE = N_EXPERTS // N_GROUPS
TOPK_GROUPS = 4
D_EXPERT = 256
D_SHARED = 256
ROUTED_SCALE = 2.5
LN_EPS = 1e-5
GN_EPS = 64e-5
RMS_EPS = 1e-6
DECAY_SCALE = -math.exp(-0.5)

SUBLANES = 8
LANES = 128

P_QKV = 0
P_Z = QKV_A
P_RW = 2048
RW_BLOCK = 2048
P_MG = P_RW + RW_BLOCK
P_AB = P_MG + 2 * D_MODEL
AB_BLOCK = LANES
PROJ_TN = 1280
N_PROJ = 5 * PROJ_TN

CHUNK = 128
BASE_BLOCK = 8
CUMSUM_PARTS = 3
STAT_PARTS = 2
EXPERTS_PER_STEP = 2
HIGH_HALF = -65536

EXPERT_TILE = 512
SC_WORKERS = 32
SC_CHUNK = 128
COMBINE_TL = 512


def _cparams(sem, vmem_mb):
    return pltpu.CompilerParams(dimension_semantics=sem, vmem_limit_bytes=vmem_mb * 1024 * 1024)


def _dot(a, b):
    return jnp.dot(a.astype(BF16), b.astype(BF16), preferred_element_type=F32)


def _rdot(a, b, dims=((1,), (0,))):
    return lax.dot_general(a.astype(BF16), b.astype(BF16), (dims, ((), ())), preferred_element_type=F32)


_NT = ((1,), (1,))
_TN = ((0,), (0,))


def _bf16_parts(x, parts):
    out = []
    rem = x
    for _ in range(parts):
        hi = rem.astype(BF16)
        out.append(hi)
        rem = rem - hi.astype(F32)
    return out


def _mask_dot_left(mask, x, parts):
    return sum(jnp.dot(mask, p, preferred_element_type=F32) for p in _bf16_parts(x, parts))


def _mask_dot_right(x, mask, parts):
    return sum(jnp.dot(p, mask, preferred_element_type=F32) for p in _bf16_parts(x, parts))


def _lane_pick(x, lane_iota, lane):
    return jnp.sum(jnp.where(lane_iota == lane, x, 0.0), axis=-1, keepdims=True)


def _pack_bf16_pairs(x):
    n = x.shape[1] // 2
    bits = lax.bitcast_convert_type(x.astype(BF16).astype(F32), jnp.int32)
    return (bits[:, :n] & HIGH_HALF) | lax.shift_right_logical(bits[:, n:], 16)


def _unpack_bf16_pairs(p):
    hi = lax.bitcast_convert_type(p & HIGH_HALF, F32)
    lo = lax.bitcast_convert_type(lax.shift_left(p, 16), F32)
    return jnp.concatenate([hi, lo], axis=1)


def _silu(x):
    return x * jax.nn.sigmoid(x)


def _softplus(x):
    return jnp.maximum(x, 0.0) + jnp.log1p(jnp.exp(-jnp.abs(x)))


def _tri_inverse(mats, eye, sign, row, col):
    base = (row // BASE_BLOCK) == (col // BASE_BLOCK)
    ds = [jnp.where(base, a, 0.0) for a in mats]
    xs = [eye + d if sign > 0 else eye - d for d in ds]
    power = 2
    while power < BASE_BLOCK:
        ds = [_rdot(d, d) for d in ds]
        xs = [x + _rdot(x, d) for x, d in zip(xs, ds)]
        power *= 2
    b = BASE_BLOCK
    while b < CHUNK:
        sibling = ((row // b) == (col // b) + 1) & ((row // (2 * b)) == (col // (2 * b)))
        offs = [jnp.where(sibling, a, 0.0) for a in mats]
        ts = [_rdot(o, x) for o, x in zip(offs, xs)]
        if sign > 0:
            xs = [x + _rdot(x, t_) for x, t_ in zip(xs, ts)]
        else:
            xs = [x - _rdot(x, t_) for x, t_ in zip(xs, ts)]
        b *= 2
    return xs


def _tri_consts():
    row = lax.broadcasted_iota(jnp.int32, (CHUNK, CHUNK), 0)
    col = lax.broadcasted_iota(jnp.int32, (CHUNK, CHUNK), 1)
    incl = row >= col
    strict = row > col
    eye = jnp.where(row == col, 1.0, 0.0).astype(F32)
    tril = jnp.where(incl, 1.0, 0.0).astype(BF16)
    return row, col, incl, strict, eye, tril


def _ada_kernel(c_ref, w_ref, b_ref, o_ref):
    o_ref[...] = _dot(_silu(c_ref[...]), w_ref[...]) + b_ref[...]


def _ada(c, w_ada, b_ada):
    n, d = c.shape
    nout = w_ada.shape[1]
    tn = 768
    return pl.pallas_call(
        _ada_kernel,
        grid=(nout // tn,),
        in_specs=[pl.BlockSpec((n, d), lambda j: (0, 0)),
                  pl.BlockSpec((d, tn), lambda j: (0, j)),
                  pl.BlockSpec((1, tn), lambda j: (0, j))],
        out_specs=pl.BlockSpec((n, tn), lambda j: (0, j)),
        out_shape=jax.ShapeDtypeStruct((n, nout), F32),
        compiler_params=_cparams(("parallel",), 32),
        name="ada",
    )(c, w_ada, b_ada.reshape(1, nout))


def _inproj_kernel(x_ref, sc_ref, sh_ref, w_ref, o_ref, hs_ref):
    @pl.when(pl.program_id(2) == 0)
    def _():
        h = x_ref[...] * (1.0 + sc_ref[...]) + sh_ref[...]
        hs_ref[...] = h.reshape(hs_ref.shape).astype(BF16)

    o = jnp.dot(hs_ref[...], w_ref[...], preferred_element_type=F32)
    o_ref[...] = o.reshape(o_ref.shape)


def _inproj(x, sc, sh, w_p, bb, tl):
    b, l, d = x.shape
    return pl.pallas_call(
        _inproj_kernel,
        grid=(b // bb, l // tl, N_PROJ // PROJ_TN),
        in_specs=[pl.BlockSpec((bb, tl, d), lambda i, j, n: (i, j, 0)),
                  pl.BlockSpec((bb, 1, d), lambda i, j, n: (i, 0, 0)),
                  pl.BlockSpec((bb, 1, d), lambda i, j, n: (i, 0, 0)),
                  pl.BlockSpec((d, PROJ_TN), lambda i, j, n: (0, n))],
        out_specs=pl.BlockSpec((bb, tl, PROJ_TN), lambda i, j, n: (i, j, n)),
        out_shape=jax.ShapeDtypeStruct((b, l, N_PROJ), F32),
        scratch_shapes=[pltpu.VMEM((bb * tl, d), BF16)],
        compiler_params=_cparams(("parallel", "parallel", "arbitrary"), 48),
        name="inproj",
    )(x, sc, sh, w_p)


def _gdn_kernel(qkv_ref, ab_ref, z_ref, cbuf_ref, s0_ref, cw_ref, alog_ref, dt_ref, nw_ref,
                ya_ref, sout_ref, ext_ref, qc_ref, s_ref, *, bb, tl, l_valid, l_total):
    t = pl.program_id(1)

    @pl.when(t == 0)
    def _():
        ext_ref[:, 0:SUBLANES, :] = cbuf_ref[...]
        s_ref[...] = s0_ref[...]

    first = SUBLANES - (CONV_W - 1)
    for bi in range(bb):
        ext_ref[bi, SUBLANES:SUBLANES + tl, :] = qkv_ref[bi]
        acc = cw_ref[0:1, :] * ext_ref[bi, first:first + tl, :]
        for j in range(1, CONV_W):
            acc = acc + cw_ref[j:j + 1, :] * ext_ref[bi, first + j:first + j + tl, :]
        ext_ref[bi, 0:SUBLANES, :] = ext_ref[bi, tl:tl + SUBLANES, :]
        qc_ref[bi] = _silu(acc)

    row, col, incl, strict, eye, tril = _tri_consts()
    neg_a = -jnp.exp(alog_ref[...])
    dt = dt_ref[...]
    nw = nw_ref[...]
    chains = [(bi, h) for bi in range(bb) for h in range(H_A)]
    n = range(len(chains))

    def chunk(c, carry):
        r0 = pl.multiple_of(c * CHUNK, CHUNK)
        rows = pl.ds(r0, CHUNK)
        g_cum = []
        b_all = []
        for bi in range(bb):
            ab = ab_ref[bi, rows, :]
            g_bi = neg_a * _softplus(ab + dt)
            b_bi = jax.nn.sigmoid(ab)
            if l_valid < l_total:
                valid = (t * tl + r0 + row) < l_valid
                g_bi = jnp.where(valid, g_bi, 0.0)
                b_bi = jnp.where(valid, b_bi, 0.0)
            g_cum.append(_mask_dot_left(tril, g_bi, CUMSUM_PARTS))
            b_all.append(b_bi)
        g_col = [_lane_pick(g_cum[bi], col, h) for bi, h in chains]
        beta = [_lane_pick(b_all[bi], col, H_A + h) for bi, h in chains]
        decay = []
        for i in n:
            g_b = jnp.broadcast_to(g_col[i], (CHUNK, CHUNK))
            decay.append(jnp.where(incl, jnp.exp(jnp.where(incl, g_b - g_b.T, 0.0)), 0.0))
        q = [qc_ref[bi, rows, h * DK_A:(h + 1) * DK_A] for bi, h in chains]
        k = [qc_ref[bi, rows, QK_A + h * DK_A:QK_A + (h + 1) * DK_A] for bi, h in chains]
        v = [qc_ref[bi, rows, 2 * QK_A + h * DK_A:2 * QK_A + (h + 1) * DK_A] for bi, h in chains]
        q = [x * lax.rsqrt(jnp.sum(x * x, -1, keepdims=True) + 1e-6) * (DK_A ** -0.5) for x in q]
        k = [x * lax.rsqrt(jnp.sum(x * x, -1, keepdims=True) + 1e-6) for x in k]
        kq = [_rdot(jnp.concatenate([k[i], q[i]], axis=0), k[i], _NT) for i in n]
        a = [jnp.where(strict, beta[i] * kq[i][:CHUNK] * decay[i], 0.0) for i in n]
        qk = [jnp.where(incl, kq[i][CHUNK:] * decay[i], 0.0) for i in n]
        x = _tri_inverse(a, eye, -1, row, col)
        e_g = [jnp.exp(g) for g in g_col]
        uw = [_rdot(x[i], jnp.concatenate([v[i] * beta[i], k[i] * (beta[i] * e_g[i])], axis=1)) for i in n]
        g_last = [g[CHUNK - 1:CHUNK, :] for g in g_col]
        kd = [k[i] * jnp.exp(g_last[i] - g_col[i]) for i in n]
        s = [s_ref[bi, h] for bi, h in chains]
        ws = [_rdot(jnp.concatenate([uw[i][:, DK_A:], q[i] * e_g[i]], axis=0), s[i]) for i in n]
        v_new = [uw[i][:, :DK_A] - ws[i][:CHUNK] for i in n]
        o = [ws[i][CHUNK:] + _rdot(qk[i], v_new[i]) for i in n]
        s_new = [s[i] * jnp.exp(g_last[i]) + _rdot(kd[i], v_new[i], _TN) for i in n]
        for i, (bi, h) in enumerate(chains):
            s_ref[bi, h] = s_new[i]
            z = z_ref[bi, rows, h * DK_A:(h + 1) * DK_A]
            on = o[i] * lax.rsqrt(jnp.mean(o[i] * o[i], -1, keepdims=True) + RMS_EPS) * nw * _silu(z)
            ya_ref[bi, rows, h * DK_A:(h + 1) * DK_A] = on.astype(BF16)
        return carry

    lax.fori_loop(0, tl // CHUNK, chunk, 0)

    @pl.when(t == pl.num_programs(1) - 1)
    def _():
        sout_ref[...] = s_ref[...]


def _gdn(proj, conv_buf8, s0, conv_w, alog_row, dt_row, norm_w, bb, tl, l_valid):
    b, l, _ = proj.shape
    kern = functools.partial(_gdn_kernel, bb=bb, tl=tl, l_valid=l_valid, l_total=l)
    full = lambda *shape: pl.BlockSpec(shape, lambda i, t: (0,) * len(shape))
    return pl.pallas_call(
        kern,
        grid=(b // bb, l // tl),
        in_specs=[pl.BlockSpec((bb, tl, QKV_A), lambda i, t: (i, t, P_QKV // QKV_A)),
                  pl.BlockSpec((bb, tl, AB_BLOCK), lambda i, t: (i, t, P_AB // AB_BLOCK)),
                  pl.BlockSpec((bb, tl, V_A), lambda i, t: (i, t, P_Z // V_A)),
                  pl.BlockSpec((bb, SUBLANES, QKV_A), lambda i, t: (i, 0, 0)),
                  pl.BlockSpec((bb, H_A, DK_A, DK_A), lambda i, t: (i, 0, 0, 0)),
                  full(CONV_W, QKV_A), full(1, LANES), full(1, LANES), full(1, DK_A)],
        out_specs=[pl.BlockSpec((bb, tl, V_A), lambda i, t: (i, t, 0)),
                   pl.BlockSpec((bb, H_A, DK_A, DK_A), lambda i, t: (i, 0, 0, 0))],
        out_shape=[jax.ShapeDtypeStruct((b, l, V_A), BF16),
                   jax.ShapeDtypeStruct((b, H_A, DK_A, DK_A), F32)],
        scratch_shapes=[pltpu.VMEM((bb, tl + SUBLANES, QKV_A), F32),
                        pltpu.VMEM((bb, tl, QKV_A), F32),
                        pltpu.VMEM((bb, H_A, DK_A, DK_A), F32)],
        compiler_params=_cparams(("parallel", "arbitrary"), 48),
        name="gdn",
    )(proj, proj, proj, conv_buf8, s0, conv_w, alog_row, dt_row, norm_w)


PAIRS = H_B // 2


def _headsum(x, bd):
    return jnp.concatenate(
        [_mask_dot_right(x[:, p * LANES:(p + 1) * LANES], bd, STAT_PARTS) for p in range(PAIRS)], axis=1)


def _rwkv_kernel(rw_ref, sbuf_ref, s0_ref, mu_ref, w0_ref, a0_ref, kk_ref, ka_ref, rk_ref, lg_ref, lb_ref,
                 w2_ref, a2_ref, g2_ref, yb_ref, sout_ref,
                 ext_ref, r_s, k_s, v_s, z_s, p_s, lw_s, y_s, bonus_s, gate_s, s_ref, *, bb, tl, l_valid, l_total):
    t = pl.program_id(1)

    @pl.when(t == 0)
    def _():
        ext_ref[:, 0:SUBLANES, :] = sbuf_ref[...]
        s_ref[...] = s0_ref[...]

    row, col, incl, strict, eye, tril = _tri_consts()
    same_head = (row // N_B) == (col // N_B)
    bd = jnp.where(same_head, 1.0, 0.0).astype(BF16)
    lane_a = col < N_B

    for bi in range(bb):
        cur = rw_ref[bi]
        ext_ref[bi, SUBLANES:SUBLANES + tl, :] = cur
        prev = ext_ref[bi, SUBLANES - 1:SUBLANES - 1 + tl, :]
        mixed = cur + (prev - cur) * mu_ref[...]
        ext_ref[bi, 0:SUBLANES, :] = ext_ref[bi, tl:tl + SUBLANES, :]
        r = mixed[:, 0:D_B]
        k = mixed[:, D_B:2 * D_B]
        v = mixed[:, 2 * D_B:3 * D_B]
        lora = mixed[:, 3 * D_B:3 * D_B + W_LORA + A_LORA]
        g_in = mixed[:, 3 * D_B + W_LORA + A_LORA:SHIFT_W]
        lw = DECAY_SCALE * jax.nn.sigmoid(w0_ref[...] + _dot(jnp.tanh(lora), w2_ref[...]))
        a = jax.nn.sigmoid(a0_ref[...] + _dot(lora, a2_ref[...]))
        kkr = k * kk_ref[...]
        kk = kkr * lax.rsqrt(_headsum(kkr * kkr, bd) + 1e-6)
        k = k * (1.0 + (a - 1.0) * ka_ref[...])
        tile_rows = pl.ds(bi * tl, tl)
        bonus_s[tile_rows, :] = _headsum(r * k * rk_ref[...], bd) * v
        gate_s[tile_rows, :] = _dot(jax.nn.sigmoid(g_in), g2_ref[...])
        if l_valid < l_total:
            rvalid = (t * tl + lax.broadcasted_iota(jnp.int32, (tl, 1), 0)) < l_valid
            lw = jnp.where(rvalid, lw, 0.0)
            kk = jnp.where(rvalid, kk, 0.0)
            k = jnp.where(rvalid, k, 0.0)
            v = jnp.where(rvalid, v, 0.0)
        r_s[tile_rows, :] = r
        k_s[tile_rows, :] = k
        v_s[tile_rows, :] = v
        z_s[tile_rows, :] = -kk
        p_s[tile_rows, :] = kk * a
        lw_s[tile_rows, :] = lw

    groups = [(bi, p) for bi in range(bb) for p in range(PAIRS)]
    pairs = range(len(groups))
    both = range(2 * len(groups))
    cols = [slice(p * LANES, (p + 1) * LANES) for _, p in groups]

    def chunk(c, carry):
        r0 = pl.multiple_of(c * CHUNK, CHUNK)
        rows = [pl.ds(bi * tl + r0, CHUNK) for bi, _ in groups]
        lw_c = [lw_s[rows[g], cols[g]] for g in pairs]
        g_inc = [_mask_dot_left(tril, x, CUMSUM_PARTS) for x in lw_c]
        g_exc = [g_inc[p] - lw_c[p] for p in pairs]
        g_mid = [g[CHUNK // 2 - 1:CHUNK // 2, :] for g in g_inc]
        g_end = [g[CHUNK - 1:CHUNK, :] for g in g_inc]
        z = [z_s[rows[g], cols[g]] for g in pairs]
        rr = [r_s[rows[g], cols[g]] for g in pairs]
        pp = [p_s[rows[g], cols[g]] for g in pairs]
        kk_ = [k_s[rows[g], cols[g]] for g in pairs]
        vv = [v_s[rows[g], cols[g]] for g in pairs]
        zt = [z[p] * jnp.exp(g_exc[p] - g_mid[p]) for p in pairs]
        rt = [rr[p] * jnp.exp(g_inc[p] - g_mid[p]) for p in pairs]
        en = [jnp.exp(g_mid[p] - g_inc[p]) for p in pairs]
        s = [s_ref[bi, p] for bi, p in groups]
        lhs = [jnp.concatenate([jnp.where(lane_a, zt[p], 0.0), jnp.where(lane_a, 0.0, zt[p]),
                                jnp.where(lane_a, rt[p], 0.0), jnp.where(lane_a, 0.0, rt[p])], axis=0) for p in pairs]
        m = [_rdot(lhs[p], jnp.concatenate([pp[p] * en[p], kk_[p] * en[p]], axis=0), _NT) for p in pairs]
        zr0 = [_rdot(jnp.concatenate([z[p] * jnp.exp(g_exc[p]), rr[p] * jnp.exp(g_inc[p])], axis=0), s[p], _NT)
               for p in pairs]
        mz = [m[i // 2][(i % 2) * CHUNK:(i % 2 + 1) * CHUNK] for i in both]
        azp = [jnp.where(strict, x[:, :CHUNK], 0.0) for x in mz]
        azk = [jnp.where(strict, x[:, CHUNK:], 0.0) for x in mz]
        minv = _tri_inverse(azp, eye, 1, row, col)
        rhs = [zr0[i // 2][:CHUNK] + _rdot(azk[i], vv[i // 2]) for i in both]
        u_h = [_rdot(minv[i], rhs[i]) for i in both]
        u = [jnp.where(lane_a, u_h[2 * p], u_h[2 * p + 1]) for p in pairs]
        uv = [jnp.concatenate([u[p], vv[p]], axis=0) for p in pairs]
        incl2 = jnp.concatenate([incl, incl], axis=1)
        y_h = [_rdot(jnp.where(incl2, m[i // 2][(2 + i % 2) * CHUNK:(3 + i % 2) * CHUNK], 0.0), uv[i // 2])
               for i in both]
        tail = [jnp.exp(g_end[p] - g_inc[p]) for p in pairs]
        s_new = [s[p] * jnp.exp(g_end[p])
                 + _rdot(uv[p], jnp.concatenate([pp[p] * tail[p], kk_[p] * tail[p]], axis=0), _TN) for p in pairs]
        for g, (bi, p) in enumerate(groups):
            s_ref[bi, p] = jnp.where(same_head, s_new[g], 0.0)
            y_s[rows[g], cols[g]] = zr0[g][CHUNK:] + jnp.where(lane_a, y_h[2 * g], y_h[2 * g + 1])
        return carry

    lax.fori_loop(0, tl // CHUNK, chunk, 0)

    for bi in range(bb):
        tile_rows = pl.ds(bi * tl, tl)
        y = y_s[tile_rows, :]
        mean = _headsum(y, bd) * (1.0 / N_B)
        dev = y - mean
        var = _headsum(dev * dev, bd) * (1.0 / N_B)
        yn = dev * lax.rsqrt(var + GN_EPS) * lg_ref[...] + lb_ref[...]
        yb_ref[bi] = ((yn + bonus_s[tile_rows, :]) * gate_s[tile_rows, :]).astype(BF16)

    @pl.when(t == pl.num_programs(1) - 1)
    def _():
        sout_ref[...] = s_ref[...]


def _rwkv(proj, shift_buf8, s0_pairs, vecs, w2p, a2p, g2, bb, tl, l_valid):
    b, l, _ = proj.shape
    kern = functools.partial(_rwkv_kernel, bb=bb, tl=tl, l_valid=l_valid, l_total=l)
    full = lambda *shape: pl.BlockSpec(shape, lambda i, t: (0,) * len(shape))
    mu, w0, a0, k_k, k_a, r_k, lnx_g, lnx_b = vecs
    return pl.pallas_call(
        kern,
        grid=(b // bb, l // tl),
        in_specs=[pl.BlockSpec((bb, tl, RW_BLOCK), lambda i, t: (i, t, P_RW // RW_BLOCK)),
                  pl.BlockSpec((bb, SUBLANES, RW_BLOCK), lambda i, t: (i, 0, 0)),
                  pl.BlockSpec((bb, PAIRS, LANES, LANES), lambda i, t: (i, 0, 0, 0)),
                  full(1, RW_BLOCK)] + [full(1, D_B)] * 7 +
                 [full(W_LORA + A_LORA, D_B), full(W_LORA + A_LORA, D_B), full(G_LORA, D_B)],
        out_specs=[pl.BlockSpec((bb, tl, D_B), lambda i, t: (i, t, 0)),
                   pl.BlockSpec((bb, PAIRS, LANES, LANES), lambda i, t: (i, 0, 0, 0))],
        out_shape=[jax.ShapeDtypeStruct((b, l, D_B), BF16),
                   jax.ShapeDtypeStruct((b, PAIRS, LANES, LANES), F32)],
        scratch_shapes=[pltpu.VMEM((bb, tl + SUBLANES, RW_BLOCK), F32)] +
                       [pltpu.VMEM((bb * tl, D_B), F32)] * 9 +
                       [pltpu.VMEM((bb, PAIRS, LANES, LANES), F32)],
        compiler_params=_cparams(("parallel", "arbitrary"), 48),
        name="rwkv",
    )(proj, shift_buf8, s0_pairs, mu, w0, a0, k_k, k_a, r_k, lnx_g, lnx_b, w2p, a2p, g2)


def _layernorm(y, g, b):
    mu = jnp.mean(y, -1, keepdims=True)
    dev = y - mu
    var = jnp.mean(dev * dev, -1, keepdims=True)
    return dev * lax.rsqrt(var + LN_EPS) * g + b


def _route(logits_t, bias, base_cnt):
    tm = logits_t.shape[1]
    scores = jax.nn.sigmoid(logits_t)
    choice = scores + bias
    neg_inf = -jnp.inf
    iota_g = lax.broadcasted_iota(jnp.int32, (GROUP_SIZE, tm), 0)
    group_score = []
    for g in range(N_GROUPS):
        xg = choice[g * GROUP_SIZE:(g + 1) * GROUP_SIZE, :]
        m1 = jnp.max(xg, axis=0, keepdims=True)
        first = jnp.min(jnp.where(xg == m1, iota_g, GROUP_SIZE), axis=0, keepdims=True)
        m2 = jnp.max(jnp.where(iota_g == first, neg_inf, xg), axis=0, keepdims=True)
        group_score.append(m1 + m2)
    masked = []
    for g in range(N_GROUPS):
        rank = jnp.zeros((1, tm), jnp.int32)
        for o in range(N_GROUPS):
            if o == g:
                continue
            ahead = group_score[o] > group_score[g]
            if o < g:
                ahead = ahead | (group_score[o] == group_score[g])
            rank = rank + ahead.astype(jnp.int32)
        keep = rank < TOPK_GROUPS
        masked.append(jnp.where(keep, choice[g * GROUP_SIZE:(g + 1) * GROUP_SIZE, :], neg_inf))
    cur = jnp.concatenate(masked, axis=0)
    iota_e = lax.broadcasted_iota(jnp.int32, (N_EXPERTS, tm), 0)
    sel = jnp.zeros((N_EXPERTS, tm), F32)
    picks = []
    firsts = []
    for _ in range(TOP_K):
        m = jnp.max(cur, axis=0, keepdims=True)
        first = jnp.min(jnp.where(cur == m, iota_e, N_EXPERTS), axis=0, keepdims=True)
        pick = iota_e == first
        sel = jnp.where(pick, 1.0, sel)
        cur = jnp.where(pick, neg_inf, cur)
        picks.append(pick)
        firsts.append(first)
    wsel = sel * scores
    denom = jnp.sum(wsel, axis=0, keepdims=True) + 1e-20
    gates = (ROUTED_SCALE * wsel) / denom

    upto = (lax.broadcasted_iota(jnp.int32, (tm, tm), 0) <= lax.broadcasted_iota(jnp.int32, (tm, tm), 1))
    csum = jnp.dot(sel.astype(BF16), jnp.where(upto, 1.0, 0.0).astype(BF16), preferred_element_type=F32)
    before = base_cnt + csum - sel
    iota_k = lax.broadcasted_iota(jnp.int32, (TOP_K, tm), 0)
    eidx = jnp.zeros((TOP_K, tm), jnp.int32)
    rank = jnp.zeros((TOP_K, tm), F32)
    w_k = jnp.zeros((TOP_K, tm), F32)
    for i in range(TOP_K):
        eidx = jnp.where(iota_k == i, firsts[i], eidx)
        rank = jnp.where(iota_k == i, jnp.sum(jnp.where(picks[i], before, 0.0), axis=0, keepdims=True), rank)
        w_k = jnp.where(iota_k == i, jnp.sum(jnp.where(picks[i], gates, 0.0), axis=0, keepdims=True), w_k)
    return gates, eidx, rank.astype(jnp.int32), w_k, jnp.sum(sel, axis=1, keepdims=True)


def _merge_kernel(ya_ref, yb_ref, mg_ref, x_ref, ga1_ref, sc2_ref, sh2_ref, pa_ref, pb_ref, wo_ref,
                  g1_ref, b1_ref, rwt_ref, rb_ref, x1_ref, h2_ref, gt_ref, hp_ref, eidx_ref, rank_ref, wk_ref,
                  cnt_ref, cnt_s, *, dn_alpha):
    bb, tl, d = x_ref.shape
    tm = bb * tl

    @pl.when((pl.program_id(0) == 0) & (pl.program_id(1) == 0))
    def _():
        cnt_s[...] = jnp.zeros_like(cnt_s)

    ya = ya_ref[...].reshape(tm, V_A)
    yb = yb_ref[...].reshape(tm, D_B)
    mg = mg_ref[...].reshape(tm, 2 * d)
    merged = (jax.nn.sigmoid(mg[:, :d]) * jnp.dot(ya, pa_ref[...], preferred_element_type=F32)
              + jax.nn.sigmoid(mg[:, d:]) * jnp.dot(yb, pb_ref[...], preferred_element_type=F32))
    mix = _dot(merged, wo_ref[...])
    y = dn_alpha * x_ref[...] + (1.0 + ga1_ref[...]) * mix.reshape(bb, tl, d)
    x1 = _layernorm(y, g1_ref[...], b1_ref[...])
    x1_ref[...] = x1
    h2 = (x1 * (1.0 + sc2_ref[...]) + sh2_ref[...]).astype(BF16)
    h2_ref[...] = h2
    h2f = h2.reshape(tm, d)
    hp_ref[...] = _pack_bf16_pairs(h2f.astype(F32))
    logits_t = lax.dot_general(rwt_ref[...], h2f, (_NT, ((), ())), preferred_element_type=F32)
    gates, eidx, rank, w_k, tile_cnt = _route(logits_t, rb_ref[...], cnt_s[...])
    gt_ref[...] = gates
    eidx_ref[...] = eidx
    rank_ref[...] = rank
    wk_ref[...] = w_k
    cnt_s[...] = cnt_s[...] + tile_cnt
    cnt_ref[...] = cnt_s[...].astype(jnp.int32)


def _merge(ya, yb, proj, x, ga1, sc2, sh2, p_a, p_b, w_o, ln_g, ln_b, rw_t, r_bias, bb, tl, dn_alpha):
    b, l, d = x.shape
    nj = l // tl
    tm = bb * tl
    t_all = b * l
    full = lambda *shape: pl.BlockSpec(shape, lambda i, j: (0,) * len(shape))
    mod = pl.BlockSpec((bb, 1, d), lambda i, j: (i, 0, 0))
    tok = lambda width, dtype=None: pl.BlockSpec((bb, tl, width), lambda i, j: (i, j, 0))
    per_k = pl.BlockSpec((TOP_K, tm), lambda i, j: (0, i * nj + j))
    return pl.pallas_call(
        functools.partial(_merge_kernel, dn_alpha=dn_alpha),
        grid=(b // bb, nj),
        in_specs=[tok(V_A), tok(D_B),
                  pl.BlockSpec((bb, tl, 2 * d), lambda i, j: (i, j, P_MG // (2 * d))),
                  tok(d), mod, mod, mod,
                  full(V_A, d), full(D_B, d), full(d, d), full(1, d), full(1, d),
                  full(N_EXPERTS, d), full(N_EXPERTS, 1)],
        out_specs=[tok(d), tok(d), pl.BlockSpec((N_EXPERTS, tm), lambda i, j: (0, i * nj + j)),
                   pl.BlockSpec((tm, d // 2), lambda i, j: (i * nj + j, 0)),
                   per_k, per_k, per_k, full(N_EXPERTS, 1)],
        out_shape=[jax.ShapeDtypeStruct((b, l, d), F32),
                   jax.ShapeDtypeStruct((b, l, d), BF16),
                   jax.ShapeDtypeStruct((N_EXPERTS, t_all), F32),
                   jax.ShapeDtypeStruct((t_all, d // 2), jnp.int32),
                   jax.ShapeDtypeStruct((TOP_K, t_all), jnp.int32),
                   jax.ShapeDtypeStruct((TOP_K, t_all), jnp.int32),
                   jax.ShapeDtypeStruct((TOP_K, t_all), F32),
                   jax.ShapeDtypeStruct((N_EXPERTS, 1), jnp.int32)],
        scratch_shapes=[pltpu.VMEM((N_EXPERTS, 1), F32)],
        compiler_params=_cparams(("arbitrary", "arbitrary"), 48),
        name="merge",
    )(ya, yb, proj, x, ga1, sc2, sh2, p_a, p_b, w_o, ln_g, ln_b, rw_t, r_bias)


def _moe_kernel(h_ref, g_ref, x1_ref, ga2_ref, wg_ref, wu_ref, wd_ref, sgu_ref, sd_ref, g2_ref, b2_ref,
                o_ref, acc_ref, *, dn_alpha):
    e = pl.program_id(2)
    bb, tl, d = h_ref.shape
    tm = bb * tl
    h = h_ref[...].reshape(tm, d)

    @pl.when(e == 0)
    def _():
        su = jnp.dot(h, sgu_ref[...], preferred_element_type=F32)
        act = _silu(su[:, :D_SHARED]) * su[:, D_SHARED:]
        acc_ref[...] = _dot(act, sd_ref[...])

    lane = lax.broadcasted_iota(jnp.int32, (tm, N_EXPERTS), 1)
    gates = g_ref[...]
    acts = []
    for i in range(EXPERTS_PER_STEP):
        g = jnp.dot(h, wg_ref[i].astype(BF16), preferred_element_type=F32)
        u = jnp.dot(h, wu_ref[i].astype(BF16), preferred_element_type=F32)
        gate = _lane_pick(gates, lane, e * EXPERTS_PER_STEP + i)
        acts.append((_silu(g) * u * gate).astype(BF16))
    wd = wd_ref[...].reshape(EXPERTS_PER_STEP * D_EXPERT, d).astype(BF16)
    acc_ref[...] += jnp.dot(jnp.concatenate(acts, axis=1), wd, preferred_element_type=F32)

    @pl.when(e == pl.num_programs(2) - 1)
    def _():
        y = dn_alpha * x1_ref[...] + (1.0 + ga2_ref[...]) * acc_ref[...].reshape(bb, tl, d)
        o_ref[...] = _layernorm(y, g2_ref[...], b2_ref[...])


def _moe(h2, gates, x1, ga2, we_gate, we_up, we_down, ws_gu, ws_down, ln_g, ln_b, bb, tl, dn_alpha):
    b, l, d = x1.shape
    nj = l // tl
    tm = bb * tl
    full = lambda *shape: pl.BlockSpec(shape, lambda i, j, e: (0,) * len(shape))
    tok = pl.BlockSpec((bb, tl, d), lambda i, j, e: (i, j, 0))
    return pl.pallas_call(
        functools.partial(_moe_kernel, dn_alpha=dn_alpha),
        grid=(b // bb, nj, N_EXPERTS // EXPERTS_PER_STEP),
        in_specs=[tok,
                  pl.BlockSpec((tm, N_EXPERTS), lambda i, j, e: (i * nj + j, 0)),
                  tok,
                  pl.BlockSpec((bb, 1, d), lambda i, j, e: (i, 0, 0)),
                  pl.BlockSpec((EXPERTS_PER_STEP, d, D_EXPERT), lambda i, j, e: (e, 0, 0)),
                  pl.BlockSpec((EXPERTS_PER_STEP, d, D_EXPERT), lambda i, j, e: (e, 0, 0)),
                  pl.BlockSpec((EXPERTS_PER_STEP, D_EXPERT, d), lambda i, j, e: (e, 0, 0)),
                  full(d, 2 * D_SHARED), full(D_SHARED, d), full(1, d), full(1, d)],
        out_specs=tok,
        out_shape=jax.ShapeDtypeStruct((b, l, d), F32),
        scratch_shapes=[pltpu.VMEM((tm, d), F32)],
        compiler_params=_cparams(("parallel", "parallel", "arbitrary"), 56),
        name="moe",
    )(h2, gates, x1, ga2, we_gate, we_up, we_down, ws_gu, ws_down, ln_g, ln_b)


def _sc_mesh():
    return plsc.VectorSubcoreMesh(core_axis_name="c", subcore_axis_name="s")


def _sc_scatter_rows(x, pos, n_out):
    t, w = x.shape
    k = pos.shape[0]
    t_per_w = t // SC_WORKERS
    n_cores = plsc.get_sparse_core_info().num_cores

    @functools.partial(
        pl.kernel, mesh=_sc_mesh(),
        out_type=jax.ShapeDtypeStruct((n_out, w), jnp.int32),
        scratch_types=[pltpu.VMEM((k, SC_CHUNK), jnp.int32), pltpu.VMEM((SC_CHUNK, w), jnp.int32),
                       pltpu.SemaphoreType.DMA],
    )
    def scatter_kernel(x_hbm, pos_hbm, out_hbm, idx_v, rows_v, sem):
        base = (lax.axis_index("s") * n_cores + lax.axis_index("c")) * t_per_w

        @pl.loop(0, t_per_w // SC_CHUNK)
        def _(i):
            off = pl.multiple_of(base + i * SC_CHUNK, SC_CHUNK)
            pltpu.sync_copy(pos_hbm.at[:, pl.ds(off, SC_CHUNK)], idx_v)
            pltpu.sync_copy(x_hbm.at[pl.ds(off, SC_CHUNK)], rows_v)
            for j in range(k):
                pltpu.async_copy(rows_v, out_hbm.at[idx_v.at[j]], sem).wait()

    return scatter_kernel(x, pos)


def _sc_gather_rows(table, idx):
    n = idx.shape[0]
    w = table.shape[1]
    n_per_w = n // SC_WORKERS
    n_cores = plsc.get_sparse_core_info().num_cores

    @functools.partial(
        pl.kernel, mesh=_sc_mesh(),
        out_type=jax.ShapeDtypeStruct((n, w), jnp.int32),
        scratch_types=[pltpu.VMEM((SC_CHUNK,), jnp.int32), pltpu.VMEM((SC_CHUNK, w), jnp.int32),
                       pltpu.SemaphoreType.DMA],
    )
    def gather_kernel(table_hbm, idx_hbm, out_hbm, idx_v, rows_v, sem):
        base = (lax.axis_index("s") * n_cores + lax.axis_index("c")) * n_per_w

        @pl.loop(0, n_per_w // SC_CHUNK)
        def _(i):
            off = pl.multiple_of(base + i * SC_CHUNK, SC_CHUNK)
            pltpu.sync_copy(idx_hbm.at[pl.ds(off, SC_CHUNK)], idx_v)
            pltpu.async_copy(table_hbm.at[idx_v], rows_v, sem).wait()
            pltpu.sync_copy(rows_v, out_hbm.at[pl.ds(off, SC_CHUNK)])

    return gather_kernel(table, idx)


def _experts_kernel(te_ref, used_ref, xs_ref, wg_ref, wu_ref, wd_ref, o_ref, wg_s, wu_s, wd_s):
    i = pl.program_id(0)

    @pl.when((i == 0) | (te_ref[i] != te_ref[jnp.maximum(i - 1, 0)]))
    def _():
        wg_s[...] = wg_ref[0].astype(BF16)
        wu_s[...] = wu_ref[0].astype(BF16)
        wd_s[...] = wd_ref[0].astype(BF16)

    @pl.when(i < used_ref[0])
    def _():
        x = _unpack_bf16_pairs(xs_ref[...]).astype(BF16)
        g = jnp.dot(x, wg_s[...], preferred_element_type=F32)
        u = jnp.dot(x, wu_s[...], preferred_element_type=F32)
        act = (_silu(g) * u).astype(BF16)
        o_ref[...] = _pack_bf16_pairs(jnp.dot(act, wd_s[...], preferred_element_type=F32))


def _experts(xs, tile_expert, tiles_used, we_gate, we_up, we_down):
    r, half = xs.shape
    d = 2 * half
    grid_spec = pltpu.PrefetchScalarGridSpec(
        num_scalar_prefetch=2,
        grid=(r // EXPERT_TILE,),
        in_specs=[pl.BlockSpec((EXPERT_TILE, half), lambda i, te, used: (i, 0)),
                  pl.BlockSpec((1, d, D_EXPERT), lambda i, te, used: (te[i], 0, 0)),
                  pl.BlockSpec((1, d, D_EXPERT), lambda i, te, used: (te[i], 0, 0)),
                  pl.BlockSpec((1, D_EXPERT, d), lambda i, te, used: (te[i], 0, 0))],
        out_specs=pl.BlockSpec((EXPERT_TILE, half), lambda i, te, used: (i, 0)),
        scratch_shapes=[pltpu.VMEM((d, D_EXPERT), BF16), pltpu.VMEM((d, D_EXPERT), BF16),
                        pltpu.VMEM((D_EXPERT, d), BF16)],
    )
    return pl.pallas_call(
        _experts_kernel,
        grid_spec=grid_spec,
        out_shape=jax.ShapeDtypeStruct((r, half), jnp.int32),
        compiler_params=_cparams(("arbitrary",), 32),
        name="experts",
    )(tile_expert, tiles_used, xs, we_gate, we_up, we_down)


def _combine_kernel(og_ref, wk_ref, h_ref, x1_ref, ga2_ref, sgu_ref, sd_ref, g2_ref, b2_ref, o_ref, *, dn_alpha):
    bb, tl, d = h_ref.shape
    tm = bb * tl
    h = h_ref[...].reshape(tm, d)
    su = jnp.dot(h, sgu_ref[...], preferred_element_type=F32)
    acc = _dot(_silu(su[:, :D_SHARED]) * su[:, D_SHARED:], sd_ref[...])
    lane = lax.broadcasted_iota(jnp.int32, (tm, TOP_K), 1)
    w_all = wk_ref[...]
    for k in range(TOP_K):
        acc = acc + _lane_pick(w_all, lane, k) * _unpack_bf16_pairs(og_ref[k])
    y = dn_alpha * x1_ref[...] + (1.0 + ga2_ref[...]) * acc.reshape(bb, tl, d)
    o_ref[...] = _layernorm(y, g2_ref[...], b2_ref[...])


def _combine(og, w_tk, h2, x1, ga2, ws_gu, ws_down, ln_g, ln_b, bb, tl, dn_alpha):
    b, l, d = x1.shape
    nj = l // tl
    tm = bb * tl
    full = lambda *shape: pl.BlockSpec(shape, lambda i, j: (0,) * len(shape))
    tok = pl.BlockSpec((bb, tl, d), lambda i, j: (i, j, 0))
    return pl.pallas_call(
        functools.partial(_combine_kernel, dn_alpha=dn_alpha),
        grid=(b // bb, nj),
        in_specs=[pl.BlockSpec((TOP_K, tm, d // 2), lambda i, j: (0, i * nj + j, 0)),
                  pl.BlockSpec((tm, TOP_K), lambda i, j: (i * nj + j, 0)),
                  tok, tok,
                  pl.BlockSpec((bb, 1, d), lambda i, j: (i, 0, 0)),
                  full(d, 2 * D_SHARED), full(D_SHARED, d), full(1, d), full(1, d)],
        out_specs=tok,
        out_shape=jax.ShapeDtypeStruct((b, l, d), F32),
        compiler_params=_cparams(("parallel", "parallel"), 48),
        name="combine",
    )(og, w_tk, h2, x1, ga2, ws_gu, ws_down, ln_g, ln_b)


def _moe_sparse(hp, eidx, rank, w_k, cnt, h2, x1, ga2, p, bb, dn_alpha):
    t_all = hp.shape[0]
    n_tiles = t_all * TOP_K // EXPERT_TILE + N_EXPERTS
    seg_tiles = (cnt[:, 0] + EXPERT_TILE - 1) // EXPERT_TILE
    seg_end = jnp.cumsum(seg_tiles)
    seg_start = (seg_end - seg_tiles) * EXPERT_TILE
    experts = jnp.arange(N_EXPERTS, dtype=jnp.int32)
    pos = rank + jnp.sum(jnp.where(eidx[None] == experts[:, None, None], seg_start[:, None, None], 0), axis=0)
    tiles = jnp.arange(n_tiles, dtype=jnp.int32)
    tile_expert = jnp.minimum(jnp.sum((seg_end[None, :] <= tiles[:, None]).astype(jnp.int32), axis=1),
                              N_EXPERTS - 1)
    xs = _sc_scatter_rows(hp, pos, n_tiles * EXPERT_TILE)
    outs = _experts(xs, tile_expert, seg_end[-1:].astype(jnp.int32), p['we_gate'], p['we_up'], p['we_down'])
    og = _sc_gather_rows(outs, pos.reshape(-1)).reshape(TOP_K, t_all, hp.shape[1])
    return _combine(og, w_k.T, h2, x1, ga2, p['ws_gu'], p['ws_down'], p['ln2_g'], p['ln2_b'], bb, COMBINE_TL, dn_alpha)


def _pad_rows(buf, width):
    b, n, w = buf.shape
    return jnp.pad(buf.astype(F32), ((0, 0), (SUBLANES - n, 0), (0, width - w)))


def _to_pairs(s):
    b = s.shape[0]
    s = s.astype(F32).reshape(b, PAIRS, 2, N_B, N_B)
    zero = jnp.zeros_like(s[:, :, 0])
    top = jnp.concatenate([s[:, :, 0], zero], axis=-1)
    bot = jnp.concatenate([zero, s[:, :, 1]], axis=-1)
    return jnp.concatenate([top, bot], axis=-2)


def _from_pairs(sp):
    b = sp.shape[0]
    return jnp.stack([sp[:, :, :N_B, :N_B], sp[:, :, N_B:, N_B:]], axis=2).reshape(b, H_B, N_B, N_B)


def _layer(x, mod, conv_buf, gdn_s, shift_buf, rwkv_s, p, tiles, dn_alpha):
    b, l, d = x.shape
    sh1, sc1, ga1, sh2, sc2, ga2 = mod
    lp = -(-l // CHUNK) * CHUNK
    x_in = x if lp == l else jnp.pad(x, ((0, 0), (0, lp - l), (0, 0)))
    proj = _inproj(x_in, sc1, sh1, p['w_in'], tiles['bb_in'], tiles['tl_in'])
    ya, gdn_new = _gdn(proj, _pad_rows(conv_buf, QKV_A), gdn_s.astype(F32), p['conv_w'], p['alog_row'],
                       p['dt_row'], p['gdn_norm_w'], tiles['bb_gdn'], tiles['tl_gdn'], l)
    yb, rwkv_pairs = _rwkv(proj, _pad_rows(shift_buf, RW_BLOCK), _to_pairs(rwkv_s), p['rwkv_vecs'],
                           p['w2p'], p['a2p'], p['g2'], tiles['bb_rwkv'], tiles['tl_rwkv'], l)
    if lp != l:
        ya, yb = ya[:, :l], yb[:, :l]

    x1, h2, gates_t, hp, eidx, rank, w_k, cnt = _merge(
        ya, yb, proj, x, ga1, sc2, sh2, p['p_a'], p['p_b'], p['w_o'], p['ln1_g'], p['ln1_b'],
        p['router_wt'], p['router_bias'], tiles['bb'], tiles['tl_merge'], dn_alpha)
    if (b * l) % (SC_WORKERS * SC_CHUNK) == 0:
        out = _moe_sparse(hp, eidx, rank, w_k, cnt, h2, x1, ga2, p, tiles['bb'], dn_alpha)
    else:
        out = _moe(h2, gates_t.T, x1, ga2, p['we_gate'], p['we_up'], p['we_down'], p['ws_gu'], p['ws_down'],
                   p['ln2_g'], p['ln2_b'], tiles['bb'], tiles['tl'], dn_alpha)

    pre = jnp.concatenate([conv_buf.astype(F32), proj[:, :l, P_QKV:P_QKV + QKV_A]], axis=1)
    conv_new = pre[:, -(CONV_W - 1):]
    shift_new = proj[:, l - 1:l, P_RW:P_RW + SHIFT_W]
    return (out, conv_new.astype(conv_buf.dtype), gdn_new.astype(gdn_s.dtype),
            shift_new.astype(shift_buf.dtype), _from_pairs(rwkv_pairs).astype(rwkv_s.dtype))


def _prep_params(l, w_in, conv_w, a_log, dt_bias, gdn_norm_w, mu_shift, w0, w2, a0, a2, g2, k_k, k_a, r_k,
                 lnx_g, lnx_b, p_a, p_b, w_o, ln1_g, ln1_b, router_w, router_bias, we_gate, we_up, we_down,
                 ws_gate, ws_up, ws_down, ln2_g, ln2_b):
    d = D_MODEL
    w = w_in[l]
    w_p = jnp.concatenate(
        [w[:, :QKV_A], w[:, OFF_Z:OFF_RWKV], w[:, OFF_RWKV:OFF_MERGE],
         jnp.zeros((d, RW_BLOCK - SHIFT_W), w.dtype), w[:, OFF_MERGE:], w[:, OFF_ALPHA:OFF_Z],
         jnp.zeros((d, N_PROJ - P_AB - 2 * H_A), w.dtype)], axis=1).astype(BF16)
    row = lambda v, width: jnp.pad(v.astype(F32).reshape(1, -1), ((0, 0), (0, width - v.size)))
    zeros_lora = jnp.zeros((W_LORA, D_B), F32)
    return {
        'w_in': w_p,
        'conv_w': conv_w[l].astype(F32),
        'alog_row': row(a_log[l], LANES),
        'dt_row': row(dt_bias[l], LANES),
        'gdn_norm_w': row(gdn_norm_w[l], DK_A),
        'rwkv_vecs': (row(mu_shift[l], RW_BLOCK), row(w0[l], D_B), row(a0[l], D_B), row(k_k[l], D_B),
                      row(k_a[l], D_B), row(r_k[l], D_B), row(lnx_g[l], D_B), row(lnx_b[l], D_B)),
        'w2p': jnp.concatenate([w2[l].astype(F32), zeros_lora], axis=0),
        'a2p': jnp.concatenate([zeros_lora, a2[l].astype(F32)], axis=0),
        'g2': g2[l].astype(F32),
        'p_a': p_a[l].astype(BF16), 'p_b': p_b[l].astype(BF16), 'w_o': w_o[l].astype(BF16),
        'ln1_g': row(ln1_g[l], d), 'ln1_b': row(ln1_b[l], d),
        'router_wt': router_w[l].T.astype(BF16),
        'router_bias': router_bias[l].astype(F32).reshape(N_EXPERTS, 1),
        'we_gate': we_gate[l], 'we_up': we_up[l], 'we_down': we_down[l],
        'ws_gu': jnp.concatenate([ws_gate[l], ws_up[l]], axis=-1).astype(BF16),
        'ws_down': ws_down[l].astype(BF16),
        'ln2_g': row(ln2_g[l], d), 'ln2_b': row(ln2_b[l], d),
    }


def kernel(x_prompt, x_sample, c_prompt, c_sample, state_gdn_conv, state_gdn, state_rwkv_shift, state_rwkv, w_ada, b_ada, w_in, conv_w, a_log, dt_bias, gdn_norm_w, mu_shift, w0, w2, a0, a2, g2, k_k, k_a, r_k, lnx_g, lnx_b, p_a, p_b, w_o, ln1_g, ln1_b, router_w, router_bias, we_gate, we_up, we_down, ws_gate, ws_up, ws_down, ln2_g, ln2_b):
    depth = w_ada.shape[0]
    dn_alpha = (2 * depth) ** 0.25
    bp, lp_, d = x_prompt.shape
    bs, ls, _ = x_sample.shape
    dtp = x_prompt.dtype
    tiles_p = {'bb': 1, 'tl': 1024, 'bb_in': 1, 'tl_in': 1024, 'bb_gdn': 4, 'tl_gdn': CHUNK,
               'bb_rwkv': 2, 'tl_rwkv': 2 * CHUNK, 'tl_merge': 512}
    tiles_s = {'bb': bs, 'tl': ls, 'bb_in': 8, 'tl_in': CHUNK, 'bb_gdn': 4, 'tl_gdn': CHUNK,
               'bb_rwkv': 2, 'tl_rwkv': CHUNK, 'tl_merge': ls}

    yp, ys = x_prompt, x_sample
    new_p = ([], [], [], [])
    new_s = ([], [], [], [])
    for l in range(depth):
        p = _prep_params(l, w_in, conv_w, a_log, dt_bias, gdn_norm_w, mu_shift, w0, w2, a0, a2, g2, k_k, k_a,
                         r_k, lnx_g, lnx_b, p_a, p_b, w_o, ln1_g, ln1_b, router_w, router_bias, we_gate, we_up,
                         we_down, ws_gate, ws_up, ws_down, ln2_g, ln2_b)
        mod = _ada(jnp.concatenate([c_prompt, c_sample], axis=0), w_ada[l], b_ada[l])
        mod_p = tuple(m[:, None, :] for m in jnp.split(mod[:bp], 6, axis=-1))
        mod_s = tuple(m[:, None, :] for m in jnp.split(mod[bp:], 6, axis=-1))
        yp, *sp = _layer(yp, mod_p,
                         jnp.zeros((bp, CONV_W - 1, QKV_A), dtp), jnp.zeros((bp, H_A, DK_A, DK_A), dtp),
                         jnp.zeros((bp, 1, SHIFT_W), dtp), jnp.zeros((bp, H_B, N_B, N_B), dtp),
                         p, tiles_p, dn_alpha)
        ys, *ss = _layer(ys, mod_s, state_gdn_conv[l], state_gdn[l], state_rwkv_shift[l], state_rwkv[l],
                         p, tiles_s, dn_alpha)
        for lst, val in zip(new_p, sp):
            lst.append(val)
        for lst, val in zip(new_s, ss):
            lst.append(val)
    conv_p, gdn_p, shift_p, rwkv_p = [jnp.stack(t, 0) for t in new_p]
    conv_s, gdn_s, shift_s, rwkv_s = [jnp.stack(t, 0) for t in new_s]
    return (yp, ys, conv_p, gdn_p, shift_p, rwkv_p, conv_s, gdn_s, shift_s, rwkv_s)
```

```python
import functools
import math

import jax
import jax.numpy as jnp
from jax import lax
from jax.experimental import pallas as pl
from jax.experimental.pallas import tpu as pltpu
from jax.experimental.pallas import tpu_sc as plsc

F32 = jnp.float32
BF16 = jnp.bfloat16

D_MODEL = 1024
DK_A = 128
H_A = 4
QK_A = H_A * DK_A
V_A = H_A * DK_A
QKV_A = 2 * QK_A + V_A
CONV_W = 4
N_B = 64
H_B = 8
D_B = H_B * N_B
W_LORA = 64
A_LORA = 64
G_LORA = 128
SHIFT_W = 3 * D_B + W_LORA + A_LORA + G_LORA
OFF_ALPHA = QKV_A
OFF_BETA = OFF_ALPHA + H_A
OFF_Z = OFF_BETA + H_A
OFF_RWKV = OFF_Z + V_A
OFF_MERGE = OFF_RWKV + SHIFT_W
N_EXPERTS = 64
TOP_K = 8
N_GROUPS = 8
GROUP_SIZE = N_EXPERTS // N_GROUPS
TOPK_GROUPS = 4
D_EXPERT = 256
D_SHARED = 256
ROUTED_SCALE = 2.5
LN_EPS = 1e-5
GN_EPS = 64e-5
RMS_EPS = 1e-6
DECAY_SCALE = -math.exp(-0.5)

SUBLANES = 8
LANES = 128

P_QKV = 0
P_Z = QKV_A
P_RW = 2048
RW_BLOCK = 2048
P_MG = P_RW + RW_BLOCK
P_AB = P_MG + 2 * D_MODEL
AB_BLOCK = LANES
PROJ_TN = 1280
N_PROJ = 5 * PROJ_TN

CHUNK = 128
BASE_BLOCK = 8
CUMSUM_PARTS = 3
STAT_PARTS = 2
EXPERTS_PER_STEP = 2
HIGH_HALF = -65536

EXPERT_TILE = 1024
SC_WORKERS = 32
SC_CHUNK = 128
COMBINE_TL = 512


def _cparams(sem, vmem_mb):
    return pltpu.CompilerParams(dimension_semantics=sem, vmem_limit_bytes=vmem_mb * 1024 * 1024)


def _dot(a, b):
    return jnp.dot(a.astype(BF16), b.astype(BF16), preferred_element_type=F32)


def _rdot(a, b, dims=((1,), (0,))):
    return lax.dot_general(a.astype(BF16), b.astype(BF16), (dims, ((), ())), preferred_element_type=F32)


_NT = ((1,), (1,))
_TN = ((0,), (0,))


def _bf16_parts(x, parts):
    out = []
    rem = x
    for _ in range(parts):
        hi = rem.astype(BF16)
        out.append(hi)
        rem = rem - hi.astype(F32)
    return out


def _mask_dot_left(mask, x, parts):
    return sum(jnp.dot(mask, p, preferred_element_type=F32) for p in _bf16_parts(x, parts))


def _mask_dot_right(x, mask, parts):
    return sum(jnp.dot(p, mask, preferred_element_type=F32) for p in _bf16_parts(x, parts))


def _lane_pick(x, lane_iota, lane):
    return jnp.sum(jnp.where(lane_iota == lane, x, 0.0), axis=-1, keepdims=True)


def _pack_bf16_pairs(x):
    n = x.shape[1] // 2
    bits = lax.bitcast_convert_type(x.astype(BF16).astype(F32), jnp.int32)
    return (bits[:, :n] & HIGH_HALF) | lax.shift_right_logical(bits[:, n:], 16)


def _unpack_bf16_pairs(p):
    hi = lax.bitcast_convert_type(p & HIGH_HALF, F32)
    lo = lax.bitcast_convert_type(lax.shift_left(p, 16), F32)
    return jnp.concatenate([hi, lo], axis=1)


def _silu(x):
    return x * jax.nn.sigmoid(x)


def _softplus(x):
    return jnp.maximum(x, 0.0) + jnp.log1p(jnp.exp(-jnp.abs(x)))


def _tri_inverse(mats, eye, sign, row, col):
    base = (row // BASE_BLOCK) == (col // BASE_BLOCK)
    ds = [jnp.where(base, a, 0.0) for a in mats]
    xs = [eye + d if sign > 0 else eye - d for d in ds]
    power = 2
    while power < BASE_BLOCK:
        ds = [_rdot(d, d) for d in ds]
        xs = [x + _rdot(x, d) for x, d in zip(xs, ds)]
        power *= 2
    b = BASE_BLOCK
    while b < CHUNK:
        sibling = ((row // b) == (col // b) + 1) & ((row // (2 * b)) == (col // (2 * b)))
        offs = [jnp.where(sibling, a, 0.0) for a in mats]
        ts = [_rdot(o, x) for o, x in zip(offs, xs)]
        if sign > 0:
            xs = [x + _rdot(x, t_) for x, t_ in zip(xs, ts)]
        else:
            xs = [x - _rdot(x, t_) for x, t_ in zip(xs, ts)]
        b *= 2
    return xs


def _tri_consts():
    row = lax.broadcasted_iota(jnp.int32, (CHUNK, CHUNK), 0)
    col = lax.broadcasted_iota(jnp.int32, (CHUNK, CHUNK), 1)
    incl = row >= col
    strict = row > col
    eye = jnp.where(row == col, 1.0, 0.0).astype(F32)
    tril = jnp.where(incl, 1.0, 0.0).astype(BF16)
    return row, col, incl, strict, eye, tril


def _ada_kernel(c_ref, w_ref, b_ref, o_ref):
    o_ref[...] = _dot(_silu(c_ref[...]), w_ref[...]) + b_ref[...]


def _ada(c, w_ada, b_ada):
    n, d = c.shape
    nout = w_ada.shape[1]
    tn = 768
    return pl.pallas_call(
        _ada_kernel,
        grid=(nout // tn,),
        in_specs=[pl.BlockSpec((n, d), lambda j: (0, 0)),
                  pl.BlockSpec((d, tn), lambda j: (0, j)),
                  pl.BlockSpec((1, tn), lambda j: (0, j))],
        out_specs=pl.BlockSpec((n, tn), lambda j: (0, j)),
        out_shape=jax.ShapeDtypeStruct((n, nout), F32),
        compiler_params=_cparams(("parallel",), 32),
        name="ada",
    )(c, w_ada, b_ada.reshape(1, nout))


def _inproj_kernel(x_ref, sc_ref, sh_ref, w_ref, o_ref, hs_ref):
    @pl.when(pl.program_id(2) == 0)
    def _():
        h = x_ref[...] * (1.0 + sc_ref[...]) + sh_ref[...]
        hs_ref[...] = h.reshape(hs_ref.shape).astype(BF16)

    o = jnp.dot(hs_ref[...], w_ref[...], preferred_element_type=F32)
    o_ref[...] = o.reshape(o_ref.shape)


def _inproj(x, sc, sh, w_p, bb, tl):
    b, l, d = x.shape
    return pl.pallas_call(
        _inproj_kernel,
        grid=(b // bb, l // tl, N_PROJ // PROJ_TN),
        in_specs=[pl.BlockSpec((bb, tl, d), lambda i, j, n: (i, j, 0)),
                  pl.BlockSpec((bb, 1, d), lambda i, j, n: (i, 0, 0)),
                  pl.BlockSpec((bb, 1, d), lambda i, j, n: (i, 0, 0)),
                  pl.BlockSpec((d, PROJ_TN), lambda i, j, n: (0, n))],
        out_specs=pl.BlockSpec((bb, tl, PROJ_TN), lambda i, j, n: (i, j, n)),
        out_shape=jax.ShapeDtypeStruct((b, l, N_PROJ), F32),
        scratch_shapes=[pltpu.VMEM((bb * tl, d), BF16)],
        compiler_params=_cparams(("parallel", "parallel", "arbitrary"), 48),
        name="inproj",
    )(x, sc, sh, w_p)


def _gdn_kernel(qkv_ref, ab_ref, z_ref, cbuf_ref, s0_ref, cw_ref, alog_ref, dt_ref, nw_ref,
                ya_ref, sout_ref, ext_ref, qc_ref, s_ref, *, bb, tl, l_valid, l_total):
    t = pl.program_id(1)

    @pl.when(t == 0)
    def _():
        ext_ref[:, 0:SUBLANES, :] = cbuf_ref[...]
        s_ref[...] = s0_ref[...]

    first = SUBLANES - (CONV_W - 1)
    for bi in range(bb):
        ext_ref[bi, SUBLANES:SUBLANES + tl, :] = qkv_ref[bi]
        acc = cw_ref[0:1, :] * ext_ref[bi, first:first + tl, :]
        for j in range(1, CONV_W):
            acc = acc + cw_ref[j:j + 1, :] * ext_ref[bi, first + j:first + j + tl, :]
        ext_ref[bi, 0:SUBLANES, :] = ext_ref[bi, tl:tl + SUBLANES, :]
        qc_ref[bi] = _silu(acc)

    row, col, incl, strict, eye, tril = _tri_consts()
    neg_a = -jnp.exp(alog_ref[...])
    dt = dt_ref[...]
    nw = nw_ref[...]
    chains = [(bi, h) for bi in range(bb) for h in range(H_A)]
    n = range(len(chains))

    def chunk(c, carry):
        r0 = pl.multiple_of(c * CHUNK, CHUNK)
        rows = pl.ds(r0, CHUNK)
        g_cum = []
        b_all = []
        for bi in range(bb):
            ab = ab_ref[bi, rows, :]
            g_bi = neg_a * _softplus(ab + dt)
            b_bi = jax.nn.sigmoid(ab)
            if l_valid < l_total:
                valid = (t * tl + r0 + row) < l_valid
                g_bi = jnp.where(valid, g_bi, 0.0)
                b_bi = jnp.where(valid, b_bi, 0.0)
            g_cum.append(_mask_dot_left(tril, g_bi, CUMSUM_PARTS))
            b_all.append(b_bi)
        g_col = [_lane_pick(g_cum[bi], col, h) for bi, h in chains]
        beta = [_lane_pick(b_all[bi], col, H_A + h) for bi, h in chains]
        decay = []
        for i in n:
            g_b = jnp.broadcast_to(g_col[i], (CHUNK, CHUNK))
            decay.append(jnp.where(incl, jnp.exp(jnp.where(incl, g_b - g_b.T, 0.0)), 0.0))
        q = [qc_ref[bi, rows, h * DK_A:(h + 1) * DK_A] for bi, h in chains]
        k = [qc_ref[bi, rows, QK_A + h * DK_A:QK_A + (h + 1) * DK_A] for bi, h in chains]
        v = [qc_ref[bi, rows, 2 * QK_A + h * DK_A:2 * QK_A + (h + 1) * DK_A] for bi, h in chains]
        q = [x * lax.rsqrt(jnp.sum(x * x, -1, keepdims=True) + 1e-6) * (DK_A ** -0.5) for x in q]
        k = [x * lax.rsqrt(jnp.sum(x * x, -1, keepdims=True) + 1e-6) for x in k]
        kq = [_rdot(jnp.concatenate([k[i], q[i]], axis=0), k[i], _NT) for i in n]
        a = [jnp.where(strict, beta[i] * kq[i][:CHUNK] * decay[i], 0.0) for i in n]
        qk = [jnp.where(incl, kq[i][CHUNK:] * decay[i], 0.0) for i in n]
        x = _tri_inverse(a, eye, -1, row, col)
        e_g = [jnp.exp(g) for g in g_col]
        uw = [_rdot(x[i], jnp.concatenate([v[i] * beta[i], k[i] * (beta[i] * e_g[i])], axis=1)) for i in n]
        g_last = [g[CHUNK - 1:CHUNK, :] for g in g_col]
        kd = [k[i] * jnp.exp(g_last[i] - g_col[i]) for i in n]
        s = [s_ref[bi, h] for bi, h in chains]
        ws = [_rdot(jnp.concatenate([uw[i][:, DK_A:], q[i] * e_g[i]], axis=0), s[i]) for i in n]
        v_new = [uw[i][:, :DK_A] - ws[i][:CHUNK] for i in n]
        o = [ws[i][CHUNK:] + _rdot(qk[i], v_new[i]) for i in n]
        s_new = [s[i] * jnp.exp(g_last[i]) + _rdot(kd[i], v_new[i], _TN) for i in n]
        for i, (bi, h) in enumerate(chains):
            s_ref[bi, h] = s_new[i]
            z = z_ref[bi, rows, h * DK_A:(h + 1) * DK_A]
            on = o[i] * lax.rsqrt(jnp.mean(o[i] * o[i], -1, keepdims=True) + RMS_EPS) * nw * _silu(z)
            ya_ref[bi, rows, h * DK_A:(h + 1) * DK_A] = on.astype(BF16)
        return carry

    lax.fori_loop(0, tl // CHUNK, chunk, 0)

    @pl.when(t == pl.num_programs(1) - 1)
    def _():
        sout_ref[...] = s_ref[...]


def _gdn(proj, conv_buf8, s0, conv_w, alog_row, dt_row, norm_w, bb, tl, l_valid):
    b, l, _ = proj.shape
    kern = functools.partial(_gdn_kernel, bb=bb, tl=tl, l_valid=l_valid, l_total=l)
    full = lambda *shape: pl.BlockSpec(shape, lambda i, t: (0,) * len(shape))
    return pl.pallas_call(
        kern,
        grid=(b // bb, l // tl),
        in_specs=[pl.BlockSpec((bb, tl, QKV_A), lambda i, t: (i, t, P_QKV // QKV_A)),
                  pl.BlockSpec((bb, tl, AB_BLOCK), lambda i, t: (i, t, P_AB // AB_BLOCK)),
                  pl.BlockSpec((bb, tl, V_A), lambda i, t: (i, t, P_Z // V_A)),
                  pl.BlockSpec((bb, SUBLANES, QKV_A), lambda i, t: (i, 0, 0)),
                  pl.BlockSpec((bb, H_A, DK_A, DK_A), lambda i, t: (i, 0, 0, 0)),
                  full(CONV_W, QKV_A), full(1, LANES), full(1, LANES), full(1, DK_A)],
        out_specs=[pl.BlockSpec((bb, tl, V_A), lambda i, t: (i, t, 0)),
                   pl.BlockSpec((bb, H_A, DK_A, DK_A), lambda i, t: (i, 0, 0, 0))],
        out_shape=[jax.ShapeDtypeStruct((b, l, V_A), BF16),
                   jax.ShapeDtypeStruct((b, H_A, DK_A, DK_A), F32)],
        scratch_shapes=[pltpu.VMEM((bb, tl + SUBLANES, QKV_A), F32),
                        pltpu.VMEM((bb, tl, QKV_A), F32),
                        pltpu.VMEM((bb, H_A, DK_A, DK_A), F32)],
        compiler_params=_cparams(("parallel", "arbitrary"), 48),
        name="gdn",
    )(proj, proj, proj, conv_buf8, s0, conv_w, alog_row, dt_row, norm_w)


PAIRS = H_B // 2


def _headsum(x, bd):
    return jnp.concatenate(
        [_mask_dot_right(x[:, p * LANES:(p + 1) * LANES], bd, STAT_PARTS) for p in range(PAIRS)], axis=1)


def _rwkv_kernel(rw_ref, sbuf_ref, s0_ref, mu_ref, w0_ref, a0_ref, kk_ref, ka_ref, rk_ref, lg_ref, lb_ref,
                 w2_ref, a2_ref, g2_ref, yb_ref, sout_ref,
                 ext_ref, r_s, k_s, v_s, z_s, p_s, lw_s, y_s, bonus_s, gate_s, s_ref, *, bb, tl, l_valid, l_total):
    t = pl.program_id(1)

    @pl.when(t == 0)
    def _():
        ext_ref[:, 0:SUBLANES, :] = sbuf_ref[...]
        s_ref[...] = s0_ref[...]

    row, col, incl, strict, eye, tril = _tri_consts()
    same_head = (row // N_B) == (col // N_B)
    bd = jnp.where(same_head, 1.0, 0.0).astype(BF16)
    lane_a = col < N_B

    for bi in range(bb):
        cur = rw_ref[bi]
        ext_ref[bi, SUBLANES:SUBLANES + tl, :] = cur
        prev = ext_ref[bi, SUBLANES - 1:SUBLANES - 1 + tl, :]
        mixed = cur + (prev - cur) * mu_ref[...]
        ext_ref[bi, 0:SUBLANES, :] = ext_ref[bi, tl:tl + SUBLANES, :]
        r = mixed[:, 0:D_B]
        k = mixed[:, D_B:2 * D_B]
        v = mixed[:, 2 * D_B:3 * D_B]
        lora = mixed[:, 3 * D_B:3 * D_B + W_LORA + A_LORA]
        g_in = mixed[:, 3 * D_B + W_LORA + A_LORA:SHIFT_W]
        lw = DECAY_SCALE * jax.nn.sigmoid(w0_ref[...] + _dot(jnp.tanh(lora), w2_ref[...]))
        a = jax.nn.sigmoid(a0_ref[...] + _dot(lora, a2_ref[...]))
        kkr = k * kk_ref[...]
        kk = kkr * lax.rsqrt(_headsum(kkr * kkr, bd) + 1e-6)
        k = k * (1.0 + (a - 1.0) * ka_ref[...])
        tile_rows = pl.ds(bi * tl, tl)
        bonus_s[tile_rows, :] = _headsum(r * k * rk_ref[...], bd) * v
        gate_s[tile_rows, :] = _dot(jax.nn.sigmoid(g_in), g2_ref[...])
        if l_valid < l_total:
            rvalid = (t * tl + lax.broadcasted_iota(jnp.int32, (tl, 1), 0)) < l_valid
            lw = jnp.where(rvalid, lw, 0.0)
            kk = jnp.where(rvalid, kk, 0.0)
            k = jnp.where(rvalid, k, 0.0)
            v = jnp.where(rvalid, v, 0.0)
        r_s[tile_rows, :] = r
        k_s[tile_rows, :] = k
        v_s[tile_rows, :] = v
        z_s[tile_rows, :] = -kk
        p_s[tile_rows, :] = kk * a
        lw_s[tile_rows, :] = lw

    groups = [(bi, p) for bi in range(bb) for p in range(PAIRS)]
    pairs = range(len(groups))
    both = range(2 * len(groups))
    cols = [slice(p * LANES, (p + 1) * LANES) for _, p in groups]

    def chunk(c, carry):
        r0 = pl.multiple_of(c * CHUNK, CHUNK)
        rows = [pl.ds(bi * tl + r0, CHUNK) for bi, _ in groups]
        lw_c = [lw_s[rows[g], cols[g]] for g in pairs]
        g_inc = [_mask_dot_left(tril, x, CUMSUM_PARTS) for x in lw_c]
        g_exc = [g_inc[p] - lw_c[p] for p in pairs]
        g_mid = [g[CHUNK // 2 - 1:CHUNK // 2, :] for g in g_inc]
        g_end = [g[CHUNK - 1:CHUNK, :] for g in g_inc]
        z = [z_s[rows[g], cols[g]] for g in pairs]
        rr = [r_s[rows[g], cols[g]] for g in pairs]
        pp = [p_s[rows[g], cols[g]] for g in pairs]
        kk_ = [k_s[rows[g], cols[g]] for g in pairs]
        vv = [v_s[rows[g], cols[g]] for g in pairs]
        zt = [z[p] * jnp.exp(g_exc[p] - g_mid[p]) for p in pairs]
        rt = [rr[p] * jnp.exp(g_inc[p] - g_mid[p]) for p in pairs]
        en = [jnp.exp(g_mid[p] - g_inc[p]) for p in pairs]
        s = [s_ref[bi, p] for bi, p in groups]
        lhs = [jnp.concatenate([jnp.where(lane_a, zt[p], 0.0), jnp.where(lane_a, 0.0, zt[p]),
                                jnp.where(lane_a, rt[p], 0.0), jnp.where(lane_a, 0.0, rt[p])], axis=0) for p in pairs]
        m = [_rdot(lhs[p], jnp.concatenate([pp[p] * en[p], kk_[p] * en[p]], axis=0), _NT) for p in pairs]
        zr0 = [_rdot(jnp.concatenate([z[p] * jnp.exp(g_exc[p]), rr[p] * jnp.exp(g_inc[p])], axis=0), s[p], _NT)
               for p in pairs]
        mz = [m[i // 2][(i % 2) * CHUNK:(i % 2 + 1) * CHUNK] for i in both]
        azp = [jnp.where(strict, x[:, :CHUNK], 0.0) for x in mz]
        azk = [jnp.where(strict, x[:, CHUNK:], 0.0) for x in mz]
        minv = _tri_inverse(azp, eye, 1, row, col)
        rhs = [zr0[i // 2][:CHUNK] + _rdot(azk[i], vv[i // 2]) for i in both]
        u_h = [_rdot(minv[i], rhs[i]) for i in both]
        u = [jnp.where(lane_a, u_h[2 * p], u_h[2 * p + 1]) for p in pairs]
        uv = [jnp.concatenate([u[p], vv[p]], axis=0) for p in pairs]
        incl2 = jnp.concatenate([incl, incl], axis=1)
        y_h = [_rdot(jnp.where(incl2, m[i // 2][(2 + i % 2) * CHUNK:(3 + i % 2) * CHUNK], 0.0), uv[i // 2])
               for i in both]
        tail = [jnp.exp(g_end[p] - g_inc[p]) for p in pairs]
        s_new = [s[p] * jnp.exp(g_end[p])
                 + _rdot(uv[p], jnp.concatenate([pp[p] * tail[p], kk_[p] * tail[p]], axis=0), _TN) for p in pairs]
        for g, (bi, p) in enumerate(groups):
            s_ref[bi, p] = jnp.where(same_head, s_new[g], 0.0)
            y_s[rows[g], cols[g]] = zr0[g][CHUNK:] + jnp.where(lane_a, y_h[2 * g], y_h[2 * g + 1])
        return carry

    lax.fori_loop(0, tl // CHUNK, chunk, 0)

    for bi in range(bb):
        tile_rows = pl.ds(bi * tl, tl)
        y = y_s[tile_rows, :]
        mean = _headsum(y, bd) * (1.0 / N_B)
        dev = y - mean
        var = _headsum(dev * dev, bd) * (1.0 / N_B)
        yn = dev * lax.rsqrt(var + GN_EPS) * lg_ref[...] + lb_ref[...]
        yb_ref[bi] = ((yn + bonus_s[tile_rows, :]) * gate_s[tile_rows, :]).astype(BF16)

    @pl.when(t == pl.num_programs(1) - 1)
    def _():
        sout_ref[...] = s_ref[...]


def _rwkv(proj, shift_buf8, s0_pairs, vecs, w2p, a2p, g2, bb, tl, l_valid):
    b, l, _ = proj.shape
    kern = functools.partial(_rwkv_kernel, bb=bb, tl=tl, l_valid=l_valid, l_total=l)
    full = lambda *shape: pl.BlockSpec(shape, lambda i, t: (0,) * len(shape))
    mu, w0, a0, k_k, k_a, r_k, lnx_g, lnx_b = vecs
    return pl.pallas_call(
        kern,
        grid=(b // bb, l // tl),
        in_specs=[pl.BlockSpec((bb, tl, RW_BLOCK), lambda i, t: (i, t, P_RW // RW_BLOCK)),
                  pl.BlockSpec((bb, SUBLANES, RW_BLOCK), lambda i, t: (i, 0, 0)),
                  pl.BlockSpec((bb, PAIRS, LANES, LANES), lambda i, t: (i, 0, 0, 0)),
                  full(1, RW_BLOCK)] + [full(1, D_B)] * 7 +
                 [full(W_LORA + A_LORA, D_B), full(W_LORA + A_LORA, D_B), full(G_LORA, D_B)],
        out_specs=[pl.BlockSpec((bb, tl, D_B), lambda i, t: (i, t, 0)),
                   pl.BlockSpec((bb, PAIRS, LANES, LANES), lambda i, t: (i, 0, 0, 0))],
        out_shape=[jax.ShapeDtypeStruct((b, l, D_B), BF16),
                   jax.ShapeDtypeStruct((b, PAIRS, LANES, LANES), F32)],
        scratch_shapes=[pltpu.VMEM((bb, tl + SUBLANES, RW_BLOCK), F32)] +
                       [pltpu.VMEM((bb * tl, D_B), F32)] * 9 +
                       [pltpu.VMEM((bb, PAIRS, LANES, LANES), F32)],
        compiler_params=_cparams(("parallel", "arbitrary"), 48),
        name="rwkv",
    )(proj, shift_buf8, s0_pairs, mu, w0, a0, k_k, k_a, r_k, lnx_g, lnx_b, w2p, a2p, g2)


def _layernorm(y, g, b):
    mu = jnp.mean(y, -1, keepdims=True)
    dev = y - mu
    var = jnp.mean(dev * dev, -1, keepdims=True)
    return dev * lax.rsqrt(var + LN_EPS) * g + b


def _route(logits_t, bias, base_cnt):
    tm = logits_t.shape[1]
    scores = jax.nn.sigmoid(logits_t)
    choice = scores + bias
    neg_inf = -jnp.inf
    iota_g = lax.broadcasted_iota(jnp.int32, (GROUP_SIZE, tm), 0)
    group_score = []
    for g in range(N_GROUPS):
        xg = choice[g * GROUP_SIZE:(g + 1) * GROUP_SIZE, :]
        m1 = jnp.max(xg, axis=0, keepdims=True)
        first = jnp.min(jnp.where(xg == m1, iota_g, GROUP_SIZE), axis=0, keepdims=True)
        m2 = jnp.max(jnp.where(iota_g == first, neg_inf, xg), axis=0, keepdims=True)
        group_score.append(m1 + m2)
    masked = []
    for g in range(N_GROUPS):
        rank = jnp.zeros((1, tm), jnp.int32)
        for o in range(N_GROUPS):
            if o == g:
                continue
            ahead = group_score[o] > group_score[g]
            if o < g:
                ahead = ahead | (group_score[o] == group_score[g])
            rank = rank + ahead.astype(jnp.int32)
        keep = rank < TOPK_GROUPS
        masked.append(jnp.where(keep, choice[g * GROUP_SIZE:(g + 1) * GROUP_SIZE, :], neg_inf))
    cur = jnp.concatenate(masked, axis=0)
    iota_e = lax.broadcasted_iota(jnp.int32, (N_EXPERTS, tm), 0)
    sel = jnp.zeros((N_EXPERTS, tm), F32)
    picks = []
    firsts = []
    for _ in range(TOP_K):
        m = jnp.max(cur, axis=0, keepdims=True)
        first = jnp.min(jnp.where(cur == m, iota_e, N_EXPERTS), axis=0, keepdims=True)
        pick = iota_e == first
        sel = jnp.where(pick, 1.0, sel)
        cur = jnp.where(pick, neg_inf, cur)
        picks.append(pick)
        firsts.append(first)
    wsel = sel * scores
    denom = jnp.sum(wsel, axis=0, keepdims=True) + 1e-20
    gates = (ROUTED_SCALE * wsel) / denom

    upto = (lax.broadcasted_iota(jnp.int32, (tm, tm), 0) <= lax.broadcasted_iota(jnp.int32, (tm, tm), 1))
    csum = jnp.dot(sel.astype(BF16), jnp.where(upto, 1.0, 0.0).astype(BF16), preferred_element_type=F32)
    before = base_cnt + csum - sel
    iota_k = lax.broadcasted_iota(jnp.int32, (TOP_K, tm), 0)
    eidx = jnp.zeros((TOP_K, tm), jnp.int32)
    rank = jnp.zeros((TOP_K, tm), F32)
    w_k = jnp.zeros((TOP_K, tm), F32)
    for i in range(TOP_K):
        eidx = jnp.where(iota_k == i, firsts[i], eidx)
        rank = jnp.where(iota_k == i, jnp.sum(jnp.where(picks[i], before, 0.0), axis=0, keepdims=True), rank)
        w_k = jnp.where(iota_k == i, jnp.sum(jnp.where(picks[i], gates, 0.0), axis=0, keepdims=True), w_k)
    return gates, eidx, rank.astype(jnp.int32), w_k, jnp.sum(sel, axis=1, keepdims=True)


def _merge_kernel(ya_ref, yb_ref, mg_ref, x_ref, ga1_ref, sc2_ref, sh2_ref, pa_ref, pb_ref, wo_ref,
                  g1_ref, b1_ref, rwt_ref, rb_ref, x1_ref, h2_ref, gt_ref, hp_ref, eidx_ref, rank_ref, wk_ref,
                  cnt_ref, cnt_s, *, dn_alpha):
    bb, tl, d = x_ref.shape
    tm = bb * tl

    @pl.when((pl.program_id(0) == 0) & (pl.program_id(1) == 0))
    def _():
        cnt_s[...] = jnp.zeros_like(cnt_s)

    ya = ya_ref[...].reshape(tm, V_A)
    yb = yb_ref[...].reshape(tm, D_B)
    mg = mg_ref[...].reshape(tm, 2 * d)
    merged = (jax.nn.sigmoid(mg[:, :d]) * jnp.dot(ya, pa_ref[...], preferred_element_type=F32)
              + jax.nn.sigmoid(mg[:, d:]) * jnp.dot(yb, pb_ref[...], preferred_element_type=F32))
    mix = _dot(merged, wo_ref[...])
    y = dn_alpha * x_ref[...] + (1.0 + ga1_ref[...]) * mix.reshape(bb, tl, d)
    x1 = _layernorm(y, g1_ref[...], b1_ref[...])
    x1_ref[...] = x1
    h2 = (x1 * (1.0 + sc2_ref[...]) + sh2_ref[...]).astype(BF16)
    h2_ref[...] = h2
    h2f = h2.reshape(tm, d)
    hp_ref[...] = _pack_bf16_pairs(h2f.astype(F32))
    logits_t = lax.dot_general(rwt_ref[...], h2f, (_NT, ((), ())), preferred_element_type=F32)
    gates, eidx, rank, w_k, tile_cnt = _route(logits_t, rb_ref[...], cnt_s[...])
    gt_ref[...] = gates
    eidx_ref[...] = eidx
    rank_ref[...] = rank
    wk_ref[...] = w_k
    cnt_s[...] = cnt_s[...] + tile_cnt
    cnt_ref[...] = cnt_s[...].astype(jnp.int32)


def _merge(ya, yb, proj, x, ga1, sc2, sh2, p_a, p_b, w_o, ln_g, ln_b, rw_t, r_bias, bb, tl, dn_alpha):
    b, l, d = x.shape
    nj = l // tl
    tm = bb * tl
    t_all = b * l
    full = lambda *shape: pl.BlockSpec(shape, lambda i, j: (0,) * len(shape))
    mod = pl.BlockSpec((bb, 1, d), lambda i, j: (i, 0, 0))
    tok = lambda width, dtype=None: pl.BlockSpec((bb, tl, width), lambda i, j: (i, j, 0))
    per_k = pl.BlockSpec((TOP_K, tm), lambda i, j: (0, i * nj + j))
    return pl.pallas_call(
        functools.partial(_merge_kernel, dn_alpha=dn_alpha),
        grid=(b // bb, nj),
        in_specs=[tok(V_A), tok(D_B),
                  pl.BlockSpec((bb, tl, 2 * d), lambda i, j: (i, j, P_MG // (2 * d))),
                  tok(d), mod, mod, mod,
                  full(V_A, d), full(D_B, d), full(d, d), full(1, d), full(1, d),
                  full(N_EXPERTS, d), full(N_EXPERTS, 1)],
        out_specs=[tok(d), tok(d), pl.BlockSpec((N_EXPERTS, tm), lambda i, j: (0, i * nj + j)),
                   pl.BlockSpec((tm, d // 2), lambda i, j: (i * nj + j, 0)),
                   per_k, per_k, per_k, full(N_EXPERTS, 1)],
        out_shape=[jax.ShapeDtypeStruct((b, l, d), F32),
                   jax.ShapeDtypeStruct((b, l, d), BF16),
                   jax.ShapeDtypeStruct((N_EXPERTS, t_all), F32),
                   jax.ShapeDtypeStruct((t_all, d // 2), jnp.int32),
                   jax.ShapeDtypeStruct((TOP_K, t_all), jnp.int32),
                   jax.ShapeDtypeStruct((TOP_K, t_all), jnp.int32),
                   jax.ShapeDtypeStruct((TOP_K, t_all), F32),
                   jax.ShapeDtypeStruct((N_EXPERTS, 1), jnp.int32)],
        scratch_shapes=[pltpu.VMEM((N_EXPERTS, 1), F32)],
        compiler_params=_cparams(("arbitrary", "arbitrary"), 48),
        name="merge",
    )(ya, yb, proj, x, ga1, sc2, sh2, p_a, p_b, w_o, ln_g, ln_b, rw_t, r_bias)


def _moe_kernel(h_ref, g_ref, x1_ref, ga2_ref, wg_ref, wu_ref, wd_ref, sgu_ref, sd_ref, g2_ref, b2_ref,
                o_ref, acc_ref, *, dn_alpha):
    e = pl.program_id(2)
    bb, tl, d = h_ref.shape
    tm = bb * tl
    h = h_ref[...].reshape(tm, d)

    @pl.when(e == 0)
    def _():
        su = jnp.dot(h, sgu_ref[...], preferred_element_type=F32)
        act = _silu(su[:, :D_SHARED]) * su[:, D_SHARED:]
        acc_ref[...] = _dot(act, sd_ref[...])

    lane = lax.broadcasted_iota(jnp.int32, (tm, N_EXPERTS), 1)
    gates = g_ref[...]
    acts = []
    for i in range(EXPERTS_PER_STEP):
        g = jnp.dot(h, wg_ref[i].astype(BF16), preferred_element_type=F32)
        u = jnp.dot(h, wu_ref[i].astype(BF16), preferred_element_type=F32)
        gate = _lane_pick(gates, lane, e * EXPERTS_PER_STEP + i)
        acts.append((_silu(g) * u * gate).astype(BF16))
    wd = wd_ref[...].reshape(EXPERTS_PER_STEP * D_EXPERT, d).astype(BF16)
    acc_ref[...] += jnp.dot(jnp.concatenate(acts, axis=1), wd, preferred_element_type=F32)

    @pl.when(e == pl.num_programs(2) - 1)
    def _():
        y = dn_alpha * x1_ref[...] + (1.0 + ga2_ref[...]) * acc_ref[...].reshape(bb, tl, d)
        o_ref[...] = _layernorm(y, g2_ref[...], b2_ref[...])


def _moe(h2, gates, x1, ga2, we_gate, we_up, we_down, ws_gu, ws_down, ln_g, ln_b, bb, tl, dn_alpha):
    b, l, d = x1.shape
    nj = l // tl
    tm = bb * tl
    full = lambda *shape: pl.BlockSpec(shape, lambda i, j, e: (0,) * len(shape))
    tok = pl.BlockSpec((bb, tl, d), lambda i, j, e: (i, j, 0))
    return pl.pallas_call(
        functools.partial(_moe_kernel, dn_alpha=dn_alpha),
        grid=(b // bb, nj, N_EXPERTS // EXPERTS_PER_STEP),
        in_specs=[tok,
                  pl.BlockSpec((tm, N_EXPERTS), lambda i, j, e: (i * nj + j, 0)),
                  tok,
                  pl.BlockSpec((bb, 1, d), lambda i, j, e: (i, 0, 0)),
                  pl.BlockSpec((EXPERTS_PER_STEP, d, D_EXPERT), lambda i, j, e: (e, 0, 0)),
                  pl.BlockSpec((EXPERTS_PER_STEP, d, D_EXPERT), lambda i, j, e: (e, 0, 0)),
                  pl.BlockSpec((EXPERTS_PER_STEP, D_EXPERT, d), lambda i, j, e: (e, 0, 0)),
                  full(d, 2 * D_SHARED), full(D_SHARED, d), full(1, d), full(1, d)],
        out_specs=tok,
        out_shape=jax.ShapeDtypeStruct((b, l, d), F32),
        scratch_shapes=[pltpu.VMEM((tm, d), F32)],
        compiler_params=_cparams(("parallel", "parallel", "arbitrary"), 56),
        name="moe",
    )(h2, gates, x1, ga2, we_gate, we_up, we_down, ws_gu, ws_down, ln_g, ln_b)


def _sc_mesh():
    return plsc.VectorSubcoreMesh(core_axis_name="c", subcore_axis_name="s")


def _sc_scatter_rows(x, pos, n_out):
    t, w = x.shape
    k = pos.shape[0]
    t_per_w = t // SC_WORKERS
    n_cores = plsc.get_sparse_core_info().num_cores

    @functools.partial(
        pl.kernel, mesh=_sc_mesh(),
        out_type=jax.ShapeDtypeStruct((n_out, w), jnp.int32),
        scratch_types=[pltpu.VMEM((k, SC_CHUNK), jnp.int32), pltpu.VMEM((SC_CHUNK, w), jnp.int32),
                       pltpu.SemaphoreType.DMA],
    )
    def scatter_kernel(x_hbm, pos_hbm, out_hbm, idx_v, rows_v, sem):
        base = (lax.axis_index("s") * n_cores + lax.axis_index("c")) * t_per_w

        @pl.loop(0, t_per_w // SC_CHUNK)
        def _(i):
            off = pl.multiple_of(base + i * SC_CHUNK, SC_CHUNK)
            pltpu.sync_copy(pos_hbm.at[:, pl.ds(off, SC_CHUNK)], idx_v)
            pltpu.sync_copy(x_hbm.at[pl.ds(off, SC_CHUNK)], rows_v)
            for j in range(k):
                pltpu.async_copy(rows_v, out_hbm.at[idx_v.at[j]], sem).wait()

    return scatter_kernel(x, pos)


def _sc_gather_rows(table, idx):
    n = idx.shape[0]
    w = table.shape[1]
    n_per_w = n // SC_WORKERS
    n_cores = plsc.get_sparse_core_info().num_cores

    half = SC_CHUNK // 2

    @functools.partial(
        pl.kernel, mesh=_sc_mesh(),
        out_type=jax.ShapeDtypeStruct((n, w), jnp.int32),
        scratch_types=[pltpu.VMEM((2, half), jnp.int32), pltpu.VMEM((2, half, w), jnp.int32)]
                      + [pltpu.SemaphoreType.DMA] * 4,
    )
    def gather_kernel(table_hbm, idx_hbm, out_hbm, idx_v, rows_v, sem_g0, sem_g1, sem_w0, sem_w1):
        base = (lax.axis_index("s") * n_cores + lax.axis_index("c")) * n_per_w
        sem_g = (sem_g0, sem_g1)
        sem_w = (sem_w0, sem_w1)

        @pl.loop(0, n_per_w // SC_CHUNK)
        def _(i):
            off = pl.multiple_of(base + i * SC_CHUNK, SC_CHUNK)
            for h in range(2):
                pltpu.sync_copy(idx_hbm.at[pl.ds(off + h * half, half)], idx_v.at[h])
            gathers = [pltpu.async_copy(table_hbm.at[idx_v.at[h]], rows_v.at[h], sem_g[h]) for h in range(2)]
            writes = []
            for h in range(2):
                gathers[h].wait()
                writes.append(pltpu.async_copy(rows_v.at[h], out_hbm.at[pl.ds(off + h * half, half)], sem_w[h]))
            for c in writes:
                c.wait()

    return gather_kernel(table, idx)


def _experts_kernel(te_ref, used_ref, xs_ref, wg_ref, wu_ref, wd_ref, o_ref, wg_s, wu_s, wd_s):
    i = pl.program_id(0)

    @pl.when((i == 0) | (te_ref[i] != te_ref[jnp.maximum(i - 1, 0)]))
    def _():
        wg_s[...] = wg_ref[0].astype(BF16)
        wu_s[...] = wu_ref[0].astype(BF16)
        wd_s[...] = wd_ref[0].astype(BF16)

    @pl.when(i < used_ref[0])
    def _():
        x = _unpack_bf16_pairs(xs_ref[...]).astype(BF16)
        g = jnp.dot(x, wg_s[...], preferred_element_type=F32)
        u = jnp.dot(x, wu_s[...], preferred_element_type=F32)
        act = (_silu(g) * u).astype(BF16)
        o_ref[...] = _pack_bf16_pairs(jnp.dot(act, wd_s[...], preferred_element_type=F32))


def _experts(xs, tile_expert, tiles_used, we_gate, we_up, we_down):
    r, half = xs.shape
    d = 2 * half
    grid_spec = pltpu.PrefetchScalarGridSpec(
        num_scalar_prefetch=2,
        grid=(r // EXPERT_TILE,),
        in_specs=[pl.BlockSpec((EXPERT_TILE, half), lambda i, te, used: (i, 0)),
                  pl.BlockSpec((1, d, D_EXPERT), lambda i, te, used: (te[i], 0, 0)),
                  pl.BlockSpec((1, d, D_EXPERT), lambda i, te, used: (te[i], 0, 0)),
                  pl.BlockSpec((1, D_EXPERT, d), lambda i, te, used: (te[i], 0, 0))],
        out_specs=pl.BlockSpec((EXPERT_TILE, half), lambda i, te, used: (i, 0)),
        scratch_shapes=[pltpu.VMEM((d, D_EXPERT), BF16), pltpu.VMEM((d, D_EXPERT), BF16),
                        pltpu.VMEM((D_EXPERT, d), BF16)],
    )
    return pl.pallas_call(
        _experts_kernel,
        grid_spec=grid_spec,
        out_shape=jax.ShapeDtypeStruct((r, half), jnp.int32),
        compiler_params=_cparams(("arbitrary",), 32),
        name="experts",
    )(tile_expert, tiles_used, xs, we_gate, we_up, we_down)


def _combine_kernel(og_ref, wk_ref, h_ref, x1_ref, ga2_ref, sgu_ref, sd_ref, g2_ref, b2_ref, o_ref, *, dn_alpha):
    bb, tl, d = h_ref.shape
    tm = bb * tl
    h = h_ref[...].reshape(tm, d)
    su = jnp.dot(h, sgu_ref[...], preferred_element_type=F32)
    acc = _dot(_silu(su[:, :D_SHARED]) * su[:, D_SHARED:], sd_ref[...])
    lane = lax.broadcasted_iota(jnp.int32, (tm, TOP_K), 1)
    w_all = wk_ref[...]
    for k in range(TOP_K):
        acc = acc + _lane_pick(w_all, lane, k) * _unpack_bf16_pairs(og_ref[k])
    y = dn_alpha * x1_ref[...] + (1.0 + ga2_ref[...]) * acc.reshape(bb, tl, d)
    o_ref[...] = _layernorm(y, g2_ref[...], b2_ref[...])


def _combine(og, w_tk, h2, x1, ga2, ws_gu, ws_down, ln_g, ln_b, bb, tl, dn_alpha):
    b, l, d = x1.shape
    nj = l // tl
    tm = bb * tl
    full = lambda *shape: pl.BlockSpec(shape, lambda i, j: (0,) * len(shape))
    tok = pl.BlockSpec((bb, tl, d), lambda i, j: (i, j, 0))
    return pl.pallas_call(
        functools.partial(_combine_kernel, dn_alpha=dn_alpha),
        grid=(b // bb, nj),
        in_specs=[pl.BlockSpec((TOP_K, tm, d // 2), lambda i, j: (0, i * nj + j, 0)),
                  pl.BlockSpec((tm, TOP_K), lambda i, j: (i * nj + j, 0)),
                  tok, tok,
                  pl.BlockSpec((bb, 1, d), lambda i, j: (i, 0, 0)),
                  full(d, 2 * D_SHARED), full(D_SHARED, d), full(1, d), full(1, d)],
        out_specs=tok,
        out_shape=jax.ShapeDtypeStruct((b, l, d), F32),
        compiler_params=_cparams(("parallel", "parallel"), 48),
        name="combine",
    )(og, w_tk, h2, x1, ga2, ws_gu, ws_down, ln_g, ln_b)


def _moe_sparse(hp, eidx, rank, w_k, cnt, h2, x1, ga2, p, bb, dn_alpha):
    t_all = hp.shape[0]
    n_tiles = t_all * TOP_K // EXPERT_TILE + N_EXPERTS
    seg_tiles = (cnt[:, 0] + EXPERT_TILE - 1) // EXPERT_TILE
    seg_end = jnp.cumsum(seg_tiles)
    seg_start = (seg_end - seg_tiles) * EXPERT_TILE
    experts = jnp.arange(N_EXPERTS, dtype=jnp.int32)
    pos = rank + jnp.sum(jnp.where(eidx[None] == experts[:, None, None], seg_start[:, None, None], 0), axis=0)
    tiles = jnp.arange(n_tiles, dtype=jnp.int32)
    tile_expert = jnp.minimum(jnp.sum((seg_end[None, :] <= tiles[:, None]).astype(jnp.int32), axis=1),
                              N_EXPERTS - 1)
    xs = _sc_scatter_rows(hp, pos, n_tiles * EXPERT_TILE)
    outs = _experts(xs, tile_expert, seg_end[-1:].astype(jnp.int32), p['we_gate'], p['we_up'], p['we_down'])
    og = _sc_gather_rows(outs, pos.reshape(-1)).reshape(TOP_K, t_all, hp.shape[1])
    return _combine(og, w_k.T, h2, x1, ga2, p['ws_gu'], p['ws_down'], p['ln2_g'], p['ln2_b'], bb, COMBINE_TL, dn_alpha)


def _pad_rows(buf, width):
    b, n, w = buf.shape
    return jnp.pad(buf.astype(F32), ((0, 0), (SUBLANES - n, 0), (0, width - w)))


def _to_pairs(s):
    b = s.shape[0]
    s = s.astype(F32).reshape(b, PAIRS, 2, N_B, N_B)
    zero = jnp.zeros_like(s[:, :, 0])
    top = jnp.concatenate([s[:, :, 0], zero], axis=-1)
    bot = jnp.concatenate([zero, s[:, :, 1]], axis=-1)
    return jnp.concatenate([top, bot], axis=-2)


def _from_pairs(sp):
    b = sp.shape[0]
    return jnp.stack([sp[:, :, :N_B, :N_B], sp[:, :, N_B:, N_B:]], axis=2).reshape(b, H_B, N_B, N_B)


def _layer(x, mod, conv_buf, gdn_s, shift_buf, rwkv_s, p, tiles, dn_alpha):
    b, l, d = x.shape
    sh1, sc1, ga1, sh2, sc2, ga2 = mod
    lp = -(-l // CHUNK) * CHUNK
    x_in = x if lp == l else jnp.pad(x, ((0, 0), (0, lp - l), (0, 0)))
    proj = _inproj(x_in, sc1, sh1, p['w_in'], tiles['bb_in'], tiles['tl_in'])
    ya, gdn_new = _gdn(proj, _pad_rows(conv_buf, QKV_A), gdn_s.astype(F32), p['conv_w'], p['alog_row'],
                       p['dt_row'], p['gdn_norm_w'], tiles['bb_gdn'], tiles['tl_gdn'], l)
    yb, rwkv_pairs = _rwkv(proj, _pad_rows(shift_buf, RW_BLOCK), _to_pairs(rwkv_s), p['rwkv_vecs'],
                           p['w2p'], p['a2p'], p['g2'], tiles['bb_rwkv'], tiles['tl_rwkv'], l)
    if lp != l:
        ya, yb = ya[:, :l], yb[:, :l]

    x1, h2, gates_t, hp, eidx, rank, w_k, cnt = _merge(
        ya, yb, proj, x, ga1, sc2, sh2, p['p_a'], p['p_b'], p['w_o'], p['ln1_g'], p['ln1_b'],
        p['router_wt'], p['router_bias'], tiles['bb'], tiles['tl_merge'], dn_alpha)
    if (b * l) % (SC_WORKERS * SC_CHUNK) == 0:
        out = _moe_sparse(hp, eidx, rank, w_k, cnt, h2, x1, ga2, p, tiles['bb'], dn_alpha)
    else:
        out = _moe(h2, gates_t.T, x1, ga2, p['we_gate'], p['we_up'], p['we_down'], p['ws_gu'], p['ws_down'],
                   p['ln2_g'], p['ln2_b'], tiles['bb'], tiles['tl'], dn_alpha)

    pre = jnp.concatenate([conv_buf.astype(F32), proj[:, :l, P_QKV:P_QKV + QKV_A]], axis=1)
    conv_new = pre[:, -(CONV_W - 1):]
    shift_new = proj[:, l - 1:l, P_RW:P_RW + SHIFT_W]
    return (out, conv_new.astype(conv_buf.dtype), gdn_new.astype(gdn_s.dtype),
            shift_new.astype(shift_buf.dtype), _from_pairs(rwkv_pairs).astype(rwkv_s.dtype))


def _prep_params(l, w_in, conv_w, a_log, dt_bias, gdn_norm_w, mu_shift, w0, w2, a0, a2, g2, k_k, k_a, r_k,
                 lnx_g, lnx_b, p_a, p_b, w_o, ln1_g, ln1_b, router_w, router_bias, we_gate, we_up, we_down,
                 ws_gate, ws_up, ws_down, ln2_g, ln2_b):
    d = D_MODEL
    w = w_in[l]
    w_p = jnp.concatenate(
        [w[:, :QKV_A], w[:, OFF_Z:OFF_RWKV], w[:, OFF_RWKV:OFF_MERGE],
         jnp.zeros((d, RW_BLOCK - SHIFT_W), w.dtype), w[:, OFF_MERGE:], w[:, OFF_ALPHA:OFF_Z],
         jnp.zeros((d, N_PROJ - P_AB - 2 * H_A), w.dtype)], axis=1).astype(BF16)
    row = lambda v, width: jnp.pad(v.astype(F32).reshape(1, -1), ((0, 0), (0, width - v.size)))
    zeros_lora = jnp.zeros((W_LORA, D_B), F32)
    return {
        'w_in': w_p,
        'conv_w': conv_w[l].astype(F32),
        'alog_row': row(a_log[l], LANES),
        'dt_row': row(dt_bias[l], LANES),
        'gdn_norm_w': row(gdn_norm_w[l], DK_A),
        'rwkv_vecs': (row(mu_shift[l], RW_BLOCK), row(w0[l], D_B), row(a0[l], D_B), row(k_k[l], D_B),
                      row(k_a[l], D_B), row(r_k[l], D_B), row(lnx_g[l], D_B), row(lnx_b[l], D_B)),
        'w2p': jnp.concatenate([w2[l].astype(F32), zeros_lora], axis=0),
        'a2p': jnp.concatenate([zeros_lora, a2[l].astype(F32)], axis=0),
        'g2': g2[l].astype(F32),
        'p_a': p_a[l].astype(BF16), 'p_b': p_b[l].astype(BF16), 'w_o': w_o[l].astype(BF16),
        'ln1_g': row(ln1_g[l], d), 'ln1_b': row(ln1_b[l], d),
        'router_wt': router_w[l].T.astype(BF16),
        'router_bias': router_bias[l].astype(F32).reshape(N_EXPERTS, 1),
        'we_gate': we_gate[l], 'we_up': we_up[l], 'we_down': we_down[l],
        'ws_gu': jnp.concatenate([ws_gate[l], ws_up[l]], axis=-1).astype(BF16),
        'ws_down': ws_down[l].astype(BF16),
        'ln2_g': row(ln2_g[l], d), 'ln2_b': row(ln2_b[l], d),
    }


def kernel(x_prompt, x_sample, c_prompt, c_sample, state_gdn_conv, state_gdn, state_rwkv_shift, state_rwkv, w_ada, b_ada, w_in, conv_w, a_log, dt_bias, gdn_norm_w, mu_shift, w0, w2, a0, a2, g2, k_k, k_a, r_k, lnx_g, lnx_b, p_a, p_b, w_o, ln1_g, ln1_b, router_w, router_bias, we_gate, we_up, we_down, ws_gate, ws_up, ws_down, ln2_g, ln2_b):
    depth = w_ada.shape[0]
    dn_alpha = (2 * depth) ** 0.25
    bp, lp_, d = x_prompt.shape
    bs, ls, _ = x_sample.shape
    dtp = x_prompt.dtype
    tiles_p = {'bb': 1, 'tl': 1024, 'bb_in': 1, 'tl_in': 1024, 'bb_gdn': 4, 'tl_gdn': CHUNK,
               'bb_rwkv': 2, 'tl_rwkv': 2 * CHUNK, 'tl_merge': 512}
    tiles_s = {'bb': bs, 'tl': ls, 'bb_in': 8, 'tl_in': CHUNK, 'bb_gdn': 4, 'tl_gdn': CHUNK,
               'bb_rwkv': 2, 'tl_rwkv': CHUNK, 'tl_merge': ls}

    yp, ys = x_prompt, x_sample
    new_p = ([], [], [], [])
    new_s = ([], [], [], [])
    for l in range(depth):
        p = _prep_params(l, w_in, conv_w, a_log, dt_bias, gdn_norm_w, mu_shift, w0, w2, a0, a2, g2, k_k, k_a,
                         r_k, lnx_g, lnx_b, p_a, p_b, w_o, ln1_g, ln1_b, router_w, router_bias, we_gate, we_up,
                         we_down, ws_gate, ws_up, ws_down, ln2_g, ln2_b)
        mod = _ada(jnp.concatenate([c_prompt, c_sample], axis=0), w_ada[l], b_ada[l])
        mod_p = tuple(m[:, None, :] for m in jnp.split(mod[:bp], 6, axis=-1))
        mod_s = tuple(m[:, None, :] for m in jnp.split(mod[bp:], 6, axis=-1))
        yp, *sp = _layer(yp, mod_p,
                         jnp.zeros((bp, CONV_W - 1, QKV_A), dtp), jnp.zeros((bp, H_A, DK_A, DK_A), dtp),
                         jnp.zeros((bp, 1, SHIFT_W), dtp), jnp.zeros((bp, H_B, N_B, N_B), dtp),
                         p, tiles_p, dn_alpha)
        ys, *ss = _layer(ys, mod_s, state_gdn_conv[l], state_gdn[l], state_rwkv_shift[l], state_rwkv[l],
                         p, tiles_s, dn_alpha)
        for lst, val in zip(new_p, sp):
            lst.append(val)
        for lst, val in zip(new_s, ss):
            lst.append(val)
    conv_p, gdn_p, shift_p, rwkv_p = [jnp.stack(t, 0) for t in new_p]
    conv_s, gdn_s, shift_s, rwkv_s = [jnp.stack(t, 0) for t in new_s]
    return (yp, ys, conv_p, gdn_p, shift_p, rwkv_p, conv_s, gdn_s, shift_s, rwkv_s)
```

```python
import functools
import math

import jax
import jax.numpy as jnp
from jax import lax
from jax.experimental import pallas as pl
from jax.experimental.pallas import tpu as pltpu
from jax.experimental.pallas import tpu_sc as plsc

F32 = jnp.float32
BF16 = jnp.bfloat16

D_MODEL = 1024
DK_A = 128
H_A = 4
QK_A = H_A * DK_A
V_A = H_A * DK_A
QKV_A = 2 * QK_A + V_A
CONV_W = 4
N_B = 64
H_B = 8
D_B = H_B * N_B
W_LORA = 64
A_LORA = 64
G_LORA = 128
SHIFT_W = 3 * D_B + W_LORA + A_LORA + G_LORA
OFF_ALPHA = QKV_A
OFF_BETA = OFF_ALPHA + H_A
OFF_Z = OFF_BETA + H_A
OFF_RWKV = OFF_Z + V_A
OFF_MERGE = OFF_RWKV + SHIFT_W
N_EXPERTS = 64
TOP_K = 8
N_GROUPS = 8
GROUP_SIZE = N_EXPERTS // N_GROUPS
TOPK_GROUPS = 4
D_EXPERT = 256
D_SHARED = 256
ROUTED_SCALE = 2.5
LN_EPS = 1e-5
GN_EPS = 64e-5
RMS_EPS = 1e-6
DECAY_SCALE = -math.exp(-0.5)

SUBLANES = 8
LANES = 128

P_QKV = 0
P_Z = QKV_A
P_RW = 2048
RW_BLOCK = 2048
P_MG = P_RW + RW_BLOCK
P_AB = P_MG + 2 * D_MODEL
AB_BLOCK = LANES
PROJ_TN = 1280
N_PROJ = 5 * PROJ_TN

CHUNK = 128
BASE_BLOCK = 8
CUMSUM_PARTS = 3
STAT_PARTS = 2
EXPERTS_PER_STEP = 2
HIGH_HALF = -65536

EXPERT_TILE = 1024
SC_WORKERS = 32
SC_CHUNK = 128
COMBINE_TL = 512
TOKEN_TILE = 1024


def _cparams(sem, vmem_mb):
    return pltpu.CompilerParams(dimension_semantics=sem, vmem_limit_bytes=vmem_mb * 1024 * 1024)


def _dot(a, b):
    return jnp.dot(a.astype(BF16), b.astype(BF16), preferred_element_type=F32)


def _rdot(a, b, dims=((1,), (0,))):
    return lax.dot_general(a.astype(BF16), b.astype(BF16), (dims, ((), ())), preferred_element_type=F32)


_NT = ((1,), (1,))
_TN = ((0,), (0,))


def _bf16_parts(x, parts):
    out = []
    rem = x
    for _ in range(parts):
        hi = rem.astype(BF16)
        out.append(hi)
        rem = rem - hi.astype(F32)
    return out


def _mask_dot_left(mask, x, parts):
    return sum(jnp.dot(mask, p, preferred_element_type=F32) for p in _bf16_parts(x, parts))


def _mask_dot_right(x, mask, parts):
    return sum(jnp.dot(p, mask, preferred_element_type=F32) for p in _bf16_parts(x, parts))


def _lane_pick(x, lane_iota, lane):
    return jnp.sum(jnp.where(lane_iota == lane, x, 0.0), axis=-1, keepdims=True)


def _pack_bf16_pairs(x):
    n = x.shape[1] // 2
    bits = lax.bitcast_convert_type(x.astype(BF16).astype(F32), jnp.int32)
    return (bits[:, :n] & HIGH_HALF) | lax.shift_right_logical(bits[:, n:], 16)


def _unpack_bf16_pairs(p):
    hi = lax.bitcast_convert_type(p & HIGH_HALF, F32)
    lo = lax.bitcast_convert_type(lax.shift_left(p, 16), F32)
    return jnp.concatenate([hi, lo], axis=1)


def _silu(x):
    return x * jax.nn.sigmoid(x)


def _softplus(x):
    return jnp.maximum(x, 0.0) + jnp.log1p(jnp.exp(-jnp.abs(x)))


def _tri_inverse(mats, eye, sign, row, col):
    base = (row // BASE_BLOCK) == (col // BASE_BLOCK)
    ds = [jnp.where(base, a, 0.0) for a in mats]
    xs = [eye + d if sign > 0 else eye - d for d in ds]
    power = 2
    while power < BASE_BLOCK:
        ds = [_rdot(d, d) for d in ds]
        xs = [x + _rdot(x, d) for x, d in zip(xs, ds)]
        power *= 2
    b = BASE_BLOCK
    while b < CHUNK:
        sibling = ((row // b) == (col // b) + 1) & ((row // (2 * b)) == (col // (2 * b)))
        offs = [jnp.where(sibling, a, 0.0) for a in mats]
        ts = [_rdot(o, x) for o, x in zip(offs, xs)]
        if sign > 0:
            xs = [x + _rdot(x, t_) for x, t_ in zip(xs, ts)]
        else:
            xs = [x - _rdot(x, t_) for x, t_ in zip(xs, ts)]
        b *= 2
    return xs


def _tri_consts():
    row = lax.broadcasted_iota(jnp.int32, (CHUNK, CHUNK), 0)
    col = lax.broadcasted_iota(jnp.int32, (CHUNK, CHUNK), 1)
    incl = row >= col
    strict = row > col
    eye = jnp.where(row == col, 1.0, 0.0).astype(F32)
    tril = jnp.where(incl, 1.0, 0.0).astype(BF16)
    return row, col, incl, strict, eye, tril


def _ada_kernel(c_ref, w_ref, b_ref, o_ref):
    o_ref[...] = _dot(_silu(c_ref[...]), w_ref[...]) + b_ref[...]


def _ada(c, w_ada, b_ada):
    n, d = c.shape
    nout = w_ada.shape[1]
    tn = 768
    return pl.pallas_call(
        _ada_kernel,
        grid=(nout // tn,),
        in_specs=[pl.BlockSpec((n, d), lambda j: (0, 0)),
                  pl.BlockSpec((d, tn), lambda j: (0, j)),
                  pl.BlockSpec((1, tn), lambda j: (0, j))],
        out_specs=pl.BlockSpec((n, tn), lambda j: (0, j)),
        out_shape=jax.ShapeDtypeStruct((n, nout), F32),
        compiler_params=_cparams(("parallel",), 32),
        name="ada",
    )(c, w_ada, b_ada.reshape(1, nout))


def _inproj_kernel(x_ref, sc_ref, sh_ref, w_ref, o_ref, hs_ref):
    @pl.when(pl.program_id(2) == 0)
    def _():
        h = x_ref[...] * (1.0 + sc_ref[...]) + sh_ref[...]
        hs_ref[...] = h.reshape(hs_ref.shape).astype(BF16)

    o = jnp.dot(hs_ref[...], w_ref[...], preferred_element_type=F32)
    o_ref[...] = o.reshape(o_ref.shape)


def _inproj(x, sc, sh, w_p, bb, tl):
    b, l, d = x.shape
    return pl.pallas_call(
        _inproj_kernel,
        grid=(b // bb, l // tl, N_PROJ // PROJ_TN),
        in_specs=[pl.BlockSpec((bb, tl, d), lambda i, j, n: (i, j, 0)),
                  pl.BlockSpec((bb, 1, d), lambda i, j, n: (i, 0, 0)),
                  pl.BlockSpec((bb, 1, d), lambda i, j, n: (i, 0, 0)),
                  pl.BlockSpec((d, PROJ_TN), lambda i, j, n: (0, n))],
        out_specs=pl.BlockSpec((bb, tl, PROJ_TN), lambda i, j, n: (i, j, n)),
        out_shape=jax.ShapeDtypeStruct((b, l, N_PROJ), F32),
        scratch_shapes=[pltpu.VMEM((bb * tl, d), BF16)],
        compiler_params=_cparams(("parallel", "parallel", "arbitrary"), 48),
        name="inproj",
    )(x, sc, sh, w_p)


def _gdn_kernel(qkv_ref, ab_ref, z_ref, cbuf_ref, s0_ref, cw_ref, alog_ref, dt_ref, nw_ref,
                ya_ref, sout_ref, ext_ref, qc_ref, s_ref, *, bb, tl, l_valid, l_total):
    t = pl.program_id(1)

    @pl.when(t == 0)
    def _():
        ext_ref[:, 0:SUBLANES, :] = cbuf_ref[...]
        s_ref[...] = s0_ref[...]

    for bi in range(bb):
        cur = qkv_ref[bi]
        ext_ref[bi, SUBLANES:SUBLANES + tl, :] = cur
        window = ext_ref[bi]
        acc = cw_ref[CONV_W - 1:CONV_W, :] * cur
        for j in range(CONV_W - 1):
            back = CONV_W - 1 - j
            acc = acc + cw_ref[j:j + 1, :] * pltpu.roll(window, tl + back, 0)[0:tl]
        ext_ref[bi, 0:SUBLANES, :] = ext_ref[bi, tl:tl + SUBLANES, :]
        qc_ref[bi] = _silu(acc)

    row, col, incl, strict, eye, tril = _tri_consts()
    neg_a = -jnp.exp(alog_ref[...])
    dt = dt_ref[...]
    nw = nw_ref[...]
    chains = [(bi, h) for bi in range(bb) for h in range(H_A)]
    n = range(len(chains))

    def chunk(c, carry):
        r0 = pl.multiple_of(c * CHUNK, CHUNK)
        rows = pl.ds(r0, CHUNK)
        g_cum = []
        b_all = []
        for bi in range(bb):
            ab = ab_ref[bi, rows, :]
            g_bi = neg_a * _softplus(ab + dt)
            b_bi = jax.nn.sigmoid(ab)
            if l_valid < l_total:
                valid = (t * tl + r0 + row) < l_valid
                g_bi = jnp.where(valid, g_bi, 0.0)
                b_bi = jnp.where(valid, b_bi, 0.0)
            g_cum.append(_mask_dot_left(tril, g_bi, CUMSUM_PARTS))
            b_all.append(b_bi)
        g_col = [_lane_pick(g_cum[bi], col, h) for bi, h in chains]
        beta = [_lane_pick(b_all[bi], col, H_A + h) for bi, h in chains]
        decay = []
        for i in n:
            g_b = jnp.broadcast_to(g_col[i], (CHUNK, CHUNK))
            decay.append(jnp.where(incl, jnp.exp(jnp.where(incl, g_b - g_b.T, 0.0)), 0.0))
        q = [qc_ref[bi, rows, h * DK_A:(h + 1) * DK_A] for bi, h in chains]
        k = [qc_ref[bi, rows, QK_A + h * DK_A:QK_A + (h + 1) * DK_A] for bi, h in chains]
        v = [qc_ref[bi, rows, 2 * QK_A + h * DK_A:2 * QK_A + (h + 1) * DK_A] for bi, h in chains]
        q = [x * lax.rsqrt(jnp.sum(x * x, -1, keepdims=True) + 1e-6) * (DK_A ** -0.5) for x in q]
        k = [x * lax.rsqrt(jnp.sum(x * x, -1, keepdims=True) + 1e-6) for x in k]
        kq = [_rdot(jnp.concatenate([k[i], q[i]], axis=0), k[i], _NT) for i in n]
        a = [jnp.where(strict, beta[i] * kq[i][:CHUNK] * decay[i], 0.0) for i in n]
        qk = [jnp.where(incl, kq[i][CHUNK:] * decay[i], 0.0) for i in n]
        x = _tri_inverse(a, eye, -1, row, col)
        e_g = [jnp.exp(g) for g in g_col]
        uw = [_rdot(x[i], jnp.concatenate([v[i] * beta[i], k[i] * (beta[i] * e_g[i])], axis=1)) for i in n]
        g_last = [g[CHUNK - 1:CHUNK, :] for g in g_col]
        kd = [k[i] * jnp.exp(g_last[i] - g_col[i]) for i in n]
        s = [s_ref[bi, h] for bi, h in chains]
        ws = [_rdot(jnp.concatenate([uw[i][:, DK_A:], q[i] * e_g[i]], axis=0), s[i]) for i in n]
        v_new = [uw[i][:, :DK_A] - ws[i][:CHUNK] for i in n]
        o = [ws[i][CHUNK:] + _rdot(qk[i], v_new[i]) for i in n]
        s_new = [s[i] * jnp.exp(g_last[i]) + _rdot(kd[i], v_new[i], _TN) for i in n]
        for i, (bi, h) in enumerate(chains):
            s_ref[bi, h] = s_new[i]
            z = z_ref[bi, rows, h * DK_A:(h + 1) * DK_A]
            on = o[i] * lax.rsqrt(jnp.mean(o[i] * o[i], -1, keepdims=True) + RMS_EPS) * nw * _silu(z)
            ya_ref[bi, rows, h * DK_A:(h + 1) * DK_A] = on.astype(BF16)
        return carry

    lax.fori_loop(0, tl // CHUNK, chunk, 0)

    @pl.when(t == pl.num_programs(1) - 1)
    def _():
        sout_ref[...] = s_ref[...]


def _gdn(proj, conv_buf8, s0, conv_w, alog_row, dt_row, norm_w, bb, tl, l_valid):
    b, l, _ = proj.shape
    kern = functools.partial(_gdn_kernel, bb=bb, tl=tl, l_valid=l_valid, l_total=l)
    full = lambda *shape: pl.BlockSpec(shape, lambda i, t: (0,) * len(shape))
    return pl.pallas_call(
        kern,
        grid=(b // bb, l // tl),
        in_specs=[pl.BlockSpec((bb, tl, QKV_A), lambda i, t: (i, t, P_QKV // QKV_A)),
                  pl.BlockSpec((bb, tl, AB_BLOCK), lambda i, t: (i, t, P_AB // AB_BLOCK)),
                  pl.BlockSpec((bb, tl, V_A), lambda i, t: (i, t, P_Z // V_A)),
                  pl.BlockSpec((bb, SUBLANES, QKV_A), lambda i, t: (i, 0, 0)),
                  pl.BlockSpec((bb, H_A, DK_A, DK_A), lambda i, t: (i, 0, 0, 0)),
                  full(CONV_W, QKV_A), full(1, LANES), full(1, LANES), full(1, DK_A)],
        out_specs=[pl.BlockSpec((bb, tl, V_A), lambda i, t: (i, t, 0)),
                   pl.BlockSpec((bb, H_A, DK_A, DK_A), lambda i, t: (i, 0, 0, 0))],
        out_shape=[jax.ShapeDtypeStruct((b, l, V_A), BF16),
                   jax.ShapeDtypeStruct((b, H_A, DK_A, DK_A), F32)],
        scratch_shapes=[pltpu.VMEM((bb, tl + SUBLANES, QKV_A), F32),
                        pltpu.VMEM((bb, tl, QKV_A), F32),
                        pltpu.VMEM((bb, H_A, DK_A, DK_A), F32)],
        compiler_params=_cparams(("parallel", "arbitrary"), 48),
        name="gdn",
    )(proj, proj, proj, conv_buf8, s0, conv_w, alog_row, dt_row, norm_w)


PAIRS = H_B // 2


def _headsum(x, bd):
    return jnp.concatenate(
        [_mask_dot_right(x[:, p * LANES:(p + 1) * LANES], bd, STAT_PARTS) for p in range(PAIRS)], axis=1)


def _rwkv_kernel(rw_ref, sbuf_ref, s0_ref, mu_ref, w0_ref, a0_ref, kk_ref, ka_ref, rk_ref, lg_ref, lb_ref,
                 w2_ref, a2_ref, g2_ref, yb_ref, sout_ref,
                 ext_ref, r_s, k_s, v_s, z_s, p_s, lw_s, y_s, bonus_s, gate_s, s_ref, *, bb, tl, l_valid, l_total):
    t = pl.program_id(1)

    @pl.when(t == 0)
    def _():
        ext_ref[:, 0:SUBLANES, :] = sbuf_ref[...]
        s_ref[...] = s0_ref[...]

    row, col, incl, strict, eye, tril = _tri_consts()
    same_head = (row // N_B) == (col // N_B)
    bd = jnp.where(same_head, 1.0, 0.0).astype(BF16)
    lane_a = col < N_B

    for bi in range(bb):
        cur = rw_ref[bi]
        ext_ref[bi, SUBLANES:SUBLANES + tl, :] = cur
        prev = pltpu.roll(ext_ref[bi], tl + 1, 0)[0:tl]
        mixed = cur + (prev - cur) * mu_ref[...]
        ext_ref[bi, 0:SUBLANES, :] = ext_ref[bi, tl:tl + SUBLANES, :]
        r = mixed[:, 0:D_B]
        k = mixed[:, D_B:2 * D_B]
        v = mixed[:, 2 * D_B:3 * D_B]
        lora = mixed[:, 3 * D_B:3 * D_B + W_LORA + A_LORA]
        g_in = mixed[:, 3 * D_B + W_LORA + A_LORA:SHIFT_W]
        lw = DECAY_SCALE * jax.nn.sigmoid(w0_ref[...] + _dot(jnp.tanh(lora), w2_ref[...]))
        a = jax.nn.sigmoid(a0_ref[...] + _dot(lora, a2_ref[...]))
        kkr = k * kk_ref[...]
        kk = kkr * lax.rsqrt(_headsum(kkr * kkr, bd) + 1e-6)
        k = k * (1.0 + (a - 1.0) * ka_ref[...])
        tile_rows = pl.ds(bi * tl, tl)
        bonus_s[tile_rows, :] = _headsum(r * k * rk_ref[...], bd) * v
        gate_s[tile_rows, :] = _dot(jax.nn.sigmoid(g_in), g2_ref[...])
        if l_valid < l_total:
            rvalid = (t * tl + lax.broadcasted_iota(jnp.int32, (tl, 1), 0)) < l_valid
            lw = jnp.where(rvalid, lw, 0.0)
            kk = jnp.where(rvalid, kk, 0.0)
            k = jnp.where(rvalid, k, 0.0)
            v = jnp.where(rvalid, v, 0.0)
        r_s[tile_rows, :] = r
        k_s[tile_rows, :] = k
        v_s[tile_rows, :] = v
        z_s[tile_rows, :] = -kk
        p_s[tile_rows, :] = kk * a
        lw_s[tile_rows, :] = lw

    groups = [(bi, p) for bi in range(bb) for p in range(PAIRS)]
    pairs = range(len(groups))
    both = range(2 * len(groups))
    cols = [slice(p * LANES, (p + 1) * LANES) for _, p in groups]

    def chunk(c, carry):
        r0 = pl.multiple_of(c * CHUNK, CHUNK)
        rows = [pl.ds(bi * tl + r0, CHUNK) for bi, _ in groups]
        lw_c = [lw_s[rows[g], cols[g]] for g in pairs]
        g_inc = [_mask_dot_left(tril, x, CUMSUM_PARTS) for x in lw_c]
        g_exc = [g_inc[p] - lw_c[p] for p in pairs]
        g_mid = [g[CHUNK // 2 - 1:CHUNK // 2, :] for g in g_inc]
        g_end = [g[CHUNK - 1:CHUNK, :] for g in g_inc]
        z = [z_s[rows[g], cols[g]] for g in pairs]
        rr = [r_s[rows[g], cols[g]] for g in pairs]
        pp = [p_s[rows[g], cols[g]] for g in pairs]
        kk_ = [k_s[rows[g], cols[g]] for g in pairs]
        vv = [v_s[rows[g], cols[g]] for g in pairs]
        zt = [z[p] * jnp.exp(g_exc[p] - g_mid[p]) for p in pairs]
        rt = [rr[p] * jnp.exp(g_inc[p] - g_mid[p]) for p in pairs]
        en = [jnp.exp(g_mid[p] - g_inc[p]) for p in pairs]
        s = [s_ref[bi, p] for bi, p in groups]
        lhs = [jnp.concatenate([jnp.where(lane_a, zt[p], 0.0), jnp.where(lane_a, 0.0, zt[p]),
                                jnp.where(lane_a, rt[p], 0.0), jnp.where(lane_a, 0.0, rt[p])], axis=0) for p in pairs]
        m = [_rdot(lhs[p], jnp.concatenate([pp[p] * en[p], kk_[p] * en[p]], axis=0), _NT) for p in pairs]
        zr0 = [_rdot(jnp.concatenate([z[p] * jnp.exp(g_exc[p]), rr[p] * jnp.exp(g_inc[p])], axis=0), s[p], _NT)
               for p in pairs]
        mz = [m[i // 2][(i % 2) * CHUNK:(i % 2 + 1) * CHUNK] for i in both]
        azp = [jnp.where(strict, x[:, :CHUNK], 0.0) for x in mz]
        azk = [jnp.where(strict, x[:, CHUNK:], 0.0) for x in mz]
        minv = _tri_inverse(azp, eye, 1, row, col)
        rhs = [zr0[i // 2][:CHUNK] + _rdot(azk[i], vv[i // 2]) for i in both]
        u_h = [_rdot(minv[i], rhs[i]) for i in both]
        u = [jnp.where(lane_a, u_h[2 * p], u_h[2 * p + 1]) for p in pairs]
        uv = [jnp.concatenate([u[p], vv[p]], axis=0) for p in pairs]
        incl2 = jnp.concatenate([incl, incl], axis=1)
        y_h = [_rdot(jnp.where(incl2, m[i // 2][(2 + i % 2) * CHUNK:(3 + i % 2) * CHUNK], 0.0), uv[i // 2])
               for i in both]
        tail = [jnp.exp(g_end[p] - g_inc[p]) for p in pairs]
        s_new = [s[p] * jnp.exp(g_end[p])
                 + _rdot(uv[p], jnp.concatenate([pp[p] * tail[p], kk_[p] * tail[p]], axis=0), _TN) for p in pairs]
        for g, (bi, p) in enumerate(groups):
            s_ref[bi, p] = jnp.where(same_head, s_new[g], 0.0)
            y_s[rows[g], cols[g]] = zr0[g][CHUNK:] + jnp.where(lane_a, y_h[2 * g], y_h[2 * g + 1])
        return carry

    lax.fori_loop(0, tl // CHUNK, chunk, 0)

    for bi in range(bb):
        tile_rows = pl.ds(bi * tl, tl)
        y = y_s[tile_rows, :]
        mean = _headsum(y, bd) * (1.0 / N_B)
        dev = y - mean
        var = _headsum(dev * dev, bd) * (1.0 / N_B)
        yn = dev * lax.rsqrt(var + GN_EPS) * lg_ref[...] + lb_ref[...]
        yb_ref[bi] = ((yn + bonus_s[tile_rows, :]) * gate_s[tile_rows, :]).astype(BF16)

    @pl.when(t == pl.num_programs(1) - 1)
    def _():
        sout_ref[...] = s_ref[...]


def _rwkv(proj, shift_buf8, s0_pairs, vecs, w2p, a2p, g2, bb, tl, l_valid):
    b, l, _ = proj.shape
    kern = functools.partial(_rwkv_kernel, bb=bb, tl=tl, l_valid=l_valid, l_total=l)
    full = lambda *shape: pl.BlockSpec(shape, lambda i, t: (0,) * len(shape))
    mu, w0, a0, k_k, k_a, r_k, lnx_g, lnx_b = vecs
    return pl.pallas_call(
        kern,
        grid=(b // bb, l // tl),
        in_specs=[pl.BlockSpec((bb, tl, RW_BLOCK), lambda i, t: (i, t, P_RW // RW_BLOCK)),
                  pl.BlockSpec((bb, SUBLANES, RW_BLOCK), lambda i, t: (i, 0, 0)),
                  pl.BlockSpec((bb, PAIRS, LANES, LANES), lambda i, t: (i, 0, 0, 0)),
                  full(1, RW_BLOCK)] + [full(1, D_B)] * 7 +
                 [full(W_LORA + A_LORA, D_B), full(W_LORA + A_LORA, D_B), full(G_LORA, D_B)],
        out_specs=[pl.BlockSpec((bb, tl, D_B), lambda i, t: (i, t, 0)),
                   pl.BlockSpec((bb, PAIRS, LANES, LANES), lambda i, t: (i, 0, 0, 0))],
        out_shape=[jax.ShapeDtypeStruct((b, l, D_B), BF16),
                   jax.ShapeDtypeStruct((b, PAIRS, LANES, LANES), F32)],
        scratch_shapes=[pltpu.VMEM((bb, tl + SUBLANES, RW_BLOCK), F32)] +
                       [pltpu.VMEM((bb * tl, D_B), F32)] * 9 +
                       [pltpu.VMEM((bb, PAIRS, LANES, LANES), F32)],
        compiler_params=_cparams(("parallel", "arbitrary"), 48),
        name="rwkv",
    )(proj, shift_buf8, s0_pairs, mu, w0, a0, k_k, k_a, r_k, lnx_g, lnx_b, w2p, a2p, g2)


def _layernorm(y, g, b):
    mu = jnp.mean(y, -1, keepdims=True)
    dev = y - mu
    var = jnp.mean(dev * dev, -1, keepdims=True)
    return dev * lax.rsqrt(var + LN_EPS) * g + b


def _route(logits_t, bias, base_cnt):
    tm = logits_t.shape[1]
    scores = jax.nn.sigmoid(logits_t)
    choice = scores + bias
    neg_inf = -jnp.inf
    iota_g = lax.broadcasted_iota(jnp.int32, (GROUP_SIZE, tm), 0)
    group_score = []
    for g in range(N_GROUPS):
        xg = choice[g * GROUP_SIZE:(g + 1) * GROUP_SIZE, :]
        m1 = jnp.max(xg, axis=0, keepdims=True)
        first = jnp.min(jnp.where(xg == m1, iota_g, GROUP_SIZE), axis=0, keepdims=True)
        m2 = jnp.max(jnp.where(iota_g == first, neg_inf, xg), axis=0, keepdims=True)
        group_score.append(m1 + m2)
    masked = []
    for g in range(N_GROUPS):
        rank = jnp.zeros((1, tm), jnp.int32)
        for o in range(N_GROUPS):
            if o == g:
                continue
            ahead = group_score[o] > group_score[g]
            if o < g:
                ahead = ahead | (group_score[o] == group_score[g])
            rank = rank + ahead.astype(jnp.int32)
        keep = rank < TOPK_GROUPS
        masked.append(jnp.where(keep, choice[g * GROUP_SIZE:(g + 1) * GROUP_SIZE, :], neg_inf))
    cur = jnp.concatenate(masked, axis=0)
    iota_e = lax.broadcasted_iota(jnp.int32, (N_EXPERTS, tm), 0)
    sel = jnp.zeros((N_EXPERTS, tm), F32)
    picks = []
    firsts = []
    for _ in range(TOP_K):
        m = jnp.max(cur, axis=0, keepdims=True)
        first = jnp.min(jnp.where(cur == m, iota_e, N_EXPERTS), axis=0, keepdims=True)
        pick = iota_e == first
        sel = jnp.where(pick, 1.0, sel)
        cur = jnp.where(pick, neg_inf, cur)
        picks.append(pick)
        firsts.append(first)
    wsel = sel * scores
    denom = jnp.sum(wsel, axis=0, keepdims=True) + 1e-20
    gates = (ROUTED_SCALE * wsel) / denom

    upto = (lax.broadcasted_iota(jnp.int32, (tm, tm), 0) <= lax.broadcasted_iota(jnp.int32, (tm, tm), 1))
    csum = jnp.dot(sel.astype(BF16), jnp.where(upto, 1.0, 0.0).astype(BF16), preferred_element_type=F32)
    before = base_cnt + csum - sel
    iota_k = lax.broadcasted_iota(jnp.int32, (TOP_K, tm), 0)
    eidx = jnp.zeros((TOP_K, tm), jnp.int32)
    rank = jnp.zeros((TOP_K, tm), F32)
    w_k = jnp.zeros((TOP_K, tm), F32)
    for i in range(TOP_K):
        eidx = jnp.where(iota_k == i, firsts[i], eidx)
        rank = jnp.where(iota_k == i, jnp.sum(jnp.where(picks[i], before, 0.0), axis=0, keepdims=True), rank)
        w_k = jnp.where(iota_k == i, jnp.sum(jnp.where(picks[i], gates, 0.0), axis=0, keepdims=True), w_k)
    return gates, eidx, rank.astype(jnp.int32), w_k, jnp.sum(sel, axis=1, keepdims=True)


def _merge_kernel(ya_ref, yb_ref, mg_ref, x_ref, ga1_ref, sc2_ref, sh2_ref, pa_ref, pb_ref, wo_ref,
                  g1_ref, b1_ref, rwt_ref, rb_ref, x1_ref, h2_ref, gt_ref, hp_ref, eidx_ref, rank_ref, wk_ref,
                  cnt_ref, cnt_s, *, dn_alpha):
    bb, tl, d = x_ref.shape
    tm = bb * tl

    @pl.when((pl.program_id(0) == 0) & (pl.program_id(1) == 0))
    def _():
        cnt_s[...] = jnp.zeros_like(cnt_s)

    ya = ya_ref[...].reshape(tm, V_A)
    yb = yb_ref[...].reshape(tm, D_B)
    mg = mg_ref[...].reshape(tm, 2 * d)
    merged = (jax.nn.sigmoid(mg[:, :d]) * jnp.dot(ya, pa_ref[...], preferred_element_type=F32)
              + jax.nn.sigmoid(mg[:, d:]) * jnp.dot(yb, pb_ref[...], preferred_element_type=F32))
    mix = _dot(merged, wo_ref[...])
    y = dn_alpha * x_ref[...] + (1.0 + ga1_ref[...]) * mix.reshape(bb, tl, d)
    x1 = _layernorm(y, g1_ref[...], b1_ref[...])
    x1_ref[...] = x1
    h2 = (x1 * (1.0 + sc2_ref[...]) + sh2_ref[...]).astype(BF16)
    h2_ref[...] = h2
    h2f = h2.reshape(tm, d)
    hp_ref[...] = _pack_bf16_pairs(h2f.astype(F32))
    logits_t = lax.dot_general(rwt_ref[...], h2f, (_NT, ((), ())), preferred_element_type=F32)
    gates, eidx, rank, w_k, tile_cnt = _route(logits_t, rb_ref[...], cnt_s[...])
    gt_ref[...] = gates
    eidx_ref[...] = eidx
    rank_ref[...] = rank
    wk_ref[...] = w_k
    cnt_s[...] = cnt_s[...] + tile_cnt
    cnt_ref[...] = cnt_s[...].astype(jnp.int32)


def _merge(ya, yb, proj, x, ga1, sc2, sh2, p_a, p_b, w_o, ln_g, ln_b, rw_t, r_bias, bb, tl, dn_alpha):
    b, l, d = x.shape
    nj = l // tl
    tm = bb * tl
    t_all = b * l
    full = lambda *shape: pl.BlockSpec(shape, lambda i, j: (0,) * len(shape))
    mod = pl.BlockSpec((bb, 1, d), lambda i, j: (i, 0, 0))
    tok = lambda width: pl.BlockSpec((bb, tl, width), lambda i, j: (i, j, 0))
    per_k = pl.BlockSpec((TOP_K, tm), lambda i, j: (0, i * nj + j))
    return pl.pallas_call(
        functools.partial(_merge_kernel, dn_alpha=dn_alpha),
        grid=(b // bb, nj),
        in_specs=[tok(V_A), tok(D_B),
                  pl.BlockSpec((bb, tl, 2 * d), lambda i, j: (i, j, P_MG // (2 * d))),
                  tok(d), mod, mod, mod,
                  full(V_A, d), full(D_B, d), full(d, d), full(1, d), full(1, d),
                  full(N_EXPERTS, d), full(N_EXPERTS, 1)],
        out_specs=[tok(d), tok(d), pl.BlockSpec((N_EXPERTS, tm), lambda i, j: (0, i * nj + j)),
                   pl.BlockSpec((tm, d // 2), lambda i, j: (i * nj + j, 0)),
                   per_k, per_k, per_k, full(N_EXPERTS, 1)],
        out_shape=[jax.ShapeDtypeStruct((b, l, d), F32),
                   jax.ShapeDtypeStruct((b, l, d), BF16),
                   jax.ShapeDtypeStruct((N_EXPERTS, t_all), F32),
                   jax.ShapeDtypeStruct((t_all, d // 2), jnp.int32),
                   jax.ShapeDtypeStruct((TOP_K, t_all), jnp.int32),
                   jax.ShapeDtypeStruct((TOP_K, t_all), jnp.int32),
                   jax.ShapeDtypeStruct((TOP_K, t_all), F32),
                   jax.ShapeDtypeStruct((N_EXPERTS, 1), jnp.int32)],
        scratch_shapes=[pltpu.VMEM((N_EXPERTS, 1), F32)],
        compiler_params=_cparams(("arbitrary", "arbitrary"), 48),
        name="merge",
    )(ya, yb, proj, x, ga1, sc2, sh2, p_a, p_b, w_o, ln_g, ln_b, rw_t, r_bias)


def _moe_kernel(h_ref, g_ref, x1_ref, ga2_ref, wg_ref, wu_ref, wd_ref, sgu_ref, sd_ref, g2_ref, b2_ref,
                o_ref, acc_ref, *, dn_alpha):
    e = pl.program_id(2)
    bb, tl, d = h_ref.shape
    tm = bb * tl
    h = h_ref[...].reshape(tm, d)

    @pl.when(e == 0)
    def _():
        su = jnp.dot(h, sgu_ref[...], preferred_element_type=F32)
        act = _silu(su[:, :D_SHARED]) * su[:, D_SHARED:]
        acc_ref[...] = _dot(act, sd_ref[...])

    lane = lax.broadcasted_iota(jnp.int32, (tm, N_EXPERTS), 1)
    gates = g_ref[...]
    acts = []
    for i in range(EXPERTS_PER_STEP):
        g = jnp.dot(h, wg_ref[i].astype(BF16), preferred_element_type=F32)
        u = jnp.dot(h, wu_ref[i].astype(BF16), preferred_element_type=F32)
        gate = _lane_pick(gates, lane, e * EXPERTS_PER_STEP + i)
        acts.append((_silu(g) * u * gate).astype(BF16))
    wd = wd_ref[...].reshape(EXPERTS_PER_STEP * D_EXPERT, d).astype(BF16)
    acc_ref[...] += jnp.dot(jnp.concatenate(acts, axis=1), wd, preferred_element_type=F32)

    @pl.when(e == pl.num_programs(2) - 1)
    def _():
        y = dn_alpha * x1_ref[...] + (1.0 + ga2_ref[...]) * acc_ref[...].reshape(bb, tl, d)
        o_ref[...] = _layernorm(y, g2_ref[...], b2_ref[...])


def _moe(h2, gates, x1, ga2, we_gate, we_up, we_down, ws_gu, ws_down, ln_g, ln_b, bb, tl, dn_alpha):
    b, l, d = x1.shape
    nj = l // tl
    tm = bb * tl
    full = lambda *shape: pl.BlockSpec(shape, lambda i, j, e: (0,) * len(shape))
    tok = pl.BlockSpec((bb, tl, d), lambda i, j, e: (i, j, 0))
    return pl.pallas_call(
        functools.partial(_moe_kernel, dn_alpha=dn_alpha),
        grid=(b // bb, nj, N_EXPERTS // EXPERTS_PER_STEP),
        in_specs=[tok,
                  pl.BlockSpec((tm, N_EXPERTS), lambda i, j, e: (i * nj + j, 0)),
                  tok,
                  pl.BlockSpec((bb, 1, d), lambda i, j, e: (i, 0, 0)),
                  pl.BlockSpec((EXPERTS_PER_STEP, d, D_EXPERT), lambda i, j, e: (e, 0, 0)),
                  pl.BlockSpec((EXPERTS_PER_STEP, d, D_EXPERT), lambda i, j, e: (e, 0, 0)),
                  pl.BlockSpec((EXPERTS_PER_STEP, D_EXPERT, d), lambda i, j, e: (e, 0, 0)),
                  full(d, 2 * D_SHARED), full(D_SHARED, d), full(1, d), full(1, d)],
        out_specs=tok,
        out_shape=jax.ShapeDtypeStruct((b, l, d), F32),
        scratch_shapes=[pltpu.VMEM((tm, d), F32)],
        compiler_params=_cparams(("parallel", "parallel", "arbitrary"), 56),
        name="moe",
    )(h2, gates, x1, ga2, we_gate, we_up, we_down, ws_gu, ws_down, ln_g, ln_b)


def _sc_mesh():
    return plsc.VectorSubcoreMesh(core_axis_name="c", subcore_axis_name="s")


def _sc_cores():
    info = plsc.get_sparse_core_info()
    assert info.num_cores * info.num_subcores == SC_WORKERS, info
    return info.num_cores


def _sc_scatter_rows(x, pos, n_out):
    t, w = x.shape
    k = pos.shape[0]
    t_per_w = t // SC_WORKERS
    n_cores = _sc_cores()

    @functools.partial(
        pl.kernel, mesh=_sc_mesh(),
        out_type=jax.ShapeDtypeStruct((n_out, w), jnp.int32),
        scratch_types=[pltpu.VMEM((k, SC_CHUNK), jnp.int32), pltpu.VMEM((SC_CHUNK, w), jnp.int32),
                       pltpu.SemaphoreType.DMA],
    )
    def scatter_kernel(x_hbm, pos_hbm, out_hbm, idx_v, rows_v, sem):
        base = (lax.axis_index("s") * n_cores + lax.axis_index("c")) * t_per_w

        @pl.loop(0, t_per_w // SC_CHUNK)
        def _(i):
            off = pl.multiple_of(base + i * SC_CHUNK, SC_CHUNK)
            pltpu.sync_copy(pos_hbm.at[:, pl.ds(off, SC_CHUNK)], idx_v)
            pltpu.sync_copy(x_hbm.at[pl.ds(off, SC_CHUNK)], rows_v)
            for j in range(k):
                pltpu.async_copy(rows_v, out_hbm.at[idx_v.at[j]], sem).wait()

    return scatter_kernel(x, pos)


def _sc_gather_rows(table, idx):
    n = idx.shape[0]
    w = table.shape[1]
    n_per_w = n // SC_WORKERS
    n_cores = _sc_cores()

    half = SC_CHUNK // 2

    @functools.partial(
        pl.kernel, mesh=_sc_mesh(),
        out_type=jax.ShapeDtypeStruct((n, w), jnp.int32),
        scratch_types=[pltpu.VMEM((2, half), jnp.int32), pltpu.VMEM((2, half, w), jnp.int32)]
                      + [pltpu.SemaphoreType.DMA] * 4,
    )
    def gather_kernel(table_hbm, idx_hbm, out_hbm, idx_v, rows_v, sem_g0, sem_g1, sem_w0, sem_w1):
        base = (lax.axis_index("s") * n_cores + lax.axis_index("c")) * n_per_w
        sem_g = (sem_g0, sem_g1)
        sem_w = (sem_w0, sem_w1)

        @pl.loop(0, n_per_w // SC_CHUNK)
        def _(i):
            off = pl.multiple_of(base + i * SC_CHUNK, SC_CHUNK)
            for h in range(2):
                pltpu.sync_copy(idx_hbm.at[pl.ds(off + h * half, half)], idx_v.at[h])
            gathers = [pltpu.async_copy(table_hbm.at[idx_v.at[h]], rows_v.at[h], sem_g[h]) for h in range(2)]
            writes = []
            for h in range(2):
                gathers[h].wait()
                writes.append(pltpu.async_copy(rows_v.at[h], out_hbm.at[pl.ds(off + h * half, half)], sem_w[h]))
            for c in writes:
                c.wait()

    return gather_kernel(table, idx)


def _experts_kernel(te_ref, used_ref, xs_ref, wg_ref, wu_ref, wd_ref, o_ref, wg_s, wu_s, wd_s):
    i = pl.program_id(0)

    @pl.when((i == 0) | (te_ref[i] != te_ref[jnp.maximum(i - 1, 0)]))
    def _():
        wg_s[...] = wg_ref[0].astype(BF16)
        wu_s[...] = wu_ref[0].astype(BF16)
        wd_s[...] = wd_ref[0].astype(BF16)

    @pl.when(i < used_ref[0])
    def _():
        x = _unpack_bf16_pairs(xs_ref[...]).astype(BF16)
        g = jnp.dot(x, wg_s[...], preferred_element_type=F32)
        u = jnp.dot(x, wu_s[...], preferred_element_type=F32)
        act = (_silu(g) * u).astype(BF16)
        o_ref[...] = _pack_bf16_pairs(jnp.dot(act, wd_s[...], preferred_element_type=F32))


def _experts(xs, tile_expert, tiles_used, we_gate, we_up, we_down):
    r, half = xs.shape
    d = 2 * half
    row_tile = lambda i, te, used: (jnp.minimum(i, used[0] - 1), 0)
    grid_spec = pltpu.PrefetchScalarGridSpec(
        num_scalar_prefetch=2,
        grid=(r // EXPERT_TILE,),
        in_specs=[pl.BlockSpec((EXPERT_TILE, half), row_tile),
                  pl.BlockSpec((1, d, D_EXPERT), lambda i, te, used: (te[i], 0, 0)),
                  pl.BlockSpec((1, d, D_EXPERT), lambda i, te, used: (te[i], 0, 0)),
                  pl.BlockSpec((1, D_EXPERT, d), lambda i, te, used: (te[i], 0, 0))],
        out_specs=pl.BlockSpec((EXPERT_TILE, half), row_tile),
        scratch_shapes=[pltpu.VMEM((d, D_EXPERT), BF16), pltpu.VMEM((d, D_EXPERT), BF16),
                        pltpu.VMEM((D_EXPERT, d), BF16)],
    )
    return pl.pallas_call(
        _experts_kernel,
        grid_spec=grid_spec,
        out_shape=jax.ShapeDtypeStruct((r, half), jnp.int32),
        compiler_params=_cparams(("arbitrary",), 32),
        name="experts",
    )(tile_expert, tiles_used, xs, we_gate, we_up, we_down)


def _combine_kernel(og_ref, wk_ref, h_ref, x1_ref, ga2_ref, sgu_ref, sd_ref, g2_ref, b2_ref, o_ref, *, dn_alpha):
    bb, tl, d = h_ref.shape
    tm = bb * tl
    h = h_ref[...].reshape(tm, d)
    su = jnp.dot(h, sgu_ref[...], preferred_element_type=F32)
    acc = _dot(_silu(su[:, :D_SHARED]) * su[:, D_SHARED:], sd_ref[...])
    lane = lax.broadcasted_iota(jnp.int32, (tm, TOP_K), 1)
    w_all = wk_ref[...]
    for k in range(TOP_K):
        acc = acc + _lane_pick(w_all, lane, k) * _unpack_bf16_pairs(og_ref[k])
    y = dn_alpha * x1_ref[...] + (1.0 + ga2_ref[...]) * acc.reshape(bb, tl, d)
    o_ref[...] = _layernorm(y, g2_ref[...], b2_ref[...])


def _combine(og, w_tk, h2, x1, ga2, ws_gu, ws_down, ln_g, ln_b, bb, tl, dn_alpha):
    b, l, d = x1.shape
    nj = l // tl
    tm = bb * tl
    full = lambda *shape: pl.BlockSpec(shape, lambda i, j: (0,) * len(shape))
    tok = pl.BlockSpec((bb, tl, d), lambda i, j: (i, j, 0))
    return pl.pallas_call(
        functools.partial(_combine_kernel, dn_alpha=dn_alpha),
        grid=(b // bb, nj),
        in_specs=[pl.BlockSpec((TOP_K, tm, d // 2), lambda i, j: (0, i * nj + j, 0)),
                  pl.BlockSpec((tm, TOP_K), lambda i, j: (i * nj + j, 0)),
                  tok, tok,
                  pl.BlockSpec((bb, 1, d), lambda i, j: (i, 0, 0)),
                  full(d, 2 * D_SHARED), full(D_SHARED, d), full(1, d), full(1, d)],
        out_specs=tok,
        out_shape=jax.ShapeDtypeStruct((b, l, d), F32),
        compiler_params=_cparams(("parallel", "parallel"), 48),
        name="combine",
    )(og, w_tk, h2, x1, ga2, ws_gu, ws_down, ln_g, ln_b)


def _moe_sparse(hp, eidx, rank, w_k, cnt, h2, x1, ga2, p, bb, dn_alpha):
    t_all = hp.shape[0]
    n_tiles = t_all * TOP_K // EXPERT_TILE + N_EXPERTS
    seg_tiles = (cnt[:, 0] + EXPERT_TILE - 1) // EXPERT_TILE
    seg_end = jnp.cumsum(seg_tiles)
    seg_start = (seg_end - seg_tiles) * EXPERT_TILE
    experts = jnp.arange(N_EXPERTS, dtype=jnp.int32)
    pos = rank + jnp.sum(jnp.where(eidx[None] == experts[:, None, None], seg_start[:, None, None], 0), axis=0)
    tiles = jnp.arange(n_tiles, dtype=jnp.int32)
    tile_expert = jnp.minimum(jnp.sum((seg_end[None, :] <= tiles[:, None]).astype(jnp.int32), axis=1),
                              N_EXPERTS - 1)
    xs = _sc_scatter_rows(hp, pos, n_tiles * EXPERT_TILE)
    outs = _experts(xs, tile_expert, seg_end[-1:].astype(jnp.int32), p['we_gate'], p['we_up'], p['we_down'])
    og = _sc_gather_rows(outs, pos.reshape(-1)).reshape(TOP_K, t_all, hp.shape[1])
    return _combine(og, w_k.T, h2, x1, ga2, p['ws_gu'], p['ws_down'], p['ln2_g'], p['ln2_b'], bb, COMBINE_TL, dn_alpha)


def _pad_rows(buf, width):
    b, n, w = buf.shape
    return jnp.pad(buf.astype(F32), ((0, 0), (SUBLANES - n, 0), (0, width - w)))


def _to_pairs(s):
    b = s.shape[0]
    s = s.astype(F32).reshape(b, PAIRS, 2, N_B, N_B)
    zero = jnp.zeros_like(s[:, :, 0])
    top = jnp.concatenate([s[:, :, 0], zero], axis=-1)
    bot = jnp.concatenate([zero, s[:, :, 1]], axis=-1)
    return jnp.concatenate([top, bot], axis=-2)


def _from_pairs(sp):
    b = sp.shape[0]
    return jnp.stack([sp[:, :, :N_B, :N_B], sp[:, :, N_B:, N_B:]], axis=2).reshape(b, H_B, N_B, N_B)


def _layer(x, mod, conv_buf, gdn_s, shift_buf, rwkv_s, p, tiles, dn_alpha):
    b, l, d = x.shape
    sh1, sc1, ga1, sh2, sc2, ga2 = mod
    lp = -(-l // CHUNK) * CHUNK
    x_in = x if lp == l else jnp.pad(x, ((0, 0), (0, lp - l), (0, 0)))
    proj = _inproj(x_in, sc1, sh1, p['w_in'], tiles['bb_in'], tiles['tl_in'])
    ya, gdn_new = _gdn(proj, _pad_rows(conv_buf, QKV_A), gdn_s.astype(F32), p['conv_w'], p['alog_row'],
                       p['dt_row'], p['gdn_norm_w'], tiles['bb_gdn'], tiles['tl_gdn'], l)
    yb, rwkv_pairs = _rwkv(proj, _pad_rows(shift_buf, RW_BLOCK), _to_pairs(rwkv_s), p['rwkv_vecs'],
                           p['w2p'], p['a2p'], p['g2'], tiles['bb_rwkv'], tiles['tl_rwkv'], l)
    if lp != l:
        ya, yb = ya[:, :l], yb[:, :l]

    x1, h2, gates_t, hp, eidx, rank, w_k, cnt = _merge(
        ya, yb, proj, x, ga1, sc2, sh2, p['p_a'], p['p_b'], p['w_o'], p['ln1_g'], p['ln1_b'],
        p['router_wt'], p['router_bias'], tiles['bb'], tiles['tl_merge'], dn_alpha)
    if (b * l) % (SC_WORKERS * SC_CHUNK) == 0:
        out = _moe_sparse(hp, eidx, rank, w_k, cnt, h2, x1, ga2, p, tiles['bb'], dn_alpha)
    else:
        out = _moe(h2, gates_t.T, x1, ga2, p['we_gate'], p['we_up'], p['we_down'], p['ws_gu'], p['ws_down'],
                   p['ln2_g'], p['ln2_b'], tiles['bb'], tiles['tl'], dn_alpha)

    pre = jnp.concatenate([conv_buf.astype(F32), proj[:, :l, P_QKV:P_QKV + QKV_A]], axis=1)
    conv_new = pre[:, -(CONV_W - 1):]
    shift_new = proj[:, l - 1:l, P_RW:P_RW + SHIFT_W]
    return (out, conv_new.astype(conv_buf.dtype), gdn_new.astype(gdn_s.dtype),
            shift_new.astype(shift_buf.dtype), _from_pairs(rwkv_pairs).astype(rwkv_s.dtype))


def _prep_params(l, w_in, conv_w, a_log, dt_bias, gdn_norm_w, mu_shift, w0, w2, a0, a2, g2, k_k, k_a, r_k,
                 lnx_g, lnx_b, p_a, p_b, w_o, ln1_g, ln1_b, router_w, router_bias, we_gate, we_up, we_down,
                 ws_gate, ws_up, ws_down, ln2_g, ln2_b):
    d = D_MODEL
    w = w_in[l]
    w_p = jnp.concatenate(
        [w[:, :QKV_A], w[:, OFF_Z:OFF_RWKV], w[:, OFF_RWKV:OFF_MERGE],
         jnp.zeros((d, RW_BLOCK - SHIFT_W), w.dtype), w[:, OFF_MERGE:], w[:, OFF_ALPHA:OFF_Z],
         jnp.zeros((d, N_PROJ - P_AB - 2 * H_A), w.dtype)], axis=1).astype(BF16)
    row = lambda v, width: jnp.pad(v.astype(F32).reshape(1, -1), ((0, 0), (0, width - v.size)))
    zeros_lora = jnp.zeros((W_LORA, D_B), F32)
    return {
        'w_in': w_p,
        'conv_w': conv_w[l].astype(F32),
        'alog_row': row(a_log[l], LANES),
        'dt_row': row(dt_bias[l], LANES),
        'gdn_norm_w': row(gdn_norm_w[l], DK_A),
        'rwkv_vecs': (row(mu_shift[l], RW_BLOCK), row(w0[l], D_B), row(a0[l], D_B), row(k_k[l], D_B),
                      row(k_a[l], D_B), row(r_k[l], D_B), row(lnx_g[l], D_B), row(lnx_b[l], D_B)),
        'w2p': jnp.concatenate([w2[l].astype(F32), zeros_lora], axis=0),
        'a2p': jnp.concatenate([zeros_lora, a2[l].astype(F32)], axis=0),
        'g2': g2[l].astype(F32),
        'p_a': p_a[l].astype(BF16), 'p_b': p_b[l].astype(BF16), 'w_o': w_o[l].astype(BF16),
        'ln1_g': row(ln1_g[l], d), 'ln1_b': row(ln1_b[l], d),
        'router_wt': router_w[l].T.astype(BF16),
        'router_bias': router_bias[l].astype(F32).reshape(N_EXPERTS, 1),
        'we_gate': we_gate[l], 'we_up': we_up[l], 'we_down': we_down[l],
        'ws_gu': jnp.concatenate([ws_gate[l], ws_up[l]], axis=-1).astype(BF16),
        'ws_down': ws_down[l].astype(BF16),
        'ln2_g': row(ln2_g[l], d), 'ln2_b': row(ln2_b[l], d),
    }


def _tiles(b, l):
    lp = -(-l // CHUNK) * CHUNK
    if l >= TOKEN_TILE:
        bb, tl, tl_merge = 1, TOKEN_TILE, TOKEN_TILE // 2
        bb_in, tl_in = 1, TOKEN_TILE
    else:
        bb, tl, tl_merge = b, l, l
        bb_in, tl_in = TOKEN_TILE // lp, lp
    return {'bb': bb, 'tl': tl, 'tl_merge': tl_merge, 'bb_in': bb_in, 'tl_in': tl_in,
            'bb_gdn': 4, 'tl_gdn': CHUNK, 'bb_rwkv': 2, 'tl_rwkv': min(2 * CHUNK, lp)}


def kernel(x_prompt, x_sample, c_prompt, c_sample, state_gdn_conv, state_gdn, state_rwkv_shift, state_rwkv, w_ada, b_ada, w_in, conv_w, a_log, dt_bias, gdn_norm_w, mu_shift, w0, w2, a0, a2, g2, k_k, k_a, r_k, lnx_g, lnx_b, p_a, p_b, w_o, ln1_g, ln1_b, router_w, router_bias, we_gate, we_up, we_down, ws_gate, ws_up, ws_down, ln2_g, ln2_b):
    depth = w_ada.shape[0]
    dn_alpha = (2 * depth) ** 0.25
    bp, lp_, d = x_prompt.shape
    bs, ls, _ = x_sample.shape
    dtp = x_prompt.dtype
    tiles_p = _tiles(bp, lp_)
    tiles_s = _tiles(bs, ls)

    yp, ys = x_prompt, x_sample
    new_p = ([], [], [], [])
    new_s = ([], [], [], [])
    for l in range(depth):
        p = _prep_params(l, w_in, conv_w, a_log, dt_bias, gdn_norm_w, mu_shift, w0, w2, a0, a2, g2, k_k, k_a,
                         r_k, lnx_g, lnx_b, p_a, p_b, w_o, ln1_g, ln1_b, router_w, router_bias, we_gate, we_up,
                         we_down, ws_gate, ws_up, ws_down, ln2_g, ln2_b)
        mod = _ada(jnp.concatenate([c_prompt, c_sample], axis=0), w_ada[l], b_ada[l])
        mod_p = tuple(m[:, None, :] for m in jnp.split(mod[:bp], 6, axis=-1))
        mod_s = tuple(m[:, None, :] for m in jnp.split(mod[bp:], 6, axis=-1))
        yp, *sp = _layer(yp, mod_p,
                         jnp.zeros((bp, CONV_W - 1, QKV_A), dtp), jnp.zeros((bp, H_A, DK_A, DK_A), dtp),
                         jnp.zeros((bp, 1, SHIFT_W), dtp), jnp.zeros((bp, H_B, N_B, N_B), dtp),
                         p, tiles_p, dn_alpha)
        ys, *ss = _layer(ys, mod_s, state_gdn_conv[l], state_gdn[l], state_rwkv_shift[l], state_rwkv[l],
                         p, tiles_s, dn_alpha)
        for lst, val in zip(new_p, sp):
            lst.append(val)
        for lst, val in zip(new_s, ss):
            lst.append(val)
    conv_p, gdn_p, shift_p, rwkv_p = [jnp.stack(t, 0) for t in new_p]
    conv_s, gdn_s, shift_s, rwkv_s = [jnp.stack(t, 0) for t in new_s]
    return (yp, ys, conv_p, gdn_p, shift_p, rwkv_p, conv_s, gdn_s, shift_s, rwkv_s)
```

```python
import functools
import math

import jax
import jax.numpy as jnp
from jax import lax
from jax.experimental import pallas as pl
from jax.experimental.pallas import tpu as pltpu
from jax.experimental.pallas import tpu_sc as plsc

F32 = jnp.float32
BF16 = jnp.bfloat16

D_MODEL = 1024
DK_A = 128
H_A = 4
QK_A = H_A * DK_A
V_A = H_A * DK_A
QKV_A = 2 * QK_A + V_A
CONV_W = 4
N_B = 64
H_B = 8
D_B = H_B * N_B
W_LORA = 64
A_LORA = 64
G_LORA = 128
SHIFT_W = 3 * D_B + W_LORA + A_LORA + G_LORA
OFF_ALPHA = QKV_A
OFF_BETA = OFF_ALPHA + H_A
OFF_Z = OFF_BETA + H_A
OFF_RWKV = OFF_Z + V_A
OFF_MERGE = OFF_RWKV + SHIFT_W
N_EXPERTS = 64
TOP_K = 8
N_GROUPS = 8
GROUP_SIZE = N_EXPERTS // N_GROUPS
TOPK_GROUPS = 4
D_EXPERT = 256
D_SHARED = 256
ROUTED_SCALE = 2.5
LN_EPS = 1e-5
GN_EPS = 64e-5
RMS_EPS = 1e-6
DECAY_SCALE = -math.exp(-0.5)

SUBLANES = 8
LANES = 128

P_QKV = 0
P_Z = QKV_A
P_RW = 2048
RW_BLOCK = 2048
P_MG = P_RW + RW_BLOCK
P_AB = P_MG + 2 * D_MODEL
AB_BLOCK = LANES
PROJ_TN = 1280
N_PROJ = 5 * PROJ_TN

CHUNK = 128
BASE_BLOCK = 8
CUMSUM_PARTS = 3
STAT_PARTS = 2
EXPERTS_PER_STEP = 2
HIGH_HALF = -65536

EXPERT_TILE = 1024
SC_WORKERS = 32
SC_CHUNK = 128
COMBINE_TL = 512
TOKEN_TILE = 1024


def _cparams(sem, vmem_mb):
    return pltpu.CompilerParams(dimension_semantics=sem, vmem_limit_bytes=vmem_mb * 1024 * 1024)


def _dot(a, b):
    return jnp.dot(a.astype(BF16), b.astype(BF16), preferred_element_type=F32)


def _rdot(a, b, dims=((1,), (0,))):
    return lax.dot_general(a.astype(BF16), b.astype(BF16), (dims, ((), ())), preferred_element_type=F32)


_NT = ((1,), (1,))
_TN = ((0,), (0,))


def _bf16_parts(x, parts):
    out = []
    rem = x
    for _ in range(parts):
        hi = rem.astype(BF16)
        out.append(hi)
        rem = rem - hi.astype(F32)
    return out


def _mask_dot_left(mask, x, parts):
    return sum(jnp.dot(mask, p, preferred_element_type=F32) for p in _bf16_parts(x, parts))


def _mask_dot_right(x, mask, parts):
    return sum(jnp.dot(p, mask, preferred_element_type=F32) for p in _bf16_parts(x, parts))


def _lane_pick(x, lane_iota, lane):
    return jnp.sum(jnp.where(lane_iota == lane, x, 0.0), axis=-1, keepdims=True)


def _pack_bf16_pairs(x):
    n = x.shape[1] // 2
    bits = lax.bitcast_convert_type(x.astype(BF16).astype(F32), jnp.int32)
    return (bits[:, :n] & HIGH_HALF) | lax.shift_right_logical(bits[:, n:], 16)


def _unpack_bf16_pairs(p):
    hi = lax.bitcast_convert_type(p & HIGH_HALF, F32)
    lo = lax.bitcast_convert_type(lax.shift_left(p, 16), F32)
    return jnp.concatenate([hi, lo], axis=1)


def _silu(x):
    return x * jax.nn.sigmoid(x)


def _softplus(x):
    return jnp.maximum(x, 0.0) + jnp.log1p(jnp.exp(-jnp.abs(x)))


def _tri_inverse(mats, eye, sign, row, col):
    base = (row // BASE_BLOCK) == (col // BASE_BLOCK)
    ds = [jnp.where(base, a, 0.0) for a in mats]
    xs = [eye + d if sign > 0 else eye - d for d in ds]
    power = 2
    while power < BASE_BLOCK:
        ds = [_rdot(d, d) for d in ds]
        xs = [x + _rdot(x, d) for x, d in zip(xs, ds)]
        power *= 2
    b = BASE_BLOCK
    while b < CHUNK:
        sibling = ((row // b) == (col // b) + 1) & ((row // (2 * b)) == (col // (2 * b)))
        offs = [jnp.where(sibling, a, 0.0) for a in mats]
        ts = [_rdot(o, x) for o, x in zip(offs, xs)]
        if sign > 0:
            xs = [x + _rdot(x, t_) for x, t_ in zip(xs, ts)]
        else:
            xs = [x - _rdot(x, t_) for x, t_ in zip(xs, ts)]
        b *= 2
    return xs


def _tri_consts():
    row = lax.broadcasted_iota(jnp.int32, (CHUNK, CHUNK), 0)
    col = lax.broadcasted_iota(jnp.int32, (CHUNK, CHUNK), 1)
    incl = row >= col
    strict = row > col
    eye = jnp.where(row == col, 1.0, 0.0).astype(F32)
    tril = jnp.where(incl, 1.0, 0.0).astype(BF16)
    return row, col, incl, strict, eye, tril


def _ada_kernel(c_ref, w_ref, b_ref, o_ref):
    o_ref[...] = _dot(_silu(c_ref[...]), w_ref[...]) + b_ref[...]


def _ada(c, w_ada, b_ada):
    n, d = c.shape
    nout = w_ada.shape[1]
    tn = 768
    return pl.pallas_call(
        _ada_kernel,
        grid=(nout // tn,),
        in_specs=[pl.BlockSpec((n, d), lambda j: (0, 0)),
                  pl.BlockSpec((d, tn), lambda j: (0, j)),
                  pl.BlockSpec((1, tn), lambda j: (0, j))],
        out_specs=pl.BlockSpec((n, tn), lambda j: (0, j)),
        out_shape=jax.ShapeDtypeStruct((n, nout), F32),
        compiler_params=_cparams(("parallel",), 32),
        name="ada",
    )(c, w_ada, b_ada.reshape(1, nout))


def _inproj_kernel(x_ref, sc_ref, sh_ref, w_ref, o_ref, hs_ref):
    @pl.when(pl.program_id(2) == 0)
    def _():
        h = x_ref[...] * (1.0 + sc_ref[...]) + sh_ref[...]
        hs_ref[...] = h.reshape(hs_ref.shape).astype(BF16)

    o = jnp.dot(hs_ref[...], w_ref[...], preferred_element_type=F32)
    o_ref[...] = o.reshape(o_ref.shape)


def _inproj(x, sc, sh, w_p, bb, tl):
    b, l, d = x.shape
    return pl.pallas_call(
        _inproj_kernel,
        grid=(b // bb, l // tl, N_PROJ // PROJ_TN),
        in_specs=[pl.BlockSpec((bb, tl, d), lambda i, j, n: (i, j, 0)),
                  pl.BlockSpec((bb, 1, d), lambda i, j, n: (i, 0, 0)),
                  pl.BlockSpec((bb, 1, d), lambda i, j, n: (i, 0, 0)),
                  pl.BlockSpec((d, PROJ_TN), lambda i, j, n: (0, n))],
        out_specs=pl.BlockSpec((bb, tl, PROJ_TN), lambda i, j, n: (i, j, n)),
        out_shape=jax.ShapeDtypeStruct((b, l, N_PROJ), F32),
        scratch_shapes=[pltpu.VMEM((bb * tl, d), BF16)],
        compiler_params=_cparams(("parallel", "parallel", "arbitrary"), 48),
        name="inproj",
    )(x, sc, sh, w_p)


def _gdn_kernel(qkv_ref, ab_ref, z_ref, cbuf_ref, s0_ref, cw_ref, alog_ref, dt_ref, nw_ref,
                ya_ref, sout_ref, ext_ref, qc_ref, s_ref, *pad_s, bb, tl, l_in, l_valid, l_total):
    t = pl.program_id(1)

    @pl.when(t == 0)
    def _():
        ext_ref[:, 0:SUBLANES, :] = cbuf_ref[...]
        s_ref[...] = s0_ref[...]

    if l_in < tl:
        for src, dst in zip((qkv_ref, ab_ref, z_ref), pad_s):
            dst[:, l_in:, :] = jnp.zeros((bb, tl - l_in, dst.shape[2]), F32)
            dst[:, 0:l_in, :] = src[...]
        qkv_ref, ab_ref, z_ref = pad_s

    for bi in range(bb):
        cur = qkv_ref[bi]
        ext_ref[bi, SUBLANES:SUBLANES + tl, :] = cur
        window = ext_ref[bi]
        acc = cw_ref[CONV_W - 1:CONV_W, :] * cur
        for j in range(CONV_W - 1):
            back = CONV_W - 1 - j
            acc = acc + cw_ref[j:j + 1, :] * pltpu.roll(window, tl + back, 0)[0:tl]
        ext_ref[bi, 0:SUBLANES, :] = ext_ref[bi, tl:tl + SUBLANES, :]
        qc_ref[bi] = _silu(acc)

    row, col, incl, strict, eye, tril = _tri_consts()
    neg_a = -jnp.exp(alog_ref[...])
    dt = dt_ref[...]
    nw = nw_ref[...]
    chains = [(bi, h) for bi in range(bb) for h in range(H_A)]
    n = range(len(chains))

    def chunk(c, carry):
        r0 = pl.multiple_of(c * CHUNK, CHUNK)
        rows = pl.ds(r0, CHUNK)
        g_cum = []
        b_all = []
        for bi in range(bb):
            ab = ab_ref[bi, rows, :]
            g_bi = neg_a * _softplus(ab + dt)
            b_bi = jax.nn.sigmoid(ab)
            if l_valid < l_total:
                valid = (t * tl + r0 + row) < l_valid
                g_bi = jnp.where(valid, g_bi, 0.0)
                b_bi = jnp.where(valid, b_bi, 0.0)
            g_cum.append(_mask_dot_left(tril, g_bi, CUMSUM_PARTS))
            b_all.append(b_bi)
        g_col = [_lane_pick(g_cum[bi], col, h) for bi, h in chains]
        beta = [_lane_pick(b_all[bi], col, H_A + h) for bi, h in chains]
        decay = []
        for i in n:
            g_b = jnp.broadcast_to(g_col[i], (CHUNK, CHUNK))
            decay.append(jnp.where(incl, jnp.exp(jnp.where(incl, g_b - g_b.T, 0.0)), 0.0))
        q = [qc_ref[bi, rows, h * DK_A:(h + 1) * DK_A] for bi, h in chains]
        k = [qc_ref[bi, rows, QK_A + h * DK_A:QK_A + (h + 1) * DK_A] for bi, h in chains]
        v = [qc_ref[bi, rows, 2 * QK_A + h * DK_A:2 * QK_A + (h + 1) * DK_A] for bi, h in chains]
        q = [x * lax.rsqrt(jnp.sum(x * x, -1, keepdims=True) + 1e-6) * (DK_A ** -0.5) for x in q]
        k = [x * lax.rsqrt(jnp.sum(x * x, -1, keepdims=True) + 1e-6) for x in k]
        kq = [_rdot(jnp.concatenate([k[i], q[i]], axis=0), k[i], _NT) for i in n]
        a = [jnp.where(strict, beta[i] * kq[i][:CHUNK] * decay[i], 0.0) for i in n]
        qk = [jnp.where(incl, kq[i][CHUNK:] * decay[i], 0.0) for i in n]
        x = _tri_inverse(a, eye, -1, row, col)
        e_g = [jnp.exp(g) for g in g_col]
        uw = [_rdot(x[i], jnp.concatenate([v[i] * beta[i], k[i] * (beta[i] * e_g[i])], axis=1)) for i in n]
        g_last = [g[CHUNK - 1:CHUNK, :] for g in g_col]
        kd = [k[i] * jnp.exp(g_last[i] - g_col[i]) for i in n]
        s = [s_ref[bi, h] for bi, h in chains]
        ws = [_rdot(jnp.concatenate([uw[i][:, DK_A:], q[i] * e_g[i]], axis=0), s[i]) for i in n]
        v_new = [uw[i][:, :DK_A] - ws[i][:CHUNK] for i in n]
        o = [ws[i][CHUNK:] + _rdot(qk[i], v_new[i]) for i in n]
        s_new = [s[i] * jnp.exp(g_last[i]) + _rdot(kd[i], v_new[i], _TN) for i in n]
        for i, (bi, h) in enumerate(chains):
            s_ref[bi, h] = s_new[i]
            z = z_ref[bi, rows, h * DK_A:(h + 1) * DK_A]
            on = o[i] * lax.rsqrt(jnp.mean(o[i] * o[i], -1, keepdims=True) + RMS_EPS) * nw * _silu(z)
            if l_in < tl:
                ya_ref[bi, :, h * DK_A:(h + 1) * DK_A] = on[0:l_in].astype(BF16)
            else:
                ya_ref[bi, rows, h * DK_A:(h + 1) * DK_A] = on.astype(BF16)
        return carry

    lax.fori_loop(0, tl // CHUNK, chunk, 0)

    @pl.when(t == pl.num_programs(1) - 1)
    def _():
        sout_ref[...] = s_ref[...]


def _rec_rows(l, tl):
    if l % tl == 0:
        return l, tl
    assert l < tl == CHUNK, (l, tl)
    return tl, l


def _gdn(proj, conv_buf8, s0, conv_w, alog_row, dt_row, norm_w, bb, tl, l_valid):
    b, l, _ = proj.shape
    lp, l_in = _rec_rows(l, tl)
    kern = functools.partial(_gdn_kernel, bb=bb, tl=tl, l_in=l_in, l_valid=l_valid, l_total=lp)
    full = lambda *shape: pl.BlockSpec(shape, lambda i, t: (0,) * len(shape))
    pad_scratch = [pltpu.VMEM((bb, tl, w), F32) for w in (QKV_A, AB_BLOCK, V_A)] if l_in < tl else []
    return pl.pallas_call(
        kern,
        grid=(b // bb, lp // tl),
        in_specs=[pl.BlockSpec((bb, l_in, QKV_A), lambda i, t: (i, t, P_QKV // QKV_A)),
                  pl.BlockSpec((bb, l_in, AB_BLOCK), lambda i, t: (i, t, P_AB // AB_BLOCK)),
                  pl.BlockSpec((bb, l_in, V_A), lambda i, t: (i, t, P_Z // V_A)),
                  pl.BlockSpec((bb, SUBLANES, QKV_A), lambda i, t: (i, 0, 0)),
                  pl.BlockSpec((bb, H_A, DK_A, DK_A), lambda i, t: (i, 0, 0, 0)),
                  full(CONV_W, QKV_A), full(1, LANES), full(1, LANES), full(1, DK_A)],
        out_specs=[pl.BlockSpec((bb, l_in, V_A), lambda i, t: (i, t, 0)),
                   pl.BlockSpec((bb, H_A, DK_A, DK_A), lambda i, t: (i, 0, 0, 0))],
        out_shape=[jax.ShapeDtypeStruct((b, l, V_A), BF16),
                   jax.ShapeDtypeStruct((b, H_A, DK_A, DK_A), F32)],
        scratch_shapes=[pltpu.VMEM((bb, tl + SUBLANES, QKV_A), F32),
                        pltpu.VMEM((bb, tl, QKV_A), F32),
                        pltpu.VMEM((bb, H_A, DK_A, DK_A), F32)] + pad_scratch,
        compiler_params=_cparams(("parallel", "arbitrary"), 48),
        name="gdn",
    )(proj, proj, proj, conv_buf8, s0, conv_w, alog_row, dt_row, norm_w)


PAIRS = H_B // 2


def _headsum(x, bd):
    return jnp.concatenate(
        [_mask_dot_right(x[:, p * LANES:(p + 1) * LANES], bd, STAT_PARTS) for p in range(PAIRS)], axis=1)


def _rwkv_kernel(rw_ref, sbuf_ref, s0_ref, mu_ref, w0_ref, a0_ref, kk_ref, ka_ref, rk_ref, lg_ref, lb_ref,
                 w2_ref, a2_ref, g2_ref, yb_ref, sout_ref,
                 ext_ref, r_s, k_s, v_s, z_s, p_s, lw_s, y_s, bonus_s, gate_s, s_ref, *,
                 bb, tl, l_in, l_valid, l_total):
    t = pl.program_id(1)

    @pl.when(t == 0)
    def _():
        ext_ref[:, 0:SUBLANES, :] = sbuf_ref[...]
        s_ref[...] = s0_ref[...]

    row, col, incl, strict, eye, tril = _tri_consts()
    same_head = (row // N_B) == (col // N_B)
    bd = jnp.where(same_head, 1.0, 0.0).astype(BF16)
    lane_a = col < N_B

    for bi in range(bb):
        if l_in < tl:
            ext_ref[bi, SUBLANES + l_in:SUBLANES + tl, :] = jnp.zeros((tl - l_in, RW_BLOCK), F32)
            ext_ref[bi, SUBLANES:SUBLANES + l_in, :] = rw_ref[bi]
            cur = ext_ref[bi, SUBLANES:SUBLANES + tl, :]
        else:
            cur = rw_ref[bi]
            ext_ref[bi, SUBLANES:SUBLANES + tl, :] = cur
        prev = pltpu.roll(ext_ref[bi], tl + 1, 0)[0:tl]
        mixed = cur + (prev - cur) * mu_ref[...]
        ext_ref[bi, 0:SUBLANES, :] = ext_ref[bi, tl:tl + SUBLANES, :]
        r = mixed[:, 0:D_B]
        k = mixed[:, D_B:2 * D_B]
        v = mixed[:, 2 * D_B:3 * D_B]
        lora = mixed[:, 3 * D_B:3 * D_B + W_LORA + A_LORA]
        g_in = mixed[:, 3 * D_B + W_LORA + A_LORA:SHIFT_W]
        lw = DECAY_SCALE * jax.nn.sigmoid(w0_ref[...] + _dot(jnp.tanh(lora), w2_ref[...]))
        a = jax.nn.sigmoid(a0_ref[...] + _dot(lora, a2_ref[...]))
        kkr = k * kk_ref[...]
        kk = kkr * lax.rsqrt(_headsum(kkr * kkr, bd) + 1e-6)
        k = k * (1.0 + (a - 1.0) * ka_ref[...])
        tile_rows = pl.ds(bi * tl, tl)
        bonus_s[tile_rows, :] = _headsum(r * k * rk_ref[...], bd) * v
        gate_s[tile_rows, :] = _dot(jax.nn.sigmoid(g_in), g2_ref[...])
        if l_valid < l_total:
            rvalid = (t * tl + lax.broadcasted_iota(jnp.int32, (tl, 1), 0)) < l_valid
            lw = jnp.where(rvalid, lw, 0.0)
            kk = jnp.where(rvalid, kk, 0.0)
            k = jnp.where(rvalid, k, 0.0)
            v = jnp.where(rvalid, v, 0.0)
        r_s[tile_rows, :] = r
        k_s[tile_rows, :] = k
        v_s[tile_rows, :] = v
        z_s[tile_rows, :] = -kk
        p_s[tile_rows, :] = kk * a
        lw_s[tile_rows, :] = lw

    groups = [(bi, p) for bi in range(bb) for p in range(PAIRS)]
    pairs = range(len(groups))
    both = range(2 * len(groups))
    cols = [slice(p * LANES, (p + 1) * LANES) for _, p in groups]

    def chunk(c, carry):
        r0 = pl.multiple_of(c * CHUNK, CHUNK)
        rows = [pl.ds(bi * tl + r0, CHUNK) for bi, _ in groups]
        lw_c = [lw_s[rows[g], cols[g]] for g in pairs]
        g_inc = [_mask_dot_left(tril, x, CUMSUM_PARTS) for x in lw_c]
        g_exc = [g_inc[p] - lw_c[p] for p in pairs]
        g_mid = [g[CHUNK // 2 - 1:CHUNK // 2, :] for g in g_inc]
        g_end = [g[CHUNK - 1:CHUNK, :] for g in g_inc]
        z = [z_s[rows[g], cols[g]] for g in pairs]
        rr = [r_s[rows[g], cols[g]] for g in pairs]
        pp = [p_s[rows[g], cols[g]] for g in pairs]
        kk_ = [k_s[rows[g], cols[g]] for g in pairs]
        vv = [v_s[rows[g], cols[g]] for g in pairs]
        zt = [z[p] * jnp.exp(g_exc[p] - g_mid[p]) for p in pairs]
        rt = [rr[p] * jnp.exp(g_inc[p] - g_mid[p]) for p in pairs]
        en = [jnp.exp(g_mid[p] - g_inc[p]) for p in pairs]
        s = [s_ref[bi, p] for bi, p in groups]
        lhs = [jnp.concatenate([jnp.where(lane_a, zt[p], 0.0), jnp.where(lane_a, 0.0, zt[p]),
                                jnp.where(lane_a, rt[p], 0.0), jnp.where(lane_a, 0.0, rt[p])], axis=0) for p in pairs]
        m = [_rdot(lhs[p], jnp.concatenate([pp[p] * en[p], kk_[p] * en[p]], axis=0), _NT) for p in pairs]
        zr0 = [_rdot(jnp.concatenate([z[p] * jnp.exp(g_exc[p]), rr[p] * jnp.exp(g_inc[p])], axis=0), s[p], _NT)
               for p in pairs]
        mz = [m[i // 2][(i % 2) * CHUNK:(i % 2 + 1) * CHUNK] for i in both]
        azp = [jnp.where(strict, x[:, :CHUNK], 0.0) for x in mz]
        azk = [jnp.where(strict, x[:, CHUNK:], 0.0) for x in mz]
        minv = _tri_inverse(azp, eye, 1, row, col)
        rhs = [zr0[i // 2][:CHUNK] + _rdot(azk[i], vv[i // 2]) for i in both]
        u_h = [_rdot(minv[i], rhs[i]) for i in both]
        u = [jnp.where(lane_a, u_h[2 * p], u_h[2 * p + 1]) for p in pairs]
        uv = [jnp.concatenate([u[p], vv[p]], axis=0) for p in pairs]
        incl2 = jnp.concatenate([incl, incl], axis=1)
        y_h = [_rdot(jnp.where(incl2, m[i // 2][(2 + i % 2) * CHUNK:(3 + i % 2) * CHUNK], 0.0), uv[i // 2])
               for i in both]
        tail = [jnp.exp(g_end[p] - g_inc[p]) for p in pairs]
        s_new = [s[p] * jnp.exp(g_end[p])
                 + _rdot(uv[p], jnp.concatenate([pp[p] * tail[p], kk_[p] * tail[p]], axis=0), _TN) for p in pairs]
        for g, (bi, p) in enumerate(groups):
            s_ref[bi, p] = jnp.where(same_head, s_new[g], 0.0)
            y_s[rows[g], cols[g]] = zr0[g][CHUNK:] + jnp.where(lane_a, y_h[2 * g], y_h[2 * g + 1])
        return carry

    lax.fori_loop(0, tl // CHUNK, chunk, 0)

    for bi in range(bb):
        tile_rows = pl.ds(bi * tl, tl)
        y = y_s[tile_rows, :]
        mean = _headsum(y, bd) * (1.0 / N_B)
        dev = y - mean
        var = _headsum(dev * dev, bd) * (1.0 / N_B)
        yn = dev * lax.rsqrt(var + GN_EPS) * lg_ref[...] + lb_ref[...]
        yb_ref[bi] = ((yn + bonus_s[tile_rows, :]) * gate_s[tile_rows, :])[0:l_in].astype(BF16)

    @pl.when(t == pl.num_programs(1) - 1)
    def _():
        sout_ref[...] = s_ref[...]


def _rwkv(proj, shift_buf8, s0_pairs, vecs, w2p, a2p, g2, bb, tl, l_valid):
    b, l, _ = proj.shape
    lp, l_in = _rec_rows(l, tl)
    kern = functools.partial(_rwkv_kernel, bb=bb, tl=tl, l_in=l_in, l_valid=l_valid, l_total=lp)
    full = lambda *shape: pl.BlockSpec(shape, lambda i, t: (0,) * len(shape))
    mu, w0, a0, k_k, k_a, r_k, lnx_g, lnx_b = vecs
    return pl.pallas_call(
        kern,
        grid=(b // bb, lp // tl),
        in_specs=[pl.BlockSpec((bb, l_in, RW_BLOCK), lambda i, t: (i, t, P_RW // RW_BLOCK)),
                  pl.BlockSpec((bb, SUBLANES, RW_BLOCK), lambda i, t: (i, 0, 0)),
                  pl.BlockSpec((bb, PAIRS, LANES, LANES), lambda i, t: (i, 0, 0, 0)),
                  full(1, RW_BLOCK)] + [full(1, D_B)] * 7 +
                 [full(W_LORA + A_LORA, D_B), full(W_LORA + A_LORA, D_B), full(G_LORA, D_B)],
        out_specs=[pl.BlockSpec((bb, l_in, D_B), lambda i, t: (i, t, 0)),
                   pl.BlockSpec((bb, PAIRS, LANES, LANES), lambda i, t: (i, 0, 0, 0))],
        out_shape=[jax.ShapeDtypeStruct((b, l, D_B), BF16),
                   jax.ShapeDtypeStruct((b, PAIRS, LANES, LANES), F32)],
        scratch_shapes=[pltpu.VMEM((bb, tl + SUBLANES, RW_BLOCK), F32)] +
                       [pltpu.VMEM((bb * tl, D_B), F32)] * 9 +
                       [pltpu.VMEM((bb, PAIRS, LANES, LANES), F32)],
        compiler_params=_cparams(("parallel", "arbitrary"), 48),
        name="rwkv",
    )(proj, shift_buf8, s0_pairs, mu, w0, a0, k_k, k_a, r_k, lnx_g, lnx_b, w2p, a2p, g2)


def _layernorm(y, g, b):
    mu = jnp.mean(y, -1, keepdims=True)
    dev = y - mu
    var = jnp.mean(dev * dev, -1, keepdims=True)
    return dev * lax.rsqrt(var + LN_EPS) * g + b


def _route(logits_t, bias, base_cnt):
    tm = logits_t.shape[1]
    scores = jax.nn.sigmoid(logits_t)
    choice = scores + bias
    neg_inf = -jnp.inf
    iota_g = lax.broadcasted_iota(jnp.int32, (GROUP_SIZE, tm), 0)
    group_score = []
    for g in range(N_GROUPS):
        xg = choice[g * GROUP_SIZE:(g + 1) * GROUP_SIZE, :]
        m1 = jnp.max(xg, axis=0, keepdims=True)
        first = jnp.min(jnp.where(xg == m1, iota_g, GROUP_SIZE), axis=0, keepdims=True)
        m2 = jnp.max(jnp.where(iota_g == first, neg_inf, xg), axis=0, keepdims=True)
        group_score.append(m1 + m2)
    masked = []
    for g in range(N_GROUPS):
        rank = jnp.zeros((1, tm), jnp.int32)
        for o in range(N_GROUPS):
            if o == g:
                continue
            ahead = group_score[o] > group_score[g]
            if o < g:
                ahead = ahead | (group_score[o] == group_score[g])
            rank = rank + ahead.astype(jnp.int32)
        keep = rank < TOPK_GROUPS
        masked.append(jnp.where(keep, choice[g * GROUP_SIZE:(g + 1) * GROUP_SIZE, :], neg_inf))
    cur = jnp.concatenate(masked, axis=0)
    iota_e = lax.broadcasted_iota(jnp.int32, (N_EXPERTS, tm), 0)
    sel = jnp.zeros((N_EXPERTS, tm), F32)
    picks = []
    firsts = []
    for _ in range(TOP_K):
        m = jnp.max(cur, axis=0, keepdims=True)
        first = jnp.min(jnp.where(cur == m, iota_e, N_EXPERTS), axis=0, keepdims=True)
        pick = iota_e == first
        sel = jnp.where(pick, 1.0, sel)
        cur = jnp.where(pick, neg_inf, cur)
        picks.append(pick)
        firsts.append(first)
    wsel = sel * scores
    denom = jnp.sum(wsel, axis=0, keepdims=True) + 1e-20
    gates = (ROUTED_SCALE * wsel) / denom

    upto = (lax.broadcasted_iota(jnp.int32, (tm, tm), 0) <= lax.broadcasted_iota(jnp.int32, (tm, tm), 1))
    csum = jnp.dot(sel.astype(BF16), jnp.where(upto, 1.0, 0.0).astype(BF16), preferred_element_type=F32)
    before = base_cnt + csum - sel
    iota_k = lax.broadcasted_iota(jnp.int32, (TOP_K, tm), 0)
    eidx = jnp.zeros((TOP_K, tm), jnp.int32)
    rank = jnp.zeros((TOP_K, tm), F32)
    w_k = jnp.zeros((TOP_K, tm), F32)
    for i in range(TOP_K):
        eidx = jnp.where(iota_k == i, firsts[i], eidx)
        rank = jnp.where(iota_k == i, jnp.sum(jnp.where(picks[i], before, 0.0), axis=0, keepdims=True), rank)
        w_k = jnp.where(iota_k == i, jnp.sum(jnp.where(picks[i], gates, 0.0), axis=0, keepdims=True), w_k)
    return gates, eidx, rank.astype(jnp.int32), w_k, jnp.sum(sel, axis=1, keepdims=True)


def _merge_kernel(ya_ref, yb_ref, mg_ref, x_ref, ga1_ref, sc2_ref, sh2_ref, pa_ref, pb_ref, wo_ref,
                  g1_ref, b1_ref, rwt_ref, rb_ref, x1_ref, h2_ref, gt_ref, hp_ref, eidx_ref, rank_ref, wk_ref,
                  cnt_ref, cnt_s, *, dn_alpha):
    bb, tl, d = x_ref.shape
    tm = bb * tl

    @pl.when((pl.program_id(0) == 0) & (pl.program_id(1) == 0))
    def _():
        cnt_s[...] = jnp.zeros_like(cnt_s)

    ya = ya_ref[...].reshape(tm, V_A)
    yb = yb_ref[...].reshape(tm, D_B)
    mg = mg_ref[...].reshape(tm, 2 * d)
    merged = (jax.nn.sigmoid(mg[:, :d]) * jnp.dot(ya, pa_ref[...], preferred_element_type=F32)
              + jax.nn.sigmoid(mg[:, d:]) * jnp.dot(yb, pb_ref[...], preferred_element_type=F32))
    mix = _dot(merged, wo_ref[...])
    y = dn_alpha * x_ref[...] + (1.0 + ga1_ref[...]) * mix.reshape(bb, tl, d)
    x1 = _layernorm(y, g1_ref[...], b1_ref[...])
    x1_ref[...] = x1
    h2 = (x1 * (1.0 + sc2_ref[...]) + sh2_ref[...]).astype(BF16)
    h2_ref[...] = h2
    h2f = h2.reshape(tm, d)
    hp_ref[...] = _pack_bf16_pairs(h2f.astype(F32))
    logits_t = lax.dot_general(rwt_ref[...], h2f, (_NT, ((), ())), preferred_element_type=F32)
    gates, eidx, rank, w_k, tile_cnt = _route(logits_t, rb_ref[...], cnt_s[...])
    gt_ref[...] = gates
    eidx_ref[...] = eidx
    rank_ref[...] = rank
    wk_ref[...] = w_k
    cnt_s[...] = cnt_s[...] + tile_cnt
    cnt_ref[...] = cnt_s[...].astype(jnp.int32)


def _merge(ya, yb, proj, x, ga1, sc2, sh2, p_a, p_b, w_o, ln_g, ln_b, rw_t, r_bias, bb, tl, dn_alpha):
    b, l, d = x.shape
    nj = l // tl
    tm = bb * tl
    t_all = b * l
    full = lambda *shape: pl.BlockSpec(shape, lambda i, j: (0,) * len(shape))
    mod = pl.BlockSpec((bb, 1, d), lambda i, j: (i, 0, 0))
    tok = lambda width: pl.BlockSpec((bb, tl, width), lambda i, j: (i, j, 0))
    per_k = pl.BlockSpec((TOP_K, tm), lambda i, j: (0, i * nj + j))
    return pl.pallas_call(
        functools.partial(_merge_kernel, dn_alpha=dn_alpha),
        grid=(b // bb, nj),
        in_specs=[tok(V_A), tok(D_B),
                  pl.BlockSpec((bb, tl, 2 * d), lambda i, j: (i, j, P_MG // (2 * d))),
                  tok(d), mod, mod, mod,
                  full(V_A, d), full(D_B, d), full(d, d), full(1, d), full(1, d),
                  full(N_EXPERTS, d), full(N_EXPERTS, 1)],
        out_specs=[tok(d), tok(d), pl.BlockSpec((N_EXPERTS, tm), lambda i, j: (0, i * nj + j)),
                   pl.BlockSpec((tm, d // 2), lambda i, j: (i * nj + j, 0)),
                   per_k, per_k, per_k, full(N_EXPERTS, 1)],
        out_shape=[jax.ShapeDtypeStruct((b, l, d), F32),
                   jax.ShapeDtypeStruct((b, l, d), BF16),
                   jax.ShapeDtypeStruct((N_EXPERTS, t_all), F32),
                   jax.ShapeDtypeStruct((t_all, d // 2), jnp.int32),
                   jax.ShapeDtypeStruct((TOP_K, t_all), jnp.int32),
                   jax.ShapeDtypeStruct((TOP_K, t_all), jnp.int32),
                   jax.ShapeDtypeStruct((TOP_K, t_all), F32),
                   jax.ShapeDtypeStruct((N_EXPERTS, 1), jnp.int32)],
        scratch_shapes=[pltpu.VMEM((N_EXPERTS, 1), F32)],
        compiler_params=_cparams(("arbitrary", "arbitrary"), 48),
        name="merge",
    )(ya, yb, proj, x, ga1, sc2, sh2, p_a, p_b, w_o, ln_g, ln_b, rw_t, r_bias)


def _moe_kernel(h_ref, g_ref, x1_ref, ga2_ref, wg_ref, wu_ref, wd_ref, sgu_ref, sd_ref, g2_ref, b2_ref,
                o_ref, acc_ref, *, dn_alpha):
    e = pl.program_id(2)
    bb, tl, d = h_ref.shape
    tm = bb * tl
    h = h_ref[...].reshape(tm, d)

    @pl.when(e == 0)
    def _():
        su = jnp.dot(h, sgu_ref[...], preferred_element_type=F32)
        act = _silu(su[:, :D_SHARED]) * su[:, D_SHARED:]
        acc_ref[...] = _dot(act, sd_ref[...])

    lane = lax.broadcasted_iota(jnp.int32, (tm, N_EXPERTS), 1)
    gates = g_ref[...]
    acts = []
    for i in range(EXPERTS_PER_STEP):
        g = jnp.dot(h, wg_ref[i].astype(BF16), preferred_element_type=F32)
        u = jnp.dot(h, wu_ref[i].astype(BF16), preferred_element_type=F32)
        gate = _lane_pick(gates, lane, e * EXPERTS_PER_STEP + i)
        acts.append((_silu(g) * u * gate).astype(BF16))
    wd = wd_ref[...].reshape(EXPERTS_PER_STEP * D_EXPERT, d).astype(BF16)
    acc_ref[...] += jnp.dot(jnp.concatenate(acts, axis=1), wd, preferred_element_type=F32)

    @pl.when(e == pl.num_programs(2) - 1)
    def _():
        y = dn_alpha * x1_ref[...] + (1.0 + ga2_ref[...]) * acc_ref[...].reshape(bb, tl, d)
        o_ref[...] = _layernorm(y, g2_ref[...], b2_ref[...])


def _moe(h2, gates, x1, ga2, we_gate, we_up, we_down, ws_gu, ws_down, ln_g, ln_b, bb, tl, dn_alpha):
    b, l, d = x1.shape
    nj = l // tl
    tm = bb * tl
    full = lambda *shape: pl.BlockSpec(shape, lambda i, j, e: (0,) * len(shape))
    tok = pl.BlockSpec((bb, tl, d), lambda i, j, e: (i, j, 0))
    return pl.pallas_call(
        functools.partial(_moe_kernel, dn_alpha=dn_alpha),
        grid=(b // bb, nj, N_EXPERTS // EXPERTS_PER_STEP),
        in_specs=[tok,
                  pl.BlockSpec((tm, N_EXPERTS), lambda i, j, e: (i * nj + j, 0)),
                  tok,
                  pl.BlockSpec((bb, 1, d), lambda i, j, e: (i, 0, 0)),
                  pl.BlockSpec((EXPERTS_PER_STEP, d, D_EXPERT), lambda i, j, e: (e, 0, 0)),
                  pl.BlockSpec((EXPERTS_PER_STEP, d, D_EXPERT), lambda i, j, e: (e, 0, 0)),
                  pl.BlockSpec((EXPERTS_PER_STEP, D_EXPERT, d), lambda i, j, e: (e, 0, 0)),
                  full(d, 2 * D_SHARED), full(D_SHARED, d), full(1, d), full(1, d)],
        out_specs=tok,
        out_shape=jax.ShapeDtypeStruct((b, l, d), F32),
        scratch_shapes=[pltpu.VMEM((tm, d), F32)],
        compiler_params=_cparams(("parallel", "parallel", "arbitrary"), 56),
        name="moe",
    )(h2, gates, x1, ga2, we_gate, we_up, we_down, ws_gu, ws_down, ln_g, ln_b)


def _sc_mesh():
    return plsc.VectorSubcoreMesh(core_axis_name="c", subcore_axis_name="s")


def _sc_cores():
    info = plsc.get_sparse_core_info()
    assert info.num_cores * info.num_subcores == SC_WORKERS, info
    return info.num_cores


def _sc_scatter_rows(x, pos, n_out):
    t, w = x.shape
    k = pos.shape[0]
    t_per_w = t // SC_WORKERS
    n_cores = _sc_cores()

    @functools.partial(
        pl.kernel, mesh=_sc_mesh(),
        out_type=jax.ShapeDtypeStruct((n_out, w), jnp.int32),
        scratch_types=[pltpu.VMEM((k, SC_CHUNK), jnp.int32), pltpu.VMEM((SC_CHUNK, w), jnp.int32),
                       pltpu.SemaphoreType.DMA],
    )
    def scatter_kernel(x_hbm, pos_hbm, out_hbm, idx_v, rows_v, sem):
        base = (lax.axis_index("s") * n_cores + lax.axis_index("c")) * t_per_w

        @pl.loop(0, t_per_w // SC_CHUNK)
        def _(i):
            off = pl.multiple_of(base + i * SC_CHUNK, SC_CHUNK)
            pltpu.sync_copy(pos_hbm.at[:, pl.ds(off, SC_CHUNK)], idx_v)
            pltpu.sync_copy(x_hbm.at[pl.ds(off, SC_CHUNK)], rows_v)
            for j in range(k):
                pltpu.async_copy(rows_v, out_hbm.at[idx_v.at[j]], sem).wait()

    return scatter_kernel(x, pos)


def _sc_gather_rows(table, idx):
    n = idx.shape[0]
    w = table.shape[1]
    n_per_w = n // SC_WORKERS
    n_cores = _sc_cores()

    half = SC_CHUNK // 2

    @functools.partial(
        pl.kernel, mesh=_sc_mesh(),
        out_type=jax.ShapeDtypeStruct((n, w), jnp.int32),
        scratch_types=[pltpu.VMEM((2, half), jnp.int32), pltpu.VMEM((2, half, w), jnp.int32)]
                      + [pltpu.SemaphoreType.DMA] * 4,
    )
    def gather_kernel(table_hbm, idx_hbm, out_hbm, idx_v, rows_v, sem_g0, sem_g1, sem_w0, sem_w1):
        base = (lax.axis_index("s") * n_cores + lax.axis_index("c")) * n_per_w
        sem_g = (sem_g0, sem_g1)
        sem_w = (sem_w0, sem_w1)

        @pl.loop(0, n_per_w // SC_CHUNK)
        def _(i):
            off = pl.multiple_of(base + i * SC_CHUNK, SC_CHUNK)
            for h in range(2):
                pltpu.sync_copy(idx_hbm.at[pl.ds(off + h * half, half)], idx_v.at[h])
            gathers = [pltpu.async_copy(table_hbm.at[idx_v.at[h]], rows_v.at[h], sem_g[h]) for h in range(2)]
            writes = []
            for h in range(2):
                gathers[h].wait()
                writes.append(pltpu.async_copy(rows_v.at[h], out_hbm.at[pl.ds(off + h * half, half)], sem_w[h]))
            for c in writes:
                c.wait()

    return gather_kernel(table, idx)


def _experts_kernel(te_ref, used_ref, xs_ref, wg_ref, wu_ref, wd_ref, o_ref, wg_s, wu_s, wd_s):
    i = pl.program_id(0)

    @pl.when((i == 0) | (te_ref[i] != te_ref[jnp.maximum(i - 1, 0)]))
    def _():
        wg_s[...] = wg_ref[0].astype(BF16)
        wu_s[...] = wu_ref[0].astype(BF16)
        wd_s[...] = wd_ref[0].astype(BF16)

    @pl.when(i < used_ref[0])
    def _():
        x = _unpack_bf16_pairs(xs_ref[...]).astype(BF16)
        g = jnp.dot(x, wg_s[...], preferred_element_type=F32)
        u = jnp.dot(x, wu_s[...], preferred_element_type=F32)
        act = (_silu(g) * u).astype(BF16)
        o_ref[...] = _pack_bf16_pairs(jnp.dot(act, wd_s[...], preferred_element_type=F32))


def _experts(xs, tile_expert, tiles_used, we_gate, we_up, we_down):
    r, half = xs.shape
    d = 2 * half
    row_tile = lambda i, te, used: (jnp.minimum(i, used[0] - 1), 0)
    grid_spec = pltpu.PrefetchScalarGridSpec(
        num_scalar_prefetch=2,
        grid=(r // EXPERT_TILE,),
        in_specs=[pl.BlockSpec((EXPERT_TILE, half), row_tile),
                  pl.BlockSpec((1, d, D_EXPERT), lambda i, te, used: (te[i], 0, 0)),
                  pl.BlockSpec((1, d, D_EXPERT), lambda i, te, used: (te[i], 0, 0)),
                  pl.BlockSpec((1, D_EXPERT, d), lambda i, te, used: (te[i], 0, 0))],
        out_specs=pl.BlockSpec((EXPERT_TILE, half), row_tile),
        scratch_shapes=[pltpu.VMEM((d, D_EXPERT), BF16), pltpu.VMEM((d, D_EXPERT), BF16),
                        pltpu.VMEM((D_EXPERT, d), BF16)],
    )
    return pl.pallas_call(
        _experts_kernel,
        grid_spec=grid_spec,
        out_shape=jax.ShapeDtypeStruct((r, half), jnp.int32),
        compiler_params=_cparams(("arbitrary",), 32),
        name="experts",
    )(tile_expert, tiles_used, xs, we_gate, we_up, we_down)


def _combine_kernel(og_ref, wk_ref, h_ref, x1_ref, ga2_ref, sgu_ref, sd_ref, g2_ref, b2_ref, o_ref, *, dn_alpha):
    bb, tl, d = h_ref.shape
    tm = bb * tl
    h = h_ref[...].reshape(tm, d)
    su = jnp.dot(h, sgu_ref[...], preferred_element_type=F32)
    acc = _dot(_silu(su[:, :D_SHARED]) * su[:, D_SHARED:], sd_ref[...])
    lane = lax.broadcasted_iota(jnp.int32, (tm, TOP_K), 1)
    w_all = wk_ref[...]
    for k in range(TOP_K):
        acc = acc + _lane_pick(w_all, lane, k) * _unpack_bf16_pairs(og_ref[k])
    y = dn_alpha * x1_ref[...] + (1.0 + ga2_ref[...]) * acc.reshape(bb, tl, d)
    o_ref[...] = _layernorm(y, g2_ref[...], b2_ref[...])


def _combine(og, w_tk, h2, x1, ga2, ws_gu, ws_down, ln_g, ln_b, bb, tl, dn_alpha):
    b, l, d = x1.shape
    nj = l // tl
    tm = bb * tl
    full = lambda *shape: pl.BlockSpec(shape, lambda i, j: (0,) * len(shape))
    tok = pl.BlockSpec((bb, tl, d), lambda i, j: (i, j, 0))
    return pl.pallas_call(
        functools.partial(_combine_kernel, dn_alpha=dn_alpha),
        grid=(b // bb, nj),
        in_specs=[pl.BlockSpec((TOP_K, tm, d // 2), lambda i, j: (0, i * nj + j, 0)),
                  pl.BlockSpec((tm, TOP_K), lambda i, j: (i * nj + j, 0)),
                  tok, tok,
                  pl.BlockSpec((bb, 1, d), lambda i, j: (i, 0, 0)),
                  full(d, 2 * D_SHARED), full(D_SHARED, d), full(1, d), full(1, d)],
        out_specs=tok,
        out_shape=jax.ShapeDtypeStruct((b, l, d), F32),
        compiler_params=_cparams(("parallel", "parallel"), 48),
        name="combine",
    )(og, w_tk, h2, x1, ga2, ws_gu, ws_down, ln_g, ln_b)


def _moe_sparse(hp, eidx, rank, w_k, cnt, h2, x1, ga2, p, bb, dn_alpha):
    t_all = hp.shape[0]
    n_tiles = t_all * TOP_K // EXPERT_TILE + N_EXPERTS
    seg_tiles = (cnt[:, 0] + EXPERT_TILE - 1) // EXPERT_TILE
    seg_end = jnp.cumsum(seg_tiles)
    seg_start = (seg_end - seg_tiles) * EXPERT_TILE
    experts = jnp.arange(N_EXPERTS, dtype=jnp.int32)
    pos = rank + jnp.sum(jnp.where(eidx[None] == experts[:, None, None], seg_start[:, None, None], 0), axis=0)
    tiles = jnp.arange(n_tiles, dtype=jnp.int32)
    tile_expert = jnp.minimum(jnp.sum((seg_end[None, :] <= tiles[:, None]).astype(jnp.int32), axis=1),
                              N_EXPERTS - 1)
    xs = _sc_scatter_rows(hp, pos, n_tiles * EXPERT_TILE)
    outs = _experts(xs, tile_expert, seg_end[-1:].astype(jnp.int32), p['we_gate'], p['we_up'], p['we_down'])
    og = _sc_gather_rows(outs, pos.reshape(-1)).reshape(TOP_K, t_all, hp.shape[1])
    return _combine(og, w_k.T, h2, x1, ga2, p['ws_gu'], p['ws_down'], p['ln2_g'], p['ln2_b'], bb, COMBINE_TL, dn_alpha)


def _pad_rows(buf, width):
    b, n, w = buf.shape
    return jnp.pad(buf.astype(F32), ((0, 0), (SUBLANES - n, 0), (0, width - w)))


def _to_pairs(s):
    b = s.shape[0]
    s = s.astype(F32).reshape(b, PAIRS, 2, N_B, N_B)
    zero = jnp.zeros_like(s[:, :, 0])
    top = jnp.concatenate([s[:, :, 0], zero], axis=-1)
    bot = jnp.concatenate([zero, s[:, :, 1]], axis=-1)
    return jnp.concatenate([top, bot], axis=-2)


def _from_pairs(sp):
    b = sp.shape[0]
    return jnp.stack([sp[:, :, :N_B, :N_B], sp[:, :, N_B:, N_B:]], axis=2).reshape(b, H_B, N_B, N_B)


def _layer(x, mod, conv_buf, gdn_s, shift_buf, rwkv_s, p, tiles, dn_alpha):
    b, l, d = x.shape
    sh1, sc1, ga1, sh2, sc2, ga2 = mod
    proj = _inproj(x, sc1, sh1, p['w_in'], tiles['bb'], tiles['tl'])
    ya, gdn_new = _gdn(proj, _pad_rows(conv_buf, QKV_A), gdn_s.astype(F32), p['conv_w'], p['alog_row'],
                       p['dt_row'], p['gdn_norm_w'], tiles['bb_gdn'], tiles['tl_gdn'], l)
    yb, rwkv_pairs = _rwkv(proj, _pad_rows(shift_buf, RW_BLOCK), _to_pairs(rwkv_s), p['rwkv_vecs'],
                           p['w2p'], p['a2p'], p['g2'], tiles['bb_rwkv'], tiles['tl_rwkv'], l)

    x1, h2, gates_t, hp, eidx, rank, w_k, cnt = _merge(
        ya, yb, proj, x, ga1, sc2, sh2, p['p_a'], p['p_b'], p['w_o'], p['ln1_g'], p['ln1_b'],
        p['router_wt'], p['router_bias'], tiles['bb'], tiles['tl_merge'], dn_alpha)
    if (b * l) % (SC_WORKERS * SC_CHUNK) == 0:
        out = _moe_sparse(hp, eidx, rank, w_k, cnt, h2, x1, ga2, p, tiles['bb'], dn_alpha)
    else:
        out = _moe(h2, gates_t.T, x1, ga2, p['we_gate'], p['we_up'], p['we_down'], p['ws_gu'], p['ws_down'],
                   p['ln2_g'], p['ln2_b'], tiles['bb'], tiles['tl'], dn_alpha)

    pre = jnp.concatenate([conv_buf.astype(F32), proj[:, :, P_QKV:P_QKV + QKV_A]], axis=1)
    conv_new = pre[:, -(CONV_W - 1):]
    shift_new = proj[:, l - 1:l, P_RW:P_RW + SHIFT_W]
    return (out, conv_new.astype(conv_buf.dtype), gdn_new.astype(gdn_s.dtype),
            shift_new.astype(shift_buf.dtype), _from_pairs(rwkv_pairs).astype(rwkv_s.dtype))


def _prep_params(l, w_in, conv_w, a_log, dt_bias, gdn_norm_w, mu_shift, w0, w2, a0, a2, g2, k_k, k_a, r_k,
                 lnx_g, lnx_b, p_a, p_b, w_o, ln1_g, ln1_b, router_w, router_bias, we_gate, we_up, we_down,
                 ws_gate, ws_up, ws_down, ln2_g, ln2_b):
    d = D_MODEL
    w = w_in[l]
    w_p = jnp.concatenate(
        [w[:, :QKV_A], w[:, OFF_Z:OFF_RWKV], w[:, OFF_RWKV:OFF_MERGE],
         jnp.zeros((d, RW_BLOCK - SHIFT_W), w.dtype), w[:, OFF_MERGE:], w[:, OFF_ALPHA:OFF_Z],
         jnp.zeros((d, N_PROJ - P_AB - 2 * H_A), w.dtype)], axis=1).astype(BF16)
    row = lambda v, width: jnp.pad(v.astype(F32).reshape(1, -1), ((0, 0), (0, width - v.size)))
    zeros_lora = jnp.zeros((W_LORA, D_B), F32)
    return {
        'w_in': w_p,
        'conv_w': conv_w[l].astype(F32),
        'alog_row': row(a_log[l], LANES),
        'dt_row': row(dt_bias[l], LANES),
        'gdn_norm_w': row(gdn_norm_w[l], DK_A),
        'rwkv_vecs': (row(mu_shift[l], RW_BLOCK), row(w0[l], D_B), row(a0[l], D_B), row(k_k[l], D_B),
                      row(k_a[l], D_B), row(r_k[l], D_B), row(lnx_g[l], D_B), row(lnx_b[l], D_B)),
        'w2p': jnp.concatenate([w2[l].astype(F32), zeros_lora], axis=0),
        'a2p': jnp.concatenate([zeros_lora, a2[l].astype(F32)], axis=0),
        'g2': g2[l].astype(F32),
        'p_a': p_a[l].astype(BF16), 'p_b': p_b[l].astype(BF16), 'w_o': w_o[l].astype(BF16),
        'ln1_g': row(ln1_g[l], d), 'ln1_b': row(ln1_b[l], d),
        'router_wt': router_w[l].T.astype(BF16),
        'router_bias': router_bias[l].astype(F32).reshape(N_EXPERTS, 1),
        'we_gate': we_gate[l], 'we_up': we_up[l], 'we_down': we_down[l],
        'ws_gu': jnp.concatenate([ws_gate[l], ws_up[l]], axis=-1).astype(BF16),
        'ws_down': ws_down[l].astype(BF16),
        'ln2_g': row(ln2_g[l], d), 'ln2_b': row(ln2_b[l], d),
    }


def _tiles(b, l):
    lp = -(-l // CHUNK) * CHUNK
    if l >= TOKEN_TILE:
        bb, tl, tl_merge = 1, TOKEN_TILE, TOKEN_TILE // 2
    else:
        bb, tl, tl_merge = b, l, l
    return {'bb': bb, 'tl': tl, 'tl_merge': tl_merge,
            'bb_gdn': 4, 'tl_gdn': CHUNK, 'bb_rwkv': 2, 'tl_rwkv': min(2 * CHUNK, lp)}


def kernel(x_prompt, x_sample, c_prompt, c_sample, state_gdn_conv, state_gdn, state_rwkv_shift, state_rwkv, w_ada, b_ada, w_in, conv_w, a_log, dt_bias, gdn_norm_w, mu_shift, w0, w2, a0, a2, g2, k_k, k_a, r_k, lnx_g, lnx_b, p_a, p_b, w_o, ln1_g, ln1_b, router_w, router_bias, we_gate, we_up, we_down, ws_gate, ws_up, ws_down, ln2_g, ln2_b):
    depth = w_ada.shape[0]
    dn_alpha = (2 * depth) ** 0.25
    bp, lp_, d = x_prompt.shape
    bs, ls, _ = x_sample.shape
    dtp = x_prompt.dtype
    tiles_p = _tiles(bp, lp_)
    tiles_s = _tiles(bs, ls)

    yp, ys = x_prompt, x_sample
    new_p = ([], [], [], [])
    new_s = ([], [], [], [])
    for l in range(depth):
        p = _prep_params(l, w_in, conv_w, a_log, dt_bias, gdn_norm_w, mu_shift, w0, w2, a0, a2, g2, k_k, k_a,
                         r_k, lnx_g, lnx_b, p_a, p_b, w_o, ln1_g, ln1_b, router_w, router_bias, we_gate, we_up,
                         we_down, ws_gate, ws_up, ws_down, ln2_g, ln2_b)
        mod = _ada(jnp.concatenate([c_prompt, c_sample], axis=0), w_ada[l], b_ada[l])
        mod_p = tuple(m[:, None, :] for m in jnp.split(mod[:bp], 6, axis=-1))
        mod_s = tuple(m[:, None, :] for m in jnp.split(mod[bp:], 6, axis=-1))
        yp, *sp = _layer(yp, mod_p,
                         jnp.zeros((bp, CONV_W - 1, QKV_A), dtp), jnp.zeros((bp, H_A, DK_A, DK_A), dtp),
                         jnp.zeros((bp, 1, SHIFT_W), dtp), jnp.zeros((bp, H_B, N_B, N_B), dtp),
                         p, tiles_p, dn_alpha)
        ys, *ss = _layer(ys, mod_s, state_gdn_conv[l], state_gdn[l], state_rwkv_shift[l], state_rwkv[l],
                         p, tiles_s, dn_alpha)
        for lst, val in zip(new_p, sp):
            lst.append(val)
        for lst, val in zip(new_s, ss):
            lst.append(val)
    conv_p, gdn_p, shift_p, rwkv_p = [jnp.stack(t, 0) for t in new_p]
    conv_s, gdn_s, shift_s, rwkv_s = [jnp.stack(t, 0) for t in new_s]
    return (yp, ys, conv_p, gdn_p, shift_p, rwkv_p, conv_s, gdn_s, shift_s, rwkv_s)
```

```python
import functools
import math

import jax
import jax.numpy as jnp
from jax import lax
from jax.experimental import pallas as pl
from jax.experimental.pallas import tpu as pltpu
from jax.experimental.pallas import tpu_sc as plsc

F32 = jnp.float32
BF16 = jnp.bfloat16

D_MODEL = 1024
DK_A = 128
H_A = 4
QK_A = H_A * DK_A
V_A = H_A * DK_A
QKV_A = 2 * QK_A + V_A
CONV_W = 4
N_B = 64
H_B = 8
D_B = H_B * N_B
W_LORA = 64
A_LORA = 64
G_LORA = 128
SHIFT_W = 3 * D_B + W_LORA + A_LORA + G_LORA
OFF_ALPHA = QKV_A
OFF_BETA = OFF_ALPHA + H_A
OFF_Z = OFF_BETA + H_A
OFF_RWKV = OFF_Z + V_A
OFF_MERGE = OFF_RWKV + SHIFT_W
N_EXPERTS = 64
TOP_K = 8
N_GROUPS = 8
GROUP_SIZE = N_EXPERTS // N_GROUPS
TOPK_GROUPS = 4
D_EXPERT = 256
D_SHARED = 256
ROUTED_SCALE = 2.5
LN_EPS = 1e-5
GN_EPS = 64e-5
RMS_EPS = 1e-6
DECAY_SCALE = -math.exp(-0.5)

SUBLANES = 8
LANES = 128

P_QKV = 0
P_Z = QKV_A
P_RW = 2048
RW_BLOCK = 2048
P_MG = P_RW + RW_BLOCK
P_AB = P_MG + 2 * D_MODEL
AB_BLOCK = LANES
PROJ_TN = 1280
N_PROJ = 5 * PROJ_TN

CHUNK = 128
BASE_BLOCK = 8
CUMSUM_PARTS = 3
STAT_PARTS = 2
EXPERTS_PER_STEP = 2
HIGH_HALF = -65536

EXPERT_TILE = 1024
SC_WORKERS = 32
SC_CHUNK = 128
COMBINE_TL = 512
GATHER_PARTS = 2
TOKEN_TILE = 1024


def _cparams(sem, vmem_mb):
    return pltpu.CompilerParams(dimension_semantics=sem, vmem_limit_bytes=vmem_mb * 1024 * 1024)


def _dot(a, b):
    return jnp.dot(a.astype(BF16), b.astype(BF16), preferred_element_type=F32)


def _rdot(a, b, dims=((1,), (0,))):
    return lax.dot_general(a.astype(BF16), b.astype(BF16), (dims, ((), ())), preferred_element_type=F32)


_NT = ((1,), (1,))
_TN = ((0,), (0,))


def _bf16_parts(x, parts):
    out = []
    rem = x
    for _ in range(parts):
        hi = rem.astype(BF16)
        out.append(hi)
        rem = rem - hi.astype(F32)
    return out


def _mask_dot_left(mask, x, parts):
    return sum(jnp.dot(mask, p, preferred_element_type=F32) for p in _bf16_parts(x, parts))


def _mask_dot_right(x, mask, parts):
    return sum(jnp.dot(p, mask, preferred_element_type=F32) for p in _bf16_parts(x, parts))


def _lane_pick(x, lane_iota, lane):
    return jnp.sum(jnp.where(lane_iota == lane, x, 0.0), axis=-1, keepdims=True)


def _pack_bf16_pairs(x):
    n = x.shape[1] // 2
    bits = lax.bitcast_convert_type(x.astype(BF16).astype(F32), jnp.int32)
    return (bits[:, :n] & HIGH_HALF) | lax.shift_right_logical(bits[:, n:], 16)


def _unpack_bf16_pairs(p):
    hi = lax.bitcast_convert_type(p & HIGH_HALF, F32)
    lo = lax.bitcast_convert_type(lax.shift_left(p, 16), F32)
    return jnp.concatenate([hi, lo], axis=1)


def _silu(x):
    return x * jax.nn.sigmoid(x)


def _softplus(x):
    return jnp.maximum(x, 0.0) + jnp.log1p(jnp.exp(-jnp.abs(x)))


def _tri_inverse(mats, eye, sign, row, col):
    base = (row // BASE_BLOCK) == (col // BASE_BLOCK)
    ds = [jnp.where(base, a, 0.0) for a in mats]
    xs = [eye + d if sign > 0 else eye - d for d in ds]
    power = 2
    while power < BASE_BLOCK:
        ds = [_rdot(d, d) for d in ds]
        xs = [x + _rdot(x, d) for x, d in zip(xs, ds)]
        power *= 2
    b = BASE_BLOCK
    while b < CHUNK:
        sibling = ((row // b) == (col // b) + 1) & ((row // (2 * b)) == (col // (2 * b)))
        offs = [jnp.where(sibling, a, 0.0) for a in mats]
        ts = [_rdot(o, x) for o, x in zip(offs, xs)]
        if sign > 0:
            xs = [x + _rdot(x, t_) for x, t_ in zip(xs, ts)]
        else:
            xs = [x - _rdot(x, t_) for x, t_ in zip(xs, ts)]
        b *= 2
    return xs


def _tri_consts():
    row = lax.broadcasted_iota(jnp.int32, (CHUNK, CHUNK), 0)
    col = lax.broadcasted_iota(jnp.int32, (CHUNK, CHUNK), 1)
    incl = row >= col
    strict = row > col
    eye = jnp.where(row == col, 1.0, 0.0).astype(F32)
    tril = jnp.where(incl, 1.0, 0.0).astype(BF16)
    return row, col, incl, strict, eye, tril


def _ada_kernel(c_ref, w_ref, b_ref, o_ref):
    o_ref[...] = _dot(_silu(c_ref[...]), w_ref[...]) + b_ref[...]


def _ada(c, w_ada, b_ada):
    n, d = c.shape
    nout = w_ada.shape[1]
    tn = 768
    return pl.pallas_call(
        _ada_kernel,
        grid=(nout // tn,),
        in_specs=[pl.BlockSpec((n, d), lambda j: (0, 0)),
                  pl.BlockSpec((d, tn), lambda j: (0, j)),
                  pl.BlockSpec((1, tn), lambda j: (0, j))],
        out_specs=pl.BlockSpec((n, tn), lambda j: (0, j)),
        out_shape=jax.ShapeDtypeStruct((n, nout), F32),
        compiler_params=_cparams(("parallel",), 32),
        name="ada",
    )(c, w_ada, b_ada.reshape(1, nout))


def _inproj_kernel(x_ref, sc_ref, sh_ref, w_ref, o_ref, hs_ref):
    @pl.when(pl.program_id(2) == 0)
    def _():
        h = x_ref[...] * (1.0 + sc_ref[...]) + sh_ref[...]
        hs_ref[...] = h.reshape(hs_ref.shape).astype(BF16)

    o = jnp.dot(hs_ref[...], w_ref[...], preferred_element_type=F32)
    o_ref[...] = o.reshape(o_ref.shape)


def _inproj(x, sc, sh, w_p, bb, tl):
    b, l, d = x.shape
    return pl.pallas_call(
        _inproj_kernel,
        grid=(b // bb, l // tl, N_PROJ // PROJ_TN),
        in_specs=[pl.BlockSpec((bb, tl, d), lambda i, j, n: (i, j, 0)),
                  pl.BlockSpec((bb, 1, d), lambda i, j, n: (i, 0, 0)),
                  pl.BlockSpec((bb, 1, d), lambda i, j, n: (i, 0, 0)),
                  pl.BlockSpec((d, PROJ_TN), lambda i, j, n: (0, n))],
        out_specs=pl.BlockSpec((bb, tl, PROJ_TN), lambda i, j, n: (i, j, n)),
        out_shape=jax.ShapeDtypeStruct((b, l, N_PROJ), F32),
        scratch_shapes=[pltpu.VMEM((bb * tl, d), BF16)],
        compiler_params=_cparams(("parallel", "parallel", "arbitrary"), 48),
        name="inproj",
    )(x, sc, sh, w_p)


def _gdn_kernel(qkv_ref, ab_ref, z_ref, cbuf_ref, s0_ref, cw_ref, alog_ref, dt_ref, nw_ref,
                ya_ref, sout_ref, ext_ref, qc_ref, s_ref, *pad_s, bb, tl, l_in, l_valid, l_total):
    t = pl.program_id(1)

    @pl.when(t == 0)
    def _():
        ext_ref[:, 0:SUBLANES, :] = cbuf_ref[...]
        s_ref[...] = s0_ref[...]

    if l_in < tl:
        for src, dst in zip((qkv_ref, ab_ref, z_ref), pad_s):
            dst[:, l_in:, :] = jnp.zeros((bb, tl - l_in, dst.shape[2]), F32)
            dst[:, 0:l_in, :] = src[...]
        qkv_ref, ab_ref, z_ref = pad_s

    for bi in range(bb):
        cur = qkv_ref[bi]
        ext_ref[bi, SUBLANES:SUBLANES + tl, :] = cur
        window = ext_ref[bi]
        acc = cw_ref[CONV_W - 1:CONV_W, :] * cur
        for j in range(CONV_W - 1):
            back = CONV_W - 1 - j
            acc = acc + cw_ref[j:j + 1, :] * pltpu.roll(window, tl + back, 0)[0:tl]
        ext_ref[bi, 0:SUBLANES, :] = ext_ref[bi, tl:tl + SUBLANES, :]
        qc_ref[bi] = _silu(acc)

    row, col, incl, strict, eye, tril = _tri_consts()
    neg_a = -jnp.exp(alog_ref[...])
    dt = dt_ref[...]
    nw = nw_ref[...]
    chains = [(bi, h) for bi in range(bb) for h in range(H_A)]
    n = range(len(chains))

    def chunk(c, carry):
        r0 = pl.multiple_of(c * CHUNK, CHUNK)
        rows = pl.ds(r0, CHUNK)
        g_cum = []
        b_all = []
        for bi in range(bb):
            ab = ab_ref[bi, rows, :]
            g_bi = neg_a * _softplus(ab + dt)
            b_bi = jax.nn.sigmoid(ab)
            if l_valid < l_total:
                valid = (t * tl + r0 + row) < l_valid
                g_bi = jnp.where(valid, g_bi, 0.0)
                b_bi = jnp.where(valid, b_bi, 0.0)
            g_cum.append(_mask_dot_left(tril, g_bi, CUMSUM_PARTS))
            b_all.append(b_bi)
        g_col = [_lane_pick(g_cum[bi], col, h) for bi, h in chains]
        beta = [_lane_pick(b_all[bi], col, H_A + h) for bi, h in chains]
        decay = []
        for i in n:
            g_b = jnp.broadcast_to(g_col[i], (CHUNK, CHUNK))
            decay.append(jnp.where(incl, jnp.exp(jnp.where(incl, g_b - g_b.T, 0.0)), 0.0))
        q = [qc_ref[bi, rows, h * DK_A:(h + 1) * DK_A] for bi, h in chains]
        k = [qc_ref[bi, rows, QK_A + h * DK_A:QK_A + (h + 1) * DK_A] for bi, h in chains]
        v = [qc_ref[bi, rows, 2 * QK_A + h * DK_A:2 * QK_A + (h + 1) * DK_A] for bi, h in chains]
        q = [x * lax.rsqrt(jnp.sum(x * x, -1, keepdims=True) + 1e-6) * (DK_A ** -0.5) for x in q]
        k = [x * lax.rsqrt(jnp.sum(x * x, -1, keepdims=True) + 1e-6) for x in k]
        kq = [_rdot(jnp.concatenate([k[i], q[i]], axis=0), k[i], _NT) for i in n]
        a = [jnp.where(strict, beta[i] * kq[i][:CHUNK] * decay[i], 0.0) for i in n]
        qk = [jnp.where(incl, kq[i][CHUNK:] * decay[i], 0.0) for i in n]
        x = _tri_inverse(a, eye, -1, row, col)
        e_g = [jnp.exp(g) for g in g_col]
        uw = [_rdot(x[i], jnp.concatenate([v[i] * beta[i], k[i] * (beta[i] * e_g[i])], axis=1)) for i in n]
        g_last = [g[CHUNK - 1:CHUNK, :] for g in g_col]
        kd = [k[i] * jnp.exp(g_last[i] - g_col[i]) for i in n]
        s = [s_ref[bi, h] for bi, h in chains]
        ws = [_rdot(jnp.concatenate([uw[i][:, DK_A:], q[i] * e_g[i]], axis=0), s[i]) for i in n]
        v_new = [uw[i][:, :DK_A] - ws[i][:CHUNK] for i in n]
        o = [ws[i][CHUNK:] + _rdot(qk[i], v_new[i]) for i in n]
        s_new = [s[i] * jnp.exp(g_last[i]) + _rdot(kd[i], v_new[i], _TN) for i in n]
        for i, (bi, h) in enumerate(chains):
            s_ref[bi, h] = s_new[i]
            z = z_ref[bi, rows, h * DK_A:(h + 1) * DK_A]
            on = o[i] * lax.rsqrt(jnp.mean(o[i] * o[i], -1, keepdims=True) + RMS_EPS) * nw * _silu(z)
            if l_in < tl:
                ya_ref[bi, :, h * DK_A:(h + 1) * DK_A] = on[0:l_in].astype(BF16)
            else:
                ya_ref[bi, rows, h * DK_A:(h + 1) * DK_A] = on.astype(BF16)
        return carry

    lax.fori_loop(0, tl // CHUNK, chunk, 0)

    @pl.when(t == pl.num_programs(1) - 1)
    def _():
        sout_ref[...] = s_ref[...]


def _rec_rows(l, tl):
    if l % tl == 0:
        return l, tl
    assert l < tl == CHUNK, (l, tl)
    return tl, l


def _gdn(proj, conv_buf8, s0, conv_w, alog_row, dt_row, norm_w, bb, tl, l_valid):
    b, l, _ = proj.shape
    lp, l_in = _rec_rows(l, tl)
    kern = functools.partial(_gdn_kernel, bb=bb, tl=tl, l_in=l_in, l_valid=l_valid, l_total=lp)
    full = lambda *shape: pl.BlockSpec(shape, lambda i, t: (0,) * len(shape))
    pad_scratch = [pltpu.VMEM((bb, tl, w), F32) for w in (QKV_A, AB_BLOCK, V_A)] if l_in < tl else []
    return pl.pallas_call(
        kern,
        grid=(b // bb, lp // tl),
        in_specs=[pl.BlockSpec((bb, l_in, QKV_A), lambda i, t: (i, t, P_QKV // QKV_A)),
                  pl.BlockSpec((bb, l_in, AB_BLOCK), lambda i, t: (i, t, P_AB // AB_BLOCK)),
                  pl.BlockSpec((bb, l_in, V_A), lambda i, t: (i, t, P_Z // V_A)),
                  pl.BlockSpec((bb, SUBLANES, QKV_A), lambda i, t: (i, 0, 0)),
                  pl.BlockSpec((bb, H_A, DK_A, DK_A), lambda i, t: (i, 0, 0, 0)),
                  full(CONV_W, QKV_A), full(1, LANES), full(1, LANES), full(1, DK_A)],
        out_specs=[pl.BlockSpec((bb, l_in, V_A), lambda i, t: (i, t, 0)),
                   pl.BlockSpec((bb, H_A, DK_A, DK_A), lambda i, t: (i, 0, 0, 0))],
        out_shape=[jax.ShapeDtypeStruct((b, l, V_A), BF16),
                   jax.ShapeDtypeStruct((b, H_A, DK_A, DK_A), F32)],
        scratch_shapes=[pltpu.VMEM((bb, tl + SUBLANES, QKV_A), F32),
                        pltpu.VMEM((bb, tl, QKV_A), F32),
                        pltpu.VMEM((bb, H_A, DK_A, DK_A), F32)] + pad_scratch,
        compiler_params=_cparams(("parallel", "arbitrary"), 48),
        name="gdn",
    )(proj, proj, proj, conv_buf8, s0, conv_w, alog_row, dt_row, norm_w)


PAIRS = H_B // 2


def _headsum(x, bd):
    return jnp.concatenate(
        [_mask_dot_right(x[:, p * LANES:(p + 1) * LANES], bd, STAT_PARTS) for p in range(PAIRS)], axis=1)


def _rwkv_kernel(rw_ref, sbuf_ref, s0_ref, mu_ref, w0_ref, a0_ref, kk_ref, ka_ref, rk_ref, lg_ref, lb_ref,
                 w2_ref, a2_ref, g2_ref, yb_ref, sout_ref,
                 ext_ref, r_s, k_s, v_s, z_s, p_s, lw_s, y_s, bonus_s, gate_s, s_ref, *,
                 bb, tl, l_in, l_valid, l_total):
    t = pl.program_id(1)

    @pl.when(t == 0)
    def _():
        ext_ref[:, 0:SUBLANES, :] = sbuf_ref[...]
        s_ref[...] = s0_ref[...]

    row, col, incl, strict, eye, tril = _tri_consts()
    same_head = (row // N_B) == (col // N_B)
    bd = jnp.where(same_head, 1.0, 0.0).astype(BF16)
    lane_a = col < N_B

    for bi in range(bb):
        if l_in < tl:
            ext_ref[bi, SUBLANES + l_in:SUBLANES + tl, :] = jnp.zeros((tl - l_in, RW_BLOCK), F32)
            ext_ref[bi, SUBLANES:SUBLANES + l_in, :] = rw_ref[bi]
            cur = ext_ref[bi, SUBLANES:SUBLANES + tl, :]
        else:
            cur = rw_ref[bi]
            ext_ref[bi, SUBLANES:SUBLANES + tl, :] = cur
        prev = pltpu.roll(ext_ref[bi], tl + 1, 0)[0:tl]
        mixed = cur + (prev - cur) * mu_ref[...]
        ext_ref[bi, 0:SUBLANES, :] = ext_ref[bi, tl:tl + SUBLANES, :]
        r = mixed[:, 0:D_B]
        k = mixed[:, D_B:2 * D_B]
        v = mixed[:, 2 * D_B:3 * D_B]
        lora = mixed[:, 3 * D_B:3 * D_B + W_LORA + A_LORA]
        g_in = mixed[:, 3 * D_B + W_LORA + A_LORA:SHIFT_W]
        lw = DECAY_SCALE * jax.nn.sigmoid(w0_ref[...] + _dot(jnp.tanh(lora), w2_ref[...]))
        a = jax.nn.sigmoid(a0_ref[...] + _dot(lora, a2_ref[...]))
        kkr = k * kk_ref[...]
        kk = kkr * lax.rsqrt(_headsum(kkr * kkr, bd) + 1e-6)
        k = k * (1.0 + (a - 1.0) * ka_ref[...])
        tile_rows = pl.ds(bi * tl, tl)
        bonus_s[tile_rows, :] = _headsum(r * k * rk_ref[...], bd) * v
        gate_s[tile_rows, :] = _dot(jax.nn.sigmoid(g_in), g2_ref[...])
        if l_valid < l_total:
            rvalid = (t * tl + lax.broadcasted_iota(jnp.int32, (tl, 1), 0)) < l_valid
            lw = jnp.where(rvalid, lw, 0.0)
            kk = jnp.where(rvalid, kk, 0.0)
            k = jnp.where(rvalid, k, 0.0)
            v = jnp.where(rvalid, v, 0.0)
        r_s[tile_rows, :] = r
        k_s[tile_rows, :] = k
        v_s[tile_rows, :] = v
        z_s[tile_rows, :] = -kk
        p_s[tile_rows, :] = kk * a
        lw_s[tile_rows, :] = lw

    groups = [(bi, p) for bi in range(bb) for p in range(PAIRS)]
    pairs = range(len(groups))
    both = range(2 * len(groups))
    cols = [slice(p * LANES, (p + 1) * LANES) for _, p in groups]

    def chunk(c, carry):
        r0 = pl.multiple_of(c * CHUNK, CHUNK)
        rows = [pl.ds(bi * tl + r0, CHUNK) for bi, _ in groups]
        lw_c = [lw_s[rows[g], cols[g]] for g in pairs]
        g_inc = [_mask_dot_left(tril, x, CUMSUM_PARTS) for x in lw_c]
        g_exc = [g_inc[p] - lw_c[p] for p in pairs]
        g_mid = [g[CHUNK // 2 - 1:CHUNK // 2, :] for g in g_inc]
        g_end = [g[CHUNK - 1:CHUNK, :] for g in g_inc]
        z = [z_s[rows[g], cols[g]] for g in pairs]
        rr = [r_s[rows[g], cols[g]] for g in pairs]
        pp = [p_s[rows[g], cols[g]] for g in pairs]
        kk_ = [k_s[rows[g], cols[g]] for g in pairs]
        vv = [v_s[rows[g], cols[g]] for g in pairs]
        zt = [z[p] * jnp.exp(g_exc[p] - g_mid[p]) for p in pairs]
        rt = [rr[p] * jnp.exp(g_inc[p] - g_mid[p]) for p in pairs]
        en = [jnp.exp(g_mid[p] - g_inc[p]) for p in pairs]
        s = [s_ref[bi, p] for bi, p in groups]
        lhs = [jnp.concatenate([jnp.where(lane_a, zt[p], 0.0), jnp.where(lane_a, 0.0, zt[p]),
                                jnp.where(lane_a, rt[p], 0.0), jnp.where(lane_a, 0.0, rt[p])], axis=0) for p in pairs]
        m = [_rdot(lhs[p], jnp.concatenate([pp[p] * en[p], kk_[p] * en[p]], axis=0), _NT) for p in pairs]
        zr0 = [_rdot(jnp.concatenate([z[p] * jnp.exp(g_exc[p]), rr[p] * jnp.exp(g_inc[p])], axis=0), s[p], _NT)
               for p in pairs]
        mz = [m[i // 2][(i % 2) * CHUNK:(i % 2 + 1) * CHUNK] for i in both]
        azp = [jnp.where(strict, x[:, :CHUNK], 0.0) for x in mz]
        azk = [jnp.where(strict, x[:, CHUNK:], 0.0) for x in mz]
        minv = _tri_inverse(azp, eye, 1, row, col)
        rhs = [zr0[i // 2][:CHUNK] + _rdot(azk[i], vv[i // 2]) for i in both]
        u_h = [_rdot(minv[i], rhs[i]) for i in both]
        u = [jnp.where(lane_a, u_h[2 * p], u_h[2 * p + 1]) for p in pairs]
        uv = [jnp.concatenate([u[p], vv[p]], axis=0) for p in pairs]
        incl2 = jnp.concatenate([incl, incl], axis=1)
        y_h = [_rdot(jnp.where(incl2, m[i // 2][(2 + i % 2) * CHUNK:(3 + i % 2) * CHUNK], 0.0), uv[i // 2])
               for i in both]
        tail = [jnp.exp(g_end[p] - g_inc[p]) for p in pairs]
        s_new = [s[p] * jnp.exp(g_end[p])
                 + _rdot(uv[p], jnp.concatenate([pp[p] * tail[p], kk_[p] * tail[p]], axis=0), _TN) for p in pairs]
        for g, (bi, p) in enumerate(groups):
            s_ref[bi, p] = jnp.where(same_head, s_new[g], 0.0)
            y_s[rows[g], cols[g]] = zr0[g][CHUNK:] + jnp.where(lane_a, y_h[2 * g], y_h[2 * g + 1])
        return carry

    lax.fori_loop(0, tl // CHUNK, chunk, 0)

    for bi in range(bb):
        tile_rows = pl.ds(bi * tl, tl)
        y = y_s[tile_rows, :]
        mean = _headsum(y, bd) * (1.0 / N_B)
        dev = y - mean
        var = _headsum(dev * dev, bd) * (1.0 / N_B)
        yn = dev * lax.rsqrt(var + GN_EPS) * lg_ref[...] + lb_ref[...]
        yb_ref[bi] = ((yn + bonus_s[tile_rows, :]) * gate_s[tile_rows, :])[0:l_in].astype(BF16)

    @pl.when(t == pl.num_programs(1) - 1)
    def _():
        sout_ref[...] = s_ref[...]


def _rwkv(proj, shift_buf8, s0_pairs, vecs, w2p, a2p, g2, bb, tl, l_valid):
    b, l, _ = proj.shape
    lp, l_in = _rec_rows(l, tl)
    kern = functools.partial(_rwkv_kernel, bb=bb, tl=tl, l_in=l_in, l_valid=l_valid, l_total=lp)
    full = lambda *shape: pl.BlockSpec(shape, lambda i, t: (0,) * len(shape))
    mu, w0, a0, k_k, k_a, r_k, lnx_g, lnx_b = vecs
    return pl.pallas_call(
        kern,
        grid=(b // bb, lp // tl),
        in_specs=[pl.BlockSpec((bb, l_in, RW_BLOCK), lambda i, t: (i, t, P_RW // RW_BLOCK)),
                  pl.BlockSpec((bb, SUBLANES, RW_BLOCK), lambda i, t: (i, 0, 0)),
                  pl.BlockSpec((bb, PAIRS, LANES, LANES), lambda i, t: (i, 0, 0, 0)),
                  full(1, RW_BLOCK)] + [full(1, D_B)] * 7 +
                 [full(W_LORA + A_LORA, D_B), full(W_LORA + A_LORA, D_B), full(G_LORA, D_B)],
        out_specs=[pl.BlockSpec((bb, l_in, D_B), lambda i, t: (i, t, 0)),
                   pl.BlockSpec((bb, PAIRS, LANES, LANES), lambda i, t: (i, 0, 0, 0))],
        out_shape=[jax.ShapeDtypeStruct((b, l, D_B), BF16),
                   jax.ShapeDtypeStruct((b, PAIRS, LANES, LANES), F32)],
        scratch_shapes=[pltpu.VMEM((bb, tl + SUBLANES, RW_BLOCK), F32)] +
                       [pltpu.VMEM((bb * tl, D_B), F32)] * 9 +
                       [pltpu.VMEM((bb, PAIRS, LANES, LANES), F32)],
        compiler_params=_cparams(("parallel", "arbitrary"), 48),
        name="rwkv",
    )(proj, shift_buf8, s0_pairs, mu, w0, a0, k_k, k_a, r_k, lnx_g, lnx_b, w2p, a2p, g2)


def _layernorm(y, g, b):
    mu = jnp.mean(y, -1, keepdims=True)
    dev = y - mu
    var = jnp.mean(dev * dev, -1, keepdims=True)
    return dev * lax.rsqrt(var + LN_EPS) * g + b


def _route(logits_t, bias, base_cnt):
    tm = logits_t.shape[1]
    scores = jax.nn.sigmoid(logits_t)
    choice = scores + bias
    neg_inf = -jnp.inf
    iota_g = lax.broadcasted_iota(jnp.int32, (GROUP_SIZE, tm), 0)
    group_score = []
    for g in range(N_GROUPS):
        xg = choice[g * GROUP_SIZE:(g + 1) * GROUP_SIZE, :]
        m1 = jnp.max(xg, axis=0, keepdims=True)
        first = jnp.min(jnp.where(xg == m1, iota_g, GROUP_SIZE), axis=0, keepdims=True)
        m2 = jnp.max(jnp.where(iota_g == first, neg_inf, xg), axis=0, keepdims=True)
        group_score.append(m1 + m2)
    masked = []
    for g in range(N_GROUPS):
        rank = jnp.zeros((1, tm), jnp.int32)
        for o in range(N_GROUPS):
            if o == g:
                continue
            ahead = group_score[o] > group_score[g]
            if o < g:
                ahead = ahead | (group_score[o] == group_score[g])
            rank = rank + ahead.astype(jnp.int32)
        keep = rank < TOPK_GROUPS
        masked.append(jnp.where(keep, choice[g * GROUP_SIZE:(g + 1) * GROUP_SIZE, :], neg_inf))
    cur = jnp.concatenate(masked, axis=0)
    iota_e = lax.broadcasted_iota(jnp.int32, (N_EXPERTS, tm), 0)
    sel = jnp.zeros((N_EXPERTS, tm), F32)
    picks = []
    firsts = []
    for _ in range(TOP_K):
        m = jnp.max(cur, axis=0, keepdims=True)
        first = jnp.min(jnp.where(cur == m, iota_e, N_EXPERTS), axis=0, keepdims=True)
        pick = iota_e == first
        sel = jnp.where(pick, 1.0, sel)
        cur = jnp.where(pick, neg_inf, cur)
        picks.append(pick)
        firsts.append(first)
    wsel = sel * scores
    denom = jnp.sum(wsel, axis=0, keepdims=True) + 1e-20
    gates = (ROUTED_SCALE * wsel) / denom

    upto = (lax.broadcasted_iota(jnp.int32, (tm, tm), 0) <= lax.broadcasted_iota(jnp.int32, (tm, tm), 1))
    csum = jnp.dot(sel.astype(BF16), jnp.where(upto, 1.0, 0.0).astype(BF16), preferred_element_type=F32)
    before = base_cnt + csum - sel
    iota_k = lax.broadcasted_iota(jnp.int32, (TOP_K, tm), 0)
    eidx = jnp.zeros((TOP_K, tm), jnp.int32)
    rank = jnp.zeros((TOP_K, tm), F32)
    w_k = jnp.zeros((TOP_K, tm), F32)
    for i in range(TOP_K):
        eidx = jnp.where(iota_k == i, firsts[i], eidx)
        rank = jnp.where(iota_k == i, jnp.sum(jnp.where(picks[i], before, 0.0), axis=0, keepdims=True), rank)
        w_k = jnp.where(iota_k == i, jnp.sum(jnp.where(picks[i], gates, 0.0), axis=0, keepdims=True), w_k)
    return gates, eidx, rank.astype(jnp.int32), w_k, jnp.sum(sel, axis=1, keepdims=True)


def _merge_kernel(ya_ref, yb_ref, mg_ref, x_ref, ga1_ref, sc2_ref, sh2_ref, pa_ref, pb_ref, wo_ref,
                  g1_ref, b1_ref, rwt_ref, rb_ref, x1_ref, h2_ref, gt_ref, hp_ref, eidx_ref, rank_ref, wk_ref,
                  cnt_ref, cnt_s, *, dn_alpha):
    bb, tl, d = x_ref.shape
    tm = bb * tl

    @pl.when((pl.program_id(0) == 0) & (pl.program_id(1) == 0))
    def _():
        cnt_s[...] = jnp.zeros_like(cnt_s)

    ya = ya_ref[...].reshape(tm, V_A)
    yb = yb_ref[...].reshape(tm, D_B)
    mg = mg_ref[...].reshape(tm, 2 * d)
    merged = (jax.nn.sigmoid(mg[:, :d]) * jnp.dot(ya, pa_ref[...], preferred_element_type=F32)
              + jax.nn.sigmoid(mg[:, d:]) * jnp.dot(yb, pb_ref[...], preferred_element_type=F32))
    mix = _dot(merged, wo_ref[...])
    y = dn_alpha * x_ref[...] + (1.0 + ga1_ref[...]) * mix.reshape(bb, tl, d)
    x1 = _layernorm(y, g1_ref[...], b1_ref[...])
    x1_ref[...] = x1
    h2 = (x1 * (1.0 + sc2_ref[...]) + sh2_ref[...]).astype(BF16)
    h2_ref[...] = h2
    h2f = h2.reshape(tm, d)
    hp_ref[...] = _pack_bf16_pairs(h2f.astype(F32))
    logits_t = lax.dot_general(rwt_ref[...], h2f, (_NT, ((), ())), preferred_element_type=F32)
    gates, eidx, rank, w_k, tile_cnt = _route(logits_t, rb_ref[...], cnt_s[...])
    gt_ref[...] = gates
    eidx_ref[...] = eidx
    rank_ref[...] = rank
    wk_ref[...] = w_k
    cnt_s[...] = cnt_s[...] + tile_cnt
    cnt_ref[...] = cnt_s[...].astype(jnp.int32)


def _merge(ya, yb, proj, x, ga1, sc2, sh2, p_a, p_b, w_o, ln_g, ln_b, rw_t, r_bias, bb, tl, dn_alpha):
    b, l, d = x.shape
    nj = l // tl
    tm = bb * tl
    t_all = b * l
    full = lambda *shape: pl.BlockSpec(shape, lambda i, j: (0,) * len(shape))
    mod = pl.BlockSpec((bb, 1, d), lambda i, j: (i, 0, 0))
    tok = lambda width: pl.BlockSpec((bb, tl, width), lambda i, j: (i, j, 0))
    per_k = pl.BlockSpec((TOP_K, tm), lambda i, j: (0, i * nj + j))
    return pl.pallas_call(
        functools.partial(_merge_kernel, dn_alpha=dn_alpha),
        grid=(b // bb, nj),
        in_specs=[tok(V_A), tok(D_B),
                  pl.BlockSpec((bb, tl, 2 * d), lambda i, j: (i, j, P_MG // (2 * d))),
                  tok(d), mod, mod, mod,
                  full(V_A, d), full(D_B, d), full(d, d), full(1, d), full(1, d),
                  full(N_EXPERTS, d), full(N_EXPERTS, 1)],
        out_specs=[tok(d), tok(d), pl.BlockSpec((N_EXPERTS, tm), lambda i, j: (0, i * nj + j)),
                   pl.BlockSpec((tm, d // 2), lambda i, j: (i * nj + j, 0)),
                   per_k, per_k, per_k, full(N_EXPERTS, 1)],
        out_shape=[jax.ShapeDtypeStruct((b, l, d), F32),
                   jax.ShapeDtypeStruct((b, l, d), BF16),
                   jax.ShapeDtypeStruct((N_EXPERTS, t_all), F32),
                   jax.ShapeDtypeStruct((t_all, d // 2), jnp.int32),
                   jax.ShapeDtypeStruct((TOP_K, t_all), jnp.int32),
                   jax.ShapeDtypeStruct((TOP_K, t_all), jnp.int32),
                   jax.ShapeDtypeStruct((TOP_K, t_all), F32),
                   jax.ShapeDtypeStruct((N_EXPERTS, 1), jnp.int32)],
        scratch_shapes=[pltpu.VMEM((N_EXPERTS, 1), F32)],
        compiler_params=_cparams(("arbitrary", "arbitrary"), 48),
        name="merge",
    )(ya, yb, proj, x, ga1, sc2, sh2, p_a, p_b, w_o, ln_g, ln_b, rw_t, r_bias)


def _moe_kernel(h_ref, g_ref, x1_ref, ga2_ref, wg_ref, wu_ref, wd_ref, sgu_ref, sd_ref, g2_ref, b2_ref,
                o_ref, acc_ref, *, dn_alpha):
    e = pl.program_id(2)
    bb, tl, d = h_ref.shape
    tm = bb * tl
    h = h_ref[...].reshape(tm, d)

    @pl.when(e == 0)
    def _():
        su = jnp.dot(h, sgu_ref[...], preferred_element_type=F32)
        act = _silu(su[:, :D_SHARED]) * su[:, D_SHARED:]
        acc_ref[...] = _dot(act, sd_ref[...])

    lane = lax.broadcasted_iota(jnp.int32, (tm, N_EXPERTS), 1)
    gates = g_ref[...]
    acts = []
    for i in range(EXPERTS_PER_STEP):
        g = jnp.dot(h, wg_ref[i].astype(BF16), preferred_element_type=F32)
        u = jnp.dot(h, wu_ref[i].astype(BF16), preferred_element_type=F32)
        gate = _lane_pick(gates, lane, e * EXPERTS_PER_STEP + i)
        acts.append((_silu(g) * u * gate).astype(BF16))
    wd = wd_ref[...].reshape(EXPERTS_PER_STEP * D_EXPERT, d).astype(BF16)
    acc_ref[...] += jnp.dot(jnp.concatenate(acts, axis=1), wd, preferred_element_type=F32)

    @pl.when(e == pl.num_programs(2) - 1)
    def _():
        y = dn_alpha * x1_ref[...] + (1.0 + ga2_ref[...]) * acc_ref[...].reshape(bb, tl, d)
        o_ref[...] = _layernorm(y, g2_ref[...], b2_ref[...])


def _moe(h2, gates, x1, ga2, we_gate, we_up, we_down, ws_gu, ws_down, ln_g, ln_b, bb, tl, dn_alpha):
    b, l, d = x1.shape
    nj = l // tl
    tm = bb * tl
    full = lambda *shape: pl.BlockSpec(shape, lambda i, j, e: (0,) * len(shape))
    tok = pl.BlockSpec((bb, tl, d), lambda i, j, e: (i, j, 0))
    return pl.pallas_call(
        functools.partial(_moe_kernel, dn_alpha=dn_alpha),
        grid=(b // bb, nj, N_EXPERTS // EXPERTS_PER_STEP),
        in_specs=[tok,
                  pl.BlockSpec((tm, N_EXPERTS), lambda i, j, e: (i * nj + j, 0)),
                  tok,
                  pl.BlockSpec((bb, 1, d), lambda i, j, e: (i, 0, 0)),
                  pl.BlockSpec((EXPERTS_PER_STEP, d, D_EXPERT), lambda i, j, e: (e, 0, 0)),
                  pl.BlockSpec((EXPERTS_PER_STEP, d, D_EXPERT), lambda i, j, e: (e, 0, 0)),
                  pl.BlockSpec((EXPERTS_PER_STEP, D_EXPERT, d), lambda i, j, e: (e, 0, 0)),
                  full(d, 2 * D_SHARED), full(D_SHARED, d), full(1, d), full(1, d)],
        out_specs=tok,
        out_shape=jax.ShapeDtypeStruct((b, l, d), F32),
        scratch_shapes=[pltpu.VMEM((tm, d), F32)],
        compiler_params=_cparams(("parallel", "parallel", "arbitrary"), 56),
        name="moe",
    )(h2, gates, x1, ga2, we_gate, we_up, we_down, ws_gu, ws_down, ln_g, ln_b)


def _sc_mesh():
    return plsc.VectorSubcoreMesh(core_axis_name="c", subcore_axis_name="s")


def _sc_cores():
    info = plsc.get_sparse_core_info()
    assert info.num_cores * info.num_subcores == SC_WORKERS, info
    return info.num_cores


def _sc_scatter_rows(x, pos, n_out):
    t, w = x.shape
    k = pos.shape[0]
    t_per_w = t // SC_WORKERS
    n_cores = _sc_cores()

    @functools.partial(
        pl.kernel, mesh=_sc_mesh(),
        out_type=jax.ShapeDtypeStruct((n_out, w), jnp.int32),
        scratch_types=[pltpu.VMEM((k, SC_CHUNK), jnp.int32), pltpu.VMEM((SC_CHUNK, w), jnp.int32),
                       pltpu.SemaphoreType.DMA],
    )
    def scatter_kernel(x_hbm, pos_hbm, out_hbm, idx_v, rows_v, sem):
        base = (lax.axis_index("s") * n_cores + lax.axis_index("c")) * t_per_w

        @pl.loop(0, t_per_w // SC_CHUNK)
        def _(i):
            off = pl.multiple_of(base + i * SC_CHUNK, SC_CHUNK)
            pltpu.sync_copy(pos_hbm.at[:, pl.ds(off, SC_CHUNK)], idx_v)
            pltpu.sync_copy(x_hbm.at[pl.ds(off, SC_CHUNK)], rows_v)
            for j in range(k):
                pltpu.async_copy(rows_v, out_hbm.at[idx_v.at[j]], sem).wait()

    return scatter_kernel(x, pos)


def _sc_gather_rows(table, idx):
    n = idx.shape[0]
    w = table.shape[1]
    n_per_w = n // SC_WORKERS
    n_cores = _sc_cores()

    half = SC_CHUNK // 2

    @functools.partial(
        pl.kernel, mesh=_sc_mesh(),
        out_type=jax.ShapeDtypeStruct((n, w), jnp.int32),
        scratch_types=[pltpu.VMEM((2, half), jnp.int32), pltpu.VMEM((2, half, w), jnp.int32)]
                      + [pltpu.SemaphoreType.DMA] * 4,
    )
    def gather_kernel(table_hbm, idx_hbm, out_hbm, idx_v, rows_v, sem_g0, sem_g1, sem_w0, sem_w1):
        base = (lax.axis_index("s") * n_cores + lax.axis_index("c")) * n_per_w
        sem_g = (sem_g0, sem_g1)
        sem_w = (sem_w0, sem_w1)

        @pl.loop(0, n_per_w // SC_CHUNK)
        def _(i):
            off = pl.multiple_of(base + i * SC_CHUNK, SC_CHUNK)
            for h in range(2):
                pltpu.sync_copy(idx_hbm.at[pl.ds(off + h * half, half)], idx_v.at[h])
            gathers = [pltpu.async_copy(table_hbm.at[idx_v.at[h]], rows_v.at[h], sem_g[h]) for h in range(2)]
            writes = []
            for h in range(2):
                gathers[h].wait()
                writes.append(pltpu.async_copy(rows_v.at[h], out_hbm.at[pl.ds(off + h * half, half)], sem_w[h]))
            for c in writes:
                c.wait()

    return gather_kernel(table, idx)


def _experts_kernel(te_ref, used_ref, xs_ref, wg_ref, wu_ref, wd_ref, o_ref, wg_s, wu_s, wd_s):
    i = pl.program_id(0)

    @pl.when((i == 0) | (te_ref[i] != te_ref[jnp.maximum(i - 1, 0)]))
    def _():
        wg_s[...] = wg_ref[0].astype(BF16)
        wu_s[...] = wu_ref[0].astype(BF16)
        wd_s[...] = wd_ref[0].astype(BF16)

    @pl.when(i < used_ref[0])
    def _():
        x = _unpack_bf16_pairs(xs_ref[...]).astype(BF16)
        g = jnp.dot(x, wg_s[...], preferred_element_type=F32)
        u = jnp.dot(x, wu_s[...], preferred_element_type=F32)
        act = (_silu(g) * u).astype(BF16)
        o_ref[...] = _pack_bf16_pairs(jnp.dot(act, wd_s[...], preferred_element_type=F32))


def _experts(xs, tile_expert, tiles_used, we_gate, we_up, we_down):
    r, half = xs.shape
    d = 2 * half
    row_tile = lambda i, te, used: (jnp.minimum(i, used[0] - 1), 0)
    grid_spec = pltpu.PrefetchScalarGridSpec(
        num_scalar_prefetch=2,
        grid=(r // EXPERT_TILE,),
        in_specs=[pl.BlockSpec((EXPERT_TILE, half), row_tile),
                  pl.BlockSpec((1, d, D_EXPERT), lambda i, te, used: (te[i], 0, 0)),
                  pl.BlockSpec((1, d, D_EXPERT), lambda i, te, used: (te[i], 0, 0)),
                  pl.BlockSpec((1, D_EXPERT, d), lambda i, te, used: (te[i], 0, 0))],
        out_specs=pl.BlockSpec((EXPERT_TILE, half), row_tile),
        scratch_shapes=[pltpu.VMEM((d, D_EXPERT), BF16), pltpu.VMEM((d, D_EXPERT), BF16),
                        pltpu.VMEM((D_EXPERT, d), BF16)],
    )
    return pl.pallas_call(
        _experts_kernel,
        grid_spec=grid_spec,
        out_shape=jax.ShapeDtypeStruct((r, half), jnp.int32),
        compiler_params=_cparams(("arbitrary",), 32),
        name="experts",
    )(tile_expert, tiles_used, xs, we_gate, we_up, we_down)


def _combine_kernel(og_ref, wk_ref, h_ref, x1_ref, ga2_ref, sgu_ref, sd_ref, g2_ref, b2_ref, *rest, dn_alpha):
    o_ref = rest[-1]
    bb, tl, d = h_ref.shape
    tm = bb * tl
    h = h_ref[...].reshape(tm, d)
    su = jnp.dot(h, sgu_ref[...], preferred_element_type=F32)
    acc = _dot(_silu(su[:, :D_SHARED]) * su[:, D_SHARED:], sd_ref[...])
    lane = lax.broadcasted_iota(jnp.int32, (tm, TOP_K), 1)
    w_all = wk_ref[...]
    for k in range(TOP_K):
        acc = acc + _lane_pick(w_all, lane, k) * _unpack_bf16_pairs(og_ref[k])
    y = dn_alpha * x1_ref[...] + (1.0 + ga2_ref[...]) * acc.reshape(bb, tl, d)
    o_ref[...] = _layernorm(y, g2_ref[...], b2_ref[...])


def _combine(og, w_tk, h2, x1, ga2, ws_gu, ws_down, ln_g, ln_b, bb, tl, dn_alpha, b0, nb, prev):
    b, l, d = x1.shape
    nj = l // tl
    tm = bb * tl
    i0 = b0 // bb
    full = lambda *shape: pl.BlockSpec(shape, lambda i, j: (0,) * len(shape))
    tok = pl.BlockSpec((bb, tl, d), lambda i, j: (i + i0, j, 0))
    in_specs = [pl.BlockSpec((TOP_K, tm, d // 2), lambda i, j: (0, i * nj + j, 0)),
                pl.BlockSpec((tm, TOP_K), lambda i, j: (i * nj + j, 0)),
                tok, tok,
                pl.BlockSpec((bb, 1, d), lambda i, j: (i + i0, 0, 0)),
                full(d, 2 * D_SHARED), full(D_SHARED, d), full(1, d), full(1, d)]
    args = [og, w_tk, h2, x1, ga2, ws_gu, ws_down, ln_g, ln_b]
    aliases = {}
    if prev is not None:
        in_specs.append(pl.BlockSpec(memory_space=pl.ANY))
        args.append(prev)
        aliases = {len(args) - 1: 0}
    return pl.pallas_call(
        functools.partial(_combine_kernel, dn_alpha=dn_alpha),
        grid=(nb // bb, nj),
        in_specs=in_specs,
        out_specs=tok,
        out_shape=jax.ShapeDtypeStruct((b, l, d), F32),
        input_output_aliases=aliases,
        compiler_params=_cparams(("parallel", "parallel"), 48),
        name="combine",
    )(*args)


def _moe_sparse(hp, eidx, rank, w_k, cnt, h2, x1, ga2, p, bb, dn_alpha):
    t_all = hp.shape[0]
    n_tiles = t_all * TOP_K // EXPERT_TILE + N_EXPERTS
    seg_tiles = (cnt[:, 0] + EXPERT_TILE - 1) // EXPERT_TILE
    seg_end = jnp.cumsum(seg_tiles)
    seg_start = (seg_end - seg_tiles) * EXPERT_TILE
    experts = jnp.arange(N_EXPERTS, dtype=jnp.int32)
    pos = rank + jnp.sum(jnp.where(eidx[None] == experts[:, None, None], seg_start[:, None, None], 0), axis=0)
    tiles = jnp.arange(n_tiles, dtype=jnp.int32)
    tile_expert = jnp.minimum(jnp.sum((seg_end[None, :] <= tiles[:, None]).astype(jnp.int32), axis=1),
                              N_EXPERTS - 1)
    xs = _sc_scatter_rows(hp, pos, n_tiles * EXPERT_TILE)
    outs = _experts(xs, tile_expert, seg_end[-1:].astype(jnp.int32), p['we_gate'], p['we_up'], p['we_down'])
    b = x1.shape[0]
    nb = b // GATHER_PARTS
    t_part = t_all // GATHER_PARTS
    w_tk = w_k.T
    out = None
    for part in range(GATHER_PARTS):
        tokens = slice(part * t_part, (part + 1) * t_part)
        og = _sc_gather_rows(outs, pos[:, tokens].reshape(-1)).reshape(TOP_K, t_part, hp.shape[1])
        out = _combine(og, w_tk[tokens], h2, x1, ga2, p['ws_gu'], p['ws_down'], p['ln2_g'], p['ln2_b'],
                       bb, COMBINE_TL, dn_alpha, part * nb, nb, out)
    return out


def _pad_rows(buf, width):
    b, n, w = buf.shape
    return jnp.pad(buf.astype(F32), ((0, 0), (SUBLANES - n, 0), (0, width - w)))


def _to_pairs(s):
    b = s.shape[0]
    s = s.astype(F32).reshape(b, PAIRS, 2, N_B, N_B)
    zero = jnp.zeros_like(s[:, :, 0])
    top = jnp.concatenate([s[:, :, 0], zero], axis=-1)
    bot = jnp.concatenate([zero, s[:, :, 1]], axis=-1)
    return jnp.concatenate([top, bot], axis=-2)


def _from_pairs(sp):
    b = sp.shape[0]
    return jnp.stack([sp[:, :, :N_B, :N_B], sp[:, :, N_B:, N_B:]], axis=2).reshape(b, H_B, N_B, N_B)


def _layer(x, mod, conv_buf, gdn_s, shift_buf, rwkv_s, p, tiles, dn_alpha):
    b, l, d = x.shape
    sh1, sc1, ga1, sh2, sc2, ga2 = mod
    proj = _inproj(x, sc1, sh1, p['w_in'], tiles['bb'], tiles['tl'])
    ya, gdn_new = _gdn(proj, _pad_rows(conv_buf, QKV_A), gdn_s.astype(F32), p['conv_w'], p['alog_row'],
                       p['dt_row'], p['gdn_norm_w'], tiles['bb_gdn'], tiles['tl_gdn'], l)
    yb, rwkv_pairs = _rwkv(proj, _pad_rows(shift_buf, RW_BLOCK), _to_pairs(rwkv_s), p['rwkv_vecs'],
                           p['w2p'], p['a2p'], p['g2'], tiles['bb_rwkv'], tiles['tl_rwkv'], l)

    x1, h2, gates_t, hp, eidx, rank, w_k, cnt = _merge(
        ya, yb, proj, x, ga1, sc2, sh2, p['p_a'], p['p_b'], p['w_o'], p['ln1_g'], p['ln1_b'],
        p['router_wt'], p['router_bias'], tiles['bb'], tiles['tl_merge'], dn_alpha)
    sc_rows = SC_WORKERS * SC_CHUNK
    if (b * l) % sc_rows == 0 and b % GATHER_PARTS == 0 and (b * l // GATHER_PARTS * TOP_K) % sc_rows == 0:
        out = _moe_sparse(hp, eidx, rank, w_k, cnt, h2, x1, ga2, p, tiles['bb'], dn_alpha)
    else:
        out = _moe(h2, gates_t.T, x1, ga2, p['we_gate'], p['we_up'], p['we_down'], p['ws_gu'], p['ws_down'],
                   p['ln2_g'], p['ln2_b'], tiles['bb'], tiles['tl'], dn_alpha)

    pre = jnp.concatenate([conv_buf.astype(F32), proj[:, :, P_QKV:P_QKV + QKV_A]], axis=1)
    conv_new = pre[:, -(CONV_W - 1):]
    shift_new = proj[:, l - 1:l, P_RW:P_RW + SHIFT_W]
    return (out, conv_new.astype(conv_buf.dtype), gdn_new.astype(gdn_s.dtype),
            shift_new.astype(shift_buf.dtype), _from_pairs(rwkv_pairs).astype(rwkv_s.dtype))


def _prep_params(l, w_in, conv_w, a_log, dt_bias, gdn_norm_w, mu_shift, w0, w2, a0, a2, g2, k_k, k_a, r_k,
                 lnx_g, lnx_b, p_a, p_b, w_o, ln1_g, ln1_b, router_w, router_bias, we_gate, we_up, we_down,
                 ws_gate, ws_up, ws_down, ln2_g, ln2_b):
    d = D_MODEL
    w = w_in[l]
    w_p = jnp.concatenate(
        [w[:, :QKV_A], w[:, OFF_Z:OFF_RWKV], w[:, OFF_RWKV:OFF_MERGE],
         jnp.zeros((d, RW_BLOCK - SHIFT_W), w.dtype), w[:, OFF_MERGE:], w[:, OFF_ALPHA:OFF_Z],
         jnp.zeros((d, N_PROJ - P_AB - 2 * H_A), w.dtype)], axis=1).astype(BF16)
    row = lambda v, width: jnp.pad(v.astype(F32).reshape(1, -1), ((0, 0), (0, width - v.size)))
    zeros_lora = jnp.zeros((W_LORA, D_B), F32)
    return {
        'w_in': w_p,
        'conv_w': conv_w[l].astype(F32),
        'alog_row': row(a_log[l], LANES),
        'dt_row': row(dt_bias[l], LANES),
        'gdn_norm_w': row(gdn_norm_w[l], DK_A),
        'rwkv_vecs': (row(mu_shift[l], RW_BLOCK), row(w0[l], D_B), row(a0[l], D_B), row(k_k[l], D_B),
                      row(k_a[l], D_B), row(r_k[l], D_B), row(lnx_g[l], D_B), row(lnx_b[l], D_B)),
        'w2p': jnp.concatenate([w2[l].astype(F32), zeros_lora], axis=0),
        'a2p': jnp.concatenate([zeros_lora, a2[l].astype(F32)], axis=0),
        'g2': g2[l].astype(F32),
        'p_a': p_a[l].astype(BF16), 'p_b': p_b[l].astype(BF16), 'w_o': w_o[l].astype(BF16),
        'ln1_g': row(ln1_g[l], d), 'ln1_b': row(ln1_b[l], d),
        'router_wt': router_w[l].T.astype(BF16),
        'router_bias': router_bias[l].astype(F32).reshape(N_EXPERTS, 1),
        'we_gate': we_gate[l], 'we_up': we_up[l], 'we_down': we_down[l],
        'ws_gu': jnp.concatenate([ws_gate[l], ws_up[l]], axis=-1).astype(BF16),
        'ws_down': ws_down[l].astype(BF16),
        'ln2_g': row(ln2_g[l], d), 'ln2_b': row(ln2_b[l], d),
    }


def _tiles(b, l):
    lp = -(-l // CHUNK) * CHUNK
    if l >= TOKEN_TILE:
        bb, tl, tl_merge = 1, TOKEN_TILE, TOKEN_TILE // 2
    else:
        bb, tl, tl_merge = b, l, l
    return {'bb': bb, 'tl': tl, 'tl_merge': tl_merge,
            'bb_gdn': 4, 'tl_gdn': CHUNK, 'bb_rwkv': 2, 'tl_rwkv': min(2 * CHUNK, lp)}


def kernel(x_prompt, x_sample, c_prompt, c_sample, state_gdn_conv, state_gdn, state_rwkv_shift, state_rwkv, w_ada, b_ada, w_in, conv_w, a_log, dt_bias, gdn_norm_w, mu_shift, w0, w2, a0, a2, g2, k_k, k_a, r_k, lnx_g, lnx_b, p_a, p_b, w_o, ln1_g, ln1_b, router_w, router_bias, we_gate, we_up, we_down, ws_gate, ws_up, ws_down, ln2_g, ln2_b):
    depth = w_ada.shape[0]
    dn_alpha = (2 * depth) ** 0.25
    bp, lp_, d = x_prompt.shape
    bs, ls, _ = x_sample.shape
    dtp = x_prompt.dtype
    tiles_p = _tiles(bp, lp_)
    tiles_s = _tiles(bs, ls)

    yp, ys = x_prompt, x_sample
    new_p = ([], [], [], [])
    new_s = ([], [], [], [])
    for l in range(depth):
        p = _prep_params(l, w_in, conv_w, a_log, dt_bias, gdn_norm_w, mu_shift, w0, w2, a0, a2, g2, k_k, k_a,
                         r_k, lnx_g, lnx_b, p_a, p_b, w_o, ln1_g, ln1_b, router_w, router_bias, we_gate, we_up,
                         we_down, ws_gate, ws_up, ws_down, ln2_g, ln2_b)
        mod = _ada(jnp.concatenate([c_prompt, c_sample], axis=0), w_ada[l], b_ada[l])
        mod_p = tuple(m[:, None, :] for m in jnp.split(mod[:bp], 6, axis=-1))
        mod_s = tuple(m[:, None, :] for m in jnp.split(mod[bp:], 6, axis=-1))
        yp, *sp = _layer(yp, mod_p,
                         jnp.zeros((bp, CONV_W - 1, QKV_A), dtp), jnp.zeros((bp, H_A, DK_A, DK_A), dtp),
                         jnp.zeros((bp, 1, SHIFT_W), dtp), jnp.zeros((bp, H_B, N_B, N_B), dtp),
                         p, tiles_p, dn_alpha)
        ys, *ss = _layer(ys, mod_s, state_gdn_conv[l], state_gdn[l], state_rwkv_shift[l], state_rwkv[l],
                         p, tiles_s, dn_alpha)
        for lst, val in zip(new_p, sp):
            lst.append(val)
        for lst, val in zip(new_s, ss):
            lst.append(val)
    conv_p, gdn_p, shift_p, rwkv_p = [jnp.stack(t, 0) for t in new_p]
    conv_s, gdn_s, shift_s, rwkv_s = [jnp.stack(t, 0) for t in new_s]
    return (yp, ys, conv_p, gdn_p, shift_p, rwkv_p, conv_s, gdn_s, shift_s, rwkv_s)
```

```python
import functools
import math

import jax
import jax.numpy as jnp
from jax import lax
from jax.experimental import pallas as pl
from jax.experimental.pallas import tpu as pltpu
from jax.experimental.pallas import tpu_sc as plsc

F32 = jnp.float32
BF16 = jnp.bfloat16

D_MODEL = 1024
DK_A = 128
H_A = 4
QK_A = H_A * DK_A
V_A = H_A * DK_A
QKV_A = 2 * QK_A + V_A
CONV_W = 4
N_B = 64
H_B = 8
D_B = H_B * N_B
W_LORA = 64
A_LORA = 64
G_LORA = 128
SHIFT_W = 3 * D_B + W_LORA + A_LORA + G_LORA
OFF_ALPHA = QKV_A
OFF_BETA = OFF_ALPHA + H_A
OFF_Z = OFF_BETA + H_A
OFF_RWKV = OFF_Z + V_A
OFF_MERGE = OFF_RWKV + SHIFT_W
N_EXPERTS = 64
TOP_K = 8
N_GROUPS = 8
GROUP_SIZE = N_EXPERTS // N_GROUPS
TOPK_GROUPS = 4
D_EXPERT = 256
D_SHARED = 256
ROUTED_SCALE = 2.5
LN_EPS = 1e-5
GN_EPS = 64e-5
RMS_EPS = 1e-6
DECAY_SCALE = -math.exp(-0.5)

SUBLANES = 8
LANES = 128

P_QKV = 0
P_Z = QKV_A
P_RW = 2048
RW_BLOCK = 2048
P_MG = P_RW + RW_BLOCK
P_AB = P_MG + 2 * D_MODEL
AB_BLOCK = LANES
PROJ_TN = 1280
N_PROJ = 5 * PROJ_TN

CHUNK = 128
BASE_BLOCK = 8
CUMSUM_PARTS = 3
STAT_PARTS = 2
EXPERTS_PER_STEP = 2
HIGH_HALF = -65536

EXPERT_TILE = 1024
SC_WORKERS = 32
SC_CHUNK = 128
COMBINE_TL = 512
MOE_PARTS = 2
TOKEN_TILE = 1024


def _cparams(sem, vmem_mb):
    return pltpu.CompilerParams(dimension_semantics=sem, vmem_limit_bytes=vmem_mb * 1024 * 1024)


def _dot(a, b):
    return jnp.dot(a.astype(BF16), b.astype(BF16), preferred_element_type=F32)


def _rdot(a, b, dims=((1,), (0,))):
    return lax.dot_general(a.astype(BF16), b.astype(BF16), (dims, ((), ())), preferred_element_type=F32)


_NT = ((1,), (1,))
_TN = ((0,), (0,))


def _bf16_parts(x, parts):
    out = []
    rem = x
    for _ in range(parts):
        hi = rem.astype(BF16)
        out.append(hi)
        rem = rem - hi.astype(F32)
    return out


def _mask_dot_left(mask, x, parts):
    return sum(jnp.dot(mask, p, preferred_element_type=F32) for p in _bf16_parts(x, parts))


def _mask_dot_right(x, mask, parts):
    return sum(jnp.dot(p, mask, preferred_element_type=F32) for p in _bf16_parts(x, parts))


def _lane_pick(x, lane_iota, lane):
    return jnp.sum(jnp.where(lane_iota == lane, x, 0.0), axis=-1, keepdims=True)


def _pack_bf16_pairs(x):
    n = x.shape[1] // 2
    bits = lax.bitcast_convert_type(x.astype(BF16).astype(F32), jnp.int32)
    return (bits[:, :n] & HIGH_HALF) | lax.shift_right_logical(bits[:, n:], 16)


def _unpack_bf16_pairs(p):
    hi = lax.bitcast_convert_type(p & HIGH_HALF, F32)
    lo = lax.bitcast_convert_type(lax.shift_left(p, 16), F32)
    return jnp.concatenate([hi, lo], axis=1)


def _silu(x):
    return x * jax.nn.sigmoid(x)


def _softplus(x):
    return jnp.maximum(x, 0.0) + jnp.log1p(jnp.exp(-jnp.abs(x)))


def _tri_inverse(mats, eye, sign, row, col):
    base = (row // BASE_BLOCK) == (col // BASE_BLOCK)
    ds = [jnp.where(base, a, 0.0) for a in mats]
    xs = [eye + d if sign > 0 else eye - d for d in ds]
    power = 2
    while power < BASE_BLOCK:
        ds = [_rdot(d, d) for d in ds]
        xs = [x + _rdot(x, d) for x, d in zip(xs, ds)]
        power *= 2
    b = BASE_BLOCK
    while b < CHUNK:
        sibling = ((row // b) == (col // b) + 1) & ((row // (2 * b)) == (col // (2 * b)))
        offs = [jnp.where(sibling, a, 0.0) for a in mats]
        ts = [_rdot(o, x) for o, x in zip(offs, xs)]
        if sign > 0:
            xs = [x + _rdot(x, t_) for x, t_ in zip(xs, ts)]
        else:
            xs = [x - _rdot(x, t_) for x, t_ in zip(xs, ts)]
        b *= 2
    return xs


def _tri_consts():
    row = lax.broadcasted_iota(jnp.int32, (CHUNK, CHUNK), 0)
    col = lax.broadcasted_iota(jnp.int32, (CHUNK, CHUNK), 1)
    incl = row >= col
    strict = row > col
    eye = jnp.where(row == col, 1.0, 0.0).astype(F32)
    tril = jnp.where(incl, 1.0, 0.0).astype(BF16)
    return row, col, incl, strict, eye, tril


def _ada_kernel(c_ref, w_ref, b_ref, o_ref):
    o_ref[...] = _dot(_silu(c_ref[...]), w_ref[...]) + b_ref[...]


def _ada(c, w_ada, b_ada):
    n, d = c.shape
    nout = w_ada.shape[1]
    tn = 768
    return pl.pallas_call(
        _ada_kernel,
        grid=(nout // tn,),
        in_specs=[pl.BlockSpec((n, d), lambda j: (0, 0)),
                  pl.BlockSpec((d, tn), lambda j: (0, j)),
                  pl.BlockSpec((1, tn), lambda j: (0, j))],
        out_specs=pl.BlockSpec((n, tn), lambda j: (0, j)),
        out_shape=jax.ShapeDtypeStruct((n, nout), F32),
        compiler_params=_cparams(("parallel",), 32),
        name="ada",
    )(c, w_ada, b_ada.reshape(1, nout))


def _inproj_kernel(x_ref, sc_ref, sh_ref, w_ref, o_ref, hs_ref):
    @pl.when(pl.program_id(2) == 0)
    def _():
        h = x_ref[...] * (1.0 + sc_ref[...]) + sh_ref[...]
        hs_ref[...] = h.reshape(hs_ref.shape).astype(BF16)

    o = jnp.dot(hs_ref[...], w_ref[...], preferred_element_type=F32)
    o_ref[...] = o.reshape(o_ref.shape)


def _inproj(x, sc, sh, w_p, bb, tl):
    b, l, d = x.shape
    return pl.pallas_call(
        _inproj_kernel,
        grid=(b // bb, l // tl, N_PROJ // PROJ_TN),
        in_specs=[pl.BlockSpec((bb, tl, d), lambda i, j, n: (i, j, 0)),
                  pl.BlockSpec((bb, 1, d), lambda i, j, n: (i, 0, 0)),
                  pl.BlockSpec((bb, 1, d), lambda i, j, n: (i, 0, 0)),
                  pl.BlockSpec((d, PROJ_TN), lambda i, j, n: (0, n))],
        out_specs=pl.BlockSpec((bb, tl, PROJ_TN), lambda i, j, n: (i, j, n)),
        out_shape=jax.ShapeDtypeStruct((b, l, N_PROJ), F32),
        scratch_shapes=[pltpu.VMEM((bb * tl, d), BF16)],
        compiler_params=_cparams(("parallel", "parallel", "arbitrary"), 48),
        name="inproj",
    )(x, sc, sh, w_p)


def _gdn_kernel(qkv_ref, ab_ref, z_ref, cbuf_ref, s0_ref, cw_ref, alog_ref, dt_ref, nw_ref,
                ya_ref, sout_ref, ext_ref, qc_ref, s_ref, *pad_s, bb, tl, l_in, l_valid, l_total):
    t = pl.program_id(1)

    @pl.when(t == 0)
    def _():
        ext_ref[:, 0:SUBLANES, :] = cbuf_ref[...]
        s_ref[...] = s0_ref[...]

    if l_in < tl:
        for src, dst in zip((qkv_ref, ab_ref, z_ref), pad_s):
            dst[:, l_in:, :] = jnp.zeros((bb, tl - l_in, dst.shape[2]), F32)
            dst[:, 0:l_in, :] = src[...]
        qkv_ref, ab_ref, z_ref = pad_s

    for bi in range(bb):
        cur = qkv_ref[bi]
        ext_ref[bi, SUBLANES:SUBLANES + tl, :] = cur
        window = ext_ref[bi]
        acc = cw_ref[CONV_W - 1:CONV_W, :] * cur
        for j in range(CONV_W - 1):
            back = CONV_W - 1 - j
            acc = acc + cw_ref[j:j + 1, :] * pltpu.roll(window, tl + back, 0)[0:tl]
        ext_ref[bi, 0:SUBLANES, :] = ext_ref[bi, tl:tl + SUBLANES, :]
        qc_ref[bi] = _silu(acc)

    row, col, incl, strict, eye, tril = _tri_consts()
    neg_a = -jnp.exp(alog_ref[...])
    dt = dt_ref[...]
    nw = nw_ref[...]
    chains = [(bi, h) for bi in range(bb) for h in range(H_A)]
    n = range(len(chains))

    def chunk(c, carry):
        r0 = pl.multiple_of(c * CHUNK, CHUNK)
        rows = pl.ds(r0, CHUNK)
        g_cum = []
        b_all = []
        for bi in range(bb):
            ab = ab_ref[bi, rows, :]
            g_bi = neg_a * _softplus(ab + dt)
            b_bi = jax.nn.sigmoid(ab)
            if l_valid < l_total:
                valid = (t * tl + r0 + row) < l_valid
                g_bi = jnp.where(valid, g_bi, 0.0)
                b_bi = jnp.where(valid, b_bi, 0.0)
            g_cum.append(_mask_dot_left(tril, g_bi, CUMSUM_PARTS))
            b_all.append(b_bi)
        g_col = [_lane_pick(g_cum[bi], col, h) for bi, h in chains]
        beta = [_lane_pick(b_all[bi], col, H_A + h) for bi, h in chains]
        decay = []
        for i in n:
            g_b = jnp.broadcast_to(g_col[i], (CHUNK, CHUNK))
            decay.append(jnp.where(incl, jnp.exp(jnp.where(incl, g_b - g_b.T, 0.0)), 0.0))
        q = [qc_ref[bi, rows, h * DK_A:(h + 1) * DK_A] for bi, h in chains]
        k = [qc_ref[bi, rows, QK_A + h * DK_A:QK_A + (h + 1) * DK_A] for bi, h in chains]
        v = [qc_ref[bi, rows, 2 * QK_A + h * DK_A:2 * QK_A + (h + 1) * DK_A] for bi, h in chains]
        q = [x * lax.rsqrt(jnp.sum(x * x, -1, keepdims=True) + 1e-6) * (DK_A ** -0.5) for x in q]
        k = [x * lax.rsqrt(jnp.sum(x * x, -1, keepdims=True) + 1e-6) for x in k]
        kq = [_rdot(jnp.concatenate([k[i], q[i]], axis=0), k[i], _NT) for i in n]
        a = [jnp.where(strict, beta[i] * kq[i][:CHUNK] * decay[i], 0.0) for i in n]
        qk = [jnp.where(incl, kq[i][CHUNK:] * decay[i], 0.0) for i in n]
        x = _tri_inverse(a, eye, -1, row, col)
        e_g = [jnp.exp(g) for g in g_col]
        uw = [_rdot(x[i], jnp.concatenate([v[i] * beta[i], k[i] * (beta[i] * e_g[i])], axis=1)) for i in n]
        g_last = [g[CHUNK - 1:CHUNK, :] for g in g_col]
        kd = [k[i] * jnp.exp(g_last[i] - g_col[i]) for i in n]
        s = [s_ref[bi, h] for bi, h in chains]
        ws = [_rdot(jnp.concatenate([uw[i][:, DK_A:], q[i] * e_g[i]], axis=0), s[i]) for i in n]
        v_new = [uw[i][:, :DK_A] - ws[i][:CHUNK] for i in n]
        o = [ws[i][CHUNK:] + _rdot(qk[i], v_new[i]) for i in n]
        s_new = [s[i] * jnp.exp(g_last[i]) + _rdot(kd[i], v_new[i], _TN) for i in n]
        for i, (bi, h) in enumerate(chains):
            s_ref[bi, h] = s_new[i]
            z = z_ref[bi, rows, h * DK_A:(h + 1) * DK_A]
            on = o[i] * lax.rsqrt(jnp.mean(o[i] * o[i], -1, keepdims=True) + RMS_EPS) * nw * _silu(z)
            if l_in < tl:
                ya_ref[bi, :, h * DK_A:(h + 1) * DK_A] = on[0:l_in].astype(BF16)
            else:
                ya_ref[bi, rows, h * DK_A:(h + 1) * DK_A] = on.astype(BF16)
        return carry

    lax.fori_loop(0, tl // CHUNK, chunk, 0)

    @pl.when(t == pl.num_programs(1) - 1)
    def _():
        sout_ref[...] = s_ref[...]


def _rec_rows(l, tl):
    if l % tl == 0:
        return l, tl
    assert l < tl == CHUNK, (l, tl)
    return tl, l


def _gdn(proj, conv_buf8, s0, conv_w, alog_row, dt_row, norm_w, bb, tl, l_valid):
    b, l, _ = proj.shape
    lp, l_in = _rec_rows(l, tl)
    kern = functools.partial(_gdn_kernel, bb=bb, tl=tl, l_in=l_in, l_valid=l_valid, l_total=lp)
    full = lambda *shape: pl.BlockSpec(shape, lambda i, t: (0,) * len(shape))
    pad_scratch = [pltpu.VMEM((bb, tl, w), F32) for w in (QKV_A, AB_BLOCK, V_A)] if l_in < tl else []
    return pl.pallas_call(
        kern,
        grid=(b // bb, lp // tl),
        in_specs=[pl.BlockSpec((bb, l_in, QKV_A), lambda i, t: (i, t, P_QKV // QKV_A)),
                  pl.BlockSpec((bb, l_in, AB_BLOCK), lambda i, t: (i, t, P_AB // AB_BLOCK)),
                  pl.BlockSpec((bb, l_in, V_A), lambda i, t: (i, t, P_Z // V_A)),
                  pl.BlockSpec((bb, SUBLANES, QKV_A), lambda i, t: (i, 0, 0)),
                  pl.BlockSpec((bb, H_A, DK_A, DK_A), lambda i, t: (i, 0, 0, 0)),
                  full(CONV_W, QKV_A), full(1, LANES), full(1, LANES), full(1, DK_A)],
        out_specs=[pl.BlockSpec((bb, l_in, V_A), lambda i, t: (i, t, 0)),
                   pl.BlockSpec((bb, H_A, DK_A, DK_A), lambda i, t: (i, 0, 0, 0))],
        out_shape=[jax.ShapeDtypeStruct((b, l, V_A), BF16),
                   jax.ShapeDtypeStruct((b, H_A, DK_A, DK_A), F32)],
        scratch_shapes=[pltpu.VMEM((bb, tl + SUBLANES, QKV_A), F32),
                        pltpu.VMEM((bb, tl, QKV_A), F32),
                        pltpu.VMEM((bb, H_A, DK_A, DK_A), F32)] + pad_scratch,
        compiler_params=_cparams(("parallel", "arbitrary"), 48),
        name="gdn",
    )(proj, proj, proj, conv_buf8, s0, conv_w, alog_row, dt_row, norm_w)


PAIRS = H_B // 2


def _headsum(x, bd):
    return jnp.concatenate(
        [_mask_dot_right(x[:, p * LANES:(p + 1) * LANES], bd, STAT_PARTS) for p in range(PAIRS)], axis=1)


def _rwkv_kernel(rw_ref, sbuf_ref, s0_ref, mu_ref, w0_ref, a0_ref, kk_ref, ka_ref, rk_ref, lg_ref, lb_ref,
                 w2_ref, a2_ref, g2_ref, yb_ref, sout_ref,
                 ext_ref, r_s, k_s, v_s, z_s, p_s, lw_s, y_s, bonus_s, gate_s, s_ref, *,
                 bb, tl, l_in, l_valid, l_total):
    t = pl.program_id(1)

    @pl.when(t == 0)
    def _():
        ext_ref[:, 0:SUBLANES, :] = sbuf_ref[...]
        s_ref[...] = s0_ref[...]

    row, col, incl, strict, eye, tril = _tri_consts()
    same_head = (row // N_B) == (col // N_B)
    bd = jnp.where(same_head, 1.0, 0.0).astype(BF16)
    lane_a = col < N_B

    for bi in range(bb):
        if l_in < tl:
            ext_ref[bi, SUBLANES + l_in:SUBLANES + tl, :] = jnp.zeros((tl - l_in, RW_BLOCK), F32)
            ext_ref[bi, SUBLANES:SUBLANES + l_in, :] = rw_ref[bi]
            cur = ext_ref[bi, SUBLANES:SUBLANES + tl, :]
        else:
            cur = rw_ref[bi]
            ext_ref[bi, SUBLANES:SUBLANES + tl, :] = cur
        prev = pltpu.roll(ext_ref[bi], tl + 1, 0)[0:tl]
        mixed = cur + (prev - cur) * mu_ref[...]
        ext_ref[bi, 0:SUBLANES, :] = ext_ref[bi, tl:tl + SUBLANES, :]
        r = mixed[:, 0:D_B]
        k = mixed[:, D_B:2 * D_B]
        v = mixed[:, 2 * D_B:3 * D_B]
        lora = mixed[:, 3 * D_B:3 * D_B + W_LORA + A_LORA]
        g_in = mixed[:, 3 * D_B + W_LORA + A_LORA:SHIFT_W]
        lw = DECAY_SCALE * jax.nn.sigmoid(w0_ref[...] + _dot(jnp.tanh(lora), w2_ref[...]))
        a = jax.nn.sigmoid(a0_ref[...] + _dot(lora, a2_ref[...]))
        kkr = k * kk_ref[...]
        kk = kkr * lax.rsqrt(_headsum(kkr * kkr, bd) + 1e-6)
        k = k * (1.0 + (a - 1.0) * ka_ref[...])
        tile_rows = pl.ds(bi * tl, tl)
        bonus_s[tile_rows, :] = _headsum(r * k * rk_ref[...], bd) * v
        gate_s[tile_rows, :] = _dot(jax.nn.sigmoid(g_in), g2_ref[...])
        if l_valid < l_total:
            rvalid = (t * tl + lax.broadcasted_iota(jnp.int32, (tl, 1), 0)) < l_valid
            lw = jnp.where(rvalid, lw, 0.0)
            kk = jnp.where(rvalid, kk, 0.0)
            k = jnp.where(rvalid, k, 0.0)
            v = jnp.where(rvalid, v, 0.0)
        r_s[tile_rows, :] = r
        k_s[tile_rows, :] = k
        v_s[tile_rows, :] = v
        z_s[tile_rows, :] = -kk
        p_s[tile_rows, :] = kk * a
        lw_s[tile_rows, :] = lw

    groups = [(bi, p) for bi in range(bb) for p in range(PAIRS)]
    pairs = range(len(groups))
    both = range(2 * len(groups))
    cols = [slice(p * LANES, (p + 1) * LANES) for _, p in groups]

    def chunk(c, carry):
        r0 = pl.multiple_of(c * CHUNK, CHUNK)
        rows = [pl.ds(bi * tl + r0, CHUNK) for bi, _ in groups]
        lw_c = [lw_s[rows[g], cols[g]] for g in pairs]
        g_inc = [_mask_dot_left(tril, x, CUMSUM_PARTS) for x in lw_c]
        g_exc = [g_inc[p] - lw_c[p] for p in pairs]
        g_mid = [g[CHUNK // 2 - 1:CHUNK // 2, :] for g in g_inc]
        g_end = [g[CHUNK - 1:CHUNK, :] for g in g_inc]
        z = [z_s[rows[g], cols[g]] for g in pairs]
        rr = [r_s[rows[g], cols[g]] for g in pairs]
        pp = [p_s[rows[g], cols[g]] for g in pairs]
        kk_ = [k_s[rows[g], cols[g]] for g in pairs]
        vv = [v_s[rows[g], cols[g]] for g in pairs]
        zt = [z[p] * jnp.exp(g_exc[p] - g_mid[p]) for p in pairs]
        rt = [rr[p] * jnp.exp(g_inc[p] - g_mid[p]) for p in pairs]
        en = [jnp.exp(g_mid[p] - g_inc[p]) for p in pairs]
        s = [s_ref[bi, p] for bi, p in groups]
        lhs = [jnp.concatenate([jnp.where(lane_a, zt[p], 0.0), jnp.where(lane_a, 0.0, zt[p]),
                                jnp.where(lane_a, rt[p], 0.0), jnp.where(lane_a, 0.0, rt[p])], axis=0) for p in pairs]
        m = [_rdot(lhs[p], jnp.concatenate([pp[p] * en[p], kk_[p] * en[p]], axis=0), _NT) for p in pairs]
        zr0 = [_rdot(jnp.concatenate([z[p] * jnp.exp(g_exc[p]), rr[p] * jnp.exp(g_inc[p])], axis=0), s[p], _NT)
               for p in pairs]
        mz = [m[i // 2][(i % 2) * CHUNK:(i % 2 + 1) * CHUNK] for i in both]
        azp = [jnp.where(strict, x[:, :CHUNK], 0.0) for x in mz]
        azk = [jnp.where(strict, x[:, CHUNK:], 0.0) for x in mz]
        minv = _tri_inverse(azp, eye, 1, row, col)
        rhs = [zr0[i // 2][:CHUNK] + _rdot(azk[i], vv[i // 2]) for i in both]
        u_h = [_rdot(minv[i], rhs[i]) for i in both]
        u = [jnp.where(lane_a, u_h[2 * p], u_h[2 * p + 1]) for p in pairs]
        uv = [jnp.concatenate([u[p], vv[p]], axis=0) for p in pairs]
        incl2 = jnp.concatenate([incl, incl], axis=1)
        y_h = [_rdot(jnp.where(incl2, m[i // 2][(2 + i % 2) * CHUNK:(3 + i % 2) * CHUNK], 0.0), uv[i // 2])
               for i in both]
        tail = [jnp.exp(g_end[p] - g_inc[p]) for p in pairs]
        s_new = [s[p] * jnp.exp(g_end[p])
                 + _rdot(uv[p], jnp.concatenate([pp[p] * tail[p], kk_[p] * tail[p]], axis=0), _TN) for p in pairs]
        for g, (bi, p) in enumerate(groups):
            s_ref[bi, p] = jnp.where(same_head, s_new[g], 0.0)
            y_s[rows[g], cols[g]] = zr0[g][CHUNK:] + jnp.where(lane_a, y_h[2 * g], y_h[2 * g + 1])
        return carry

    lax.fori_loop(0, tl // CHUNK, chunk, 0)

    for bi in range(bb):
        tile_rows = pl.ds(bi * tl, tl)
        y = y_s[tile_rows, :]
        mean = _headsum(y, bd) * (1.0 / N_B)
        dev = y - mean
        var = _headsum(dev * dev, bd) * (1.0 / N_B)
        yn = dev * lax.rsqrt(var + GN_EPS) * lg_ref[...] + lb_ref[...]
        yb_ref[bi] = ((yn + bonus_s[tile_rows, :]) * gate_s[tile_rows, :])[0:l_in].astype(BF16)

    @pl.when(t == pl.num_programs(1) - 1)
    def _():
        sout_ref[...] = s_ref[...]


def _rwkv(proj, shift_buf8, s0_pairs, vecs, w2p, a2p, g2, bb, tl, l_valid):
    b, l, _ = proj.shape
    lp, l_in = _rec_rows(l, tl)
    kern = functools.partial(_rwkv_kernel, bb=bb, tl=tl, l_in=l_in, l_valid=l_valid, l_total=lp)
    full = lambda *shape: pl.BlockSpec(shape, lambda i, t: (0,) * len(shape))
    mu, w0, a0, k_k, k_a, r_k, lnx_g, lnx_b = vecs
    return pl.pallas_call(
        kern,
        grid=(b // bb, lp // tl),
        in_specs=[pl.BlockSpec((bb, l_in, RW_BLOCK), lambda i, t: (i, t, P_RW // RW_BLOCK)),
                  pl.BlockSpec((bb, SUBLANES, RW_BLOCK), lambda i, t: (i, 0, 0)),
                  pl.BlockSpec((bb, PAIRS, LANES, LANES), lambda i, t: (i, 0, 0, 0)),
                  full(1, RW_BLOCK)] + [full(1, D_B)] * 7 +
                 [full(W_LORA + A_LORA, D_B), full(W_LORA + A_LORA, D_B), full(G_LORA, D_B)],
        out_specs=[pl.BlockSpec((bb, l_in, D_B), lambda i, t: (i, t, 0)),
                   pl.BlockSpec((bb, PAIRS, LANES, LANES), lambda i, t: (i, 0, 0, 0))],
        out_shape=[jax.ShapeDtypeStruct((b, l, D_B), BF16),
                   jax.ShapeDtypeStruct((b, PAIRS, LANES, LANES), F32)],
        scratch_shapes=[pltpu.VMEM((bb, tl + SUBLANES, RW_BLOCK), F32)] +
                       [pltpu.VMEM((bb * tl, D_B), F32)] * 9 +
                       [pltpu.VMEM((bb, PAIRS, LANES, LANES), F32)],
        compiler_params=_cparams(("parallel", "arbitrary"), 48),
        name="rwkv",
    )(proj, shift_buf8, s0_pairs, mu, w0, a0, k_k, k_a, r_k, lnx_g, lnx_b, w2p, a2p, g2)


def _layernorm(y, g, b):
    mu = jnp.mean(y, -1, keepdims=True)
    dev = y - mu
    var = jnp.mean(dev * dev, -1, keepdims=True)
    return dev * lax.rsqrt(var + LN_EPS) * g + b


def _route(logits_t, bias, base_cnt):
    tm = logits_t.shape[1]
    scores = jax.nn.sigmoid(logits_t)
    choice = scores + bias
    neg_inf = -jnp.inf
    iota_g = lax.broadcasted_iota(jnp.int32, (GROUP_SIZE, tm), 0)
    group_score = []
    for g in range(N_GROUPS):
        xg = choice[g * GROUP_SIZE:(g + 1) * GROUP_SIZE, :]
        m1 = jnp.max(xg, axis=0, keepdims=True)
        first = jnp.min(jnp.where(xg == m1, iota_g, GROUP_SIZE), axis=0, keepdims=True)
        m2 = jnp.max(jnp.where(iota_g == first, neg_inf, xg), axis=0, keepdims=True)
        group_score.append(m1 + m2)
    masked = []
    for g in range(N_GROUPS):
        rank = jnp.zeros((1, tm), jnp.int32)
        for o in range(N_GROUPS):
            if o == g:
                continue
            ahead = group_score[o] > group_score[g]
            if o < g:
                ahead = ahead | (group_score[o] == group_score[g])
            rank = rank + ahead.astype(jnp.int32)
        keep = rank < TOPK_GROUPS
        masked.append(jnp.where(keep, choice[g * GROUP_SIZE:(g + 1) * GROUP_SIZE, :], neg_inf))
    cur = jnp.concatenate(masked, axis=0)
    iota_e = lax.broadcasted_iota(jnp.int32, (N_EXPERTS, tm), 0)
    sel = jnp.zeros((N_EXPERTS, tm), F32)
    picks = []
    firsts = []
    for _ in range(TOP_K):
        m = jnp.max(cur, axis=0, keepdims=True)
        first = jnp.min(jnp.where(cur == m, iota_e, N_EXPERTS), axis=0, keepdims=True)
        pick = iota_e == first
        sel = jnp.where(pick, 1.0, sel)
        cur = jnp.where(pick, neg_inf, cur)
        picks.append(pick)
        firsts.append(first)
    wsel = sel * scores
    denom = jnp.sum(wsel, axis=0, keepdims=True) + 1e-20
    gates = (ROUTED_SCALE * wsel) / denom

    upto = (lax.broadcasted_iota(jnp.int32, (tm, tm), 0) <= lax.broadcasted_iota(jnp.int32, (tm, tm), 1))
    csum = jnp.dot(sel.astype(BF16), jnp.where(upto, 1.0, 0.0).astype(BF16), preferred_element_type=F32)
    before = base_cnt + csum - sel
    iota_k = lax.broadcasted_iota(jnp.int32, (TOP_K, tm), 0)
    eidx = jnp.zeros((TOP_K, tm), jnp.int32)
    rank = jnp.zeros((TOP_K, tm), F32)
    w_k = jnp.zeros((TOP_K, tm), F32)
    for i in range(TOP_K):
        eidx = jnp.where(iota_k == i, firsts[i], eidx)
        rank = jnp.where(iota_k == i, jnp.sum(jnp.where(picks[i], before, 0.0), axis=0, keepdims=True), rank)
        w_k = jnp.where(iota_k == i, jnp.sum(jnp.where(picks[i], gates, 0.0), axis=0, keepdims=True), w_k)
    return gates, eidx, rank.astype(jnp.int32), w_k, jnp.sum(sel, axis=1, keepdims=True)


def _merge_kernel(ya_ref, yb_ref, mg_ref, x_ref, ga1_ref, sc2_ref, sh2_ref, pa_ref, pb_ref, wo_ref,
                  g1_ref, b1_ref, rwt_ref, rb_ref, x1_ref, h2_ref, gt_ref, hp_ref, eidx_ref, rank_ref, wk_ref,
                  cnt_ref, cnt_s, *, dn_alpha):
    bb, tl, d = x_ref.shape
    tm = bb * tl

    @pl.when((pl.program_id(0) == 0) & (pl.program_id(1) == 0))
    def _():
        cnt_s[...] = jnp.zeros_like(cnt_s)

    ya = ya_ref[...].reshape(tm, V_A)
    yb = yb_ref[...].reshape(tm, D_B)
    mg = mg_ref[...].reshape(tm, 2 * d)
    merged = (jax.nn.sigmoid(mg[:, :d]) * jnp.dot(ya, pa_ref[...], preferred_element_type=F32)
              + jax.nn.sigmoid(mg[:, d:]) * jnp.dot(yb, pb_ref[...], preferred_element_type=F32))
    mix = _dot(merged, wo_ref[...])
    y = dn_alpha * x_ref[...] + (1.0 + ga1_ref[...]) * mix.reshape(bb, tl, d)
    x1 = _layernorm(y, g1_ref[...], b1_ref[...])
    x1_ref[...] = x1
    h2 = (x1 * (1.0 + sc2_ref[...]) + sh2_ref[...]).astype(BF16)
    h2_ref[...] = h2
    h2f = h2.reshape(tm, d)
    hp_ref[...] = _pack_bf16_pairs(h2f.astype(F32))
    logits_t = lax.dot_general(rwt_ref[...], h2f, (_NT, ((), ())), preferred_element_type=F32)
    gates, eidx, rank, w_k, tile_cnt = _route(logits_t, rb_ref[...], cnt_s[...])
    gt_ref[...] = gates
    eidx_ref[...] = eidx
    rank_ref[...] = rank
    wk_ref[...] = w_k
    cnt_s[...] = cnt_s[...] + tile_cnt
    cnt_ref[...] = cnt_s[...].astype(jnp.int32)


def _merge(ya, yb, proj, x, ga1, sc2, sh2, p_a, p_b, w_o, ln_g, ln_b, rw_t, r_bias, bb, tl, dn_alpha, b0, nb):
    _, l, d = x.shape
    b = nb
    nj = l // tl
    tm = bb * tl
    t_all = b * l
    i0 = b0 // bb
    full = lambda *shape: pl.BlockSpec(shape, lambda i, j: (0,) * len(shape))
    mod = pl.BlockSpec((bb, 1, d), lambda i, j: (i + i0, 0, 0))
    tok_in = lambda width: pl.BlockSpec((bb, tl, width), lambda i, j: (i + i0, j, 0))
    tok = lambda width: pl.BlockSpec((bb, tl, width), lambda i, j: (i, j, 0))
    per_k = pl.BlockSpec((TOP_K, tm), lambda i, j: (0, i * nj + j))
    return pl.pallas_call(
        functools.partial(_merge_kernel, dn_alpha=dn_alpha),
        grid=(b // bb, nj),
        in_specs=[tok_in(V_A), tok_in(D_B),
                  pl.BlockSpec((bb, tl, 2 * d), lambda i, j: (i + i0, j, P_MG // (2 * d))),
                  tok_in(d), mod, mod, mod,
                  full(V_A, d), full(D_B, d), full(d, d), full(1, d), full(1, d),
                  full(N_EXPERTS, d), full(N_EXPERTS, 1)],
        out_specs=[tok(d), tok(d), pl.BlockSpec((N_EXPERTS, tm), lambda i, j: (0, i * nj + j)),
                   pl.BlockSpec((tm, d // 2), lambda i, j: (i * nj + j, 0)),
                   per_k, per_k, per_k, full(N_EXPERTS, 1)],
        out_shape=[jax.ShapeDtypeStruct((b, l, d), F32),
                   jax.ShapeDtypeStruct((b, l, d), BF16),
                   jax.ShapeDtypeStruct((N_EXPERTS, t_all), F32),
                   jax.ShapeDtypeStruct((t_all, d // 2), jnp.int32),
                   jax.ShapeDtypeStruct((TOP_K, t_all), jnp.int32),
                   jax.ShapeDtypeStruct((TOP_K, t_all), jnp.int32),
                   jax.ShapeDtypeStruct((TOP_K, t_all), F32),
                   jax.ShapeDtypeStruct((N_EXPERTS, 1), jnp.int32)],
        scratch_shapes=[pltpu.VMEM((N_EXPERTS, 1), F32)],
        compiler_params=_cparams(("arbitrary", "arbitrary"), 48),
        name="merge",
    )(ya, yb, proj, x, ga1, sc2, sh2, p_a, p_b, w_o, ln_g, ln_b, rw_t, r_bias)


def _moe_kernel(h_ref, g_ref, x1_ref, ga2_ref, wg_ref, wu_ref, wd_ref, sgu_ref, sd_ref, g2_ref, b2_ref,
                o_ref, acc_ref, *, dn_alpha):
    e = pl.program_id(2)
    bb, tl, d = h_ref.shape
    tm = bb * tl
    h = h_ref[...].reshape(tm, d)

    @pl.when(e == 0)
    def _():
        su = jnp.dot(h, sgu_ref[...], preferred_element_type=F32)
        act = _silu(su[:, :D_SHARED]) * su[:, D_SHARED:]
        acc_ref[...] = _dot(act, sd_ref[...])

    lane = lax.broadcasted_iota(jnp.int32, (tm, N_EXPERTS), 1)
    gates = g_ref[...]
    acts = []
    for i in range(EXPERTS_PER_STEP):
        g = jnp.dot(h, wg_ref[i].astype(BF16), preferred_element_type=F32)
        u = jnp.dot(h, wu_ref[i].astype(BF16), preferred_element_type=F32)
        gate = _lane_pick(gates, lane, e * EXPERTS_PER_STEP + i)
        acts.append((_silu(g) * u * gate).astype(BF16))
    wd = wd_ref[...].reshape(EXPERTS_PER_STEP * D_EXPERT, d).astype(BF16)
    acc_ref[...] += jnp.dot(jnp.concatenate(acts, axis=1), wd, preferred_element_type=F32)

    @pl.when(e == pl.num_programs(2) - 1)
    def _():
        y = dn_alpha * x1_ref[...] + (1.0 + ga2_ref[...]) * acc_ref[...].reshape(bb, tl, d)
        o_ref[...] = _layernorm(y, g2_ref[...], b2_ref[...])


def _moe(h2, gates, x1, ga2, we_gate, we_up, we_down, ws_gu, ws_down, ln_g, ln_b, bb, tl, dn_alpha):
    b, l, d = x1.shape
    nj = l // tl
    tm = bb * tl
    full = lambda *shape: pl.BlockSpec(shape, lambda i, j, e: (0,) * len(shape))
    tok = pl.BlockSpec((bb, tl, d), lambda i, j, e: (i, j, 0))
    return pl.pallas_call(
        functools.partial(_moe_kernel, dn_alpha=dn_alpha),
        grid=(b // bb, nj, N_EXPERTS // EXPERTS_PER_STEP),
        in_specs=[tok,
                  pl.BlockSpec((tm, N_EXPERTS), lambda i, j, e: (i * nj + j, 0)),
                  tok,
                  pl.BlockSpec((bb, 1, d), lambda i, j, e: (i, 0, 0)),
                  pl.BlockSpec((EXPERTS_PER_STEP, d, D_EXPERT), lambda i, j, e: (e, 0, 0)),
                  pl.BlockSpec((EXPERTS_PER_STEP, d, D_EXPERT), lambda i, j, e: (e, 0, 0)),
                  pl.BlockSpec((EXPERTS_PER_STEP, D_EXPERT, d), lambda i, j, e: (e, 0, 0)),
                  full(d, 2 * D_SHARED), full(D_SHARED, d), full(1, d), full(1, d)],
        out_specs=tok,
        out_shape=jax.ShapeDtypeStruct((b, l, d), F32),
        scratch_shapes=[pltpu.VMEM((tm, d), F32)],
        compiler_params=_cparams(("parallel", "parallel", "arbitrary"), 56),
        name="moe",
    )(h2, gates, x1, ga2, we_gate, we_up, we_down, ws_gu, ws_down, ln_g, ln_b)


def _sc_mesh():
    return plsc.VectorSubcoreMesh(core_axis_name="c", subcore_axis_name="s")


def _sc_cores():
    info = plsc.get_sparse_core_info()
    assert info.num_cores * info.num_subcores == SC_WORKERS, info
    return info.num_cores


def _sc_scatter_rows(x, pos, n_out):
    t, w = x.shape
    k = pos.shape[0]
    t_per_w = t // SC_WORKERS
    n_cores = _sc_cores()

    @functools.partial(
        pl.kernel, mesh=_sc_mesh(),
        out_type=jax.ShapeDtypeStruct((n_out, w), jnp.int32),
        scratch_types=[pltpu.VMEM((k, SC_CHUNK), jnp.int32), pltpu.VMEM((SC_CHUNK, w), jnp.int32),
                       pltpu.SemaphoreType.DMA],
    )
    def scatter_kernel(x_hbm, pos_hbm, out_hbm, idx_v, rows_v, sem):
        base = (lax.axis_index("s") * n_cores + lax.axis_index("c")) * t_per_w

        @pl.loop(0, t_per_w // SC_CHUNK)
        def _(i):
            off = pl.multiple_of(base + i * SC_CHUNK, SC_CHUNK)
            pltpu.sync_copy(pos_hbm.at[:, pl.ds(off, SC_CHUNK)], idx_v)
            pltpu.sync_copy(x_hbm.at[pl.ds(off, SC_CHUNK)], rows_v)
            for j in range(k):
                pltpu.async_copy(rows_v, out_hbm.at[idx_v.at[j]], sem).wait()

    return scatter_kernel(x, pos)


def _sc_gather_rows(table, idx):
    n = idx.shape[0]
    w = table.shape[1]
    n_per_w = n // SC_WORKERS
    n_cores = _sc_cores()

    half = SC_CHUNK // 2

    @functools.partial(
        pl.kernel, mesh=_sc_mesh(),
        out_type=jax.ShapeDtypeStruct((n, w), jnp.int32),
        scratch_types=[pltpu.VMEM((2, half), jnp.int32), pltpu.VMEM((2, half, w), jnp.int32)]
                      + [pltpu.SemaphoreType.DMA] * 4,
    )
    def gather_kernel(table_hbm, idx_hbm, out_hbm, idx_v, rows_v, sem_g0, sem_g1, sem_w0, sem_w1):
        base = (lax.axis_index("s") * n_cores + lax.axis_index("c")) * n_per_w
        sem_g = (sem_g0, sem_g1)
        sem_w = (sem_w0, sem_w1)

        @pl.loop(0, n_per_w // SC_CHUNK)
        def _(i):
            off = pl.multiple_of(base + i * SC_CHUNK, SC_CHUNK)
            for h in range(2):
                pltpu.sync_copy(idx_hbm.at[pl.ds(off + h * half, half)], idx_v.at[h])
            gathers = [pltpu.async_copy(table_hbm.at[idx_v.at[h]], rows_v.at[h], sem_g[h]) for h in range(2)]
            writes = []
            for h in range(2):
                gathers[h].wait()
                writes.append(pltpu.async_copy(rows_v.at[h], out_hbm.at[pl.ds(off + h * half, half)], sem_w[h]))
            for c in writes:
                c.wait()

    return gather_kernel(table, idx)


def _experts_kernel(te_ref, used_ref, xs_ref, wg_ref, wu_ref, wd_ref, o_ref, wg_s, wu_s, wd_s):
    i = pl.program_id(0)

    @pl.when((i == 0) | (te_ref[i] != te_ref[jnp.maximum(i - 1, 0)]))
    def _():
        wg_s[...] = wg_ref[0].astype(BF16)
        wu_s[...] = wu_ref[0].astype(BF16)
        wd_s[...] = wd_ref[0].astype(BF16)

    @pl.when(i < used_ref[0])
    def _():
        x = _unpack_bf16_pairs(xs_ref[...]).astype(BF16)
        g = jnp.dot(x, wg_s[...], preferred_element_type=F32)
        u = jnp.dot(x, wu_s[...], preferred_element_type=F32)
        act = (_silu(g) * u).astype(BF16)
        o_ref[...] = _pack_bf16_pairs(jnp.dot(act, wd_s[...], preferred_element_type=F32))


def _experts(xs, tile_expert, tiles_used, we_gate, we_up, we_down):
    r, half = xs.shape
    d = 2 * half
    row_tile = lambda i, te, used: (jnp.minimum(i, used[0] - 1), 0)
    grid_spec = pltpu.PrefetchScalarGridSpec(
        num_scalar_prefetch=2,
        grid=(r // EXPERT_TILE,),
        in_specs=[pl.BlockSpec((EXPERT_TILE, half), row_tile),
                  pl.BlockSpec((1, d, D_EXPERT), lambda i, te, used: (te[i], 0, 0)),
                  pl.BlockSpec((1, d, D_EXPERT), lambda i, te, used: (te[i], 0, 0)),
                  pl.BlockSpec((1, D_EXPERT, d), lambda i, te, used: (te[i], 0, 0))],
        out_specs=pl.BlockSpec((EXPERT_TILE, half), row_tile),
        scratch_shapes=[pltpu.VMEM((d, D_EXPERT), BF16), pltpu.VMEM((d, D_EXPERT), BF16),
                        pltpu.VMEM((D_EXPERT, d), BF16)],
    )
    return pl.pallas_call(
        _experts_kernel,
        grid_spec=grid_spec,
        out_shape=jax.ShapeDtypeStruct((r, half), jnp.int32),
        compiler_params=_cparams(("arbitrary",), 32),
        name="experts",
    )(tile_expert, tiles_used, xs, we_gate, we_up, we_down)


def _combine_kernel(og_ref, wk_ref, h_ref, x1_ref, ga2_ref, sgu_ref, sd_ref, g2_ref, b2_ref, *rest, dn_alpha):
    o_ref = rest[-1]
    bb, tl, d = h_ref.shape
    tm = bb * tl
    h = h_ref[...].reshape(tm, d)
    su = jnp.dot(h, sgu_ref[...], preferred_element_type=F32)
    acc = _dot(_silu(su[:, :D_SHARED]) * su[:, D_SHARED:], sd_ref[...])
    lane = lax.broadcasted_iota(jnp.int32, (tm, TOP_K), 1)
    w_all = wk_ref[...]
    for k in range(TOP_K):
        acc = acc + _lane_pick(w_all, lane, k) * _unpack_bf16_pairs(og_ref[k])
    y = dn_alpha * x1_ref[...] + (1.0 + ga2_ref[...]) * acc.reshape(bb, tl, d)
    o_ref[...] = _layernorm(y, g2_ref[...], b2_ref[...])


def _combine(og, w_tk, h2, x1, ga2, ws_gu, ws_down, ln_g, ln_b, bb, tl, dn_alpha, b_total, b0, prev):
    nb, l, d = x1.shape
    nj = l // tl
    tm = bb * tl
    i0 = b0 // bb
    full = lambda *shape: pl.BlockSpec(shape, lambda i, j: (0,) * len(shape))
    tok = pl.BlockSpec((bb, tl, d), lambda i, j: (i, j, 0))
    in_specs = [pl.BlockSpec((TOP_K, tm, d // 2), lambda i, j: (0, i * nj + j, 0)),
                pl.BlockSpec((tm, TOP_K), lambda i, j: (i * nj + j, 0)),
                tok, tok,
                pl.BlockSpec((bb, 1, d), lambda i, j: (i + i0, 0, 0)),
                full(d, 2 * D_SHARED), full(D_SHARED, d), full(1, d), full(1, d)]
    args = [og, w_tk, h2, x1, ga2, ws_gu, ws_down, ln_g, ln_b]
    aliases = {}
    if prev is not None:
        in_specs.append(pl.BlockSpec(memory_space=pl.ANY))
        args.append(prev)
        aliases = {len(args) - 1: 0}
    return pl.pallas_call(
        functools.partial(_combine_kernel, dn_alpha=dn_alpha),
        grid=(nb // bb, nj),
        in_specs=in_specs,
        out_specs=pl.BlockSpec((bb, tl, d), lambda i, j: (i + i0, j, 0)),
        out_shape=jax.ShapeDtypeStruct((b_total, l, d), F32),
        input_output_aliases=aliases,
        compiler_params=_cparams(("parallel", "parallel"), 48),
        name="combine",
    )(*args)


def _moe_sparse(hp, eidx, rank, w_k, cnt, h2, x1, ga2, p, bb, dn_alpha, b_total, b0, prev):
    t_all = hp.shape[0]
    n_tiles = t_all * TOP_K // EXPERT_TILE + N_EXPERTS
    seg_tiles = (cnt[:, 0] + EXPERT_TILE - 1) // EXPERT_TILE
    seg_end = jnp.cumsum(seg_tiles)
    seg_start = (seg_end - seg_tiles) * EXPERT_TILE
    experts = jnp.arange(N_EXPERTS, dtype=jnp.int32)
    pos = rank + jnp.sum(jnp.where(eidx[None] == experts[:, None, None], seg_start[:, None, None], 0), axis=0)
    tiles = jnp.arange(n_tiles, dtype=jnp.int32)
    tile_expert = jnp.minimum(jnp.sum((seg_end[None, :] <= tiles[:, None]).astype(jnp.int32), axis=1),
                              N_EXPERTS - 1)
    xs = _sc_scatter_rows(hp, pos, n_tiles * EXPERT_TILE)
    outs = _experts(xs, tile_expert, seg_end[-1:].astype(jnp.int32), p['we_gate'], p['we_up'], p['we_down'])
    og = _sc_gather_rows(outs, pos.reshape(-1)).reshape(TOP_K, t_all, hp.shape[1])
    return _combine(og, w_k.T, h2, x1, ga2, p['ws_gu'], p['ws_down'], p['ln2_g'], p['ln2_b'], bb, COMBINE_TL, dn_alpha,
                    b_total, b0, prev)


def _pad_rows(buf, width):
    b, n, w = buf.shape
    return jnp.pad(buf.astype(F32), ((0, 0), (SUBLANES - n, 0), (0, width - w)))


def _to_pairs(s):
    b = s.shape[0]
    s = s.astype(F32).reshape(b, PAIRS, 2, N_B, N_B)
    zero = jnp.zeros_like(s[:, :, 0])
    top = jnp.concatenate([s[:, :, 0], zero], axis=-1)
    bot = jnp.concatenate([zero, s[:, :, 1]], axis=-1)
    return jnp.concatenate([top, bot], axis=-2)


def _from_pairs(sp):
    b = sp.shape[0]
    return jnp.stack([sp[:, :, :N_B, :N_B], sp[:, :, N_B:, N_B:]], axis=2).reshape(b, H_B, N_B, N_B)


def _layer(x, mod, conv_buf, gdn_s, shift_buf, rwkv_s, p, tiles, dn_alpha):
    b, l, d = x.shape
    sh1, sc1, ga1, sh2, sc2, ga2 = mod
    proj = _inproj(x, sc1, sh1, p['w_in'], tiles['bb'], tiles['tl'])
    ya, gdn_new = _gdn(proj, _pad_rows(conv_buf, QKV_A), gdn_s.astype(F32), p['conv_w'], p['alog_row'],
                       p['dt_row'], p['gdn_norm_w'], tiles['bb_gdn'], tiles['tl_gdn'], l)
    yb, rwkv_pairs = _rwkv(proj, _pad_rows(shift_buf, RW_BLOCK), _to_pairs(rwkv_s), p['rwkv_vecs'],
                           p['w2p'], p['a2p'], p['g2'], tiles['bb_rwkv'], tiles['tl_rwkv'], l)

    merge = functools.partial(_merge, ya, yb, proj, x, ga1, sc2, sh2, p['p_a'], p['p_b'], p['w_o'], p['ln1_g'],
                              p['ln1_b'], p['router_wt'], p['router_bias'], tiles['bb'], tiles['tl_merge'], dn_alpha)
    nb = b // MOE_PARTS
    if b % MOE_PARTS == 0 and (nb * l) % (SC_WORKERS * SC_CHUNK) == 0:
        out = None
        for part in range(MOE_PARTS):
            x1, h2, _, hp, eidx, rank, w_k, cnt = merge(part * nb, nb)
            out = _moe_sparse(hp, eidx, rank, w_k, cnt, h2, x1, ga2, p, tiles['bb'], dn_alpha, b, part * nb, out)
    else:
        x1, h2, gates_t, _, _, _, _, _ = merge(0, b)
        out = _moe(h2, gates_t.T, x1, ga2, p['we_gate'], p['we_up'], p['we_down'], p['ws_gu'], p['ws_down'],
                   p['ln2_g'], p['ln2_b'], tiles['bb'], tiles['tl'], dn_alpha)

    pre = jnp.concatenate([conv_buf.astype(F32), proj[:, :, P_QKV:P_QKV + QKV_A]], axis=1)
    conv_new = pre[:, -(CONV_W - 1):]
    shift_new = proj[:, l - 1:l, P_RW:P_RW + SHIFT_W]
    return (out, conv_new.astype(conv_buf.dtype), gdn_new.astype(gdn_s.dtype),
            shift_new.astype(shift_buf.dtype), _from_pairs(rwkv_pairs).astype(rwkv_s.dtype))


def _prep_params(l, w_in, conv_w, a_log, dt_bias, gdn_norm_w, mu_shift, w0, w2, a0, a2, g2, k_k, k_a, r_k,
                 lnx_g, lnx_b, p_a, p_b, w_o, ln1_g, ln1_b, router_w, router_bias, we_gate, we_up, we_down,
                 ws_gate, ws_up, ws_down, ln2_g, ln2_b):
    d = D_MODEL
    w = w_in[l]
    w_p = jnp.concatenate(
        [w[:, :QKV_A], w[:, OFF_Z:OFF_RWKV], w[:, OFF_RWKV:OFF_MERGE],
         jnp.zeros((d, RW_BLOCK - SHIFT_W), w.dtype), w[:, OFF_MERGE:], w[:, OFF_ALPHA:OFF_Z],
         jnp.zeros((d, N_PROJ - P_AB - 2 * H_A), w.dtype)], axis=1).astype(BF16)
    row = lambda v, width: jnp.pad(v.astype(F32).reshape(1, -1), ((0, 0), (0, width - v.size)))
    zeros_lora = jnp.zeros((W_LORA, D_B), F32)
    return {
        'w_in': w_p,
        'conv_w': conv_w[l].astype(F32),
        'alog_row': row(a_log[l], LANES),
        'dt_row': row(dt_bias[l], LANES),
        'gdn_norm_w': row(gdn_norm_w[l], DK_A),
        'rwkv_vecs': (row(mu_shift[l], RW_BLOCK), row(w0[l], D_B), row(a0[l], D_B), row(k_k[l], D_B),
                      row(k_a[l], D_B), row(r_k[l], D_B), row(lnx_g[l], D_B), row(lnx_b[l], D_B)),
        'w2p': jnp.concatenate([w2[l].astype(F32), zeros_lora], axis=0),
        'a2p': jnp.concatenate([zeros_lora, a2[l].astype(F32)], axis=0),
        'g2': g2[l].astype(F32),
        'p_a': p_a[l].astype(BF16), 'p_b': p_b[l].astype(BF16), 'w_o': w_o[l].astype(BF16),
        'ln1_g': row(ln1_g[l], d), 'ln1_b': row(ln1_b[l], d),
        'router_wt': router_w[l].T.astype(BF16),
        'router_bias': router_bias[l].astype(F32).reshape(N_EXPERTS, 1),
        'we_gate': we_gate[l], 'we_up': we_up[l], 'we_down': we_down[l],
        'ws_gu': jnp.concatenate([ws_gate[l], ws_up[l]], axis=-1).astype(BF16),
        'ws_down': ws_down[l].astype(BF16),
        'ln2_g': row(ln2_g[l], d), 'ln2_b': row(ln2_b[l], d),
    }


def _tiles(b, l):
    lp = -(-l // CHUNK) * CHUNK
    if l >= TOKEN_TILE:
        bb, tl, tl_merge = 1, TOKEN_TILE, TOKEN_TILE // 2
    else:
        bb, tl, tl_merge = b, l, l
    return {'bb': bb, 'tl': tl, 'tl_merge': tl_merge,
            'bb_gdn': 4, 'tl_gdn': CHUNK, 'bb_rwkv': 2, 'tl_rwkv': min(2 * CHUNK, lp)}


def kernel(x_prompt, x_sample, c_prompt, c_sample, state_gdn_conv, state_gdn, state_rwkv_shift, state_rwkv, w_ada, b_ada, w_in, conv_w, a_log, dt_bias, gdn_norm_w, mu_shift, w0, w2, a0, a2, g2, k_k, k_a, r_k, lnx_g, lnx_b, p_a, p_b, w_o, ln1_g, ln1_b, router_w, router_bias, we_gate, we_up, we_down, ws_gate, ws_up, ws_down, ln2_g, ln2_b):
    depth = w_ada.shape[0]
    dn_alpha = (2 * depth) ** 0.25
    bp, lp_, d = x_prompt.shape
    bs, ls, _ = x_sample.shape
    dtp = x_prompt.dtype
    tiles_p = _tiles(bp, lp_)
    tiles_s = _tiles(bs, ls)

    yp, ys = x_prompt, x_sample
    new_p = ([], [], [], [])
    new_s = ([], [], [], [])
    for l in range(depth):
        p = _prep_params(l, w_in, conv_w, a_log, dt_bias, gdn_norm_w, mu_shift, w0, w2, a0, a2, g2, k_k, k_a,
                         r_k, lnx_g, lnx_b, p_a, p_b, w_o, ln1_g, ln1_b, router_w, router_bias, we_gate, we_up,
                         we_down, ws_gate, ws_up, ws_down, ln2_g, ln2_b)
        mod = _ada(jnp.concatenate([c_prompt, c_sample], axis=0), w_ada[l], b_ada[l])
        mod_p = tuple(m[:, None, :] for m in jnp.split(mod[:bp], 6, axis=-1))
        mod_s = tuple(m[:, None, :] for m in jnp.split(mod[bp:], 6, axis=-1))
        yp, *sp = _layer(yp, mod_p,
                         jnp.zeros((bp, CONV_W - 1, QKV_A), dtp), jnp.zeros((bp, H_A, DK_A, DK_A), dtp),
                         jnp.zeros((bp, 1, SHIFT_W), dtp), jnp.zeros((bp, H_B, N_B, N_B), dtp),
                         p, tiles_p, dn_alpha)
        ys, *ss = _layer(ys, mod_s, state_gdn_conv[l], state_gdn[l], state_rwkv_shift[l], state_rwkv[l],
                         p, tiles_s, dn_alpha)
        for lst, val in zip(new_p, sp):
            lst.append(val)
        for lst, val in zip(new_s, ss):
            lst.append(val)
    conv_p, gdn_p, shift_p, rwkv_p = [jnp.stack(t, 0) for t in new_p]
    conv_s, gdn_s, shift_s, rwkv_s = [jnp.stack(t, 0) for t in new_s]
    return (yp, ys, conv_p, gdn_p, shift_p, rwkv_p, conv_s, gdn_s, shift_s, rwkv_s)
```

```python
import functools
import math

import jax
import jax.numpy as jnp
from jax import lax
from jax.experimental import pallas as pl
from jax.experimental.pallas import tpu as pltpu
from jax.experimental.pallas import tpu_sc as plsc

F32 = jnp.float32
BF16 = jnp.bfloat16

D_MODEL = 1024
DK_A = 128
H_A = 4
QK_A = H_A * DK_A
V_A = H_A * DK_A
QKV_A = 2 * QK_A + V_A
CONV_W = 4
N_B = 64
H_B = 8
D_B = H_B * N_B
W_LORA = 64
A_LORA = 64
G_LORA = 128
SHIFT_W = 3 * D_B + W_LORA + A_LORA + G_LORA
OFF_ALPHA = QKV_A
OFF_BETA = OFF_ALPHA + H_A
OFF_Z = OFF_BETA + H_A
OFF_RWKV = OFF_Z + V_A
OFF_MERGE = OFF_RWKV + SHIFT_W
N_EXPERTS = 64
TOP_K = 8
N_GROUPS = 8
GROUP_SIZE = N_EXPERTS // N_GROUPS
TOPK_GROUPS = 4
D_EXPERT = 256
D_SHARED = 256
ROUTED_SCALE = 2.5
LN_EPS = 1e-5
GN_EPS = 64e-5
RMS_EPS = 1e-6
DECAY_SCALE = -math.exp(-0.5)

SUBLANES = 8
LANES = 128

P_QKV = 0
P_Z = QKV_A
P_RW = 2048
RW_BLOCK = 2048
P_MG = P_RW + RW_BLOCK
P_AB = P_MG + 2 * D_MODEL
AB_BLOCK = LANES
PROJ_TN = 1280
N_PROJ = 5 * PROJ_TN

CHUNK = 128
BASE_BLOCK = 8
CUMSUM_PARTS = 3
STAT_PARTS = 2
EXPERTS_PER_STEP = 2
HIGH_HALF = -65536

EXPERT_TILE = 1024
SC_WORKERS = 32
SC_CHUNK = 128
COMBINE_TL = 512
MERGE_TL = 512
TOKEN_TILE = 2048


def _cparams(sem, vmem_mb):
    return pltpu.CompilerParams(dimension_semantics=sem, vmem_limit_bytes=vmem_mb * 1024 * 1024)


def _dot(a, b):
    return jnp.dot(a.astype(BF16), b.astype(BF16), preferred_element_type=F32)


def _rdot(a, b, dims=((1,), (0,))):
    return lax.dot_general(a.astype(BF16), b.astype(BF16), (dims, ((), ())), preferred_element_type=F32)


_NT = ((1,), (1,))
_TN = ((0,), (0,))


def _bf16_parts(x, parts):
    out = []
    rem = x
    for _ in range(parts):
        hi = rem.astype(BF16)
        out.append(hi)
        rem = rem - hi.astype(F32)
    return out


def _mask_dot_left(mask, x, parts):
    return sum(jnp.dot(mask, p, preferred_element_type=F32) for p in _bf16_parts(x, parts))


def _mask_dot_right(x, mask, parts):
    return sum(jnp.dot(p, mask, preferred_element_type=F32) for p in _bf16_parts(x, parts))


def _lane_pick(x, lane_iota, lane):
    return jnp.sum(jnp.where(lane_iota == lane, x, 0.0), axis=-1, keepdims=True)


def _pack_bf16_pairs(x):
    n = x.shape[1] // 2
    bits = lax.bitcast_convert_type(x.astype(BF16).astype(F32), jnp.int32)
    return (bits[:, :n] & HIGH_HALF) | lax.shift_right_logical(bits[:, n:], 16)


def _unpack_bf16_pairs(p):
    hi = lax.bitcast_convert_type(p & HIGH_HALF, F32)
    lo = lax.bitcast_convert_type(lax.shift_left(p, 16), F32)
    return jnp.concatenate([hi, lo], axis=1)


def _silu(x):
    return x * jax.nn.sigmoid(x)


def _softplus(x):
    return jnp.maximum(x, 0.0) + jnp.log1p(jnp.exp(-jnp.abs(x)))


def _tri_inverse(mats, eye, sign, row, col):
    base = (row // BASE_BLOCK) == (col // BASE_BLOCK)
    ds = [jnp.where(base, a, 0.0) for a in mats]
    xs = [eye + d if sign > 0 else eye - d for d in ds]
    power = 2
    while power < BASE_BLOCK:
        ds = [_rdot(d, d) for d in ds]
        xs = [x + _rdot(x, d) for x, d in zip(xs, ds)]
        power *= 2
    b = BASE_BLOCK
    while b < CHUNK:
        sibling = ((row // b) == (col // b) + 1) & ((row // (2 * b)) == (col // (2 * b)))
        offs = [jnp.where(sibling, a, 0.0) for a in mats]
        ts = [_rdot(o, x) for o, x in zip(offs, xs)]
        if sign > 0:
            xs = [x + _rdot(x, t_) for x, t_ in zip(xs, ts)]
        else:
            xs = [x - _rdot(x, t_) for x, t_ in zip(xs, ts)]
        b *= 2
    return xs


def _tri_consts():
    row = lax.broadcasted_iota(jnp.int32, (CHUNK, CHUNK), 0)
    col = lax.broadcasted_iota(jnp.int32, (CHUNK, CHUNK), 1)
    incl = row >= col
    strict = row > col
    eye = jnp.where(row == col, 1.0, 0.0).astype(F32)
    tril = jnp.where(incl, 1.0, 0.0).astype(BF16)
    return row, col, incl, strict, eye, tril


def _ada_kernel(c_ref, w_ref, b_ref, o_ref):
    o_ref[...] = _dot(_silu(c_ref[...]), w_ref[...]) + b_ref[...]


def _ada(c, w_ada, b_ada):
    n, d = c.shape
    nout = w_ada.shape[1]
    tn = 768
    return pl.pallas_call(
        _ada_kernel,
        grid=(nout // tn,),
        in_specs=[pl.BlockSpec((n, d), lambda j: (0, 0)),
                  pl.BlockSpec((d, tn), lambda j: (0, j)),
                  pl.BlockSpec((1, tn), lambda j: (0, j))],
        out_specs=pl.BlockSpec((n, tn), lambda j: (0, j)),
        out_shape=jax.ShapeDtypeStruct((n, nout), F32),
        compiler_params=_cparams(("parallel",), 32),
        name="ada",
    )(c, w_ada, b_ada.reshape(1, nout))


def _inproj_kernel(x_ref, sc_ref, sh_ref, w_ref, o_ref, hs_ref):
    @pl.when(pl.program_id(2) == 0)
    def _():
        h = x_ref[...] * (1.0 + sc_ref[...]) + sh_ref[...]
        hs_ref[...] = h.reshape(hs_ref.shape).astype(BF16)

    o = jnp.dot(hs_ref[...], w_ref[...], preferred_element_type=F32)
    o_ref[...] = o.reshape(o_ref.shape)


def _inproj(x, sc, sh, w_p, bb, tl):
    b, l, d = x.shape
    return pl.pallas_call(
        _inproj_kernel,
        grid=(b // bb, l // tl, N_PROJ // PROJ_TN),
        in_specs=[pl.BlockSpec((bb, tl, d), lambda i, j, n: (i, j, 0)),
                  pl.BlockSpec((bb, 1, d), lambda i, j, n: (i, 0, 0)),
                  pl.BlockSpec((bb, 1, d), lambda i, j, n: (i, 0, 0)),
                  pl.BlockSpec((d, PROJ_TN), lambda i, j, n: (0, n))],
        out_specs=pl.BlockSpec((bb, tl, PROJ_TN), lambda i, j, n: (i, j, n)),
        out_shape=jax.ShapeDtypeStruct((b, l, N_PROJ), F32),
        scratch_shapes=[pltpu.VMEM((bb * tl, d), BF16)],
        compiler_params=_cparams(("parallel", "parallel", "arbitrary"), 58),
        name="inproj",
    )(x, sc, sh, w_p)


def _gdn_kernel(qkv_ref, ab_ref, z_ref, cbuf_ref, s0_ref, cw_ref, alog_ref, dt_ref, nw_ref,
                ya_ref, sout_ref, ext_ref, qc_ref, s_ref, *pad_s, bb, tl, l_in, l_valid, l_total):
    t = pl.program_id(1)

    @pl.when(t == 0)
    def _():
        ext_ref[:, 0:SUBLANES, :] = cbuf_ref[...]
        s_ref[...] = s0_ref[...]

    if l_in < tl:
        for src, dst in zip((qkv_ref, ab_ref, z_ref), pad_s):
            dst[:, l_in:, :] = jnp.zeros((bb, tl - l_in, dst.shape[2]), F32)
            dst[:, 0:l_in, :] = src[...]
        qkv_ref, ab_ref, z_ref = pad_s

    for bi in range(bb):
        cur = qkv_ref[bi]
        ext_ref[bi, SUBLANES:SUBLANES + tl, :] = cur
        window = ext_ref[bi]
        acc = cw_ref[CONV_W - 1:CONV_W, :] * cur
        for j in range(CONV_W - 1):
            back = CONV_W - 1 - j
            acc = acc + cw_ref[j:j + 1, :] * pltpu.roll(window, tl + back, 0)[0:tl]
        ext_ref[bi, 0:SUBLANES, :] = ext_ref[bi, tl:tl + SUBLANES, :]
        qc_ref[bi] = _silu(acc)

    row, col, incl, strict, eye, tril = _tri_consts()
    neg_a = -jnp.exp(alog_ref[...])
    dt = dt_ref[...]
    nw = nw_ref[...]
    chains = [(bi, h) for bi in range(bb) for h in range(H_A)]
    n = range(len(chains))

    def chunk(c, carry):
        r0 = pl.multiple_of(c * CHUNK, CHUNK)
        rows = pl.ds(r0, CHUNK)
        g_cum = []
        b_all = []
        for bi in range(bb):
            ab = ab_ref[bi, rows, :]
            g_bi = neg_a * _softplus(ab + dt)
            b_bi = jax.nn.sigmoid(ab)
            if l_valid < l_total:
                valid = (t * tl + r0 + row) < l_valid
                g_bi = jnp.where(valid, g_bi, 0.0)
                b_bi = jnp.where(valid, b_bi, 0.0)
            g_cum.append(_mask_dot_left(tril, g_bi, CUMSUM_PARTS))
            b_all.append(b_bi)
        g_col = [_lane_pick(g_cum[bi], col, h) for bi, h in chains]
        beta = [_lane_pick(b_all[bi], col, H_A + h) for bi, h in chains]
        decay = []
        for i in n:
            g_b = jnp.broadcast_to(g_col[i], (CHUNK, CHUNK))
            decay.append(jnp.where(incl, jnp.exp(jnp.where(incl, g_b - g_b.T, 0.0)), 0.0))
        q = [qc_ref[bi, rows, h * DK_A:(h + 1) * DK_A] for bi, h in chains]
        k = [qc_ref[bi, rows, QK_A + h * DK_A:QK_A + (h + 1) * DK_A] for bi, h in chains]
        v = [qc_ref[bi, rows, 2 * QK_A + h * DK_A:2 * QK_A + (h + 1) * DK_A] for bi, h in chains]
        q = [x * lax.rsqrt(jnp.sum(x * x, -1, keepdims=True) + 1e-6) * (DK_A ** -0.5) for x in q]
        k = [x * lax.rsqrt(jnp.sum(x * x, -1, keepdims=True) + 1e-6) for x in k]
        kq = [_rdot(jnp.concatenate([k[i], q[i]], axis=0), k[i], _NT) for i in n]
        a = [jnp.where(strict, beta[i] * kq[i][:CHUNK] * decay[i], 0.0) for i in n]
        qk = [jnp.where(incl, kq[i][CHUNK:] * decay[i], 0.0) for i in n]
        x = _tri_inverse(a, eye, -1, row, col)
        e_g = [jnp.exp(g) for g in g_col]
        uw = [_rdot(x[i], jnp.concatenate([v[i] * beta[i], k[i] * (beta[i] * e_g[i])], axis=1)) for i in n]
        g_last = [g[CHUNK - 1:CHUNK, :] for g in g_col]
        kd = [k[i] * jnp.exp(g_last[i] - g_col[i]) for i in n]
        s = [s_ref[bi, h] for bi, h in chains]
        ws = [_rdot(jnp.concatenate([uw[i][:, DK_A:], q[i] * e_g[i]], axis=0), s[i]) for i in n]
        v_new = [uw[i][:, :DK_A] - ws[i][:CHUNK] for i in n]
        o = [ws[i][CHUNK:] + _rdot(qk[i], v_new[i]) for i in n]
        s_new = [s[i] * jnp.exp(g_last[i]) + _rdot(kd[i], v_new[i], _TN) for i in n]
        for i, (bi, h) in enumerate(chains):
            s_ref[bi, h] = s_new[i]
            z = z_ref[bi, rows, h * DK_A:(h + 1) * DK_A]
            on = o[i] * lax.rsqrt(jnp.mean(o[i] * o[i], -1, keepdims=True) + RMS_EPS) * nw * _silu(z)
            if l_in < tl:
                ya_ref[bi, :, h * DK_A:(h + 1) * DK_A] = on[0:l_in].astype(BF16)
            else:
                ya_ref[bi, rows, h * DK_A:(h + 1) * DK_A] = on.astype(BF16)
        return carry

    lax.fori_loop(0, tl // CHUNK, chunk, 0)

    @pl.when(t == pl.num_programs(1) - 1)
    def _():
        sout_ref[...] = s_ref[...]


def _rec_rows(l, tl):
    if l % tl == 0:
        return l, tl
    assert l < tl == CHUNK, (l, tl)
    return tl, l


def _gdn(proj, conv_buf8, s0, conv_w, alog_row, dt_row, norm_w, bb, tl, l_valid):
    b, l, _ = proj.shape
    lp, l_in = _rec_rows(l, tl)
    kern = functools.partial(_gdn_kernel, bb=bb, tl=tl, l_in=l_in, l_valid=l_valid, l_total=lp)
    full = lambda *shape: pl.BlockSpec(shape, lambda i, t: (0,) * len(shape))
    pad_scratch = [pltpu.VMEM((bb, tl, w), F32) for w in (QKV_A, AB_BLOCK, V_A)] if l_in < tl else []
    return pl.pallas_call(
        kern,
        grid=(b // bb, lp // tl),
        in_specs=[pl.BlockSpec((bb, l_in, QKV_A), lambda i, t: (i, t, P_QKV // QKV_A)),
                  pl.BlockSpec((bb, l_in, AB_BLOCK), lambda i, t: (i, t, P_AB // AB_BLOCK)),
                  pl.BlockSpec((bb, l_in, V_A), lambda i, t: (i, t, P_Z // V_A)),
                  pl.BlockSpec((bb, SUBLANES, QKV_A), lambda i, t: (i, 0, 0)),
                  pl.BlockSpec((bb, H_A, DK_A, DK_A), lambda i, t: (i, 0, 0, 0)),
                  full(CONV_W, QKV_A), full(1, LANES), full(1, LANES), full(1, DK_A)],
        out_specs=[pl.BlockSpec((bb, l_in, V_A), lambda i, t: (i, t, 0)),
                   pl.BlockSpec((bb, H_A, DK_A, DK_A), lambda i, t: (i, 0, 0, 0))],
        out_shape=[jax.ShapeDtypeStruct((b, l, V_A), BF16),
                   jax.ShapeDtypeStruct((b, H_A, DK_A, DK_A), F32)],
        scratch_shapes=[pltpu.VMEM((bb, tl + SUBLANES, QKV_A), F32),
                        pltpu.VMEM((bb, tl, QKV_A), F32),
                        pltpu.VMEM((bb, H_A, DK_A, DK_A), F32)] + pad_scratch,
        compiler_params=_cparams(("parallel", "arbitrary"), 48),
        name="gdn",
    )(proj, proj, proj, conv_buf8, s0, conv_w, alog_row, dt_row, norm_w)


PAIRS = H_B // 2


def _headsum(x, bd):
    return jnp.concatenate(
        [_mask_dot_right(x[:, p * LANES:(p + 1) * LANES], bd, STAT_PARTS) for p in range(PAIRS)], axis=1)


def _rwkv_kernel(rw_ref, sbuf_ref, s0_ref, mu_ref, w0_ref, a0_ref, kk_ref, ka_ref, rk_ref, lg_ref, lb_ref,
                 w2_ref, a2_ref, g2_ref, yb_ref, sout_ref,
                 ext_ref, r_s, k_s, v_s, z_s, p_s, lw_s, y_s, bonus_s, gate_s, s_ref, *,
                 bb, tl, l_in, l_valid, l_total):
    t = pl.program_id(1)

    @pl.when(t == 0)
    def _():
        ext_ref[:, 0:SUBLANES, :] = sbuf_ref[...]
        s_ref[...] = s0_ref[...]

    row, col, incl, strict, eye, tril = _tri_consts()
    same_head = (row // N_B) == (col // N_B)
    bd = jnp.where(same_head, 1.0, 0.0).astype(BF16)
    lane_a = col < N_B

    for bi in range(bb):
        if l_in < tl:
            ext_ref[bi, SUBLANES + l_in:SUBLANES + tl, :] = jnp.zeros((tl - l_in, RW_BLOCK), F32)
            ext_ref[bi, SUBLANES:SUBLANES + l_in, :] = rw_ref[bi]
            cur = ext_ref[bi, SUBLANES:SUBLANES + tl, :]
        else:
            cur = rw_ref[bi]
            ext_ref[bi, SUBLANES:SUBLANES + tl, :] = cur
        prev = pltpu.roll(ext_ref[bi], tl + 1, 0)[0:tl]
        mixed = cur + (prev - cur) * mu_ref[...]
        ext_ref[bi, 0:SUBLANES, :] = ext_ref[bi, tl:tl + SUBLANES, :]
        r = mixed[:, 0:D_B]
        k = mixed[:, D_B:2 * D_B]
        v = mixed[:, 2 * D_B:3 * D_B]
        lora = mixed[:, 3 * D_B:3 * D_B + W_LORA + A_LORA]
        g_in = mixed[:, 3 * D_B + W_LORA + A_LORA:SHIFT_W]
        lw = DECAY_SCALE * jax.nn.sigmoid(w0_ref[...] + _dot(jnp.tanh(lora), w2_ref[...]))
        a = jax.nn.sigmoid(a0_ref[...] + _dot(lora, a2_ref[...]))
        kkr = k * kk_ref[...]
        kk = kkr * lax.rsqrt(_headsum(kkr * kkr, bd) + 1e-6)
        k = k * (1.0 + (a - 1.0) * ka_ref[...])
        tile_rows = pl.ds(bi * tl, tl)
        bonus_s[tile_rows, :] = _headsum(r * k * rk_ref[...], bd) * v
        gate_s[tile_rows, :] = _dot(jax.nn.sigmoid(g_in), g2_ref[...])
        if l_valid < l_total:
            rvalid = (t * tl + lax.broadcasted_iota(jnp.int32, (tl, 1), 0)) < l_valid
            lw = jnp.where(rvalid, lw, 0.0)
            kk = jnp.where(rvalid, kk, 0.0)
            k = jnp.where(rvalid, k, 0.0)
            v = jnp.where(rvalid, v, 0.0)
        r_s[tile_rows, :] = r
        k_s[tile_rows, :] = k
        v_s[tile_rows, :] = v
        z_s[tile_rows, :] = -kk
        p_s[tile_rows, :] = kk * a
        lw_s[tile_rows, :] = lw

    groups = [(bi, p) for bi in range(bb) for p in range(PAIRS)]
    pairs = range(len(groups))
    both = range(2 * len(groups))
    cols = [slice(p * LANES, (p + 1) * LANES) for _, p in groups]

    def chunk(c, carry):
        r0 = pl.multiple_of(c * CHUNK, CHUNK)
        rows = [pl.ds(bi * tl + r0, CHUNK) for bi, _ in groups]
        lw_c = [lw_s[rows[g], cols[g]] for g in pairs]
        g_inc = [_mask_dot_left(tril, x, CUMSUM_PARTS) for x in lw_c]
        g_exc = [g_inc[p] - lw_c[p] for p in pairs]
        g_mid = [g[CHUNK // 2 - 1:CHUNK // 2, :] for g in g_inc]
        g_end = [g[CHUNK - 1:CHUNK, :] for g in g_inc]
        z = [z_s[rows[g], cols[g]] for g in pairs]
        rr = [r_s[rows[g], cols[g]] for g in pairs]
        pp = [p_s[rows[g], cols[g]] for g in pairs]
        kk_ = [k_s[rows[g], cols[g]] for g in pairs]
        vv = [v_s[rows[g], cols[g]] for g in pairs]
        zt = [z[p] * jnp.exp(g_exc[p] - g_mid[p]) for p in pairs]
        rt = [rr[p] * jnp.exp(g_inc[p] - g_mid[p]) for p in pairs]
        en = [jnp.exp(g_mid[p] - g_inc[p]) for p in pairs]
        s = [s_ref[bi, p] for bi, p in groups]
        lhs = [jnp.concatenate([jnp.where(lane_a, zt[p], 0.0), jnp.where(lane_a, 0.0, zt[p]),
                                jnp.where(lane_a, rt[p], 0.0), jnp.where(lane_a, 0.0, rt[p])], axis=0) for p in pairs]
        m = [_rdot(lhs[p], jnp.concatenate([pp[p] * en[p], kk_[p] * en[p]], axis=0), _NT) for p in pairs]
        zr0 = [_rdot(jnp.concatenate([z[p] * jnp.exp(g_exc[p]), rr[p] * jnp.exp(g_inc[p])], axis=0), s[p], _NT)
               for p in pairs]
        mz = [m[i // 2][(i % 2) * CHUNK:(i % 2 + 1) * CHUNK] for i in both]
        azp = [jnp.where(strict, x[:, :CHUNK], 0.0) for x in mz]
        azk = [jnp.where(strict, x[:, CHUNK:], 0.0) for x in mz]
        minv = _tri_inverse(azp, eye, 1, row, col)
        rhs = [zr0[i // 2][:CHUNK] + _rdot(azk[i], vv[i // 2]) for i in both]
        u_h = [_rdot(minv[i], rhs[i]) for i in both]
        u = [jnp.where(lane_a, u_h[2 * p], u_h[2 * p + 1]) for p in pairs]
        uv = [jnp.concatenate([u[p], vv[p]], axis=0) for p in pairs]
        incl2 = jnp.concatenate([incl, incl], axis=1)
        y_h = [_rdot(jnp.where(incl2, m[i // 2][(2 + i % 2) * CHUNK:(3 + i % 2) * CHUNK], 0.0), uv[i // 2])
               for i in both]
        tail = [jnp.exp(g_end[p] - g_inc[p]) for p in pairs]
        s_new = [s[p] * jnp.exp(g_end[p])
                 + _rdot(uv[p], jnp.concatenate([pp[p] * tail[p], kk_[p] * tail[p]], axis=0), _TN) for p in pairs]
        for g, (bi, p) in enumerate(groups):
            s_ref[bi, p] = jnp.where(same_head, s_new[g], 0.0)
            y_s[rows[g], cols[g]] = zr0[g][CHUNK:] + jnp.where(lane_a, y_h[2 * g], y_h[2 * g + 1])
        return carry

    lax.fori_loop(0, tl // CHUNK, chunk, 0)

    for bi in range(bb):
        tile_rows = pl.ds(bi * tl, tl)
        y = y_s[tile_rows, :]
        mean = _headsum(y, bd) * (1.0 / N_B)
        dev = y - mean
        var = _headsum(dev * dev, bd) * (1.0 / N_B)
        yn = dev * lax.rsqrt(var + GN_EPS) * lg_ref[...] + lb_ref[...]
        yb_ref[bi] = ((yn + bonus_s[tile_rows, :]) * gate_s[tile_rows, :])[0:l_in].astype(BF16)

    @pl.when(t == pl.num_programs(1) - 1)
    def _():
        sout_ref[...] = s_ref[...]


def _rwkv(proj, shift_buf8, s0_pairs, vecs, w2p, a2p, g2, bb, tl, l_valid):
    b, l, _ = proj.shape
    lp, l_in = _rec_rows(l, tl)
    kern = functools.partial(_rwkv_kernel, bb=bb, tl=tl, l_in=l_in, l_valid=l_valid, l_total=lp)
    full = lambda *shape: pl.BlockSpec(shape, lambda i, t: (0,) * len(shape))
    mu, w0, a0, k_k, k_a, r_k, lnx_g, lnx_b = vecs
    return pl.pallas_call(
        kern,
        grid=(b // bb, lp // tl),
        in_specs=[pl.BlockSpec((bb, l_in, RW_BLOCK), lambda i, t: (i, t, P_RW // RW_BLOCK)),
                  pl.BlockSpec((bb, SUBLANES, RW_BLOCK), lambda i, t: (i, 0, 0)),
                  pl.BlockSpec((bb, PAIRS, LANES, LANES), lambda i, t: (i, 0, 0, 0)),
                  full(1, RW_BLOCK)] + [full(1, D_B)] * 7 +
                 [full(W_LORA + A_LORA, D_B), full(W_LORA + A_LORA, D_B), full(G_LORA, D_B)],
        out_specs=[pl.BlockSpec((bb, l_in, D_B), lambda i, t: (i, t, 0)),
                   pl.BlockSpec((bb, PAIRS, LANES, LANES), lambda i, t: (i, 0, 0, 0))],
        out_shape=[jax.ShapeDtypeStruct((b, l, D_B), BF16),
                   jax.ShapeDtypeStruct((b, PAIRS, LANES, LANES), F32)],
        scratch_shapes=[pltpu.VMEM((bb, tl + SUBLANES, RW_BLOCK), F32)] +
                       [pltpu.VMEM((bb * tl, D_B), F32)] * 9 +
                       [pltpu.VMEM((bb, PAIRS, LANES, LANES), F32)],
        compiler_params=_cparams(("parallel", "arbitrary"), 48),
        name="rwkv",
    )(proj, shift_buf8, s0_pairs, mu, w0, a0, k_k, k_a, r_k, lnx_g, lnx_b, w2p, a2p, g2)


def _layernorm(y, g, b):
    mu = jnp.mean(y, -1, keepdims=True)
    dev = y - mu
    var = jnp.mean(dev * dev, -1, keepdims=True)
    return dev * lax.rsqrt(var + LN_EPS) * g + b


def _route(logits_t, bias, base_cnt):
    tm = logits_t.shape[1]
    scores = jax.nn.sigmoid(logits_t)
    choice = scores + bias
    neg_inf = -jnp.inf
    iota_g = lax.broadcasted_iota(jnp.int32, (GROUP_SIZE, tm), 0)
    group_score = []
    for g in range(N_GROUPS):
        xg = choice[g * GROUP_SIZE:(g + 1) * GROUP_SIZE, :]
        m1 = jnp.max(xg, axis=0, keepdims=True)
        first = jnp.min(jnp.where(xg == m1, iota_g, GROUP_SIZE), axis=0, keepdims=True)
        m2 = jnp.max(jnp.where(iota_g == first, neg_inf, xg), axis=0, keepdims=True)
        group_score.append(m1 + m2)
    masked = []
    for g in range(N_GROUPS):
        rank = jnp.zeros((1, tm), jnp.int32)
        for o in range(N_GROUPS):
            if o == g:
                continue
            ahead = group_score[o] > group_score[g]
            if o < g:
                ahead = ahead | (group_score[o] == group_score[g])
            rank = rank + ahead.astype(jnp.int32)
        keep = rank < TOPK_GROUPS
        masked.append(jnp.where(keep, choice[g * GROUP_SIZE:(g + 1) * GROUP_SIZE, :], neg_inf))
    cur = jnp.concatenate(masked, axis=0)
    iota_e = lax.broadcasted_iota(jnp.int32, (N_EXPERTS, tm), 0)
    sel = jnp.zeros((N_EXPERTS, tm), F32)
    picks = []
    firsts = []
    for _ in range(TOP_K):
        m = jnp.max(cur, axis=0, keepdims=True)
        first = jnp.min(jnp.where(cur == m, iota_e, N_EXPERTS), axis=0, keepdims=True)
        pick = iota_e == first
        sel = jnp.where(pick, 1.0, sel)
        cur = jnp.where(pick, neg_inf, cur)
        picks.append(pick)
        firsts.append(first)
    wsel = sel * scores
    denom = jnp.sum(wsel, axis=0, keepdims=True) + 1e-20
    gates = (ROUTED_SCALE * wsel) / denom

    upto = (lax.broadcasted_iota(jnp.int32, (tm, tm), 0) <= lax.broadcasted_iota(jnp.int32, (tm, tm), 1))
    csum = jnp.dot(sel.astype(BF16), jnp.where(upto, 1.0, 0.0).astype(BF16), preferred_element_type=F32)
    before = base_cnt + csum - sel
    iota_k = lax.broadcasted_iota(jnp.int32, (TOP_K, tm), 0)
    eidx = jnp.zeros((TOP_K, tm), jnp.int32)
    rank = jnp.zeros((TOP_K, tm), F32)
    w_k = jnp.zeros((TOP_K, tm), F32)
    for i in range(TOP_K):
        eidx = jnp.where(iota_k == i, firsts[i], eidx)
        rank = jnp.where(iota_k == i, jnp.sum(jnp.where(picks[i], before, 0.0), axis=0, keepdims=True), rank)
        w_k = jnp.where(iota_k == i, jnp.sum(jnp.where(picks[i], gates, 0.0), axis=0, keepdims=True), w_k)
    return gates, eidx, rank.astype(jnp.int32), w_k, jnp.sum(sel, axis=1, keepdims=True)


def _merge_kernel(ya_ref, yb_ref, mg_ref, x_ref, ga1_ref, sc2_ref, sh2_ref, pa_ref, pb_ref, wo_ref,
                  g1_ref, b1_ref, rwt_ref, rb_ref, x1_ref, h2_ref, gt_ref, hp_ref, eidx_ref, rank_ref, wk_ref,
                  cnt_ref, cnt_s, *, dn_alpha):
    bb, tl, d = x_ref.shape
    tm = bb * tl

    @pl.when((pl.program_id(0) == 0) & (pl.program_id(1) == 0))
    def _():
        cnt_s[...] = jnp.zeros_like(cnt_s)

    ya = ya_ref[...].reshape(tm, V_A)
    yb = yb_ref[...].reshape(tm, D_B)
    mg = mg_ref[...].reshape(tm, 2 * d)
    merged = (jax.nn.sigmoid(mg[:, :d]) * jnp.dot(ya, pa_ref[...], preferred_element_type=F32)
              + jax.nn.sigmoid(mg[:, d:]) * jnp.dot(yb, pb_ref[...], preferred_element_type=F32))
    mix = _dot(merged, wo_ref[...])
    y = dn_alpha * x_ref[...] + (1.0 + ga1_ref[...]) * mix.reshape(bb, tl, d)
    x1 = _layernorm(y, g1_ref[...], b1_ref[...])
    x1_ref[...] = x1
    h2 = (x1 * (1.0 + sc2_ref[...]) + sh2_ref[...]).astype(BF16)
    h2_ref[...] = h2
    h2f = h2.reshape(tm, d)
    hp_ref[...] = _pack_bf16_pairs(h2f.astype(F32))
    logits_t = lax.dot_general(rwt_ref[...], h2f, (_NT, ((), ())), preferred_element_type=F32)
    gates, eidx, rank, w_k, tile_cnt = _route(logits_t, rb_ref[...], cnt_s[...])
    gt_ref[...] = gates
    eidx_ref[...] = eidx
    rank_ref[...] = rank
    wk_ref[...] = w_k
    cnt_s[...] = cnt_s[...] + tile_cnt
    cnt_ref[...] = cnt_s[...].astype(jnp.int32)


def _merge(ya, yb, proj, x, ga1, sc2, sh2, p_a, p_b, w_o, ln_g, ln_b, rw_t, r_bias, bb, tl, dn_alpha):
    b, l, d = x.shape
    nj = l // tl
    tm = bb * tl
    t_all = b * l
    full = lambda *shape: pl.BlockSpec(shape, lambda i, j: (0,) * len(shape))
    mod = pl.BlockSpec((bb, 1, d), lambda i, j: (i, 0, 0))
    tok = lambda width: pl.BlockSpec((bb, tl, width), lambda i, j: (i, j, 0))
    per_k = pl.BlockSpec((TOP_K, tm), lambda i, j: (0, i * nj + j))
    return pl.pallas_call(
        functools.partial(_merge_kernel, dn_alpha=dn_alpha),
        grid=(b // bb, nj),
        in_specs=[tok(V_A), tok(D_B),
                  pl.BlockSpec((bb, tl, 2 * d), lambda i, j: (i, j, P_MG // (2 * d))),
                  tok(d), mod, mod, mod,
                  full(V_A, d), full(D_B, d), full(d, d), full(1, d), full(1, d),
                  full(N_EXPERTS, d), full(N_EXPERTS, 1)],
        out_specs=[tok(d), tok(d), pl.BlockSpec((N_EXPERTS, tm), lambda i, j: (0, i * nj + j)),
                   pl.BlockSpec((tm, d // 2), lambda i, j: (i * nj + j, 0)),
                   per_k, per_k, per_k, full(N_EXPERTS, 1)],
        out_shape=[jax.ShapeDtypeStruct((b, l, d), F32),
                   jax.ShapeDtypeStruct((b, l, d), BF16),
                   jax.ShapeDtypeStruct((N_EXPERTS, t_all), F32),
                   jax.ShapeDtypeStruct((t_all, d // 2), jnp.int32),
                   jax.ShapeDtypeStruct((TOP_K, t_all), jnp.int32),
                   jax.ShapeDtypeStruct((TOP_K, t_all), jnp.int32),
                   jax.ShapeDtypeStruct((TOP_K, t_all), F32),
                   jax.ShapeDtypeStruct((N_EXPERTS, 1), jnp.int32)],
        scratch_shapes=[pltpu.VMEM((N_EXPERTS, 1), F32)],
        compiler_params=_cparams(("arbitrary", "arbitrary"), 48),
        name="merge",
    )(ya, yb, proj, x, ga1, sc2, sh2, p_a, p_b, w_o, ln_g, ln_b, rw_t, r_bias)


def _moe_kernel(h_ref, g_ref, x1_ref, ga2_ref, wg_ref, wu_ref, wd_ref, sgu_ref, sd_ref, g2_ref, b2_ref,
                o_ref, acc_ref, *, dn_alpha):
    e = pl.program_id(2)
    bb, tl, d = h_ref.shape
    tm = bb * tl
    h = h_ref[...].reshape(tm, d)

    @pl.when(e == 0)
    def _():
        su = jnp.dot(h, sgu_ref[...], preferred_element_type=F32)
        act = _silu(su[:, :D_SHARED]) * su[:, D_SHARED:]
        acc_ref[...] = _dot(act, sd_ref[...])

    lane = lax.broadcasted_iota(jnp.int32, (tm, N_EXPERTS), 1)
    gates = g_ref[...]
    acts = []
    for i in range(EXPERTS_PER_STEP):
        g = jnp.dot(h, wg_ref[i].astype(BF16), preferred_element_type=F32)
        u = jnp.dot(h, wu_ref[i].astype(BF16), preferred_element_type=F32)
        gate = _lane_pick(gates, lane, e * EXPERTS_PER_STEP + i)
        acts.append((_silu(g) * u * gate).astype(BF16))
    wd = wd_ref[...].reshape(EXPERTS_PER_STEP * D_EXPERT, d).astype(BF16)
    acc_ref[...] += jnp.dot(jnp.concatenate(acts, axis=1), wd, preferred_element_type=F32)

    @pl.when(e == pl.num_programs(2) - 1)
    def _():
        y = dn_alpha * x1_ref[...] + (1.0 + ga2_ref[...]) * acc_ref[...].reshape(bb, tl, d)
        o_ref[...] = _layernorm(y, g2_ref[...], b2_ref[...])


def _moe(h2, gates, x1, ga2, we_gate, we_up, we_down, ws_gu, ws_down, ln_g, ln_b, bb, tl, dn_alpha):
    b, l, d = x1.shape
    nj = l // tl
    tm = bb * tl
    full = lambda *shape: pl.BlockSpec(shape, lambda i, j, e: (0,) * len(shape))
    tok = pl.BlockSpec((bb, tl, d), lambda i, j, e: (i, j, 0))
    return pl.pallas_call(
        functools.partial(_moe_kernel, dn_alpha=dn_alpha),
        grid=(b // bb, nj, N_EXPERTS // EXPERTS_PER_STEP),
        in_specs=[tok,
                  pl.BlockSpec((tm, N_EXPERTS), lambda i, j, e: (i * nj + j, 0)),
                  tok,
                  pl.BlockSpec((bb, 1, d), lambda i, j, e: (i, 0, 0)),
                  pl.BlockSpec((EXPERTS_PER_STEP, d, D_EXPERT), lambda i, j, e: (e, 0, 0)),
                  pl.BlockSpec((EXPERTS_PER_STEP, d, D_EXPERT), lambda i, j, e: (e, 0, 0)),
                  pl.BlockSpec((EXPERTS_PER_STEP, D_EXPERT, d), lambda i, j, e: (e, 0, 0)),
                  full(d, 2 * D_SHARED), full(D_SHARED, d), full(1, d), full(1, d)],
        out_specs=tok,
        out_shape=jax.ShapeDtypeStruct((b, l, d), F32),
        scratch_shapes=[pltpu.VMEM((tm, d), F32)],
        compiler_params=_cparams(("parallel", "parallel", "arbitrary"), 56),
        name="moe",
    )(h2, gates, x1, ga2, we_gate, we_up, we_down, ws_gu, ws_down, ln_g, ln_b)


def _sc_mesh():
    return plsc.VectorSubcoreMesh(core_axis_name="c", subcore_axis_name="s")


def _sc_cores():
    info = plsc.get_sparse_core_info()
    assert info.num_cores * info.num_subcores == SC_WORKERS, info
    return info.num_cores


def _sc_scatter_rows(x, pos, n_out):
    t, w = x.shape
    k = pos.shape[0]
    t_per_w = t // SC_WORKERS
    n_cores = _sc_cores()

    @functools.partial(
        pl.kernel, mesh=_sc_mesh(),
        out_type=jax.ShapeDtypeStruct((n_out, w), jnp.int32),
        scratch_types=[pltpu.VMEM((k, SC_CHUNK), jnp.int32), pltpu.VMEM((SC_CHUNK, w), jnp.int32),
                       pltpu.SemaphoreType.DMA],
    )
    def scatter_kernel(x_hbm, pos_hbm, out_hbm, idx_v, rows_v, sem):
        base = (lax.axis_index("s") * n_cores + lax.axis_index("c")) * t_per_w

        @pl.loop(0, t_per_w // SC_CHUNK)
        def _(i):
            off = pl.multiple_of(base + i * SC_CHUNK, SC_CHUNK)
            pltpu.sync_copy(pos_hbm.at[:, pl.ds(off, SC_CHUNK)], idx_v)
            pltpu.sync_copy(x_hbm.at[pl.ds(off, SC_CHUNK)], rows_v)
            for j in range(k):
                pltpu.async_copy(rows_v, out_hbm.at[idx_v.at[j]], sem).wait()

    return scatter_kernel(x, pos)


def _sc_gather_rows(table, idx):
    n = idx.shape[0]
    w = table.shape[1]
    n_per_w = n // SC_WORKERS
    n_cores = _sc_cores()

    half = SC_CHUNK // 2

    @functools.partial(
        pl.kernel, mesh=_sc_mesh(),
        out_type=jax.ShapeDtypeStruct((n, w), jnp.int32),
        scratch_types=[pltpu.VMEM((2, half), jnp.int32), pltpu.VMEM((2, half, w), jnp.int32)]
                      + [pltpu.SemaphoreType.DMA] * 4,
    )
    def gather_kernel(table_hbm, idx_hbm, out_hbm, idx_v, rows_v, sem_g0, sem_g1, sem_w0, sem_w1):
        base = (lax.axis_index("s") * n_cores + lax.axis_index("c")) * n_per_w
        sem_g = (sem_g0, sem_g1)
        sem_w = (sem_w0, sem_w1)

        @pl.loop(0, n_per_w // SC_CHUNK)
        def _(i):
            off = pl.multiple_of(base + i * SC_CHUNK, SC_CHUNK)
            for h in range(2):
                pltpu.sync_copy(idx_hbm.at[pl.ds(off + h * half, half)], idx_v.at[h])
            gathers = [pltpu.async_copy(table_hbm.at[idx_v.at[h]], rows_v.at[h], sem_g[h]) for h in range(2)]
            writes = []
            for h in range(2):
                gathers[h].wait()
                writes.append(pltpu.async_copy(rows_v.at[h], out_hbm.at[pl.ds(off + h * half, half)], sem_w[h]))
            for c in writes:
                c.wait()

    return gather_kernel(table, idx)


def _experts_kernel(te_ref, used_ref, xs_ref, wg_ref, wu_ref, wd_ref, o_ref, wg_s, wu_s, wd_s):
    i = pl.program_id(0)

    @pl.when((i == 0) | (te_ref[i] != te_ref[jnp.maximum(i - 1, 0)]))
    def _():
        wg_s[...] = wg_ref[0].astype(BF16)
        wu_s[...] = wu_ref[0].astype(BF16)
        wd_s[...] = wd_ref[0].astype(BF16)

    @pl.when(i < used_ref[0])
    def _():
        x = _unpack_bf16_pairs(xs_ref[...]).astype(BF16)
        g = jnp.dot(x, wg_s[...], preferred_element_type=F32)
        u = jnp.dot(x, wu_s[...], preferred_element_type=F32)
        act = (_silu(g) * u).astype(BF16)
        o_ref[...] = _pack_bf16_pairs(jnp.dot(act, wd_s[...], preferred_element_type=F32))


def _experts(xs, tile_expert, tiles_used, we_gate, we_up, we_down):
    r, half = xs.shape
    d = 2 * half
    row_tile = lambda i, te, used: (jnp.minimum(i, used[0] - 1), 0)
    grid_spec = pltpu.PrefetchScalarGridSpec(
        num_scalar_prefetch=2,
        grid=(r // EXPERT_TILE,),
        in_specs=[pl.BlockSpec((EXPERT_TILE, half), row_tile),
                  pl.BlockSpec((1, d, D_EXPERT), lambda i, te, used: (te[i], 0, 0)),
                  pl.BlockSpec((1, d, D_EXPERT), lambda i, te, used: (te[i], 0, 0)),
                  pl.BlockSpec((1, D_EXPERT, d), lambda i, te, used: (te[i], 0, 0))],
        out_specs=pl.BlockSpec((EXPERT_TILE, half), row_tile),
        scratch_shapes=[pltpu.VMEM((d, D_EXPERT), BF16), pltpu.VMEM((d, D_EXPERT), BF16),
                        pltpu.VMEM((D_EXPERT, d), BF16)],
    )
    return pl.pallas_call(
        _experts_kernel,
        grid_spec=grid_spec,
        out_shape=jax.ShapeDtypeStruct((r, half), jnp.int32),
        compiler_params=_cparams(("arbitrary",), 32),
        name="experts",
    )(tile_expert, tiles_used, xs, we_gate, we_up, we_down)


def _combine_kernel(og_ref, wk_ref, h_ref, x1_ref, ga2_ref, sgu_ref, sd_ref, g2_ref, b2_ref, o_ref, *, dn_alpha):
    bb, tl, d = h_ref.shape
    tm = bb * tl
    h = h_ref[...].reshape(tm, d)
    su = jnp.dot(h, sgu_ref[...], preferred_element_type=F32)
    acc = _dot(_silu(su[:, :D_SHARED]) * su[:, D_SHARED:], sd_ref[...])
    lane = lax.broadcasted_iota(jnp.int32, (tm, TOP_K), 1)
    w_all = wk_ref[...]
    for k in range(TOP_K):
        acc = acc + _lane_pick(w_all, lane, k) * _unpack_bf16_pairs(og_ref[k])
    y = dn_alpha * x1_ref[...] + (1.0 + ga2_ref[...]) * acc.reshape(bb, tl, d)
    o_ref[...] = _layernorm(y, g2_ref[...], b2_ref[...])


def _combine(og, w_tk, h2, x1, ga2, ws_gu, ws_down, ln_g, ln_b, bb, tl, dn_alpha):
    b, l, d = x1.shape
    nj = l // tl
    tm = bb * tl
    full = lambda *shape: pl.BlockSpec(shape, lambda i, j: (0,) * len(shape))
    tok = pl.BlockSpec((bb, tl, d), lambda i, j: (i, j, 0))
    return pl.pallas_call(
        functools.partial(_combine_kernel, dn_alpha=dn_alpha),
        grid=(b // bb, nj),
        in_specs=[pl.BlockSpec((TOP_K, tm, d // 2), lambda i, j: (0, i * nj + j, 0)),
                  pl.BlockSpec((tm, TOP_K), lambda i, j: (i * nj + j, 0)),
                  tok, tok,
                  pl.BlockSpec((bb, 1, d), lambda i, j: (i, 0, 0)),
                  full(d, 2 * D_SHARED), full(D_SHARED, d), full(1, d), full(1, d)],
        out_specs=tok,
        out_shape=jax.ShapeDtypeStruct((b, l, d), F32),
        compiler_params=_cparams(("parallel", "parallel"), 48),
        name="combine",
    )(og, w_tk, h2, x1, ga2, ws_gu, ws_down, ln_g, ln_b)


def _moe_sparse(hp, eidx, rank, w_k, cnt, h2, x1, ga2, p, bb, dn_alpha):
    t_all = hp.shape[0]
    n_tiles = t_all * TOP_K // EXPERT_TILE + N_EXPERTS
    seg_tiles = (cnt[:, 0] + EXPERT_TILE - 1) // EXPERT_TILE
    seg_end = jnp.cumsum(seg_tiles)
    seg_start = (seg_end - seg_tiles) * EXPERT_TILE
    experts = jnp.arange(N_EXPERTS, dtype=jnp.int32)
    pos = rank + jnp.sum(jnp.where(eidx[None] == experts[:, None, None], seg_start[:, None, None], 0), axis=0)
    tiles = jnp.arange(n_tiles, dtype=jnp.int32)
    tile_expert = jnp.minimum(jnp.sum((seg_end[None, :] <= tiles[:, None]).astype(jnp.int32), axis=1),
                              N_EXPERTS - 1)
    xs = _sc_scatter_rows(hp, pos, n_tiles * EXPERT_TILE)
    outs = _experts(xs, tile_expert, seg_end[-1:].astype(jnp.int32), p['we_gate'], p['we_up'], p['we_down'])
    og = _sc_gather_rows(outs, pos.reshape(-1)).reshape(TOP_K, t_all, hp.shape[1])
    return _combine(og, w_k.T, h2, x1, ga2, p['ws_gu'], p['ws_down'], p['ln2_g'], p['ln2_b'], bb, COMBINE_TL, dn_alpha)


def _pad_rows(buf, width):
    b, n, w = buf.shape
    return jnp.pad(buf.astype(F32), ((0, 0), (SUBLANES - n, 0), (0, width - w)))


def _to_pairs(s):
    b = s.shape[0]
    s = s.astype(F32).reshape(b, PAIRS, 2, N_B, N_B)
    zero = jnp.zeros_like(s[:, :, 0])
    top = jnp.concatenate([s[:, :, 0], zero], axis=-1)
    bot = jnp.concatenate([zero, s[:, :, 1]], axis=-1)
    return jnp.concatenate([top, bot], axis=-2)


def _from_pairs(sp):
    b = sp.shape[0]
    return jnp.stack([sp[:, :, :N_B, :N_B], sp[:, :, N_B:, N_B:]], axis=2).reshape(b, H_B, N_B, N_B)


def _layer(x, mod, conv_buf, gdn_s, shift_buf, rwkv_s, p, tiles, dn_alpha):
    b, l, d = x.shape
    sh1, sc1, ga1, sh2, sc2, ga2 = mod
    proj = _inproj(x, sc1, sh1, p['w_in'], tiles['bb'], tiles['tl'])
    ya, gdn_new = _gdn(proj, _pad_rows(conv_buf, QKV_A), gdn_s.astype(F32), p['conv_w'], p['alog_row'],
                       p['dt_row'], p['gdn_norm_w'], tiles['bb_gdn'], tiles['tl_gdn'], l)
    yb, rwkv_pairs = _rwkv(proj, _pad_rows(shift_buf, RW_BLOCK), _to_pairs(rwkv_s), p['rwkv_vecs'],
                           p['w2p'], p['a2p'], p['g2'], tiles['bb_rwkv'], tiles['tl_rwkv'], l)

    x1, h2, gates_t, hp, eidx, rank, w_k, cnt = _merge(
        ya, yb, proj, x, ga1, sc2, sh2, p['p_a'], p['p_b'], p['w_o'], p['ln1_g'], p['ln1_b'],
        p['router_wt'], p['router_bias'], tiles['bb'], tiles['tl_merge'], dn_alpha)
    if (b * l) % (SC_WORKERS * SC_CHUNK) == 0:
        out = _moe_sparse(hp, eidx, rank, w_k, cnt, h2, x1, ga2, p, tiles['bb'], dn_alpha)
    else:
        out = _moe(h2, gates_t.T, x1, ga2, p['we_gate'], p['we_up'], p['we_down'], p['ws_gu'], p['ws_down'],
                   p['ln2_g'], p['ln2_b'], tiles['bb'], tiles['tl'], dn_alpha)

    pre = jnp.concatenate([conv_buf.astype(F32), proj[:, :, P_QKV:P_QKV + QKV_A]], axis=1)
    conv_new = pre[:, -(CONV_W - 1):]
    shift_new = proj[:, l - 1:l, P_RW:P_RW + SHIFT_W]
    return (out, conv_new.astype(conv_buf.dtype), gdn_new.astype(gdn_s.dtype),
            shift_new.astype(shift_buf.dtype), _from_pairs(rwkv_pairs).astype(rwkv_s.dtype))


def _prep_params(l, w_in, conv_w, a_log, dt_bias, gdn_norm_w, mu_shift, w0, w2, a0, a2, g2, k_k, k_a, r_k,
                 lnx_g, lnx_b, p_a, p_b, w_o, ln1_g, ln1_b, router_w, router_bias, we_gate, we_up, we_down,
                 ws_gate, ws_up, ws_down, ln2_g, ln2_b):
    d = D_MODEL
    w = w_in[l]
    w_p = jnp.concatenate(
        [w[:, :QKV_A], w[:, OFF_Z:OFF_RWKV], w[:, OFF_RWKV:OFF_MERGE],
         jnp.zeros((d, RW_BLOCK - SHIFT_W), w.dtype), w[:, OFF_MERGE:], w[:, OFF_ALPHA:OFF_Z],
         jnp.zeros((d, N_PROJ - P_AB - 2 * H_A), w.dtype)], axis=1).astype(BF16)
    row = lambda v, width: jnp.pad(v.astype(F32).reshape(1, -1), ((0, 0), (0, width - v.size)))
    zeros_lora = jnp.zeros((W_LORA, D_B), F32)
    return {
        'w_in': w_p,
        'conv_w': conv_w[l].astype(F32),
        'alog_row': row(a_log[l], LANES),
        'dt_row': row(dt_bias[l], LANES),
        'gdn_norm_w': row(gdn_norm_w[l], DK_A),
        'rwkv_vecs': (row(mu_shift[l], RW_BLOCK), row(w0[l], D_B), row(a0[l], D_B), row(k_k[l], D_B),
                      row(k_a[l], D_B), row(r_k[l], D_B), row(lnx_g[l], D_B), row(lnx_b[l], D_B)),
        'w2p': jnp.concatenate([w2[l].astype(F32), zeros_lora], axis=0),
        'a2p': jnp.concatenate([zeros_lora, a2[l].astype(F32)], axis=0),
        'g2': g2[l].astype(F32),
        'p_a': p_a[l].astype(BF16), 'p_b': p_b[l].astype(BF16), 'w_o': w_o[l].astype(BF16),
        'ln1_g': row(ln1_g[l], d), 'ln1_b': row(ln1_b[l], d),
        'router_wt': router_w[l].T.astype(BF16),
        'router_bias': router_bias[l].astype(F32).reshape(N_EXPERTS, 1),
        'we_gate': we_gate[l], 'we_up': we_up[l], 'we_down': we_down[l],
        'ws_gu': jnp.concatenate([ws_gate[l], ws_up[l]], axis=-1).astype(BF16),
        'ws_down': ws_down[l].astype(BF16),
        'ln2_g': row(ln2_g[l], d), 'ln2_b': row(ln2_b[l], d),
    }


def _tiles(b, l):
    lp = -(-l // CHUNK) * CHUNK
    if l >= TOKEN_TILE:
        bb, tl, tl_merge = 1, TOKEN_TILE, MERGE_TL
    else:
        bb, tl, tl_merge = b, l, l
    return {'bb': bb, 'tl': tl, 'tl_merge': tl_merge,
            'bb_gdn': 4, 'tl_gdn': CHUNK, 'bb_rwkv': 2, 'tl_rwkv': min(2 * CHUNK, lp)}


def kernel(x_prompt, x_sample, c_prompt, c_sample, state_gdn_conv, state_gdn, state_rwkv_shift, state_rwkv, w_ada, b_ada, w_in, conv_w, a_log, dt_bias, gdn_norm_w, mu_shift, w0, w2, a0, a2, g2, k_k, k_a, r_k, lnx_g, lnx_b, p_a, p_b, w_o, ln1_g, ln1_b, router_w, router_bias, we_gate, we_up, we_down, ws_gate, ws_up, ws_down, ln2_g, ln2_b):
    depth = w_ada.shape[0]
    dn_alpha = (2 * depth) ** 0.25
    bp, lp_, d = x_prompt.shape
    bs, ls, _ = x_sample.shape
    dtp = x_prompt.dtype
    tiles_p = _tiles(bp, lp_)
    tiles_s = _tiles(bs, ls)

    yp, ys = x_prompt, x_sample
    new_p = ([], [], [], [])
    new_s = ([], [], [], [])
    for l in range(depth):
        p = _prep_params(l, w_in, conv_w, a_log, dt_bias, gdn_norm_w, mu_shift, w0, w2, a0, a2, g2, k_k, k_a,
                         r_k, lnx_g, lnx_b, p_a, p_b, w_o, ln1_g, ln1_b, router_w, router_bias, we_gate, we_up,
                         we_down, ws_gate, ws_up, ws_down, ln2_g, ln2_b)
        mod = _ada(jnp.concatenate([c_prompt, c_sample], axis=0), w_ada[l], b_ada[l])
        mod_p = tuple(m[:, None, :] for m in jnp.split(mod[:bp], 6, axis=-1))
        mod_s = tuple(m[:, None, :] for m in jnp.split(mod[bp:], 6, axis=-1))
        yp, *sp = _layer(yp, mod_p,
                         jnp.zeros((bp, CONV_W - 1, QKV_A), dtp), jnp.zeros((bp, H_A, DK_A, DK_A), dtp),
                         jnp.zeros((bp, 1, SHIFT_W), dtp), jnp.zeros((bp, H_B, N_B, N_B), dtp),
                         p, tiles_p, dn_alpha)
        ys, *ss = _layer(ys, mod_s, state_gdn_conv[l], state_gdn[l], state_rwkv_shift[l], state_rwkv[l],
                         p, tiles_s, dn_alpha)
        for lst, val in zip(new_p, sp):
            lst.append(val)
        for lst, val in zip(new_s, ss):
            lst.append(val)
    conv_p, gdn_p, shift_p, rwkv_p = [jnp.stack(t, 0) for t in new_p]
    conv_s, gdn_s, shift_s, rwkv_s = [jnp.stack(t, 0) for t in new_s]
    return (yp, ys, conv_p, gdn_p, shift_p, rwkv_p, conv_s, gdn_s, shift_s, rwkv_s)
```
